```python
import jax
import jax.numpy as jnp
from jax import lax
import numpy as np

D_MODEL = 4096
BATCH = 4
SEQ = 2048
DEPTH = 1

CHUNK = 64
NORM_EPS = 1e-6
RET_HEADS = 8
RET_QK_DIM = 256
RET_V_DIM = 512
ROPE_BASE = 10000.0
ATT_HEADS = 16
ATT_HEAD_DIM = 128
PREV_CHUNKS = 8
MAX_REL = 128
REL_TABLE = MAX_REL + CHUNK
N_EXPERTS = 32
TOP_K = 4
D_EXPERT = 1536
SWIGLU_ALPHA = 1.702
SWIGLU_LIMIT = 7.0
EXPERT_BLOCK = 256
N_MOD = 6
RET_QK_W = RET_HEADS * RET_QK_DIM
RET_V_W = RET_HEADS * RET_V_DIM
ATT_W = ATT_HEADS * ATT_HEAD_DIM
IN_SPLITS = (RET_QK_W, RET_QK_W, RET_V_W, RET_V_W, ATT_W, ATT_W, ATT_W, D_MODEL, D_MODEL)
IN_W = 2 * RET_QK_W + 2 * RET_V_W + 3 * ATT_W + 2 * D_MODEL

kernel_name = 'hybrid_retention_chunkattn_moe_block'


def _rms_normalize(t):
    tf = t.astype(jnp.float32)
    return (tf * lax.rsqrt(jnp.mean(tf * tf, axis=-1, keepdims=True) + NORM_EPS)).astype(t.dtype)


def _rms_norm(t, g):
    return _rms_normalize(t) * g


def _split_cols(t, widths):
    parts, off = [], 0
    for w in widths:
        parts.append(t[..., off:off + w])
        off += w
    return parts


def _rotary(t):
    s, d = t.shape[1], t.shape[-1]
    half = d // 2
    inv_freq = ROPE_BASE ** (-jnp.arange(half, dtype=jnp.float32) / half)
    ang = jnp.arange(s, dtype=jnp.float32)[:, None] * inv_freq[None, :]
    cos = jnp.cos(ang)[None, :, None, :].astype(t.dtype)
    sin = jnp.sin(ang)[None, :, None, :].astype(t.dtype)
    t1, t2 = t[..., :half], t[..., half:]
    return jnp.concatenate([t1 * cos - t2 * sin, t1 * sin + t2 * cos], axis=-1)


def _retention(q, k, v, g):
    b, s, _ = q.shape
    n = s // CHUNK
    dt = q.dtype
    q = _rotary(q.reshape(b, s, RET_HEADS, RET_QK_DIM))
    k = _rotary(k.reshape(b, s, RET_HEADS, RET_QK_DIM)) * (RET_QK_DIM ** -0.5)
    v = v.reshape(b, s, RET_HEADS, RET_V_DIM)

    def to_chunks(t):
        return t.reshape(b, n, CHUNK, RET_HEADS, t.shape[-1]).transpose(1, 0, 3, 2, 4)

    log_gamma = jnp.log(1.0 - 2.0 ** (-5.0 - jnp.arange(RET_HEADS, dtype=jnp.float32)))
    pos = jnp.arange(CHUNK, dtype=jnp.float32)
    inner_decay = jnp.exp(log_gamma[:, None, None] * jnp.abs(pos[:, None] - pos[None, :])).astype(dt)
    q_decay = jnp.exp(log_gamma[:, None] * (pos + 1.0))[:, :, None].astype(dt)
    k_decay = jnp.exp(log_gamma[:, None] * (CHUNK - 1.0 - pos))[:, :, None].astype(dt)
    chunk_decay = jnp.exp(log_gamma * CHUNK)[:, None, None].astype(dt)

    def step(state, qkv):
        qc, kc, vc = qkv
        scores = jnp.einsum('bhqd,bhkd->bhqk', qc, kc) * inner_decay
        out = (jnp.einsum('bhqk,bhke->bhqe', scores, vc)
               + jnp.einsum('bhqd,bhde->bhqe', qc * q_decay, state))
        state = state * chunk_decay + jnp.einsum('bhkd,bhke->bhde', kc * k_decay, vc)
        return state, out

    state0 = jnp.zeros((b, RET_HEADS, RET_QK_DIM, RET_V_DIM), dt)
    _, o = lax.scan(step, state0, (to_chunks(q), to_chunks(k), to_chunks(v)))
    o = _rms_normalize(o.transpose(1, 0, 3, 2, 4).reshape(b, s, RET_HEADS, RET_V_DIM))
    return jax.nn.silu(g) * o.reshape(b, s, RET_V_W)


def _chunk_attention(q, k, v, q_norm_g, k_norm_g, rel_bias):
    b, s, _ = q.shape
    n = s // CHUNK
    pad = PREV_CHUNKS * CHUNK
    band = pad + CHUNK
    q = _rms_norm(q.reshape(b, s, ATT_HEADS, ATT_HEAD_DIM), q_norm_g)
    k = _rms_norm(k.reshape(b, s, ATT_HEADS, ATT_HEAD_DIM), k_norm_g)
    v = v.reshape(b, s, ATT_HEADS, ATT_HEAD_DIM)
    q_chunks = q.reshape(b, n, CHUNK, ATT_HEADS, ATT_HEAD_DIM).transpose(1, 0, 3, 2, 4)
    k_pad = jnp.pad(k.transpose(0, 2, 1, 3), ((0, 0), (0, 0), (pad, 0), (0, 0)))
    v_pad = jnp.pad(v.transpose(0, 2, 1, 3), ((0, 0), (0, 0), (pad, 0), (0, 0)))
    q_idx = jnp.arange(CHUNK)
    k_idx = jnp.arange(band)
    rel = q_idx[:, None] + pad - k_idx[None, :]
    bias = rel_bias[:, jnp.minimum(rel, MAX_REL) + (CHUNK - 1)].astype(jnp.float32)
    scale = ATT_HEAD_DIM ** -0.5
    neg = jnp.finfo(jnp.float32).min

    def attend(args):
        qc, i = args
        kc = lax.dynamic_slice_in_dim(k_pad, i * CHUNK, band, axis=2)
        vc = lax.dynamic_slice_in_dim(v_pad, i * CHUNK, band, axis=2)
        sc = jnp.einsum('bhqd,bhkd->bhqk', qc, kc).astype(jnp.float32) * scale + bias
        valid = k_idx >= (PREV_CHUNKS - i) * CHUNK
        p = jax.nn.softmax(jnp.where(valid, sc, neg), axis=-1).astype(vc.dtype)
        return jnp.einsum('bhqk,bhkd->bhqd', p, vc)

    o = lax.map(attend, (q_chunks, jnp.arange(n)))
    return o.transpose(1, 0, 3, 2, 4).reshape(b, s, ATT_W)


def _clamped_swiglu(h):
    x_glu = jnp.minimum(h[..., ::2], SWIGLU_LIMIT)
    x_lin = jnp.clip(h[..., 1::2], -SWIGLU_LIMIT, SWIGLU_LIMIT)
    return x_glu * jax.nn.sigmoid(SWIGLU_ALPHA * x_glu) * (x_lin + 1.0)


def _expert_ffn(h, w_router, b_router, w1, b1, w2, b2, layer):
    t, d = h.shape
    logits = (h @ w_router + b_router).astype(jnp.float32)
    top_val, top_idx = lax.top_k(logits, TOP_K)
    gate = jax.nn.softmax(top_val, axis=-1)
    a = t * TOP_K
    e_flat = top_idx.reshape(a)
    tok_flat = jnp.arange(a, dtype=jnp.int32) // TOP_K
    order = jnp.argsort(e_flat)
    e_sorted = e_flat[order]
    sizes = jnp.bincount(e_flat, length=N_EXPERTS)
    padded = (sizes + EXPERT_BLOCK - 1) // EXPERT_BLOCK * EXPERT_BLOCK
    pad_end = jnp.cumsum(padded)
    pad_start = pad_end - padded
    start = jnp.cumsum(sizes) - sizes
    dest = pad_start[e_sorted] + jnp.arange(a, dtype=jnp.int32) - start[e_sorted]
    n_blocks = -(-a // EXPERT_BLOCK) + N_EXPERTS
    p = n_blocks * EXPERT_BLOCK
    row_tok = jnp.zeros((p,), jnp.int32).at[dest].set(tok_flat[order])
    row_gate = jnp.zeros((p,), jnp.float32).at[dest].set(gate.reshape(a)[order])
    block_e = jnp.minimum(
        jnp.searchsorted(pad_end, jnp.arange(n_blocks) * EXPERT_BLOCK, side='right'), N_EXPERTS - 1)

    def run_block(args):
        xb, e = args
        hb = xb @ w1[layer, e] + b1[layer, e]
        return _clamped_swiglu(hb) @ w2[layer, e] + b2[layer, e]

    xs = h[row_tok].reshape(n_blocks, EXPERT_BLOCK, d)
    ys = lax.map(run_block, (xs, block_e)).reshape(p, d)
    return jax.ops.segment_sum(ys * row_gate[:, None].astype(ys.dtype), row_tok, num_segments=t)


def setup_inputs(seed: int = 0) -> dict:
    key = jax.random.key(seed)
    ks = jax.random.split(key, 19)
    f32 = jnp.float32

    def nrm(k, shape, scale):
        return jax.random.normal(k, shape, f32) * scale

    D, E, F = D_MODEL, N_EXPERTS, D_EXPERT
    return {
        'x': nrm(ks[0], (BATCH, SEQ, D), 1.0),
        'c': nrm(ks[1], (BATCH, D), 1.0),
        'w_ada': nrm(ks[2], (DEPTH, D, N_MOD * D), 0.5 * D ** -0.5),
        'b_ada': nrm(ks[3], (DEPTH, N_MOD * D), 0.02),
        'norm1_g': 1.0 + nrm(ks[4], (DEPTH, D), 0.05),
        'w_in': nrm(ks[5], (DEPTH, D, IN_W), D ** -0.5),
        'q_norm_g': 1.0 + nrm(ks[6], (DEPTH, ATT_HEAD_DIM), 0.05),
        'k_norm_g': 1.0 + nrm(ks[7], (DEPTH, ATT_HEAD_DIM), 0.05),
        'rel_bias': nrm(ks[8], (DEPTH, ATT_HEADS, REL_TABLE), 0.5),
        'w_ret_out': nrm(ks[9], (DEPTH, RET_V_W, D), RET_V_W ** -0.5),
        'w_att_out': nrm(ks[10], (DEPTH, ATT_W, D), ATT_W ** -0.5),
        'w_out': nrm(ks[11], (DEPTH, D, D), D ** -0.5),
        'norm2_g': 1.0 + nrm(ks[12], (DEPTH, D), 0.05),
        'w_router': nrm(ks[13], (DEPTH, D, E), D ** -0.5),
        'b_router': nrm(ks[14], (DEPTH, E), 0.01),
        'w1': nrm(ks[15], (DEPTH, E, D, 2 * F), D ** -0.5),
        'b1': nrm(ks[16], (DEPTH, E, 2 * F), 0.02),
        'w2': nrm(ks[17], (DEPTH, E, F, D), F ** -0.5),
        'b2': nrm(ks[18], (DEPTH, E, D), 0.02),
    }


def reference(x, c, w_ada, b_ada, norm1_g, w_in, q_norm_g, k_norm_g, rel_bias, w_ret_out,
              w_att_out, w_out, norm2_g, w_router, b_router, w1, b1, w2, b2):
    b, s, d = x.shape
    c_act = jax.nn.silu(c)
    for layer in range(DEPTH):
        mod = (c_act @ w_ada[layer] + b_ada[layer]).reshape(b, N_MOD, d)[:, :, None, :]
        shift1, scale1, gate1, shift2, scale2, gate2 = [mod[:, i] for i in range(N_MOD)]
        h = _rms_norm(x, norm1_g[layer]) * (1.0 + scale1) + shift1
        rq, rk, rv, rg, aq, ak, av, gr, ga = _split_cols(h @ w_in[layer], IN_SPLITS)
        y_ret = _retention(rq, rk, rv, rg) @ w_ret_out[layer]
        y_att = _chunk_attention(aq, ak, av, q_norm_g[layer], k_norm_g[layer], rel_bias[layer]) @ w_att_out[layer]
        merged = jax.nn.sigmoid(gr) * y_ret + jax.nn.sigmoid(ga) * y_att
        x = x + gate1 * (merged @ w_out[layer])
        h = _rms_norm(x, norm2_g[layer]) * (1.0 + scale2) + shift2
        y = _expert_ffn(h.reshape(b * s, d), w_router[layer], b_router[layer], w1, b1, w2, b2, layer)
        x = x + gate2 * y.reshape(b, s, d)
    return x
```

```python
import functools

import numpy as np
import jax
import jax.numpy as jnp
from jax import lax
from jax.experimental import pallas as pl
from jax.experimental.pallas import tpu as pltpu

F32 = jnp.float32
BF16 = jnp.bfloat16
U32 = jnp.uint32

CHUNK = 64
NORM_EPS = 1e-6
RET_HEADS = 8
RET_QK_DIM = 256
RET_V_DIM = 512
ROPE_BASE = 10000.0
ATT_HEADS = 16
ATT_HEAD_DIM = 128
PREV_CHUNKS = 8
MAX_REL = 128
N_EXPERTS = 32
TOP_K = 4
SWIGLU_ALPHA = 1.702
SWIGLU_LIMIT = 7.0
N_MOD = 6

RET_QK_W = RET_HEADS * RET_QK_DIM
RET_V_W = RET_HEADS * RET_V_DIM
ATT_W = ATT_HEADS * ATT_HEAD_DIM

V7X_VMEM_LIMIT_BYTES = 58 * 1024 * 1024
HI16 = 0xFFFF0000

ADA_TN = 1024
NORM_TM = 256
PROJ_TM, PROJ_TN = 1024, 512
RET_BLOCK = 256
ATT_BLOCK = 256
MERGE_TM, MERGE_TN = 512, 512
OUT_TM, OUT_TN = 1024, 512
EXPERT_BM = 512
EXPERT_TF = 256
COMBINE_TM = 256


def _params(semantics):
    return pltpu.CompilerParams(dimension_semantics=semantics, vmem_limit_bytes=V7X_VMEM_LIMIT_BYTES)


def _silu(t):
    return t * jax.nn.sigmoid(t)


def _pack_halves(y):
    n = y.shape[1] // 2
    hi = lax.bitcast_convert_type(y[:, :n].astype(BF16).astype(F32), U32)
    lo = lax.bitcast_convert_type(y[:, n:].astype(BF16).astype(F32), U32)
    return hi | (lo >> 16)


def _unpack_halves(p):
    a = lax.bitcast_convert_type(p & jnp.uint32(HI16), F32)
    b = lax.bitcast_convert_type(p << 16, F32)
    return a, b


def _ada_kernel(c_ref, w_ref, b_ref, o_ref):
    a = _silu(c_ref[...]).astype(BF16)
    o_ref[...] = jnp.dot(a, w_ref[...].astype(BF16), preferred_element_type=F32) + b_ref[...]


def _ada_ln(c8, w_ada, b_ada, layer):
    d = c8.shape[1]
    n = w_ada.shape[2]
    return pl.pallas_call(
        _ada_kernel,
        out_shape=jax.ShapeDtypeStruct((8, n), F32),
        grid=(n // ADA_TN,),
        in_specs=[
            pl.BlockSpec((8, d), lambda j: (0, 0)),
            pl.BlockSpec((None, d, ADA_TN), lambda j: (layer, 0, j)),
            pl.BlockSpec((None, 1, ADA_TN), lambda j: (layer, 0, j)),
        ],
        out_specs=pl.BlockSpec((8, ADA_TN), lambda j: (0, j)),
        compiler_params=_params(("arbitrary",)),
        name="ada_ln",
    )(c8, w_ada, b_ada.reshape(b_ada.shape[0], 1, n))


def _modulated_norm(x, g, scale, shift):
    xn = x * lax.rsqrt(jnp.mean(x * x, axis=-1, keepdims=True) + NORM_EPS)
    return (xn * g) * (1.0 + scale) + shift


def _norm_mod_kernel(x_ref, g_ref, sc_ref, sh_ref, o_ref):
    o_ref[...] = _modulated_norm(x_ref[...], g_ref[...], sc_ref[...], sh_ref[...]).astype(o_ref.dtype)


def _norm_mod(x2, g, mod3, layer, seq, shift_idx, scale_idx):
    t, d = x2.shape
    tm = NORM_TM

    def mod_spec(idx):
        return pl.BlockSpec((None, 1, d), lambda i: ((i * tm // seq) * N_MOD + idx, 0, 0))

    return pl.pallas_call(
        _norm_mod_kernel,
        out_shape=jax.ShapeDtypeStruct((t, d), BF16),
        grid=(t // tm,),
        in_specs=[
            pl.BlockSpec((tm, d), lambda i: (i, 0)),
            pl.BlockSpec((None, 1, d), lambda i: (layer, 0, 0)),
            mod_spec(scale_idx),
            mod_spec(shift_idx),
        ],
        out_specs=pl.BlockSpec((tm, d), lambda i: (i, 0)),
        compiler_params=_params(("arbitrary",)),
        name="norm_mod",
    )(x2, g, mod3, mod3)


def _norm_router_kernel(x_ref, g_ref, sc_ref, sh_ref, wr_ref, br_ref, h_ref, lg_ref):
    h = _modulated_norm(x_ref[...], g_ref[...], sc_ref[...], sh_ref[...])
    h_ref[...] = _pack_halves(h)
    h_hi = h.astype(BF16)
    h_lo = (h - h_hi.astype(F32)).astype(BF16)
    w = wr_ref[...]
    w_hi = w.astype(BF16)
    w_lo = (w - w_hi.astype(F32)).astype(BF16)
    lg = (jnp.dot(h_hi, w_hi, preferred_element_type=F32)
          + jnp.dot(h_lo, w_hi, preferred_element_type=F32)
          + jnp.dot(h_hi, w_lo, preferred_element_type=F32))
    lg_ref[...] = lg + br_ref[...]


def _norm_router(x2, g, mod3, w_router, b_router, layer, seq, shift_idx, scale_idx):
    t, d = x2.shape
    e = w_router.shape[2]
    tm = NORM_TM

    def mod_spec(idx):
        return pl.BlockSpec((None, 1, d), lambda i: ((i * tm // seq) * N_MOD + idx, 0, 0))

    return pl.pallas_call(
        _norm_router_kernel,
        out_shape=(jax.ShapeDtypeStruct((t, d // 2), U32), jax.ShapeDtypeStruct((t, e), F32)),
        grid=(t // tm,),
        in_specs=[
            pl.BlockSpec((tm, d), lambda i: (i, 0)),
            pl.BlockSpec((None, 1, d), lambda i: (layer, 0, 0)),
            mod_spec(scale_idx),
            mod_spec(shift_idx),
            pl.BlockSpec((None, d, e), lambda i: (layer, 0, 0)),
            pl.BlockSpec((None, 1, e), lambda i: (layer, 0, 0)),
        ],
        out_specs=(pl.BlockSpec((tm, d // 2), lambda i: (i, 0)), pl.BlockSpec((tm, e), lambda i: (i, 0))),
        compiler_params=_params(("arbitrary",)),
        name="norm_router",
    )(x2, g, mod3, mod3, w_router, b_router.reshape(b_router.shape[0], 1, e))


def _proj_kernel(a_ref, w_ref, o_ref):
    o_ref[...] = jnp.dot(a_ref[...], w_ref[...].astype(BF16), preferred_element_type=F32).astype(o_ref.dtype)


def _project(a, w, layer):
    t, k = a.shape
    n = w.shape[2]
    tm, tn = PROJ_TM, PROJ_TN
    return pl.pallas_call(
        _proj_kernel,
        out_shape=jax.ShapeDtypeStruct((t, n), BF16),
        grid=(n // tn, t // tm),
        in_specs=[
            pl.BlockSpec((tm, k), lambda j, i: (i, 0)),
            pl.BlockSpec((None, k, tn), lambda j, i: (layer, 0, j)),
        ],
        out_specs=pl.BlockSpec((tm, tn), lambda j, i: (i, j)),
        compiler_params=_params(("arbitrary", "arbitrary")),
        name="in_proj",
    )(a, w)


def _retention_kernel(q_ref, k_ref, v_ref, g_ref, cos_ref, sin_ref, dec_ref, qd_ref, kd_ref, cd_ref,
                      o_ref, state_ref):
    @pl.when(pl.program_id(2) == 0)
    def _():
        state_ref[...] = jnp.zeros_like(state_ref)

    cos = cos_ref[...]
    sin = sin_ref[...]
    half = RET_QK_DIM // 2

    def rotary(t):
        t1, t2 = t[:, :half], t[:, half:]
        return jnp.concatenate([t1 * cos - t2 * sin, t1 * sin + t2 * cos], axis=1)

    q = rotary(q_ref[...].astype(F32))
    k = rotary(k_ref[...].astype(F32)) * (RET_QK_DIM ** -0.5)
    v = v_ref[...]
    contract_last = (((1,), (1,)), ((), ()))
    contract_rows = (((0,), (0,)), ((), ()))
    scores = lax.dot_general(q.astype(BF16), k.astype(BF16), contract_last,
                             preferred_element_type=F32) * dec_ref[...]
    state = state_ref[...]
    out = (jnp.dot(scores.astype(BF16), v, preferred_element_type=F32)
           + jnp.dot((q * qd_ref[...]).astype(BF16), state.astype(BF16), preferred_element_type=F32))
    state_ref[...] = state * cd_ref[...] + lax.dot_general(
        (k * kd_ref[...]).astype(BF16), v, contract_rows, preferred_element_type=F32)
    on = out * lax.rsqrt(jnp.mean(out * out, axis=-1, keepdims=True) + NORM_EPS)
    o_ref[...] = (_silu(g_ref[...].astype(F32)) * on).astype(o_ref.dtype)


def _retention_tables(seq):
    half = RET_QK_DIM // 2
    inv_freq = ROPE_BASE ** (-jnp.arange(half, dtype=F32) / half)
    ang = jnp.arange(seq, dtype=F32)[:, None] * inv_freq[None, :]
    log_gamma = jnp.log(1.0 - 2.0 ** (-5.0 - jnp.arange(RET_HEADS, dtype=F32)))[:, None, None]
    pos = jnp.arange(RET_BLOCK, dtype=F32)
    n, m = pos[:, None], pos[None, :]
    cn, cm = jnp.floor(n / CHUNK), jnp.floor(m / CHUNK)
    dist = jnp.where(cn == cm, jnp.abs(n - m), n - m)
    dec = jnp.where((cm <= cn)[None], jnp.exp(log_gamma * dist[None]), 0.0)
    qd = jnp.broadcast_to(jnp.exp(log_gamma * (pos[None, :, None] + 1.0)), (RET_HEADS, RET_BLOCK, RET_QK_DIM))
    kd = jnp.broadcast_to(jnp.exp(log_gamma * (RET_BLOCK - 1.0 - pos[None, :, None])),
                          (RET_HEADS, RET_BLOCK, RET_QK_DIM))
    cd = jnp.broadcast_to(jnp.exp(log_gamma * RET_BLOCK), (RET_HEADS, 1, RET_V_DIM))
    return jnp.cos(ang), jnp.sin(ang), dec, qd, kd, cd


def _retention(proj, batch, seq, col0):
    t = proj.shape[0]
    r = RET_BLOCK
    nsb = seq // r
    cos, sin, dec, qd, kd, cd = _retention_tables(seq)
    qk_blk0 = col0 // RET_QK_DIM
    v_blk0 = (col0 + 2 * RET_QK_W) // RET_V_DIM
    row = lambda b, h, s: b * nsb + s
    head_tab = lambda shape: pl.BlockSpec((None,) + shape, lambda b, h, s: (h, 0, 0))
    half = RET_QK_DIM // 2
    return pl.pallas_call(
        _retention_kernel,
        out_shape=jax.ShapeDtypeStruct((t, RET_V_W), BF16),
        grid=(batch, RET_HEADS, nsb),
        in_specs=[
            pl.BlockSpec((r, RET_QK_DIM), lambda b, h, s: (row(b, h, s), qk_blk0 + h)),
            pl.BlockSpec((r, RET_QK_DIM), lambda b, h, s: (row(b, h, s), qk_blk0 + RET_HEADS + h)),
            pl.BlockSpec((r, RET_V_DIM), lambda b, h, s: (row(b, h, s), v_blk0 + h)),
            pl.BlockSpec((r, RET_V_DIM), lambda b, h, s: (row(b, h, s), v_blk0 + RET_HEADS + h)),
            pl.BlockSpec((r, half), lambda b, h, s: (s, 0)),
            pl.BlockSpec((r, half), lambda b, h, s: (s, 0)),
            head_tab((r, r)),
            head_tab((r, RET_QK_DIM)),
            head_tab((r, RET_QK_DIM)),
            head_tab((1, RET_V_DIM)),
        ],
        out_specs=pl.BlockSpec((r, RET_V_DIM), lambda b, h, s: (row(b, h, s), h)),
        scratch_shapes=[pltpu.VMEM((RET_QK_DIM, RET_V_DIM), F32)],
        compiler_params=_params(("arbitrary", "arbitrary", "arbitrary")),
        name="retention",
    )(proj, proj, proj, proj, cos, sin, dec, qd, kd, cd)


ATT_KEY_BLOCKS = PREV_CHUNKS * CHUNK // ATT_BLOCK + 1
NEG = float(np.finfo(np.float32).min)


def _attention_kernel(q_ref, k0_ref, k1_ref, k2_ref, v0_ref, v1_ref, v2_ref, tab_ref, gq_ref, gk_ref, o_ref):
    i = pl.program_id(1)

    def rms(t, g):
        tf = t.astype(F32)
        return (tf * lax.rsqrt(jnp.mean(tf * tf, axis=-1, keepdims=True) + NORM_EPS)) * g

    q = rms(q_ref[...], gq_ref[...]).astype(BF16)
    k = rms(jnp.concatenate([k0_ref[...], k1_ref[...], k2_ref[...]], axis=0), gk_ref[...]).astype(BF16)
    v = jnp.concatenate([v0_ref[...], v1_ref[...], v2_ref[...]], axis=0)
    sc = lax.dot_general(q, k, (((1,), (1,)), ((), ())), preferred_element_type=F32)
    sc = sc * (ATT_HEAD_DIM ** -0.5) + tab_ref[...]
    col = lax.broadcasted_iota(jnp.int32, sc.shape, 1)
    sc = jnp.where(col >= (ATT_KEY_BLOCKS - 1 - i) * ATT_BLOCK, sc, NEG)
    p = jnp.exp(sc - jnp.max(sc, axis=-1, keepdims=True))
    denom = jnp.sum(p, axis=-1, keepdims=True)
    o = jnp.dot(p.astype(BF16), v, preferred_element_type=F32) / denom
    o_ref[...] = o.astype(o_ref.dtype)


def _attention_table(rel_bias):
    pad = PREV_CHUNKS * CHUNK
    n = jnp.arange(ATT_BLOCK)[:, None]
    j = jnp.arange(ATT_KEY_BLOCKS * ATT_BLOCK)[None, :]
    rel = n + pad - j
    qc = n // CHUNK
    kc = j // CHUNK - PREV_CHUNKS
    valid = (kc <= qc) & (kc >= qc - PREV_CHUNKS)
    idx = jnp.clip(jnp.minimum(rel, MAX_REL) + (CHUNK - 1), 0, rel_bias.shape[-1] - 1)
    return jnp.where(valid[None], rel_bias[:, idx].astype(F32), NEG)


def _chunk_attention(proj, rel_bias, q_norm_g, k_norm_g, batch, seq, col0):
    assert ATT_KEY_BLOCKS == 3
    t = proj.shape[0]
    a = ATT_BLOCK
    nqb = seq // a
    dh = ATT_HEAD_DIM
    blk0 = col0 // dh
    tab = _attention_table(rel_bias)

    def kv_spec(which, jj):
        return pl.BlockSpec(
            (a, dh),
            lambda h, i, b: (b * nqb + jnp.maximum(i - (ATT_KEY_BLOCKS - 1) + jj, 0), blk0 + which * ATT_HEADS + h))

    return pl.pallas_call(
        _attention_kernel,
        out_shape=jax.ShapeDtypeStruct((t, ATT_W), BF16),
        grid=(ATT_HEADS, nqb, batch),
        in_specs=[
            pl.BlockSpec((a, dh), lambda h, i, b: (b * nqb + i, blk0 + h)),
            kv_spec(1, 0), kv_spec(1, 1), kv_spec(1, 2),
            kv_spec(2, 0), kv_spec(2, 1), kv_spec(2, 2),
            pl.BlockSpec((None, a, ATT_KEY_BLOCKS * a), lambda h, i, b: (h, 0, 0)),
            pl.BlockSpec((1, dh), lambda h, i, b: (0, 0)),
            pl.BlockSpec((1, dh), lambda h, i, b: (0, 0)),
        ],
        out_specs=pl.BlockSpec((a, dh), lambda h, i, b: (b * nqb + i, h)),
        compiler_params=_params(("arbitrary", "arbitrary", "arbitrary")),
        name="chunk_attention",
    )(proj, proj, proj, proj, proj, proj, proj, tab, q_norm_g.reshape(1, dh), k_norm_g.reshape(1, dh))


def _merge_kernel(r_ref, a_ref, wr_ref, wa_ref, gr_ref, ga_ref, o_ref):
    yr = jnp.dot(r_ref[...], wr_ref[...].astype(BF16), preferred_element_type=F32)
    ya = jnp.dot(a_ref[...], wa_ref[...].astype(BF16), preferred_element_type=F32)
    merged = jax.nn.sigmoid(gr_ref[...].astype(F32)) * yr + jax.nn.sigmoid(ga_ref[...].astype(F32)) * ya
    o_ref[...] = merged.astype(o_ref.dtype)


def _merge(ret, att, w_ret_out, w_att_out, proj, gate_col0, layer):
    t = ret.shape[0]
    d = w_ret_out.shape[2]
    tm, tn = MERGE_TM, MERGE_TN
    gblk = gate_col0 // tn
    return pl.pallas_call(
        _merge_kernel,
        out_shape=jax.ShapeDtypeStruct((t, d), BF16),
        grid=(d // tn, t // tm),
        in_specs=[
            pl.BlockSpec((tm, ret.shape[1]), lambda j, i: (i, 0)),
            pl.BlockSpec((tm, att.shape[1]), lambda j, i: (i, 0)),
            pl.BlockSpec((None, ret.shape[1], tn), lambda j, i: (layer, 0, j)),
            pl.BlockSpec((None, att.shape[1], tn), lambda j, i: (layer, 0, j)),
            pl.BlockSpec((tm, tn), lambda j, i: (i, gblk + j)),
            pl.BlockSpec((tm, tn), lambda j, i: (i, gblk + d // tn + j)),
        ],
        out_specs=pl.BlockSpec((tm, tn), lambda j, i: (i, j)),
        compiler_params=_params(("arbitrary", "arbitrary")),
        name="merge",
    )(ret, att, w_ret_out, w_att_out, proj, proj)


def _out_proj_kernel(a_ref, w_ref, x_ref, gate_ref, o_ref):
    acc = jnp.dot(a_ref[...], w_ref[...].astype(BF16), preferred_element_type=F32)
    o_ref[...] = x_ref[...] + gate_ref[...] * acc


def _out_proj(a, w, x2, mod3, layer, seq, gate_idx):
    t, k = a.shape
    d = w.shape[2]
    tm, tn = OUT_TM, OUT_TN
    return pl.pallas_call(
        _out_proj_kernel,
        out_shape=jax.ShapeDtypeStruct((t, d), F32),
        grid=(d // tn, t // tm),
        in_specs=[
            pl.BlockSpec((tm, k), lambda j, i: (i, 0)),
            pl.BlockSpec((None, k, tn), lambda j, i: (layer, 0, j)),
            pl.BlockSpec((tm, tn), lambda j, i: (i, j)),
            pl.BlockSpec((None, 1, tn), lambda j, i: ((i * tm // seq) * N_MOD + gate_idx, 0, j)),
        ],
        out_specs=pl.BlockSpec((tm, tn), lambda j, i: (i, j)),
        compiler_params=_params(("arbitrary", "arbitrary")),
        name="out_proj",
    )(a, w, x2, mod3)


def _routing(logits):
    t = logits.shape[0]
    bm = EXPERT_BM
    top_val, top_idx = lax.top_k(logits, TOP_K)
    gate = jax.nn.softmax(top_val, axis=-1)
    a = t * TOP_K
    e_flat = top_idx.reshape(a)
    order = jnp.argsort(e_flat)
    e_sorted = e_flat[order]
    sizes = jnp.bincount(e_flat, length=N_EXPERTS)
    padded = (sizes + bm - 1) // bm * bm
    pad_end = jnp.cumsum(padded)
    pad_start = pad_end - padded
    start = jnp.cumsum(sizes) - sizes
    dest_sorted = (pad_start[e_sorted] + jnp.arange(a, dtype=jnp.int32) - start[e_sorted]).astype(jnp.int32)
    n_blocks = -(-a // bm) + N_EXPERTS
    row_tok = jnp.zeros((n_blocks * bm,), jnp.int32).at[dest_sorted].set((order // TOP_K).astype(jnp.int32))
    dest = jnp.zeros((a,), jnp.int32).at[order].set(dest_sorted)
    block_row0 = jnp.arange(n_blocks, dtype=jnp.int32) * bm
    block_e = jnp.minimum(jnp.searchsorted(pad_end, block_row0, side='right'), N_EXPERTS - 1).astype(jnp.int32)
    block_valid = (block_row0 < pad_end[-1]).astype(jnp.int32)
    return gate, dest, row_tok, block_e, block_valid


def _dispatch_kernel(bv_ref, tok_ref, h_hbm, xs_hbm, sem):
    b = pl.program_id(0)
    bm = tok_ref.shape[1]

    def row_copy(src_row, dst_row):
        return pltpu.make_async_copy(h_hbm.at[pl.ds(src_row, 1)], xs_hbm.at[pl.ds(dst_row, 1)], sem)

    @pl.when(bv_ref[b] == 1)
    def _():
        def start(r, carry):
            row_copy(tok_ref[0, r], b * bm + r).start()
            return carry

        def wait(r, carry):
            row_copy(0, b * bm + r).wait()
            return carry

        lax.fori_loop(0, bm, start, 0)
        lax.fori_loop(0, bm, wait, 0)

    @pl.when(bv_ref[b] != 1)
    def _():
        fill = pltpu.make_async_copy(h_hbm.at[pl.ds(0, bm)], xs_hbm.at[pl.ds(b * bm, bm)], sem)
        fill.start()
        fill.wait()


def _dispatch(h_packed, row_tok, block_valid):
    bm = EXPERT_BM
    n_blocks = block_valid.shape[0]
    w = h_packed.shape[1]
    return pl.pallas_call(
        _dispatch_kernel,
        out_shape=jax.ShapeDtypeStruct((n_blocks * bm, w), h_packed.dtype),
        grid_spec=pltpu.PrefetchScalarGridSpec(
            num_scalar_prefetch=1,
            grid=(n_blocks,),
            in_specs=[
                pl.BlockSpec((None, 1, bm), lambda b, bv: (b, 0, 0), memory_space=pltpu.SMEM),
                pl.BlockSpec(memory_space=pl.ANY),
            ],
            out_specs=pl.BlockSpec(memory_space=pl.ANY),
            scratch_shapes=[pltpu.SemaphoreType.DMA(())],
        ),
        compiler_params=_params(("arbitrary",)),
        name="dispatch",
    )(block_valid, row_tok.reshape(n_blocks, 1, bm), h_packed)


def _expert_kernel(be_ref, bv_ref, xs_ref, w1_ref, b1_ref, w2_ref, b2_ref, perm_ref, o_ref, x_scr, acc_scr):
    b = pl.program_id(0)
    j = pl.program_id(1)
    valid = bv_ref[b] == 1
    tf = w2_ref.shape[0]
    half = xs_ref.shape[1]

    @pl.when(valid)
    def _():
        @pl.when(j == 0)
        def _():
            xa, xb = _unpack_halves(xs_ref[...])
            x_scr[:, :half] = xa.astype(BF16)
            x_scr[:, half:] = xb.astype(BF16)
            acc_scr[...] = jnp.broadcast_to(b2_ref[...], acc_scr.shape)

        hb = jnp.dot(x_scr[...], w1_ref[...].astype(BF16), preferred_element_type=F32) + b1_ref[...]
        sel = jnp.dot(hb.astype(BF16), perm_ref[...], preferred_element_type=F32)
        x_glu = jnp.minimum(sel[:, :tf], SWIGLU_LIMIT)
        x_lin = jnp.clip(sel[:, tf:], -SWIGLU_LIMIT, SWIGLU_LIMIT)
        act = x_glu * jax.nn.sigmoid(SWIGLU_ALPHA * x_glu) * (x_lin + 1.0)
        acc_scr[...] += jnp.dot(act.astype(BF16), w2_ref[...].astype(BF16), preferred_element_type=F32)

    is_last = j == pl.num_programs(1) - 1

    @pl.when(is_last & valid)
    def _():
        o_ref[...] = _pack_halves(acc_scr[...])

    @pl.when(is_last & jnp.logical_not(valid))
    def _():
        o_ref[...] = jnp.zeros_like(o_ref)


def _deinterleave_matrix(tf):
    p = np.zeros((2 * tf, 2 * tf), np.float32)
    f = np.arange(tf)
    p[2 * f, f] = 1.0
    p[2 * f + 1, tf + f] = 1.0
    return jnp.asarray(p, BF16)


def _experts(xs, block_e, block_valid, w1, b1, w2, b2, layer):
    p, half = xs.shape
    d = 2 * half
    f = w2.shape[2]
    bm, tf = EXPERT_BM, EXPERT_TF
    n_blocks = p // bm
    nj = f // tf
    last = nj - 1

    def jj(b, j, bv):
        return jnp.where(bv[b] == 1, j, last)

    return pl.pallas_call(
        _expert_kernel,
        out_shape=jax.ShapeDtypeStruct((p, half), U32),
        grid_spec=pltpu.PrefetchScalarGridSpec(
            num_scalar_prefetch=2,
            grid=(n_blocks, nj),
            in_specs=[
                pl.BlockSpec((bm, half), lambda b, j, be, bv: (jnp.where(bv[b] == 1, b, 0), 0)),
                pl.BlockSpec((None, None, d, 2 * tf), lambda b, j, be, bv: (layer, be[b], 0, jj(b, j, bv))),
                pl.BlockSpec((None, None, 1, 2 * tf), lambda b, j, be, bv: (layer, be[b], 0, jj(b, j, bv))),
                pl.BlockSpec((None, None, tf, d), lambda b, j, be, bv: (layer, be[b], jj(b, j, bv), 0)),
                pl.BlockSpec((None, None, 1, d), lambda b, j, be, bv: (layer, be[b], 0, 0)),
                pl.BlockSpec((2 * tf, 2 * tf), lambda b, j, be, bv: (0, 0)),
            ],
            out_specs=pl.BlockSpec((bm, half), lambda b, j, be, bv: (b, 0)),
            scratch_shapes=[pltpu.VMEM((bm, d), BF16), pltpu.VMEM((bm, d), F32)],
        ),
        compiler_params=_params(("arbitrary", "arbitrary")),
        name="experts",
    )(block_e, block_valid, xs, w1, b1.reshape(b1.shape[0], b1.shape[1], 1, 2 * f), w2,
      b2.reshape(b2.shape[0], b2.shape[1], 1, d), _deinterleave_matrix(tf))


def _combine_kernel(dest_ref, ys_hbm, x_ref, gate_ref, g_ref, o_ref, buf, sem):
    tm = x_ref.shape[0]
    half = buf.shape[2]

    def row_copy(src_row, k, r):
        return pltpu.make_async_copy(ys_hbm.at[pl.ds(src_row, 1)], buf.at[k, pl.ds(r, 1)], sem)

    def start(r, carry):
        for k in range(TOP_K):
            row_copy(dest_ref[0, r * TOP_K + k], k, r).start()
        return carry

    def wait(r, carry):
        for k in range(TOP_K):
            row_copy(0, k, r).wait()
        return carry

    lax.fori_loop(0, tm, start, 0)
    lax.fori_loop(0, tm, wait, 0)
    g = g_ref[...]
    ya = jnp.zeros((tm, half), F32)
    yb = jnp.zeros((tm, half), F32)
    for k in range(TOP_K):
        a, b = _unpack_halves(buf[k])
        ya = ya + g[:, k:k + 1] * a
        yb = yb + g[:, k:k + 1] * b
    o_ref[:, :half] = x_ref[:, :half] + gate_ref[:, :half] * ya
    o_ref[:, half:] = x_ref[:, half:] + gate_ref[:, half:] * yb


def _combine(ys, dest, gates, x2, mod3, seq, gate_idx):
    t, d = x2.shape
    tm = COMBINE_TM
    half = ys.shape[1]
    return pl.pallas_call(
        _combine_kernel,
        out_shape=jax.ShapeDtypeStruct((t, d), F32),
        grid=(t // tm,),
        in_specs=[
            pl.BlockSpec((None, 1, tm * TOP_K), lambda i: (i, 0, 0), memory_space=pltpu.SMEM),
            pl.BlockSpec(memory_space=pl.ANY),
            pl.BlockSpec((tm, d), lambda i: (i, 0)),
            pl.BlockSpec((None, 1, d), lambda i: ((i * tm // seq) * N_MOD + gate_idx, 0, 0)),
            pl.BlockSpec((tm, TOP_K), lambda i: (i, 0)),
        ],
        out_specs=pl.BlockSpec((tm, d), lambda i: (i, 0)),
        scratch_shapes=[pltpu.VMEM((TOP_K, tm, half), U32), pltpu.SemaphoreType.DMA(())],
        compiler_params=_params(("arbitrary",)),
        name="combine",
    )(dest.reshape(t // tm, 1, tm * TOP_K), ys, x2, mod3, gates)


def kernel(x, c, w_ada, b_ada, norm1_g, w_in, q_norm_g, k_norm_g, rel_bias, w_ret_out, w_att_out, w_out,
           norm2_g, w_router, b_router, w1, b1, w2, b2):
    batch, seq, d = x.shape
    depth = w_ada.shape[0]
    t = batch * seq
    assert batch <= 8 and seq % max(RET_BLOCK, ATT_BLOCK, PROJ_TM, OUT_TM) == 0
    x2 = x.reshape(t, d)
    c8 = jnp.zeros((8, d), F32).at[:batch].set(c)
    ret_col0 = 0
    att_col0 = 2 * RET_QK_W + 2 * RET_V_W
    gate_col0 = att_col0 + 3 * ATT_W
    for layer in range(depth):
        mod3 = _ada_ln(c8, w_ada, b_ada, layer)[:batch].reshape(batch * N_MOD, 1, d)
        h = _norm_mod(x2, norm1_g.reshape(depth, 1, d), mod3, layer, seq, shift_idx=0, scale_idx=1)
        proj = _project(h, w_in, layer)
        ret = _retention(proj, batch, seq, ret_col0)
        att = _chunk_attention(proj, rel_bias[layer], q_norm_g[layer], k_norm_g[layer], batch, seq, att_col0)
        merged = _merge(ret, att, w_ret_out, w_att_out, proj, gate_col0, layer)
        x2 = _out_proj(merged, w_out, x2, mod3, layer, seq, gate_idx=2)
        h_packed, logits = _norm_router(x2, norm2_g.reshape(depth, 1, d), mod3, w_router, b_router, layer, seq,
                                        shift_idx=3, scale_idx=4)
        gates, dest, row_tok, block_e, block_valid = _routing(logits)
        xs = _dispatch(h_packed, row_tok, block_valid)
        ys = _experts(xs, block_e, block_valid, w1, b1, w2, b2, layer)
        x2 = _combine(ys, dest, gates, x2, mod3, seq, gate_idx=5)
    return x2.reshape(batch, seq, d)
```

```python
import functools

import numpy as np
import jax
import jax.numpy as jnp
from jax import lax
from jax.experimental import pallas as pl
from jax.experimental.pallas import tpu as pltpu

F32 = jnp.float32
BF16 = jnp.bfloat16
U32 = jnp.uint32

CHUNK = 64
NORM_EPS = 1e-6
RET_HEADS = 8
RET_QK_DIM = 256
RET_V_DIM = 512
ROPE_BASE = 10000.0
ATT_HEADS = 16
ATT_HEAD_DIM = 128
PREV_CHUNKS = 8
MAX_REL = 128
N_EXPERTS = 32
TOP_K = 4
SWIGLU_ALPHA = 1.702
SWIGLU_LIMIT = 7.0
N_MOD = 6

RET_QK_W = RET_HEADS * RET_QK_DIM
RET_V_W = RET_HEADS * RET_V_DIM
ATT_W = ATT_HEADS * ATT_HEAD_DIM

V7X_VMEM_LIMIT_BYTES = 58 * 1024 * 1024
HI16 = 0xFFFF0000

ADA_TN = 1024
NORM_TM = 256
PROJ_TM, PROJ_TN = 1024, 512
RET_BLOCK = 256
ATT_BLOCK = 256
MERGE_TM, MERGE_TN = 512, 512
OUT_TM, OUT_TN = 1024, 512
EXPERT_BM = 512
EXPERT_TF = 256
COMBINE_TM = 256


def _params(semantics):
    return pltpu.CompilerParams(dimension_semantics=semantics, vmem_limit_bytes=V7X_VMEM_LIMIT_BYTES)


def _silu(t):
    return t * jax.nn.sigmoid(t)


def _pack_halves(y):
    n = y.shape[1] // 2
    hi = lax.bitcast_convert_type(y[:, :n].astype(BF16).astype(F32), U32)
    lo = lax.bitcast_convert_type(y[:, n:].astype(BF16).astype(F32), U32)
    return hi | (lo >> 16)


def _unpack_halves(p):
    a = lax.bitcast_convert_type(p & jnp.uint32(HI16), F32)
    b = lax.bitcast_convert_type(p << 16, F32)
    return a, b


def _ada_kernel(c_ref, w_ref, b_ref, o_ref):
    a = _silu(c_ref[...]).astype(BF16)
    o_ref[...] = jnp.dot(a, w_ref[...].astype(BF16), preferred_element_type=F32) + b_ref[...]


def _ada_ln(c8, w_ada, b_ada, layer):
    d = c8.shape[1]
    n = w_ada.shape[2]
    return pl.pallas_call(
        _ada_kernel,
        out_shape=jax.ShapeDtypeStruct((8, n), F32),
        grid=(n // ADA_TN,),
        in_specs=[
            pl.BlockSpec((8, d), lambda j: (0, 0)),
            pl.BlockSpec((None, d, ADA_TN), lambda j: (layer, 0, j)),
            pl.BlockSpec((None, 1, ADA_TN), lambda j: (layer, 0, j)),
        ],
        out_specs=pl.BlockSpec((8, ADA_TN), lambda j: (0, j)),
        compiler_params=_params(("arbitrary",)),
        name="ada_ln",
    )(c8, w_ada, b_ada.reshape(b_ada.shape[0], 1, n))


def _modulated_norm(x, g, scale, shift):
    xn = x * lax.rsqrt(jnp.mean(x * x, axis=-1, keepdims=True) + NORM_EPS)
    return (xn * g) * (1.0 + scale) + shift


def _norm_mod_kernel(x_ref, g_ref, sc_ref, sh_ref, o_ref):
    o_ref[...] = _modulated_norm(x_ref[...], g_ref[...], sc_ref[...], sh_ref[...]).astype(o_ref.dtype)


def _norm_mod(x2, g, mod3, layer, seq, shift_idx, scale_idx):
    t, d = x2.shape
    tm = NORM_TM

    def mod_spec(idx):
        return pl.BlockSpec((None, 1, d), lambda i: ((i * tm // seq) * N_MOD + idx, 0, 0))

    return pl.pallas_call(
        _norm_mod_kernel,
        out_shape=jax.ShapeDtypeStruct((t, d), BF16),
        grid=(t // tm,),
        in_specs=[
            pl.BlockSpec((tm, d), lambda i: (i, 0)),
            pl.BlockSpec((None, 1, d), lambda i: (layer, 0, 0)),
            mod_spec(scale_idx),
            mod_spec(shift_idx),
        ],
        out_specs=pl.BlockSpec((tm, d), lambda i: (i, 0)),
        compiler_params=_params(("arbitrary",)),
        name="norm_mod",
    )(x2, g, mod3, mod3)


def _norm_router_kernel(x_ref, g_ref, sc_ref, sh_ref, wr_ref, br_ref, h_ref, lg_ref):
    h = _modulated_norm(x_ref[...], g_ref[...], sc_ref[...], sh_ref[...])
    h_ref[...] = _pack_halves(h)
    h_hi = h.astype(BF16)
    h_lo = (h - h_hi.astype(F32)).astype(BF16)
    w = wr_ref[...]
    w_hi = w.astype(BF16)
    w_lo = (w - w_hi.astype(F32)).astype(BF16)
    lg = (jnp.dot(h_hi, w_hi, preferred_element_type=F32)
          + jnp.dot(h_lo, w_hi, preferred_element_type=F32)
          + jnp.dot(h_hi, w_lo, preferred_element_type=F32))
    lg_ref[...] = lg + br_ref[...]


def _norm_router(x2, g, mod3, w_router, b_router, layer, seq, shift_idx, scale_idx):
    t, d = x2.shape
    e = w_router.shape[2]
    tm = NORM_TM

    def mod_spec(idx):
        return pl.BlockSpec((None, 1, d), lambda i: ((i * tm // seq) * N_MOD + idx, 0, 0))

    return pl.pallas_call(
        _norm_router_kernel,
        out_shape=(jax.ShapeDtypeStruct((t, d // 2), U32), jax.ShapeDtypeStruct((t, e), F32)),
        grid=(t // tm,),
        in_specs=[
            pl.BlockSpec((tm, d), lambda i: (i, 0)),
            pl.BlockSpec((None, 1, d), lambda i: (layer, 0, 0)),
            mod_spec(scale_idx),
            mod_spec(shift_idx),
            pl.BlockSpec((None, d, e), lambda i: (layer, 0, 0)),
            pl.BlockSpec((None, 1, e), lambda i: (layer, 0, 0)),
        ],
        out_specs=(pl.BlockSpec((tm, d // 2), lambda i: (i, 0)), pl.BlockSpec((tm, e), lambda i: (i, 0))),
        compiler_params=_params(("arbitrary",)),
        name="norm_router",
    )(x2, g, mod3, mod3, w_router, b_router.reshape(b_router.shape[0], 1, e))


def _proj_kernel(a_ref, w_ref, o_ref):
    o_ref[...] = jnp.dot(a_ref[...], w_ref[...].astype(BF16), preferred_element_type=F32).astype(o_ref.dtype)


def _project(a, w, layer):
    t, k = a.shape
    n = w.shape[2]
    tm, tn = PROJ_TM, PROJ_TN
    return pl.pallas_call(
        _proj_kernel,
        out_shape=jax.ShapeDtypeStruct((t, n), BF16),
        grid=(n // tn, t // tm),
        in_specs=[
            pl.BlockSpec((tm, k), lambda j, i: (i, 0)),
            pl.BlockSpec((None, k, tn), lambda j, i: (layer, 0, j)),
        ],
        out_specs=pl.BlockSpec((tm, tn), lambda j, i: (i, j)),
        compiler_params=_params(("arbitrary", "arbitrary")),
        name="in_proj",
    )(a, w)


def _retention_kernel(q_ref, k_ref, v_ref, g_ref, cos_ref, sin_ref, dec_ref, qd_ref, kd_ref, cd_ref,
                      o_ref, state_ref):
    @pl.when(pl.program_id(2) == 0)
    def _():
        state_ref[...] = jnp.zeros_like(state_ref)

    cos = cos_ref[...]
    sin = sin_ref[...]
    half = RET_QK_DIM // 2

    def rotary(t):
        t1, t2 = t[:, :half], t[:, half:]
        return jnp.concatenate([t1 * cos - t2 * sin, t1 * sin + t2 * cos], axis=1)

    q = rotary(q_ref[...].astype(F32))
    k = rotary(k_ref[...].astype(F32)) * (RET_QK_DIM ** -0.5)
    v = v_ref[...]
    contract_last = (((1,), (1,)), ((), ()))
    contract_rows = (((0,), (0,)), ((), ()))
    scores = lax.dot_general(q.astype(BF16), k.astype(BF16), contract_last,
                             preferred_element_type=F32) * dec_ref[...]
    state = state_ref[...]
    out = (jnp.dot(scores.astype(BF16), v, preferred_element_type=F32)
           + jnp.dot((q * qd_ref[...]).astype(BF16), state.astype(BF16), preferred_element_type=F32))
    state_ref[...] = state * cd_ref[...] + lax.dot_general(
        (k * kd_ref[...]).astype(BF16), v, contract_rows, preferred_element_type=F32)
    on = out * lax.rsqrt(jnp.mean(out * out, axis=-1, keepdims=True) + NORM_EPS)
    o_ref[...] = (_silu(g_ref[...].astype(F32)) * on).astype(o_ref.dtype)


def _retention_tables(seq):
    half = RET_QK_DIM // 2
    inv_freq = ROPE_BASE ** (-np.arange(half, dtype=np.float64) / half)
    ang = np.arange(seq, dtype=np.float64)[:, None] * inv_freq[None, :]
    log_gamma = np.log(1.0 - 2.0 ** (-5.0 - np.arange(RET_HEADS, dtype=np.float64)))[:, None, None]
    pos = np.arange(RET_BLOCK, dtype=np.float64)
    n, m = pos[:, None], pos[None, :]
    cn, cm = np.floor(n / CHUNK), np.floor(m / CHUNK)
    dist = np.where(cn == cm, np.abs(n - m), n - m)
    dec = np.where((cm <= cn)[None], np.exp(log_gamma * dist[None]), 0.0)
    qd = np.broadcast_to(np.exp(log_gamma * (pos[None, :, None] + 1.0)), (RET_HEADS, RET_BLOCK, RET_QK_DIM))
    kd = np.broadcast_to(np.exp(log_gamma * (RET_BLOCK - 1.0 - pos[None, :, None])),
                         (RET_HEADS, RET_BLOCK, RET_QK_DIM))
    cd = np.broadcast_to(np.exp(log_gamma * RET_BLOCK), (RET_HEADS, 1, RET_V_DIM))
    return tuple(jnp.asarray(np.ascontiguousarray(v), F32) for v in (np.cos(ang), np.sin(ang), dec, qd, kd, cd))


def _retention(proj, batch, seq, col0):
    t = proj.shape[0]
    r = RET_BLOCK
    nsb = seq // r
    cos, sin, dec, qd, kd, cd = _retention_tables(seq)
    qk_blk0 = col0 // RET_QK_DIM
    v_blk0 = (col0 + 2 * RET_QK_W) // RET_V_DIM
    row = lambda b, h, s: b * nsb + s
    head_tab = lambda shape: pl.BlockSpec((None,) + shape, lambda b, h, s: (h, 0, 0))
    half = RET_QK_DIM // 2
    return pl.pallas_call(
        _retention_kernel,
        out_shape=jax.ShapeDtypeStruct((t, RET_V_W), BF16),
        grid=(batch, RET_HEADS, nsb),
        in_specs=[
            pl.BlockSpec((r, RET_QK_DIM), lambda b, h, s: (row(b, h, s), qk_blk0 + h)),
            pl.BlockSpec((r, RET_QK_DIM), lambda b, h, s: (row(b, h, s), qk_blk0 + RET_HEADS + h)),
            pl.BlockSpec((r, RET_V_DIM), lambda b, h, s: (row(b, h, s), v_blk0 + h)),
            pl.BlockSpec((r, RET_V_DIM), lambda b, h, s: (row(b, h, s), v_blk0 + RET_HEADS + h)),
            pl.BlockSpec((r, half), lambda b, h, s: (s, 0)),
            pl.BlockSpec((r, half), lambda b, h, s: (s, 0)),
            head_tab((r, r)),
            head_tab((r, RET_QK_DIM)),
            head_tab((r, RET_QK_DIM)),
            head_tab((1, RET_V_DIM)),
        ],
        out_specs=pl.BlockSpec((r, RET_V_DIM), lambda b, h, s: (row(b, h, s), h)),
        scratch_shapes=[pltpu.VMEM((RET_QK_DIM, RET_V_DIM), F32)],
        compiler_params=_params(("arbitrary", "arbitrary", "arbitrary")),
        name="retention",
    )(proj, proj, proj, proj, cos, sin, dec, qd, kd, cd)


ATT_KEY_BLOCKS = PREV_CHUNKS * CHUNK // ATT_BLOCK + 1
NEG = float(np.finfo(np.float32).min)


def _attention_kernel(q_ref, k0_ref, k1_ref, k2_ref, v0_ref, v1_ref, v2_ref, tab_ref, gq_ref, gk_ref, o_ref):
    i = pl.program_id(1)

    def rms(t, g):
        tf = t.astype(F32)
        return (tf * lax.rsqrt(jnp.mean(tf * tf, axis=-1, keepdims=True) + NORM_EPS)) * g

    q = rms(q_ref[...], gq_ref[...]).astype(BF16)
    k = rms(jnp.concatenate([k0_ref[...], k1_ref[...], k2_ref[...]], axis=0), gk_ref[...]).astype(BF16)
    v = jnp.concatenate([v0_ref[...], v1_ref[...], v2_ref[...]], axis=0)
    sc = lax.dot_general(q, k, (((1,), (1,)), ((), ())), preferred_element_type=F32)
    sc = sc * (ATT_HEAD_DIM ** -0.5) + tab_ref[...]
    col = lax.broadcasted_iota(jnp.int32, sc.shape, 1)
    sc = jnp.where(col >= (ATT_KEY_BLOCKS - 1 - i) * ATT_BLOCK, sc, NEG)
    p = jnp.exp(sc - jnp.max(sc, axis=-1, keepdims=True))
    denom = jnp.sum(p, axis=-1, keepdims=True)
    o = jnp.dot(p.astype(BF16), v, preferred_element_type=F32) / denom
    o_ref[...] = o.astype(o_ref.dtype)


def _attention_table(rel_bias):
    a = ATT_BLOCK
    w = ATT_KEY_BLOCKS * a
    pad = PREV_CHUNKS * CHUNK
    heads = rel_bias.shape[0]
    period = 1024
    assert a - 1 + w - 1 <= period - 2
    rel_of_i = pad + a - 1 - np.arange(period)
    idx = np.clip(np.minimum(rel_of_i, MAX_REL) + (CHUNK - 1), 0, rel_bias.shape[-1] - 1)
    u = rel_bias[:, idx].astype(F32)
    skew = jnp.tile(u, (1, a))[:, :a * (period - 1)].reshape(heads, a, period - 1)
    bias = skew[:, :, a - 1:a - 1 + w]
    n = np.arange(a)[:, None]
    j = np.arange(w)[None, :]
    qc = n // CHUNK
    kc = j // CHUNK - PREV_CHUNKS
    valid = (kc <= qc) & (kc >= qc - PREV_CHUNKS)
    return jnp.where(jnp.asarray(valid)[None], bias, NEG)


def _chunk_attention(proj, rel_bias, q_norm_g, k_norm_g, batch, seq, col0):
    assert ATT_KEY_BLOCKS == 3
    t = proj.shape[0]
    a = ATT_BLOCK
    nqb = seq // a
    dh = ATT_HEAD_DIM
    blk0 = col0 // dh
    tab = _attention_table(rel_bias)

    def kv_spec(which, jj):
        return pl.BlockSpec(
            (a, dh),
            lambda h, i, b: (b * nqb + jnp.maximum(i - (ATT_KEY_BLOCKS - 1) + jj, 0), blk0 + which * ATT_HEADS + h))

    return pl.pallas_call(
        _attention_kernel,
        out_shape=jax.ShapeDtypeStruct((t, ATT_W), BF16),
        grid=(ATT_HEADS, nqb, batch),
        in_specs=[
            pl.BlockSpec((a, dh), lambda h, i, b: (b * nqb + i, blk0 + h)),
            kv_spec(1, 0), kv_spec(1, 1), kv_spec(1, 2),
            kv_spec(2, 0), kv_spec(2, 1), kv_spec(2, 2),
            pl.BlockSpec((None, a, ATT_KEY_BLOCKS * a), lambda h, i, b: (h, 0, 0)),
            pl.BlockSpec((1, dh), lambda h, i, b: (0, 0)),
            pl.BlockSpec((1, dh), lambda h, i, b: (0, 0)),
        ],
        out_specs=pl.BlockSpec((a, dh), lambda h, i, b: (b * nqb + i, h)),
        compiler_params=_params(("arbitrary", "arbitrary", "arbitrary")),
        name="chunk_attention",
    )(proj, proj, proj, proj, proj, proj, proj, tab, q_norm_g.reshape(1, dh), k_norm_g.reshape(1, dh))


def _merge_kernel(r_ref, a_ref, wr_ref, wa_ref, gr_ref, ga_ref, o_ref):
    yr = jnp.dot(r_ref[...], wr_ref[...].astype(BF16), preferred_element_type=F32)
    ya = jnp.dot(a_ref[...], wa_ref[...].astype(BF16), preferred_element_type=F32)
    merged = jax.nn.sigmoid(gr_ref[...].astype(F32)) * yr + jax.nn.sigmoid(ga_ref[...].astype(F32)) * ya
    o_ref[...] = merged.astype(o_ref.dtype)


def _merge(ret, att, w_ret_out, w_att_out, proj, gate_col0, layer):
    t = ret.shape[0]
    d = w_ret_out.shape[2]
    tm, tn = MERGE_TM, MERGE_TN
    gblk = gate_col0 // tn
    return pl.pallas_call(
        _merge_kernel,
        out_shape=jax.ShapeDtypeStruct((t, d), BF16),
        grid=(d // tn, t // tm),
        in_specs=[
            pl.BlockSpec((tm, ret.shape[1]), lambda j, i: (i, 0)),
            pl.BlockSpec((tm, att.shape[1]), lambda j, i: (i, 0)),
            pl.BlockSpec((None, ret.shape[1], tn), lambda j, i: (layer, 0, j)),
            pl.BlockSpec((None, att.shape[1], tn), lambda j, i: (layer, 0, j)),
            pl.BlockSpec((tm, tn), lambda j, i: (i, gblk + j)),
            pl.BlockSpec((tm, tn), lambda j, i: (i, gblk + d // tn + j)),
        ],
        out_specs=pl.BlockSpec((tm, tn), lambda j, i: (i, j)),
        compiler_params=_params(("arbitrary", "arbitrary")),
        name="merge",
    )(ret, att, w_ret_out, w_att_out, proj, proj)


def _out_proj_kernel(a_ref, w_ref, x_ref, gate_ref, o_ref):
    acc = jnp.dot(a_ref[...], w_ref[...].astype(BF16), preferred_element_type=F32)
    o_ref[...] = x_ref[...] + gate_ref[...] * acc


def _out_proj(a, w, x2, mod3, layer, seq, gate_idx):
    t, k = a.shape
    d = w.shape[2]
    tm, tn = OUT_TM, OUT_TN
    return pl.pallas_call(
        _out_proj_kernel,
        out_shape=jax.ShapeDtypeStruct((t, d), F32),
        grid=(d // tn, t // tm),
        in_specs=[
            pl.BlockSpec((tm, k), lambda j, i: (i, 0)),
            pl.BlockSpec((None, k, tn), lambda j, i: (layer, 0, j)),
            pl.BlockSpec((tm, tn), lambda j, i: (i, j)),
            pl.BlockSpec((None, 1, tn), lambda j, i: ((i * tm // seq) * N_MOD + gate_idx, 0, j)),
        ],
        out_specs=pl.BlockSpec((tm, tn), lambda j, i: (i, j)),
        compiler_params=_params(("arbitrary", "arbitrary")),
        name="out_proj",
    )(a, w, x2, mod3)


def _routing(logits):
    t = logits.shape[0]
    bm = EXPERT_BM
    a = t * TOP_K
    top_val, top_idx = lax.top_k(logits, TOP_K)
    gate = jax.nn.softmax(top_val, axis=-1)
    onehot = top_idx[:, :, None] == jnp.arange(N_EXPERTS, dtype=jnp.int32)[None, None, :]
    picked = jnp.any(onehot, axis=1).astype(jnp.int32)
    inclusive = jnp.cumsum(picked, axis=0)
    sizes = inclusive[-1]
    padded = (sizes + bm - 1) // bm * bm
    pad_end = jnp.cumsum(padded)
    pad_start = pad_end - padded
    start = jnp.cumsum(sizes) - sizes
    slot = (pad_start[None, :] + inclusive - picked)[:, None, :]
    dest = jnp.sum(jnp.where(onehot, slot, 0), axis=-1).astype(jnp.int32).reshape(a)
    order = jnp.argsort(top_idx.reshape(a))
    n_blocks = -(-a // bm) + N_EXPERTS
    block_row0 = jnp.arange(n_blocks, dtype=jnp.int32) * bm
    block_e = jnp.minimum(jnp.searchsorted(pad_end, block_row0, side='right'), N_EXPERTS - 1).astype(jnp.int32)
    block_valid = (block_row0 < pad_end[-1]).astype(jnp.int32)
    row = jnp.arange(n_blocks * bm, dtype=jnp.int32)
    row_e = jnp.repeat(block_e, bm)
    offset = row - pad_start[row_e]
    real = (offset < sizes[row_e]) & (jnp.repeat(block_valid, bm) == 1)
    sorted_pos = jnp.clip(start[row_e] + offset, 0, a - 1)
    row_tok = jnp.where(real, order[sorted_pos] // TOP_K, 0).astype(jnp.int32)
    return gate, dest, row_tok, block_e, block_valid


def _expert_kernel(be_ref, bv_ref, tok0_ref, tokn_ref, h_hbm, w1_ref, b1_ref, w2_ref, b2_ref, perm_ref, o_ref,
                   gbuf, x_scr, acc_scr, sems, *, per_step):
    b = pl.program_id(0)
    j = pl.program_id(1)
    nb = pl.num_programs(0)
    nj = pl.num_programs(1)
    valid = bv_ref[b] == 1
    tf = w2_ref.shape[0]
    bm, d = x_scr.shape
    half = d // 2
    slot = lax.rem(b, 2)

    def row_copy(tok_ref, r, dst_slot):
        src = tok_ref[0, jnp.minimum(r, bm - 1)]
        return pltpu.make_async_copy(h_hbm.at[pl.ds(src, 1)], gbuf.at[dst_slot, pl.ds(r, 1)], sems.at[dst_slot])

    def wait_slot(s):
        pltpu.make_async_copy(h_hbm.at[pl.ds(0, gbuf.shape[1])], gbuf.at[s], sems.at[s]).wait()

    @pl.when((j == 0) & (b == 0))
    def _():
        def start(r, carry):
            row_copy(tok0_ref, r, 0).start()
            return carry
        lax.fori_loop(0, gbuf.shape[1], start, 0)

    @pl.when((j == 0) & ((b == 0) | (bv_ref[jnp.maximum(b - 1, 0)] == 1)))
    def _():
        wait_slot(slot)

    @pl.when(valid)
    def _():
        @pl.when(j == 0)
        def _():
            xa, xb = _unpack_halves(gbuf[slot, :bm, :])
            x_scr[:, :half] = xa.astype(BF16)
            x_scr[:, half:] = xb.astype(BF16)
            acc_scr[...] = jnp.broadcast_to(b2_ref[...], acc_scr.shape)

        for i in range(per_step):
            row_copy(tokn_ref, j * per_step + i, 1 - slot).start()

        hb = jnp.dot(x_scr[...], w1_ref[...].astype(BF16), preferred_element_type=F32) + b1_ref[...]
        sel = jnp.dot(hb.astype(BF16), perm_ref[...], preferred_element_type=F32)
        x_glu = jnp.minimum(sel[:, :tf], SWIGLU_LIMIT)
        x_lin = jnp.clip(sel[:, tf:], -SWIGLU_LIMIT, SWIGLU_LIMIT)
        act = x_glu * jax.nn.sigmoid(SWIGLU_ALPHA * x_glu) * (x_lin + 1.0)
        acc_scr[...] += jnp.dot(act.astype(BF16), w2_ref[...].astype(BF16), preferred_element_type=F32)

    is_last = j == nj - 1

    @pl.when(is_last & valid)
    def _():
        o_ref[...] = _pack_halves(acc_scr[...])

    @pl.when(is_last & jnp.logical_not(valid))
    def _():
        o_ref[...] = jnp.zeros_like(o_ref)

    @pl.when(is_last & valid & (b == nb - 1))
    def _():
        wait_slot(1 - slot)


def _deinterleave_matrix(tf):
    p = np.zeros((2 * tf, 2 * tf), np.float32)
    f = np.arange(tf)
    p[2 * f, f] = 1.0
    p[2 * f + 1, tf + f] = 1.0
    return jnp.asarray(p, BF16)


def _experts(h_packed, row_tok, block_e, block_valid, w1, b1, w2, b2, layer):
    half = h_packed.shape[1]
    d = 2 * half
    f = w2.shape[2]
    bm, tf = EXPERT_BM, EXPERT_TF
    n_blocks = block_e.shape[0]
    p = n_blocks * bm
    nj = f // tf
    last = nj - 1
    sublanes = 8
    per_step = -(-bm // (nj * sublanes)) * sublanes
    assert h_packed.shape[0] >= nj * per_step

    def jj(b, j, bv):
        return jnp.where(bv[b] == 1, j, last)

    tok3 = row_tok.reshape(n_blocks, 1, bm)
    return pl.pallas_call(
        functools.partial(_expert_kernel, per_step=per_step),
        out_shape=jax.ShapeDtypeStruct((p, half), U32),
        grid_spec=pltpu.PrefetchScalarGridSpec(
            num_scalar_prefetch=2,
            grid=(n_blocks, nj),
            in_specs=[
                pl.BlockSpec((None, 1, bm), lambda b, j, be, bv: (0, 0, 0), memory_space=pltpu.SMEM),
                pl.BlockSpec((None, 1, bm), lambda b, j, be, bv: (jnp.minimum(b + 1, n_blocks - 1), 0, 0),
                             memory_space=pltpu.SMEM),
                pl.BlockSpec(memory_space=pl.ANY),
                pl.BlockSpec((None, None, d, 2 * tf), lambda b, j, be, bv: (layer, be[b], 0, jj(b, j, bv))),
                pl.BlockSpec((None, None, 1, 2 * tf), lambda b, j, be, bv: (layer, be[b], 0, jj(b, j, bv))),
                pl.BlockSpec((None, None, tf, d), lambda b, j, be, bv: (layer, be[b], jj(b, j, bv), 0)),
                pl.BlockSpec((None, None, 1, d), lambda b, j, be, bv: (layer, be[b], 0, 0)),
                pl.BlockSpec((2 * tf, 2 * tf), lambda b, j, be, bv: (0, 0)),
            ],
            out_specs=pl.BlockSpec((bm, half), lambda b, j, be, bv: (b, 0)),
            scratch_shapes=[
                pltpu.VMEM((2, nj * per_step, half), U32),
                pltpu.VMEM((bm, d), BF16),
                pltpu.VMEM((bm, d), F32),
                pltpu.SemaphoreType.DMA((2,)),
            ],
        ),
        compiler_params=_params(("arbitrary", "arbitrary")),
        name="experts",
    )(block_e, block_valid, tok3, tok3, h_packed, w1, b1.reshape(b1.shape[0], b1.shape[1], 1, 2 * f), w2,
      b2.reshape(b2.shape[0], b2.shape[1], 1, d), _deinterleave_matrix(tf))


def _combine_kernel(dest_ref, ys_hbm, x_ref, gate_ref, g_ref, o_ref, buf, sem):
    tm = x_ref.shape[0]
    half = buf.shape[2]

    def row_copy(src_row, k, r):
        return pltpu.make_async_copy(ys_hbm.at[pl.ds(src_row, 1)], buf.at[k, pl.ds(r, 1)], sem)

    def start(r, carry):
        for k in range(TOP_K):
            row_copy(dest_ref[0, r * TOP_K + k], k, r).start()
        return carry

    def wait(r, carry):
        for k in range(TOP_K):
            row_copy(0, k, r).wait()
        return carry

    lax.fori_loop(0, tm, start, 0)
    lax.fori_loop(0, tm, wait, 0)
    g = g_ref[...]
    ya = jnp.zeros((tm, half), F32)
    yb = jnp.zeros((tm, half), F32)
    for k in range(TOP_K):
        a, b = _unpack_halves(buf[k])
        ya = ya + g[:, k:k + 1] * a
        yb = yb + g[:, k:k + 1] * b
    o_ref[:, :half] = x_ref[:, :half] + gate_ref[:, :half] * ya
    o_ref[:, half:] = x_ref[:, half:] + gate_ref[:, half:] * yb


def _combine(ys, dest, gates, x2, mod3, seq, gate_idx):
    t, d = x2.shape
    tm = COMBINE_TM
    half = ys.shape[1]
    return pl.pallas_call(
        _combine_kernel,
        out_shape=jax.ShapeDtypeStruct((t, d), F32),
        grid=(t // tm,),
        in_specs=[
            pl.BlockSpec((None, 1, tm * TOP_K), lambda i: (i, 0, 0), memory_space=pltpu.SMEM),
            pl.BlockSpec(memory_space=pl.ANY),
            pl.BlockSpec((tm, d), lambda i: (i, 0)),
            pl.BlockSpec((None, 1, d), lambda i: ((i * tm // seq) * N_MOD + gate_idx, 0, 0)),
            pl.BlockSpec((tm, TOP_K), lambda i: (i, 0)),
        ],
        out_specs=pl.BlockSpec((tm, d), lambda i: (i, 0)),
        scratch_shapes=[pltpu.VMEM((TOP_K, tm, half), U32), pltpu.SemaphoreType.DMA(())],
        compiler_params=_params(("arbitrary",)),
        name="combine",
    )(dest.reshape(t // tm, 1, tm * TOP_K), ys, x2, mod3, gates)


def kernel(x, c, w_ada, b_ada, norm1_g, w_in, q_norm_g, k_norm_g, rel_bias, w_ret_out, w_att_out, w_out,
           norm2_g, w_router, b_router, w1, b1, w2, b2):
    batch, seq, d = x.shape
    depth = w_ada.shape[0]
    t = batch * seq
    assert batch <= 8 and seq % max(RET_BLOCK, ATT_BLOCK, PROJ_TM, OUT_TM) == 0
    x2 = x.reshape(t, d)
    c8 = jnp.zeros((8, d), F32).at[:batch].set(c)
    ret_col0 = 0
    att_col0 = 2 * RET_QK_W + 2 * RET_V_W
    gate_col0 = att_col0 + 3 * ATT_W
    for layer in range(depth):
        mod3 = _ada_ln(c8, w_ada, b_ada, layer)[:batch].reshape(batch * N_MOD, 1, d)
        h = _norm_mod(x2, norm1_g.reshape(depth, 1, d), mod3, layer, seq, shift_idx=0, scale_idx=1)
        proj = _project(h, w_in, layer)
        ret = _retention(proj, batch, seq, ret_col0)
        att = _chunk_attention(proj, rel_bias[layer], q_norm_g[layer], k_norm_g[layer], batch, seq, att_col0)
        merged = _merge(ret, att, w_ret_out, w_att_out, proj, gate_col0, layer)
        x2 = _out_proj(merged, w_out, x2, mod3, layer, seq, gate_idx=2)
        h_packed, logits = _norm_router(x2, norm2_g.reshape(depth, 1, d), mod3, w_router, b_router, layer, seq,
                                        shift_idx=3, scale_idx=4)
        gates, dest, row_tok, block_e, block_valid = _routing(logits)
        ys = _experts(h_packed, row_tok, block_e, block_valid, w1, b1, w2, b2, layer)
        x2 = _combine(ys, dest, gates, x2, mod3, seq, gate_idx=5)
    return x2.reshape(batch, seq, d)
```

```python
import functools

import numpy as np
import jax
import jax.numpy as jnp
from jax import lax
from jax.experimental import pallas as pl
from jax.experimental.pallas import tpu as pltpu

F32 = jnp.float32
BF16 = jnp.bfloat16
U32 = jnp.uint32

CHUNK = 64
NORM_EPS = 1e-6
RET_HEADS = 8
RET_QK_DIM = 256
RET_V_DIM = 512
ROPE_BASE = 10000.0
ATT_HEADS = 16
ATT_HEAD_DIM = 128
PREV_CHUNKS = 8
MAX_REL = 128
N_EXPERTS = 32
TOP_K = 4
SWIGLU_ALPHA = 1.702
SWIGLU_LIMIT = 7.0
N_MOD = 6

RET_QK_W = RET_HEADS * RET_QK_DIM
RET_V_W = RET_HEADS * RET_V_DIM
ATT_W = ATT_HEADS * ATT_HEAD_DIM

V7X_VMEM_LIMIT_BYTES = 58 * 1024 * 1024
HI16 = 0xFFFF0000

ADA_TN = 1024
NORM_TM = 256
PROJ_TM, PROJ_TN = 1024, 512
RET_BLOCK = 256
ATT_BLOCK = 256
MERGE_TM, MERGE_TN = 512, 512
OUT_TM, OUT_TN = 1024, 512
EXPERT_BM = 1280
EXPERT_SUB = 256
EXPERT_TF = 256
EXPERT_TN = 256
EXPERT_ISSUE_STEPS = 10
GATHER_GROUP = 32
COMBINE_TM = 256


def _params(semantics):
    return pltpu.CompilerParams(dimension_semantics=semantics, vmem_limit_bytes=V7X_VMEM_LIMIT_BYTES)


def _silu(t):
    return t * jax.nn.sigmoid(t)


def _pack_halves(y):
    n = y.shape[1] // 2
    hi = lax.bitcast_convert_type(y[:, :n].astype(BF16).astype(F32), U32)
    lo = lax.bitcast_convert_type(y[:, n:].astype(BF16).astype(F32), U32)
    return hi | (lo >> 16)


def _unpack_halves(p):
    a = lax.bitcast_convert_type(p & jnp.uint32(HI16), F32)
    b = lax.bitcast_convert_type(p << 16, F32)
    return a, b


def _ada_kernel(c_ref, w_ref, b_ref, o_ref):
    a = _silu(c_ref[...]).astype(BF16)
    o_ref[...] = jnp.dot(a, w_ref[...].astype(BF16), preferred_element_type=F32) + b_ref[...]


def _ada_ln(c8, w_ada, b_ada, layer):
    d = c8.shape[1]
    n = w_ada.shape[2]
    return pl.pallas_call(
        _ada_kernel,
        out_shape=jax.ShapeDtypeStruct((8, n), F32),
        grid=(n // ADA_TN,),
        in_specs=[
            pl.BlockSpec((8, d), lambda j: (0, 0)),
            pl.BlockSpec((None, d, ADA_TN), lambda j: (layer, 0, j)),
            pl.BlockSpec((None, 1, ADA_TN), lambda j: (layer, 0, j)),
        ],
        out_specs=pl.BlockSpec((8, ADA_TN), lambda j: (0, j)),
        compiler_params=_params(("arbitrary",)),
        name="ada_ln",
    )(c8, w_ada, b_ada.reshape(b_ada.shape[0], 1, n))


def _modulated_norm(x, g, scale, shift):
    xn = x * lax.rsqrt(jnp.mean(x * x, axis=-1, keepdims=True) + NORM_EPS)
    return (xn * g) * (1.0 + scale) + shift


def _norm_mod_kernel(x_ref, g_ref, sc_ref, sh_ref, o_ref):
    o_ref[...] = _modulated_norm(x_ref[...], g_ref[...], sc_ref[...], sh_ref[...]).astype(o_ref.dtype)


def _norm_mod(x2, g, mod3, layer, seq, shift_idx, scale_idx):
    t, d = x2.shape
    tm = NORM_TM

    def mod_spec(idx):
        return pl.BlockSpec((None, 1, d), lambda i: ((i * tm // seq) * N_MOD + idx, 0, 0))

    return pl.pallas_call(
        _norm_mod_kernel,
        out_shape=jax.ShapeDtypeStruct((t, d), BF16),
        grid=(t // tm,),
        in_specs=[
            pl.BlockSpec((tm, d), lambda i: (i, 0)),
            pl.BlockSpec((None, 1, d), lambda i: (layer, 0, 0)),
            mod_spec(scale_idx),
            mod_spec(shift_idx),
        ],
        out_specs=pl.BlockSpec((tm, d), lambda i: (i, 0)),
        compiler_params=_params(("arbitrary",)),
        name="norm_mod",
    )(x2, g, mod3, mod3)


def _norm_router_kernel(x_ref, g_ref, sc_ref, sh_ref, wr_ref, br_ref, h_ref, lg_ref):
    h = _modulated_norm(x_ref[...], g_ref[...], sc_ref[...], sh_ref[...])
    h_ref[...] = _pack_halves(h)
    h_hi = h.astype(BF16)
    h_lo = (h - h_hi.astype(F32)).astype(BF16)
    w = wr_ref[...]
    w_hi = w.astype(BF16)
    w_lo = (w - w_hi.astype(F32)).astype(BF16)
    lg = (jnp.dot(h_hi, w_hi, preferred_element_type=F32)
          + jnp.dot(h_lo, w_hi, preferred_element_type=F32)
          + jnp.dot(h_hi, w_lo, preferred_element_type=F32))
    lg_ref[...] = lg + br_ref[...]


def _norm_router(x2, g, mod3, w_router, b_router, layer, seq, shift_idx, scale_idx):
    t, d = x2.shape
    e = w_router.shape[2]
    tm = NORM_TM

    def mod_spec(idx):
        return pl.BlockSpec((None, 1, d), lambda i: ((i * tm // seq) * N_MOD + idx, 0, 0))

    return pl.pallas_call(
        _norm_router_kernel,
        out_shape=(jax.ShapeDtypeStruct((t, d // 2), U32), jax.ShapeDtypeStruct((t, e), F32)),
        grid=(t // tm,),
        in_specs=[
            pl.BlockSpec((tm, d), lambda i: (i, 0)),
            pl.BlockSpec((None, 1, d), lambda i: (layer, 0, 0)),
            mod_spec(scale_idx),
            mod_spec(shift_idx),
            pl.BlockSpec((None, d, e), lambda i: (layer, 0, 0)),
            pl.BlockSpec((None, 1, e), lambda i: (layer, 0, 0)),
        ],
        out_specs=(pl.BlockSpec((tm, d // 2), lambda i: (i, 0)), pl.BlockSpec((tm, e), lambda i: (i, 0))),
        compiler_params=_params(("arbitrary",)),
        name="norm_router",
    )(x2, g, mod3, mod3, w_router, b_router.reshape(b_router.shape[0], 1, e))


def _proj_kernel(a_ref, w_ref, o_ref):
    o_ref[...] = jnp.dot(a_ref[...], w_ref[...].astype(BF16), preferred_element_type=F32).astype(o_ref.dtype)


def _project(a, w, layer):
    t, k = a.shape
    n = w.shape[2]
    tm, tn = PROJ_TM, PROJ_TN
    return pl.pallas_call(
        _proj_kernel,
        out_shape=jax.ShapeDtypeStruct((t, n), BF16),
        grid=(n // tn, t // tm),
        in_specs=[
            pl.BlockSpec((tm, k), lambda j, i: (i, 0)),
            pl.BlockSpec((None, k, tn), lambda j, i: (layer, 0, j)),
        ],
        out_specs=pl.BlockSpec((tm, tn), lambda j, i: (i, j)),
        compiler_params=_params(("arbitrary", "arbitrary")),
        name="in_proj",
    )(a, w)


def _retention_kernel(q_ref, k_ref, v_ref, g_ref, cos_ref, sin_ref, dec_ref, qd_ref, kd_ref, cd_ref,
                      o_ref, state_ref):
    @pl.when(pl.program_id(2) == 0)
    def _():
        state_ref[...] = jnp.zeros_like(state_ref)

    cos = cos_ref[...]
    sin = sin_ref[...]
    half = RET_QK_DIM // 2

    def rotary(t):
        t1, t2 = t[:, :half], t[:, half:]
        return jnp.concatenate([t1 * cos - t2 * sin, t1 * sin + t2 * cos], axis=1)

    q = rotary(q_ref[...].astype(F32))
    k = rotary(k_ref[...].astype(F32)) * (RET_QK_DIM ** -0.5)
    v = v_ref[...]
    contract_last = (((1,), (1,)), ((), ()))
    contract_rows = (((0,), (0,)), ((), ()))
    scores = lax.dot_general(q.astype(BF16), k.astype(BF16), contract_last,
                             preferred_element_type=F32) * dec_ref[...]
    state = state_ref[...]
    out = (jnp.dot(scores.astype(BF16), v, preferred_element_type=F32)
           + jnp.dot((q * qd_ref[...]).astype(BF16), state.astype(BF16), preferred_element_type=F32))
    state_ref[...] = state * cd_ref[...] + lax.dot_general(
        (k * kd_ref[...]).astype(BF16), v, contract_rows, preferred_element_type=F32)
    on = out * lax.rsqrt(jnp.mean(out * out, axis=-1, keepdims=True) + NORM_EPS)
    o_ref[...] = (_silu(g_ref[...].astype(F32)) * on).astype(o_ref.dtype)


def _retention_tables(seq):
    half = RET_QK_DIM // 2
    inv_freq = ROPE_BASE ** (-np.arange(half, dtype=np.float64) / half)
    ang = np.arange(seq, dtype=np.float64)[:, None] * inv_freq[None, :]
    log_gamma = np.log(1.0 - 2.0 ** (-5.0 - np.arange(RET_HEADS, dtype=np.float64)))[:, None, None]
    pos = np.arange(RET_BLOCK, dtype=np.float64)
    n, m = pos[:, None], pos[None, :]
    cn, cm = np.floor(n / CHUNK), np.floor(m / CHUNK)
    dist = np.where(cn == cm, np.abs(n - m), n - m)
    dec = np.where((cm <= cn)[None], np.exp(log_gamma * dist[None]), 0.0)
    qd = np.broadcast_to(np.exp(log_gamma * (pos[None, :, None] + 1.0)), (RET_HEADS, RET_BLOCK, RET_QK_DIM))
    kd = np.broadcast_to(np.exp(log_gamma * (RET_BLOCK - 1.0 - pos[None, :, None])),
                         (RET_HEADS, RET_BLOCK, RET_QK_DIM))
    cd = np.broadcast_to(np.exp(log_gamma * RET_BLOCK), (RET_HEADS, 1, RET_V_DIM))
    return tuple(jnp.asarray(np.ascontiguousarray(v), F32) for v in (np.cos(ang), np.sin(ang), dec, qd, kd, cd))


def _retention(proj, batch, seq, col0):
    t = proj.shape[0]
    r = RET_BLOCK
    nsb = seq // r
    cos, sin, dec, qd, kd, cd = _retention_tables(seq)
    qk_blk0 = col0 // RET_QK_DIM
    v_blk0 = (col0 + 2 * RET_QK_W) // RET_V_DIM
    row = lambda b, h, s: b * nsb + s
    head_tab = lambda shape: pl.BlockSpec((None,) + shape, lambda b, h, s: (h, 0, 0))
    half = RET_QK_DIM // 2
    return pl.pallas_call(
        _retention_kernel,
        out_shape=jax.ShapeDtypeStruct((t, RET_V_W), BF16),
        grid=(batch, RET_HEADS, nsb),
        in_specs=[
            pl.BlockSpec((r, RET_QK_DIM), lambda b, h, s: (row(b, h, s), qk_blk0 + h)),
            pl.BlockSpec((r, RET_QK_DIM), lambda b, h, s: (row(b, h, s), qk_blk0 + RET_HEADS + h)),
            pl.BlockSpec((r, RET_V_DIM), lambda b, h, s: (row(b, h, s), v_blk0 + h)),
            pl.BlockSpec((r, RET_V_DIM), lambda b, h, s: (row(b, h, s), v_blk0 + RET_HEADS + h)),
            pl.BlockSpec((r, half), lambda b, h, s: (s, 0)),
            pl.BlockSpec((r, half), lambda b, h, s: (s, 0)),
            head_tab((r, r)),
            head_tab((r, RET_QK_DIM)),
            head_tab((r, RET_QK_DIM)),
            head_tab((1, RET_V_DIM)),
        ],
        out_specs=pl.BlockSpec((r, RET_V_DIM), lambda b, h, s: (row(b, h, s), h)),
        scratch_shapes=[pltpu.VMEM((RET_QK_DIM, RET_V_DIM), F32)],
        compiler_params=_params(("arbitrary", "arbitrary", "arbitrary")),
        name="retention",
    )(proj, proj, proj, proj, cos, sin, dec, qd, kd, cd)


ATT_KEY_BLOCKS = PREV_CHUNKS * CHUNK // ATT_BLOCK + 1
NEG = float(np.finfo(np.float32).min)


def _attention_kernel(q_ref, k0_ref, k1_ref, k2_ref, v0_ref, v1_ref, v2_ref, tab_ref, gq_ref, gk_ref, o_ref):
    i = pl.program_id(1)

    def rms(t, g):
        tf = t.astype(F32)
        return (tf * lax.rsqrt(jnp.mean(tf * tf, axis=-1, keepdims=True) + NORM_EPS)) * g

    q = rms(q_ref[...], gq_ref[...]).astype(BF16)
    k = rms(jnp.concatenate([k0_ref[...], k1_ref[...], k2_ref[...]], axis=0), gk_ref[...]).astype(BF16)
    v = jnp.concatenate([v0_ref[...], v1_ref[...], v2_ref[...]], axis=0)
    sc = lax.dot_general(q, k, (((1,), (1,)), ((), ())), preferred_element_type=F32)
    sc = sc * (ATT_HEAD_DIM ** -0.5) + tab_ref[...]
    col = lax.broadcasted_iota(jnp.int32, sc.shape, 1)
    sc = jnp.where(col >= (ATT_KEY_BLOCKS - 1 - i) * ATT_BLOCK, sc, NEG)
    p = jnp.exp(sc - jnp.max(sc, axis=-1, keepdims=True))
    denom = jnp.sum(p, axis=-1, keepdims=True)
    o = jnp.dot(p.astype(BF16), v, preferred_element_type=F32) / denom
    o_ref[...] = o.astype(o_ref.dtype)


def _attention_table(rel_bias):
    a = ATT_BLOCK
    w = ATT_KEY_BLOCKS * a
    pad = PREV_CHUNKS * CHUNK
    heads = rel_bias.shape[0]
    period = 1024
    assert a - 1 + w - 1 <= period - 2
    rel_of_i = pad + a - 1 - np.arange(period)
    idx = np.clip(np.minimum(rel_of_i, MAX_REL) + (CHUNK - 1), 0, rel_bias.shape[-1] - 1)
    u = rel_bias[:, idx].astype(F32)
    skew = jnp.tile(u, (1, a))[:, :a * (period - 1)].reshape(heads, a, period - 1)
    bias = skew[:, :, a - 1:a - 1 + w]
    n = np.arange(a)[:, None]
    j = np.arange(w)[None, :]
    qc = n // CHUNK
    kc = j // CHUNK - PREV_CHUNKS
    valid = (kc <= qc) & (kc >= qc - PREV_CHUNKS)
    return jnp.where(jnp.asarray(valid)[None], bias, NEG)


def _chunk_attention(proj, rel_bias, q_norm_g, k_norm_g, batch, seq, col0):
    assert ATT_KEY_BLOCKS == 3
    t = proj.shape[0]
    a = ATT_BLOCK
    nqb = seq // a
    dh = ATT_HEAD_DIM
    blk0 = col0 // dh
    tab = _attention_table(rel_bias)

    def kv_spec(which, jj):
        return pl.BlockSpec(
            (a, dh),
            lambda h, i, b: (b * nqb + jnp.maximum(i - (ATT_KEY_BLOCKS - 1) + jj, 0), blk0 + which * ATT_HEADS + h))

    return pl.pallas_call(
        _attention_kernel,
        out_shape=jax.ShapeDtypeStruct((t, ATT_W), BF16),
        grid=(ATT_HEADS, nqb, batch),
        in_specs=[
            pl.BlockSpec((a, dh), lambda h, i, b: (b * nqb + i, blk0 + h)),
            kv_spec(1, 0), kv_spec(1, 1), kv_spec(1, 2),
            kv_spec(2, 0), kv_spec(2, 1), kv_spec(2, 2),
            pl.BlockSpec((None, a, ATT_KEY_BLOCKS * a), lambda h, i, b: (h, 0, 0)),
            pl.BlockSpec((1, dh), lambda h, i, b: (0, 0)),
            pl.BlockSpec((1, dh), lambda h, i, b: (0, 0)),
        ],
        out_specs=pl.BlockSpec((a, dh), lambda h, i, b: (b * nqb + i, h)),
        compiler_params=_params(("arbitrary", "arbitrary", "arbitrary")),
        name="chunk_attention",
    )(proj, proj, proj, proj, proj, proj, proj, tab, q_norm_g.reshape(1, dh), k_norm_g.reshape(1, dh))


def _merge_kernel(r_ref, a_ref, wr_ref, wa_ref, gr_ref, ga_ref, o_ref):
    yr = jnp.dot(r_ref[...], wr_ref[...].astype(BF16), preferred_element_type=F32)
    ya = jnp.dot(a_ref[...], wa_ref[...].astype(BF16), preferred_element_type=F32)
    merged = jax.nn.sigmoid(gr_ref[...].astype(F32)) * yr + jax.nn.sigmoid(ga_ref[...].astype(F32)) * ya
    o_ref[...] = merged.astype(o_ref.dtype)


def _merge(ret, att, w_ret_out, w_att_out, proj, gate_col0, layer):
    t = ret.shape[0]
    d = w_ret_out.shape[2]
    tm, tn = MERGE_TM, MERGE_TN
    gblk = gate_col0 // tn
    return pl.pallas_call(
        _merge_kernel,
        out_shape=jax.ShapeDtypeStruct((t, d), BF16),
        grid=(d // tn, t // tm),
        in_specs=[
            pl.BlockSpec((tm, ret.shape[1]), lambda j, i: (i, 0)),
            pl.BlockSpec((tm, att.shape[1]), lambda j, i: (i, 0)),
            pl.BlockSpec((None, ret.shape[1], tn), lambda j, i: (layer, 0, j)),
            pl.BlockSpec((None, att.shape[1], tn), lambda j, i: (layer, 0, j)),
            pl.BlockSpec((tm, tn), lambda j, i: (i, gblk + j)),
            pl.BlockSpec((tm, tn), lambda j, i: (i, gblk + d // tn + j)),
        ],
        out_specs=pl.BlockSpec((tm, tn), lambda j, i: (i, j)),
        compiler_params=_params(("arbitrary", "arbitrary")),
        name="merge",
    )(ret, att, w_ret_out, w_att_out, proj, proj)


def _out_proj_kernel(a_ref, w_ref, x_ref, gate_ref, o_ref):
    acc = jnp.dot(a_ref[...], w_ref[...].astype(BF16), preferred_element_type=F32)
    o_ref[...] = x_ref[...] + gate_ref[...] * acc


def _out_proj(a, w, x2, mod3, layer, seq, gate_idx):
    t, k = a.shape
    d = w.shape[2]
    tm, tn = OUT_TM, OUT_TN
    return pl.pallas_call(
        _out_proj_kernel,
        out_shape=jax.ShapeDtypeStruct((t, d), F32),
        grid=(d // tn, t // tm),
        in_specs=[
            pl.BlockSpec((tm, k), lambda j, i: (i, 0)),
            pl.BlockSpec((None, k, tn), lambda j, i: (layer, 0, j)),
            pl.BlockSpec((tm, tn), lambda j, i: (i, j)),
            pl.BlockSpec((None, 1, tn), lambda j, i: ((i * tm // seq) * N_MOD + gate_idx, 0, j)),
        ],
        out_specs=pl.BlockSpec((tm, tn), lambda j, i: (i, j)),
        compiler_params=_params(("arbitrary", "arbitrary")),
        name="out_proj",
    )(a, w, x2, mod3)


def _routing(logits):
    t = logits.shape[0]
    bm = EXPERT_BM
    a = t * TOP_K
    top_val, top_idx = lax.top_k(logits, TOP_K)
    gate = jax.nn.softmax(top_val, axis=-1)
    onehot = top_idx[:, :, None] == jnp.arange(N_EXPERTS, dtype=jnp.int32)[None, None, :]
    picked = jnp.any(onehot, axis=1).astype(jnp.int32)
    inclusive = jnp.cumsum(picked, axis=0)
    sizes = inclusive[-1]
    padded = (sizes + bm - 1) // bm * bm
    pad_end = jnp.cumsum(padded)
    pad_start = pad_end - padded
    start = jnp.cumsum(sizes) - sizes
    slot = (pad_start[None, :] + inclusive - picked)[:, None, :]
    dest = jnp.sum(jnp.where(onehot, slot, 0), axis=-1).astype(jnp.int32).reshape(a)
    order = jnp.argsort(top_idx.reshape(a))
    n_blocks = -(-a // bm) + N_EXPERTS
    block_row0 = jnp.arange(n_blocks, dtype=jnp.int32) * bm
    block_e = jnp.minimum(jnp.searchsorted(pad_end, block_row0, side='right'), N_EXPERTS - 1).astype(jnp.int32)
    block_valid = block_row0 < pad_end[-1]
    block_rows = jnp.clip(sizes[block_e] - (block_row0 - pad_start[block_e]), 0, bm)
    block_nsub = jnp.where(block_valid, (block_rows + EXPERT_SUB - 1) // EXPERT_SUB, 0).astype(jnp.int32)
    row = jnp.arange(n_blocks * bm, dtype=jnp.int32)
    row_e = jnp.repeat(block_e, bm)
    offset = row - pad_start[row_e]
    real = (offset < sizes[row_e]) & jnp.repeat(block_valid, bm)
    sorted_pos = jnp.clip(start[row_e] + offset, 0, a - 1)
    row_tok = jnp.where(real, order[sorted_pos] // TOP_K, 0).astype(jnp.int32)
    return gate, dest, row_tok, block_e, block_nsub


def _expert_kernel(be_ref, ns_ref, tok0_ref, tokn_ref, h_hbm, w1_ref, b1_ref, w2a_ref, w2b_ref, b2a_ref, b2b_ref,
                   perm_ref, o_ref, gbuf, x_scr, act_scr, sem, *, nj1, issue_steps):
    b = pl.program_id(0)
    j = pl.program_id(1)
    nb = pl.num_programs(0)
    nsub = ns_ref[b]
    sub = EXPERT_SUB
    group = GATHER_GROUP
    bm, d = x_scr.shape
    half = d // 2
    tf = perm_ref.shape[0] // 2
    per_step = bm // issue_steps
    rows_next = jnp.where(b + 1 < nb, ns_ref[jnp.minimum(b + 1, nb - 1)], 0) * sub

    def row_copy(tok_ref, r):
        return pltpu.make_async_copy(h_hbm.at[pl.ds(tok_ref[0, r], 1)], gbuf.at[pl.ds(r, 1)], sem)

    def sub_rows(s):
        return pl.ds(pl.multiple_of(s * sub, sub), sub)

    @pl.when((j == 0) & (b == 0))
    def _():
        def start(r, carry):
            row_copy(tok0_ref, r).start()
            return carry
        lax.fori_loop(0, nsub * sub, start, 0)

    @pl.when(j == 0)
    def _():
        def wait_rows(s, carry):
            pltpu.make_async_copy(h_hbm.at[pl.ds(0, sub)], gbuf.at[pl.ds(0, sub)], sem).wait()
            return carry
        lax.fori_loop(0, nsub, wait_rows, 0)

        def unpack_rows(s, carry):
            xa, xb = _unpack_halves(gbuf[sub_rows(s), :])
            x_scr[sub_rows(s), pl.ds(0, half)] = xa.astype(BF16)
            x_scr[sub_rows(s), pl.ds(half, half)] = xb.astype(BF16)
            return carry
        lax.fori_loop(0, nsub, unpack_rows, 0)

    @pl.when(j < issue_steps)
    def _():
        for g in range(per_step // group):
            row0 = j * per_step + g * group

            @pl.when(row0 < rows_next)
            def _():
                for i in range(group):
                    row_copy(tokn_ref, row0 + i).start()

    @pl.when(j < nj1)
    def _():
        def body(s, carry):
            hb = jnp.dot(x_scr[sub_rows(s), :], w1_ref[...].astype(BF16), preferred_element_type=F32) + b1_ref[...]
            sel = jnp.dot(hb.astype(BF16), perm_ref[...], preferred_element_type=F32)
            x_glu = jnp.minimum(sel[:, :tf], SWIGLU_LIMIT)
            x_lin = jnp.clip(sel[:, tf:], -SWIGLU_LIMIT, SWIGLU_LIMIT)
            act = x_glu * jax.nn.sigmoid(SWIGLU_ALPHA * x_glu) * (x_lin + 1.0)
            act_scr[jnp.minimum(j, nj1 - 1), sub_rows(s), :] = act.astype(BF16)
            return carry
        lax.fori_loop(0, nsub, body, 0)

    @pl.when(j >= nj1)
    def _():
        def body(s, carry):
            act = jnp.concatenate([act_scr[t, sub_rows(s), :] for t in range(nj1)], axis=1)
            w2 = jnp.concatenate([w2a_ref[...], w2b_ref[...]], axis=1).astype(BF16)
            y = jnp.dot(act, w2, preferred_element_type=F32)
            y = y + jnp.concatenate([b2a_ref[...], b2b_ref[...]], axis=1)
            o_ref[sub_rows(s), :] = _pack_halves(y)
            return carry
        lax.fori_loop(0, nsub, body, 0)

        def clear(s, carry):
            o_ref[sub_rows(s), :] = jnp.zeros((sub, o_ref.shape[1]), o_ref.dtype)
            return carry
        lax.fori_loop(nsub, bm // sub, clear, 0)


def _deinterleave_matrix(tf):
    p = np.zeros((2 * tf, 2 * tf), np.float32)
    f = np.arange(tf)
    p[2 * f, f] = 1.0
    p[2 * f + 1, tf + f] = 1.0
    return jnp.asarray(p, BF16)


def _experts(h_packed, row_tok, block_e, block_nsub, w1, b1, w2, b2, layer):
    half = h_packed.shape[1]
    d = 2 * half
    f = w2.shape[2]
    bm, tf, tn = EXPERT_BM, EXPERT_TF, EXPERT_TN
    n_blocks = block_e.shape[0]
    nj1 = f // tf
    nj2 = half // tn
    issue_steps = EXPERT_ISSUE_STEPS
    assert issue_steps <= nj1 + nj2 and bm % (issue_steps * GATHER_GROUP) == 0 and bm % EXPERT_SUB == 0
    assert h_packed.shape[0] >= EXPERT_SUB

    def w1_idx(b, j, be, ns):
        nxt = jnp.minimum(b + 1, n_blocks - 1)
        to_next = (j >= nj1) & (ns[nxt] > 0)
        e = jnp.where(to_next, be[nxt], be[b])
        tile = jnp.where(to_next, 0, jnp.where(ns[b] > 0, jnp.minimum(j, nj1 - 1), nj1 - 1))
        return e, tile

    def w2_tile(b, j, ns):
        return jnp.where(ns[b] > 0, jnp.maximum(j - nj1, 0), nj2 - 1)

    tok3 = row_tok.reshape(n_blocks, 1, bm)
    b1r = b1.reshape(b1.shape[0], b1.shape[1], 1, 2 * f)
    b2r = b2.reshape(b2.shape[0], b2.shape[1], 1, d)
    return pl.pallas_call(
        functools.partial(_expert_kernel, nj1=nj1, issue_steps=issue_steps),
        out_shape=jax.ShapeDtypeStruct((n_blocks * bm, half), U32),
        grid_spec=pltpu.PrefetchScalarGridSpec(
            num_scalar_prefetch=2,
            grid=(n_blocks, nj1 + nj2),
            in_specs=[
                pl.BlockSpec((None, 1, bm), lambda b, j, be, ns: (0, 0, 0), memory_space=pltpu.SMEM),
                pl.BlockSpec((None, 1, bm), lambda b, j, be, ns: (jnp.minimum(b + 1, n_blocks - 1), 0, 0),
                             memory_space=pltpu.SMEM),
                pl.BlockSpec(memory_space=pl.ANY),
                pl.BlockSpec((None, None, d, 2 * tf),
                             lambda b, j, be, ns: (layer, w1_idx(b, j, be, ns)[0], 0, w1_idx(b, j, be, ns)[1])),
                pl.BlockSpec((None, None, 1, 2 * tf),
                             lambda b, j, be, ns: (layer, be[b], 0, jnp.minimum(j, nj1 - 1))),
                pl.BlockSpec((None, None, f, tn), lambda b, j, be, ns: (layer, be[b], 0, w2_tile(b, j, ns))),
                pl.BlockSpec((None, None, f, tn), lambda b, j, be, ns: (layer, be[b], 0, nj2 + w2_tile(b, j, ns))),
                pl.BlockSpec((None, None, 1, tn), lambda b, j, be, ns: (layer, be[b], 0, w2_tile(b, j, ns))),
                pl.BlockSpec((None, None, 1, tn), lambda b, j, be, ns: (layer, be[b], 0, nj2 + w2_tile(b, j, ns))),
                pl.BlockSpec((2 * tf, 2 * tf), lambda b, j, be, ns: (0, 0)),
            ],
            out_specs=pl.BlockSpec((bm, tn), lambda b, j, be, ns: (b, jnp.maximum(j - nj1, 0))),
            scratch_shapes=[
                pltpu.VMEM((bm, half), U32),
                pltpu.VMEM((bm, d), BF16),
                pltpu.VMEM((nj1, bm, tf), BF16),
                pltpu.SemaphoreType.DMA(()),
            ],
        ),
        compiler_params=_params(("arbitrary", "arbitrary")),
        name="experts",
    )(block_e, block_nsub, tok3, tok3, h_packed, w1, b1r, w2, w2, b2r, b2r, _deinterleave_matrix(tf))


def _combine_kernel(dest_ref, ys_hbm, x_ref, gate_ref, g_ref, o_ref, buf, sem):
    tm = x_ref.shape[0]
    half = buf.shape[2]

    def row_copy(src_row, k, r):
        return pltpu.make_async_copy(ys_hbm.at[pl.ds(src_row, 1)], buf.at[k, pl.ds(r, 1)], sem)

    def start(r, carry):
        for k in range(TOP_K):
            row_copy(dest_ref[0, r * TOP_K + k], k, r).start(priority=k % 2)
        return carry

    def wait(r, carry):
        for k in range(TOP_K):
            row_copy(0, k, r).wait()
        return carry

    lax.fori_loop(0, tm, start, 0)
    lax.fori_loop(0, tm, wait, 0)
    g = g_ref[...]
    ya = jnp.zeros((tm, half), F32)
    yb = jnp.zeros((tm, half), F32)
    for k in range(TOP_K):
        a, b = _unpack_halves(buf[k])
        ya = ya + g[:, k:k + 1] * a
        yb = yb + g[:, k:k + 1] * b
    o_ref[:, :half] = x_ref[:, :half] + gate_ref[:, :half] * ya
    o_ref[:, half:] = x_ref[:, half:] + gate_ref[:, half:] * yb


def _combine(ys, dest, gates, x2, mod3, seq, gate_idx):
    t, d = x2.shape
    tm = COMBINE_TM
    half = ys.shape[1]
    return pl.pallas_call(
        _combine_kernel,
        out_shape=jax.ShapeDtypeStruct((t, d), F32),
        grid=(t // tm,),
        in_specs=[
            pl.BlockSpec((None, 1, tm * TOP_K), lambda i: (i, 0, 0), memory_space=pltpu.SMEM),
            pl.BlockSpec(memory_space=pl.ANY),
            pl.BlockSpec((tm, d), lambda i: (i, 0)),
            pl.BlockSpec((None, 1, d), lambda i: ((i * tm // seq) * N_MOD + gate_idx, 0, 0)),
            pl.BlockSpec((tm, TOP_K), lambda i: (i, 0)),
        ],
        out_specs=pl.BlockSpec((tm, d), lambda i: (i, 0)),
        scratch_shapes=[pltpu.VMEM((TOP_K, tm, half), U32), pltpu.SemaphoreType.DMA(())],
        compiler_params=_params(("arbitrary",)),
        name="combine",
    )(dest.reshape(t // tm, 1, tm * TOP_K), ys, x2, mod3, gates)


def kernel(x, c, w_ada, b_ada, norm1_g, w_in, q_norm_g, k_norm_g, rel_bias, w_ret_out, w_att_out, w_out,
           norm2_g, w_router, b_router, w1, b1, w2, b2):
    batch, seq, d = x.shape
    depth = w_ada.shape[0]
    t = batch * seq
    assert batch <= 8 and seq % max(RET_BLOCK, ATT_BLOCK, PROJ_TM, OUT_TM) == 0
    x2 = x.reshape(t, d)
    c8 = jnp.zeros((8, d), F32).at[:batch].set(c)
    ret_col0 = 0
    att_col0 = 2 * RET_QK_W + 2 * RET_V_W
    gate_col0 = att_col0 + 3 * ATT_W
    for layer in range(depth):
        mod3 = _ada_ln(c8, w_ada, b_ada, layer)[:batch].reshape(batch * N_MOD, 1, d)
        h = _norm_mod(x2, norm1_g.reshape(depth, 1, d), mod3, layer, seq, shift_idx=0, scale_idx=1)
        proj = _project(h, w_in, layer)
        ret = _retention(proj, batch, seq, ret_col0)
        att = _chunk_attention(proj, rel_bias[layer], q_norm_g[layer], k_norm_g[layer], batch, seq, att_col0)
        merged = _merge(ret, att, w_ret_out, w_att_out, proj, gate_col0, layer)
        x2 = _out_proj(merged, w_out, x2, mod3, layer, seq, gate_idx=2)
        h_packed, logits = _norm_router(x2, norm2_g.reshape(depth, 1, d), mod3, w_router, b_router, layer, seq,
                                        shift_idx=3, scale_idx=4)
        gates, dest, row_tok, block_e, block_nsub = _routing(logits)
        ys = _experts(h_packed, row_tok, block_e, block_nsub, w1, b1, w2, b2, layer)
        x2 = _combine(ys, dest, gates, x2, mod3, seq, gate_idx=5)
    return x2.reshape(batch, seq, d)
```

```python
import functools

import numpy as np
import jax
import jax.numpy as jnp
from jax import lax
from jax.experimental import pallas as pl
from jax.experimental.pallas import tpu as pltpu

F32 = jnp.float32
BF16 = jnp.bfloat16
U32 = jnp.uint32

CHUNK = 64
NORM_EPS = 1e-6
RET_HEADS = 8
RET_QK_DIM = 256
RET_V_DIM = 512
ROPE_BASE = 10000.0
ATT_HEADS = 16
ATT_HEAD_DIM = 128
PREV_CHUNKS = 8
MAX_REL = 128
N_EXPERTS = 32
TOP_K = 4
SWIGLU_ALPHA = 1.702
SWIGLU_LIMIT = 7.0
N_MOD = 6

RET_QK_W = RET_HEADS * RET_QK_DIM
RET_V_W = RET_HEADS * RET_V_DIM
ATT_W = ATT_HEADS * ATT_HEAD_DIM

V7X_VMEM_LIMIT_BYTES = 58 * 1024 * 1024
HI16 = 0xFFFF0000

ADA_TN = 1024
NORM_TM = 256
PROJ_TM, PROJ_TN = 1024, 512
RET_BLOCK = 256
ATT_BLOCK = 256
ATT_HEADS_PER_STEP = 2
MERGE_TM, MERGE_TN = 512, 512
OUT_TM, OUT_TN = 1024, 512
EXPERT_BM = 1280
EXPERT_SUB = 256
EXPERT_TF = 256
EXPERT_TN = 256
EXPERT_PERM_W = 256
EXPERT_ISSUE_STEPS = 10
GATHER_GROUP = 32
COMBINE_TM = 256


def _params(semantics):
    return pltpu.CompilerParams(dimension_semantics=semantics, vmem_limit_bytes=V7X_VMEM_LIMIT_BYTES)


def _silu(t):
    return t * jax.nn.sigmoid(t)


def _pack_halves(y):
    n = y.shape[1] // 2
    hi = lax.bitcast_convert_type(y[:, :n].astype(BF16).astype(F32), U32)
    lo = lax.bitcast_convert_type(y[:, n:].astype(BF16).astype(F32), U32)
    return hi | (lo >> 16)


def _unpack_halves(p):
    a = lax.bitcast_convert_type(p & jnp.uint32(HI16), F32)
    b = lax.bitcast_convert_type(p << 16, F32)
    return a, b


def _ada_kernel(c_ref, w_ref, b_ref, o_ref):
    a = _silu(c_ref[...]).astype(BF16)
    o_ref[...] = jnp.dot(a, w_ref[...].astype(BF16), preferred_element_type=F32) + b_ref[...]


def _ada_ln(c8, w_ada, b_ada, layer):
    d = c8.shape[1]
    n = w_ada.shape[2]
    return pl.pallas_call(
        _ada_kernel,
        out_shape=jax.ShapeDtypeStruct((8, n), F32),
        grid=(n // ADA_TN,),
        in_specs=[
            pl.BlockSpec((8, d), lambda j: (0, 0)),
            pl.BlockSpec((None, d, ADA_TN), lambda j: (layer, 0, j)),
            pl.BlockSpec((None, 1, ADA_TN), lambda j: (layer, 0, j)),
        ],
        out_specs=pl.BlockSpec((8, ADA_TN), lambda j: (0, j)),
        compiler_params=_params(("arbitrary",)),
        name="ada_ln",
    )(c8, w_ada, b_ada.reshape(b_ada.shape[0], 1, n))


def _modulated_norm(x, g, scale, shift):
    xn = x * lax.rsqrt(jnp.mean(x * x, axis=-1, keepdims=True) + NORM_EPS)
    return (xn * g) * (1.0 + scale) + shift


def _norm_mod_kernel(x_ref, g_ref, sc_ref, sh_ref, o_ref):
    o_ref[...] = _modulated_norm(x_ref[...], g_ref[...], sc_ref[...], sh_ref[...]).astype(o_ref.dtype)


def _norm_mod(x2, g, mod3, layer, seq, shift_idx, scale_idx):
    t, d = x2.shape
    tm = NORM_TM

    def mod_spec(idx):
        return pl.BlockSpec((None, 1, d), lambda i: ((i * tm // seq) * N_MOD + idx, 0, 0))

    return pl.pallas_call(
        _norm_mod_kernel,
        out_shape=jax.ShapeDtypeStruct((t, d), BF16),
        grid=(t // tm,),
        in_specs=[
            pl.BlockSpec((tm, d), lambda i: (i, 0)),
            pl.BlockSpec((None, 1, d), lambda i: (layer, 0, 0)),
            mod_spec(scale_idx),
            mod_spec(shift_idx),
        ],
        out_specs=pl.BlockSpec((tm, d), lambda i: (i, 0)),
        compiler_params=_params(("arbitrary",)),
        name="norm_mod",
    )(x2, g, mod3, mod3)


def _norm_router_kernel(x_ref, g_ref, sc_ref, sh_ref, wr_ref, br_ref, h_ref, lg_ref):
    h = _modulated_norm(x_ref[...], g_ref[...], sc_ref[...], sh_ref[...])
    h_ref[...] = _pack_halves(h)
    h_hi = h.astype(BF16)
    h_lo = (h - h_hi.astype(F32)).astype(BF16)
    w = wr_ref[...]
    w_hi = w.astype(BF16)
    w_lo = (w - w_hi.astype(F32)).astype(BF16)
    lg = (jnp.dot(h_hi, w_hi, preferred_element_type=F32)
          + jnp.dot(h_lo, w_hi, preferred_element_type=F32)
          + jnp.dot(h_hi, w_lo, preferred_element_type=F32))
    lg_ref[...] = lg + br_ref[...]


def _norm_router(x2, g, mod3, w_router, b_router, layer, seq, shift_idx, scale_idx):
    t, d = x2.shape
    e = w_router.shape[2]
    tm = NORM_TM

    def mod_spec(idx):
        return pl.BlockSpec((None, 1, d), lambda i: ((i * tm // seq) * N_MOD + idx, 0, 0))

    return pl.pallas_call(
        _norm_router_kernel,
        out_shape=(jax.ShapeDtypeStruct((t, d // 2), U32), jax.ShapeDtypeStruct((t, e), F32)),
        grid=(t // tm,),
        in_specs=[
            pl.BlockSpec((tm, d), lambda i: (i, 0)),
            pl.BlockSpec((None, 1, d), lambda i: (layer, 0, 0)),
            mod_spec(scale_idx),
            mod_spec(shift_idx),
            pl.BlockSpec((None, d, e), lambda i: (layer, 0, 0)),
            pl.BlockSpec((None, 1, e), lambda i: (layer, 0, 0)),
        ],
        out_specs=(pl.BlockSpec((tm, d // 2), lambda i: (i, 0)), pl.BlockSpec((tm, e), lambda i: (i, 0))),
        compiler_params=_params(("arbitrary",)),
        name="norm_router",
    )(x2, g, mod3, mod3, w_router, b_router.reshape(b_router.shape[0], 1, e))


def _proj_kernel(a_ref, w_ref, o_ref):
    o_ref[...] = jnp.dot(a_ref[...], w_ref[...].astype(BF16), preferred_element_type=F32).astype(o_ref.dtype)


def _project(a, w, layer):
    t, k = a.shape
    n = w.shape[2]
    tm, tn = PROJ_TM, PROJ_TN
    return pl.pallas_call(
        _proj_kernel,
        out_shape=jax.ShapeDtypeStruct((t, n), BF16),
        grid=(n // tn, t // tm),
        in_specs=[
            pl.BlockSpec((tm, k), lambda j, i: (i, 0)),
            pl.BlockSpec((None, k, tn), lambda j, i: (layer, 0, j)),
        ],
        out_specs=pl.BlockSpec((tm, tn), lambda j, i: (i, j)),
        compiler_params=_params(("arbitrary", "arbitrary")),
        name="in_proj",
    )(a, w)


def _retention_kernel(q_ref, k_ref, v_ref, g_ref, cos_ref, sin_ref, dec_ref, qd_ref, kd_ref, cd_ref,
                      o_ref, state_ref):
    @pl.when(pl.program_id(2) == 0)
    def _():
        state_ref[...] = jnp.zeros_like(state_ref)

    cos = cos_ref[...]
    sin = sin_ref[...]
    half = RET_QK_DIM // 2

    def rotary(t):
        t1, t2 = t[:, :half], t[:, half:]
        return jnp.concatenate([t1 * cos - t2 * sin, t1 * sin + t2 * cos], axis=1)

    q = rotary(q_ref[...].astype(F32))
    k = rotary(k_ref[...].astype(F32)) * (RET_QK_DIM ** -0.5)
    v = v_ref[...]
    contract_last = (((1,), (1,)), ((), ()))
    contract_rows = (((0,), (0,)), ((), ()))
    scores = lax.dot_general(q.astype(BF16), k.astype(BF16), contract_last,
                             preferred_element_type=F32) * dec_ref[...]
    state = state_ref[...]
    out = (jnp.dot(scores.astype(BF16), v, preferred_element_type=F32)
           + jnp.dot((q * qd_ref[...]).astype(BF16), state.astype(BF16), preferred_element_type=F32))
    state_ref[...] = state * cd_ref[...] + lax.dot_general(
        (k * kd_ref[...]).astype(BF16), v, contract_rows, preferred_element_type=F32)
    on = out * lax.rsqrt(jnp.mean(out * out, axis=-1, keepdims=True) + NORM_EPS)
    o_ref[...] = (_silu(g_ref[...].astype(F32)) * on).astype(o_ref.dtype)


def _retention_tables(seq):
    half = RET_QK_DIM // 2
    inv_freq = ROPE_BASE ** (-np.arange(half, dtype=np.float64) / half)
    ang = np.arange(seq, dtype=np.float64)[:, None] * inv_freq[None, :]
    log_gamma = np.log(1.0 - 2.0 ** (-5.0 - np.arange(RET_HEADS, dtype=np.float64)))[:, None, None]
    pos = np.arange(RET_BLOCK, dtype=np.float64)
    n, m = pos[:, None], pos[None, :]
    cn, cm = np.floor(n / CHUNK), np.floor(m / CHUNK)
    dist = np.where(cn == cm, np.abs(n - m), n - m)
    dec = np.where((cm <= cn)[None], np.exp(log_gamma * dist[None]), 0.0)
    qd = np.broadcast_to(np.exp(log_gamma * (pos[None, :, None] + 1.0)), (RET_HEADS, RET_BLOCK, RET_QK_DIM))
    kd = np.broadcast_to(np.exp(log_gamma * (RET_BLOCK - 1.0 - pos[None, :, None])),
                         (RET_HEADS, RET_BLOCK, RET_QK_DIM))
    cd = np.broadcast_to(np.exp(log_gamma * RET_BLOCK), (RET_HEADS, 1, RET_V_DIM))
    return tuple(jnp.asarray(np.ascontiguousarray(v), F32) for v in (np.cos(ang), np.sin(ang), dec, qd, kd, cd))


def _retention(proj, batch, seq, col0):
    t = proj.shape[0]
    r = RET_BLOCK
    nsb = seq // r
    cos, sin, dec, qd, kd, cd = _retention_tables(seq)
    qk_blk0 = col0 // RET_QK_DIM
    v_blk0 = (col0 + 2 * RET_QK_W) // RET_V_DIM
    row = lambda b, h, s: b * nsb + s
    head_tab = lambda shape: pl.BlockSpec((None,) + shape, lambda b, h, s: (h, 0, 0))
    half = RET_QK_DIM // 2
    return pl.pallas_call(
        _retention_kernel,
        out_shape=jax.ShapeDtypeStruct((t, RET_V_W), BF16),
        grid=(batch, RET_HEADS, nsb),
        in_specs=[
            pl.BlockSpec((r, RET_QK_DIM), lambda b, h, s: (row(b, h, s), qk_blk0 + h)),
            pl.BlockSpec((r, RET_QK_DIM), lambda b, h, s: (row(b, h, s), qk_blk0 + RET_HEADS + h)),
            pl.BlockSpec((r, RET_V_DIM), lambda b, h, s: (row(b, h, s), v_blk0 + h)),
            pl.BlockSpec((r, RET_V_DIM), lambda b, h, s: (row(b, h, s), v_blk0 + RET_HEADS + h)),
            pl.BlockSpec((r, half), lambda b, h, s: (s, 0)),
            pl.BlockSpec((r, half), lambda b, h, s: (s, 0)),
            head_tab((r, r)),
            head_tab((r, RET_QK_DIM)),
            head_tab((r, RET_QK_DIM)),
            head_tab((1, RET_V_DIM)),
        ],
        out_specs=pl.BlockSpec((r, RET_V_DIM), lambda b, h, s: (row(b, h, s), h)),
        scratch_shapes=[pltpu.VMEM((RET_QK_DIM, RET_V_DIM), F32)],
        compiler_params=_params(("arbitrary", "arbitrary", "arbitrary")),
        name="retention",
    )(proj, proj, proj, proj, cos, sin, dec, qd, kd, cd)


ATT_KEY_BLOCKS = PREV_CHUNKS * CHUNK // ATT_BLOCK + 1
NEG = float(np.finfo(np.float32).min)


def _attention_kernel(q_ref, k0_ref, k1_ref, k2_ref, v0_ref, v1_ref, v2_ref, tab_ref, gq_ref, gk_ref, o_ref):
    i = pl.program_id(1)

    def rms(t, g):
        tf = t.astype(F32)
        return (tf * lax.rsqrt(jnp.mean(tf * tf, axis=-1, keepdims=True) + NORM_EPS)) * g

    dh = ATT_HEAD_DIM
    k_all = jnp.concatenate([k0_ref[...], k1_ref[...], k2_ref[...]], axis=0)
    v_all = jnp.concatenate([v0_ref[...], v1_ref[...], v2_ref[...]], axis=0)
    outs = []
    for hh in range(ATT_HEADS_PER_STEP):
        lanes = slice(hh * dh, (hh + 1) * dh)
        q = rms(q_ref[:, lanes], gq_ref[...]).astype(BF16)
        k = rms(k_all[:, lanes], gk_ref[...]).astype(BF16)
        sc = lax.dot_general(q, k, (((1,), (1,)), ((), ())), preferred_element_type=F32)
        sc = sc * (ATT_HEAD_DIM ** -0.5) + tab_ref[hh]
        col = lax.broadcasted_iota(jnp.int32, sc.shape, 1)
        sc = jnp.where(col >= (ATT_KEY_BLOCKS - 1 - i) * ATT_BLOCK, sc, NEG)
        p = jnp.exp(sc - jnp.max(sc, axis=-1, keepdims=True))
        denom = jnp.sum(p, axis=-1, keepdims=True)
        outs.append(jnp.dot(p.astype(BF16), v_all[:, lanes], preferred_element_type=F32) / denom)
    o_ref[...] = jnp.concatenate(outs, axis=1).astype(o_ref.dtype)


def _attention_table(rel_bias):
    a = ATT_BLOCK
    w = ATT_KEY_BLOCKS * a
    pad = PREV_CHUNKS * CHUNK
    heads = rel_bias.shape[0]
    period = 1024
    assert a - 1 + w - 1 <= period - 2
    rel_of_i = pad + a - 1 - np.arange(period)
    idx = np.clip(np.minimum(rel_of_i, MAX_REL) + (CHUNK - 1), 0, rel_bias.shape[-1] - 1)
    u = rel_bias[:, idx].astype(F32)
    skew = jnp.tile(u, (1, a))[:, :a * (period - 1)].reshape(heads, a, period - 1)
    bias = skew[:, :, a - 1:a - 1 + w]
    n = np.arange(a)[:, None]
    j = np.arange(w)[None, :]
    qc = n // CHUNK
    kc = j // CHUNK - PREV_CHUNKS
    valid = (kc <= qc) & (kc >= qc - PREV_CHUNKS)
    return jnp.where(jnp.asarray(valid)[None], bias, NEG)


def _chunk_attention(proj, rel_bias, q_norm_g, k_norm_g, batch, seq, col0):
    assert ATT_KEY_BLOCKS == 3
    t = proj.shape[0]
    a = ATT_BLOCK
    nqb = seq // a
    dh = ATT_HEAD_DIM
    hw = ATT_HEADS_PER_STEP * dh
    groups = ATT_HEADS // ATT_HEADS_PER_STEP
    assert col0 % hw == 0 and ATT_W % hw == 0
    blk0 = col0 // hw
    tab = _attention_table(rel_bias)

    def kv_spec(which, jj):
        return pl.BlockSpec(
            (a, hw),
            lambda h, i, b: (b * nqb + jnp.maximum(i - (ATT_KEY_BLOCKS - 1) + jj, 0), blk0 + which * groups + h))

    return pl.pallas_call(
        _attention_kernel,
        out_shape=jax.ShapeDtypeStruct((t, ATT_W), BF16),
        grid=(groups, nqb, batch),
        in_specs=[
            pl.BlockSpec((a, hw), lambda h, i, b: (b * nqb + i, blk0 + h)),
            kv_spec(1, 0), kv_spec(1, 1), kv_spec(1, 2),
            kv_spec(2, 0), kv_spec(2, 1), kv_spec(2, 2),
            pl.BlockSpec((ATT_HEADS_PER_STEP, a, ATT_KEY_BLOCKS * a), lambda h, i, b: (h, 0, 0)),
            pl.BlockSpec((1, dh), lambda h, i, b: (0, 0)),
            pl.BlockSpec((1, dh), lambda h, i, b: (0, 0)),
        ],
        out_specs=pl.BlockSpec((a, hw), lambda h, i, b: (b * nqb + i, h)),
        compiler_params=_params(("arbitrary", "arbitrary", "arbitrary")),
        name="chunk_attention",
    )(proj, proj, proj, proj, proj, proj, proj, tab, q_norm_g.reshape(1, dh), k_norm_g.reshape(1, dh))


def _merge_kernel(r_ref, a_ref, wr_ref, wa_ref, gr_ref, ga_ref, o_ref):
    yr = jnp.dot(r_ref[...], wr_ref[...].astype(BF16), preferred_element_type=F32)
    ya = jnp.dot(a_ref[...], wa_ref[...].astype(BF16), preferred_element_type=F32)
    merged = jax.nn.sigmoid(gr_ref[...].astype(F32)) * yr + jax.nn.sigmoid(ga_ref[...].astype(F32)) * ya
    o_ref[...] = merged.astype(o_ref.dtype)


def _merge(ret, att, w_ret_out, w_att_out, proj, gate_col0, layer):
    t = ret.shape[0]
    d = w_ret_out.shape[2]
    tm, tn = MERGE_TM, MERGE_TN
    gblk = gate_col0 // tn
    return pl.pallas_call(
        _merge_kernel,
        out_shape=jax.ShapeDtypeStruct((t, d), BF16),
        grid=(d // tn, t // tm),
        in_specs=[
            pl.BlockSpec((tm, ret.shape[1]), lambda j, i: (i, 0)),
            pl.BlockSpec((tm, att.shape[1]), lambda j, i: (i, 0)),
            pl.BlockSpec((None, ret.shape[1], tn), lambda j, i: (layer, 0, j)),
            pl.BlockSpec((None, att.shape[1], tn), lambda j, i: (layer, 0, j)),
            pl.BlockSpec((tm, tn), lambda j, i: (i, gblk + j)),
            pl.BlockSpec((tm, tn), lambda j, i: (i, gblk + d // tn + j)),
        ],
        out_specs=pl.BlockSpec((tm, tn), lambda j, i: (i, j)),
        compiler_params=_params(("arbitrary", "arbitrary")),
        name="merge",
    )(ret, att, w_ret_out, w_att_out, proj, proj)


def _out_proj_kernel(a_ref, w_ref, x_ref, gate_ref, o_ref):
    acc = jnp.dot(a_ref[...], w_ref[...].astype(BF16), preferred_element_type=F32)
    o_ref[...] = x_ref[...] + gate_ref[...] * acc


def _out_proj(a, w, x2, mod3, layer, seq, gate_idx):
    t, k = a.shape
    d = w.shape[2]
    tm, tn = OUT_TM, OUT_TN
    return pl.pallas_call(
        _out_proj_kernel,
        out_shape=jax.ShapeDtypeStruct((t, d), F32),
        grid=(d // tn, t // tm),
        in_specs=[
            pl.BlockSpec((tm, k), lambda j, i: (i, 0)),
            pl.BlockSpec((None, k, tn), lambda j, i: (layer, 0, j)),
            pl.BlockSpec((tm, tn), lambda j, i: (i, j)),
            pl.BlockSpec((None, 1, tn), lambda j, i: ((i * tm // seq) * N_MOD + gate_idx, 0, j)),
        ],
        out_specs=pl.BlockSpec((tm, tn), lambda j, i: (i, j)),
        compiler_params=_params(("arbitrary", "arbitrary")),
        name="out_proj",
    )(a, w, x2, mod3)


def _routing(logits):
    t = logits.shape[0]
    bm = EXPERT_BM
    a = t * TOP_K
    top_val, top_idx = lax.top_k(logits, TOP_K)
    gate = jax.nn.softmax(top_val, axis=-1)
    onehot = top_idx[:, :, None] == jnp.arange(N_EXPERTS, dtype=jnp.int32)[None, None, :]
    picked = jnp.any(onehot, axis=1).astype(jnp.int32)
    inclusive = jnp.cumsum(picked, axis=0)
    sizes = inclusive[-1]
    padded = (sizes + bm - 1) // bm * bm
    pad_end = jnp.cumsum(padded)
    pad_start = pad_end - padded
    start = jnp.cumsum(sizes) - sizes
    slot = (pad_start[None, :] + inclusive - picked)[:, None, :]
    dest = jnp.sum(jnp.where(onehot, slot, 0), axis=-1).astype(jnp.int32).reshape(a)
    order = jnp.argsort(top_idx.reshape(a))
    n_blocks = -(-a // bm) + N_EXPERTS
    block_row0 = jnp.arange(n_blocks, dtype=jnp.int32) * bm
    block_e = jnp.minimum(jnp.searchsorted(pad_end, block_row0, side='right'), N_EXPERTS - 1).astype(jnp.int32)
    block_valid = block_row0 < pad_end[-1]
    block_rows = jnp.clip(sizes[block_e] - (block_row0 - pad_start[block_e]), 0, bm)
    block_nsub = jnp.where(block_valid, (block_rows + EXPERT_SUB - 1) // EXPERT_SUB, 0).astype(jnp.int32)
    block_rows = jnp.where(block_valid, block_rows, 0)
    sorted_tok = jnp.concatenate([(order // TOP_K).astype(jnp.int32), jnp.zeros((bm,), jnp.int32)])
    first = jnp.where(block_valid, start[block_e] + block_row0 - pad_start[block_e], 0)
    tok = jax.vmap(lambda s: lax.dynamic_slice(sorted_tok, (s,), (bm,)))(first)
    row_tok = jnp.where(jnp.arange(bm, dtype=jnp.int32)[None, :] < block_rows[:, None], tok, 0)
    return gate, dest, row_tok, block_e, block_nsub


def _expert_kernel(be_ref, ns_ref, tok0_ref, tokn_ref, h_hbm, w1_ref, b1_ref, w2a_ref, w2b_ref, b2a_ref, b2b_ref,
                   perm_ref, o_ref, gbuf, x_scr, act_scr, sem, *, nj1, issue_steps):
    b = pl.program_id(0)
    j = pl.program_id(1)
    nb = pl.num_programs(0)
    nsub = ns_ref[b]
    sub = EXPERT_SUB
    group = GATHER_GROUP
    bm, d = x_scr.shape
    half = d // 2
    tf = act_scr.shape[2]
    per_step = bm // issue_steps
    rows_next = jnp.where(b + 1 < nb, ns_ref[jnp.minimum(b + 1, nb - 1)], 0) * sub

    def row_copy(tok_ref, r):
        return pltpu.make_async_copy(h_hbm.at[pl.ds(tok_ref[0, r], 1)], gbuf.at[pl.ds(r, 1)], sem)

    def sub_rows(s):
        return pl.ds(pl.multiple_of(s * sub, sub), sub)

    @pl.when((j == 0) & (b == 0))
    def _():
        def start(r, carry):
            row_copy(tok0_ref, r).start()
            return carry
        lax.fori_loop(0, nsub * sub, start, 0)

    @pl.when(j == 0)
    def _():
        def wait_rows(s, carry):
            pltpu.make_async_copy(h_hbm.at[pl.ds(0, sub)], gbuf.at[pl.ds(0, sub)], sem).wait()
            return carry
        lax.fori_loop(0, nsub, wait_rows, 0)

        def unpack_rows(s, carry):
            xa, xb = _unpack_halves(gbuf[sub_rows(s), :])
            x_scr[sub_rows(s), pl.ds(0, half)] = xa.astype(BF16)
            x_scr[sub_rows(s), pl.ds(half, half)] = xb.astype(BF16)
            return carry
        lax.fori_loop(0, nsub, unpack_rows, 0)

    @pl.when(j < issue_steps)
    def _():
        for g in range(per_step // group):
            row0 = j * per_step + g * group

            @pl.when(row0 < rows_next)
            def _():
                for i in range(group):
                    row_copy(tokn_ref, row0 + i).start()

    def for_real_rows(fn):
        def pair(p, carry):
            fn(pl.ds(pl.multiple_of(p * 2 * sub, 2 * sub), 2 * sub))
            return carry
        lax.fori_loop(0, nsub // 2, pair, 0)

        @pl.when(nsub % 2 == 1)
        def _():
            fn(sub_rows(nsub - 1))

    def first_matmul(rows):
        hb = jnp.dot(x_scr[rows, :], w1_ref[...].astype(BF16), preferred_element_type=F32) + b1_ref[...]
        hb = hb.astype(BF16)
        gw = perm_ref.shape[0]
        sel = [jnp.dot(hb[:, c * gw:(c + 1) * gw], perm_ref[...], preferred_element_type=F32)
               for c in range(2 * tf // gw)]
        x_glu = jnp.minimum(jnp.concatenate([s[:, :gw // 2] for s in sel], axis=1), SWIGLU_LIMIT)
        x_lin = jnp.clip(jnp.concatenate([s[:, gw // 2:] for s in sel], axis=1), -SWIGLU_LIMIT, SWIGLU_LIMIT)
        act = x_glu * jax.nn.sigmoid(SWIGLU_ALPHA * x_glu) * (x_lin + 1.0)
        act_scr[jnp.minimum(j, nj1 - 1), rows, :] = act.astype(BF16)

    def second_matmul(rows):
        act = jnp.concatenate([act_scr[t, rows, :] for t in range(nj1)], axis=1)
        w2 = jnp.concatenate([w2a_ref[...], w2b_ref[...]], axis=1).astype(BF16)
        y = jnp.dot(act, w2, preferred_element_type=F32)
        y = y + jnp.concatenate([b2a_ref[...], b2b_ref[...]], axis=1)
        o_ref[rows, :] = _pack_halves(y)

    @pl.when(j < nj1)
    def _():
        for_real_rows(first_matmul)

    @pl.when(j >= nj1)
    def _():
        for_real_rows(second_matmul)

        def clear(s, carry):
            o_ref[sub_rows(s), :] = jnp.zeros((sub, o_ref.shape[1]), o_ref.dtype)
            return carry
        lax.fori_loop(nsub, bm // sub, clear, 0)


def _deinterleave_matrix(tf):
    p = np.zeros((2 * tf, 2 * tf), np.float32)
    f = np.arange(tf)
    p[2 * f, f] = 1.0
    p[2 * f + 1, tf + f] = 1.0
    return jnp.asarray(p, BF16)


def _experts(h_packed, row_tok, block_e, block_nsub, w1, b1, w2, b2, layer):
    half = h_packed.shape[1]
    d = 2 * half
    f = w2.shape[2]
    bm, tf, tn = EXPERT_BM, EXPERT_TF, EXPERT_TN
    n_blocks = block_e.shape[0]
    nj1 = f // tf
    nj2 = half // tn
    issue_steps = EXPERT_ISSUE_STEPS
    assert issue_steps <= nj1 + nj2 and bm % (issue_steps * GATHER_GROUP) == 0 and bm % EXPERT_SUB == 0
    assert h_packed.shape[0] >= EXPERT_SUB

    def w1_idx(b, j, be, ns):
        nxt = jnp.minimum(b + 1, n_blocks - 1)
        to_next = (j >= nj1) & (ns[nxt] > 0)
        e = jnp.where(to_next, be[nxt], be[b])
        tile = jnp.where(to_next, 0, jnp.where(ns[b] > 0, jnp.minimum(j, nj1 - 1), nj1 - 1))
        return e, tile

    def w2_tile(b, j, ns):
        return jnp.where(ns[b] > 0, jnp.maximum(j - nj1, 0), nj2 - 1)

    tok3 = row_tok.reshape(n_blocks, 1, bm)
    b1r = b1.reshape(b1.shape[0], b1.shape[1], 1, 2 * f)
    b2r = b2.reshape(b2.shape[0], b2.shape[1], 1, d)
    return pl.pallas_call(
        functools.partial(_expert_kernel, nj1=nj1, issue_steps=issue_steps),
        out_shape=jax.ShapeDtypeStruct((n_blocks * bm, half), U32),
        grid_spec=pltpu.PrefetchScalarGridSpec(
            num_scalar_prefetch=2,
            grid=(n_blocks, nj1 + nj2),
            in_specs=[
                pl.BlockSpec((None, 1, bm), lambda b, j, be, ns: (0, 0, 0), memory_space=pltpu.SMEM),
                pl.BlockSpec((None, 1, bm), lambda b, j, be, ns: (jnp.minimum(b + 1, n_blocks - 1), 0, 0),
                             memory_space=pltpu.SMEM),
                pl.BlockSpec(memory_space=pl.ANY),
                pl.BlockSpec((None, None, d, 2 * tf),
                             lambda b, j, be, ns: (layer, w1_idx(b, j, be, ns)[0], 0, w1_idx(b, j, be, ns)[1])),
                pl.BlockSpec((None, None, 1, 2 * tf),
                             lambda b, j, be, ns: (layer, be[b], 0, jnp.minimum(j, nj1 - 1))),
                pl.BlockSpec((None, None, f, tn), lambda b, j, be, ns: (layer, be[b], 0, w2_tile(b, j, ns))),
                pl.BlockSpec((None, None, f, tn), lambda b, j, be, ns: (layer, be[b], 0, nj2 + w2_tile(b, j, ns))),
                pl.BlockSpec((None, None, 1, tn), lambda b, j, be, ns: (layer, be[b], 0, w2_tile(b, j, ns))),
                pl.BlockSpec((None, None, 1, tn), lambda b, j, be, ns: (layer, be[b], 0, nj2 + w2_tile(b, j, ns))),
                pl.BlockSpec((EXPERT_PERM_W, EXPERT_PERM_W), lambda b, j, be, ns: (0, 0)),
            ],
            out_specs=pl.BlockSpec((bm, tn), lambda b, j, be, ns: (b, jnp.maximum(j - nj1, 0))),
            scratch_shapes=[
                pltpu.VMEM((bm, half), U32),
                pltpu.VMEM((bm, d), BF16),
                pltpu.VMEM((nj1, bm, tf), BF16),
                pltpu.SemaphoreType.DMA(()),
            ],
        ),
        compiler_params=_params(("arbitrary", "arbitrary")),
        name="experts",
    )(block_e, block_nsub, tok3, tok3, h_packed, w1, b1r, w2, w2, b2r, b2r,
      _deinterleave_matrix(EXPERT_PERM_W // 2))


def _combine_kernel(dest_ref, ys_hbm, x_ref, gate_ref, g_ref, o_ref, buf, sem):
    tm = x_ref.shape[0]
    half = buf.shape[2]

    def row_copy(src_row, k, r):
        return pltpu.make_async_copy(ys_hbm.at[pl.ds(src_row, 1)], buf.at[k, pl.ds(r, 1)], sem)

    def start(r, carry):
        for k in range(TOP_K):
            row_copy(dest_ref[0, r * TOP_K + k], k, r).start(priority=k % 2)
        return carry

    def wait(r, carry):
        for k in range(TOP_K):
            row_copy(0, k, r).wait()
        return carry

    lax.fori_loop(0, tm, start, 0)
    lax.fori_loop(0, tm, wait, 0)
    sublanes = 8

    def chunk(c, carry):
        rows = pl.ds(pl.multiple_of(c * sublanes, sublanes), sublanes)
        g = g_ref[rows, :]
        ya = jnp.zeros((sublanes, half), F32)
        yb = jnp.zeros((sublanes, half), F32)
        for k in range(TOP_K):
            a, b = _unpack_halves(buf[k, rows, :])
            ya = ya + g[:, k:k + 1] * a
            yb = yb + g[:, k:k + 1] * b
        o_ref[rows, pl.ds(0, half)] = x_ref[rows, pl.ds(0, half)] + gate_ref[:, pl.ds(0, half)] * ya
        o_ref[rows, pl.ds(half, half)] = x_ref[rows, pl.ds(half, half)] + gate_ref[:, pl.ds(half, half)] * yb
        return carry

    lax.fori_loop(0, tm // sublanes, chunk, 0)


def _combine(ys, dest, gates, x2, mod3, seq, gate_idx):
    t, d = x2.shape
    tm = COMBINE_TM
    half = ys.shape[1]
    return pl.pallas_call(
        _combine_kernel,
        out_shape=jax.ShapeDtypeStruct((t, d), F32),
        grid=(t // tm,),
        in_specs=[
            pl.BlockSpec((None, 1, tm * TOP_K), lambda i: (i, 0, 0), memory_space=pltpu.SMEM),
            pl.BlockSpec(memory_space=pl.ANY),
            pl.BlockSpec((tm, d), lambda i: (i, 0)),
            pl.BlockSpec((None, 1, d), lambda i: ((i * tm // seq) * N_MOD + gate_idx, 0, 0)),
            pl.BlockSpec((tm, TOP_K), lambda i: (i, 0)),
        ],
        out_specs=pl.BlockSpec((tm, d), lambda i: (i, 0)),
        scratch_shapes=[pltpu.VMEM((TOP_K, tm, half), U32), pltpu.SemaphoreType.DMA(())],
        compiler_params=_params(("arbitrary",)),
        name="combine",
    )(dest.reshape(t // tm, 1, tm * TOP_K), ys, x2, mod3, gates)


def kernel(x, c, w_ada, b_ada, norm1_g, w_in, q_norm_g, k_norm_g, rel_bias, w_ret_out, w_att_out, w_out,
           norm2_g, w_router, b_router, w1, b1, w2, b2):
    batch, seq, d = x.shape
    depth = w_ada.shape[0]
    t = batch * seq
    assert batch <= 8 and seq % max(RET_BLOCK, ATT_BLOCK, PROJ_TM, OUT_TM) == 0
    x2 = x.reshape(t, d)
    c8 = jnp.zeros((8, d), F32).at[:batch].set(c)
    ret_col0 = 0
    att_col0 = 2 * RET_QK_W + 2 * RET_V_W
    gate_col0 = att_col0 + 3 * ATT_W
    for layer in range(depth):
        mod3 = _ada_ln(c8, w_ada, b_ada, layer)[:batch].reshape(batch * N_MOD, 1, d)
        h = _norm_mod(x2, norm1_g.reshape(depth, 1, d), mod3, layer, seq, shift_idx=0, scale_idx=1)
        proj = _project(h, w_in, layer)
        ret = _retention(proj, batch, seq, ret_col0)
        att = _chunk_attention(proj, rel_bias[layer], q_norm_g[layer], k_norm_g[layer], batch, seq, att_col0)
        merged = _merge(ret, att, w_ret_out, w_att_out, proj, gate_col0, layer)
        x2 = _out_proj(merged, w_out, x2, mod3, layer, seq, gate_idx=2)
        h_packed, logits = _norm_router(x2, norm2_g.reshape(depth, 1, d), mod3, w_router, b_router, layer, seq,
                                        shift_idx=3, scale_idx=4)
        gates, dest, row_tok, block_e, block_nsub = _routing(logits)
        ys = _experts(h_packed, row_tok, block_e, block_nsub, w1, b1, w2, b2, layer)
        x2 = _combine(ys, dest, gates, x2, mod3, seq, gate_idx=5)
    return x2.reshape(batch, seq, d)
```

```python
import functools

import numpy as np
import jax
import jax.numpy as jnp
from jax import lax
from jax.experimental import pallas as pl
from jax.experimental.pallas import tpu as pltpu

F32 = jnp.float32
BF16 = jnp.bfloat16
U32 = jnp.uint32

CHUNK = 64
NORM_EPS = 1e-6
RET_HEADS = 8
RET_QK_DIM = 256
RET_V_DIM = 512
ROPE_BASE = 10000.0
ATT_HEADS = 16
ATT_HEAD_DIM = 128
PREV_CHUNKS = 8
MAX_REL = 128
N_EXPERTS = 32
TOP_K = 4
SWIGLU_ALPHA = 1.702
SWIGLU_LIMIT = 7.0
N_MOD = 6

RET_QK_W = RET_HEADS * RET_QK_DIM
RET_V_W = RET_HEADS * RET_V_DIM
ATT_W = ATT_HEADS * ATT_HEAD_DIM

V7X_VMEM_LIMIT_BYTES = 58 * 1024 * 1024
HI16 = 0xFFFF0000

ADA_TN = 1024
NORM_TM = 256
PROJ_TM, PROJ_TN = 2048, 512
RET_BLOCK = 256
RET_HEADS_PER_STEP = 2
ATT_BLOCK = 256
ATT_HEADS_PER_STEP = 2
MERGE_TM, MERGE_TN = 512, 512
OUT_TM, OUT_TN = 1024, 512
EXPERT_BM = 1280
EXPERT_SUB = 256
EXPERT_TF = 256
EXPERT_TN = 256
EXPERT_PERM_W = 256
EXPERT_ISSUE_STEPS = 10
GATHER_GROUP = 32
COMBINE_TM = 256


def _params(semantics):
    return pltpu.CompilerParams(dimension_semantics=semantics, vmem_limit_bytes=V7X_VMEM_LIMIT_BYTES)


def _silu(t):
    return t * jax.nn.sigmoid(t)


def _pack_halves(y):
    n = y.shape[1] // 2
    hi = lax.bitcast_convert_type(y[:, :n].astype(BF16).astype(F32), U32)
    lo = lax.bitcast_convert_type(y[:, n:].astype(BF16).astype(F32), U32)
    return hi | (lo >> 16)


def _unpack_halves(p):
    a = lax.bitcast_convert_type(p & jnp.uint32(HI16), F32)
    b = lax.bitcast_convert_type(p << 16, F32)
    return a, b


def _ada_kernel(c_ref, w_ref, b_ref, o_ref):
    a = _silu(c_ref[...]).astype(BF16)
    o_ref[...] = jnp.dot(a, w_ref[...].astype(BF16), preferred_element_type=F32) + b_ref[...]


def _ada_ln(c8, w_ada, b_ada, layer):
    d = c8.shape[1]
    n = w_ada.shape[2]
    return pl.pallas_call(
        _ada_kernel,
        out_shape=jax.ShapeDtypeStruct((8, n), F32),
        grid=(n // ADA_TN,),
        in_specs=[
            pl.BlockSpec((8, d), lambda j: (0, 0)),
            pl.BlockSpec((None, d, ADA_TN), lambda j: (layer, 0, j)),
            pl.BlockSpec((None, 1, ADA_TN), lambda j: (layer, 0, j)),
        ],
        out_specs=pl.BlockSpec((8, ADA_TN), lambda j: (0, j)),
        compiler_params=_params(("arbitrary",)),
        name="ada_ln",
    )(c8, w_ada, b_ada.reshape(b_ada.shape[0], 1, n))


def _modulated_norm(x, g, scale, shift):
    xn = x * lax.rsqrt(jnp.mean(x * x, axis=-1, keepdims=True) + NORM_EPS)
    return (xn * g) * (1.0 + scale) + shift


def _norm_mod_kernel(x_ref, g_ref, sc_ref, sh_ref, o_ref):
    o_ref[...] = _modulated_norm(x_ref[...], g_ref[...], sc_ref[...], sh_ref[...]).astype(o_ref.dtype)


def _norm_mod(x2, g, mod3, layer, seq, shift_idx, scale_idx):
    t, d = x2.shape
    tm = NORM_TM

    def mod_spec(idx):
        return pl.BlockSpec((None, 1, d), lambda i: ((i * tm // seq) * N_MOD + idx, 0, 0))

    return pl.pallas_call(
        _norm_mod_kernel,
        out_shape=jax.ShapeDtypeStruct((t, d), BF16),
        grid=(t // tm,),
        in_specs=[
            pl.BlockSpec((tm, d), lambda i: (i, 0)),
            pl.BlockSpec((None, 1, d), lambda i: (layer, 0, 0)),
            mod_spec(scale_idx),
            mod_spec(shift_idx),
        ],
        out_specs=pl.BlockSpec((tm, d), lambda i: (i, 0)),
        compiler_params=_params(("arbitrary",)),
        name="norm_mod",
    )(x2, g, mod3, mod3)


def _norm_router_kernel(x_ref, g_ref, sc_ref, sh_ref, wr_ref, br_ref, h_ref, lg_ref):
    h = _modulated_norm(x_ref[...], g_ref[...], sc_ref[...], sh_ref[...])
    h_ref[...] = _pack_halves(h)
    h_hi = h.astype(BF16)
    h_lo = (h - h_hi.astype(F32)).astype(BF16)
    w = wr_ref[...]
    w_hi = w.astype(BF16)
    w_lo = (w - w_hi.astype(F32)).astype(BF16)
    lg = (jnp.dot(h_hi, w_hi, preferred_element_type=F32)
          + jnp.dot(h_lo, w_hi, preferred_element_type=F32)
          + jnp.dot(h_hi, w_lo, preferred_element_type=F32))
    lg_ref[...] = lg + br_ref[...]


def _norm_router(x2, g, mod3, w_router, b_router, layer, seq, shift_idx, scale_idx):
    t, d = x2.shape
    e = w_router.shape[2]
    tm = NORM_TM

    def mod_spec(idx):
        return pl.BlockSpec((None, 1, d), lambda i: ((i * tm // seq) * N_MOD + idx, 0, 0))

    return pl.pallas_call(
        _norm_router_kernel,
        out_shape=(jax.ShapeDtypeStruct((t, d // 2), U32), jax.ShapeDtypeStruct((t, e), F32)),
        grid=(t // tm,),
        in_specs=[
            pl.BlockSpec((tm, d), lambda i: (i, 0)),
            pl.BlockSpec((None, 1, d), lambda i: (layer, 0, 0)),
            mod_spec(scale_idx),
            mod_spec(shift_idx),
            pl.BlockSpec((None, d, e), lambda i: (layer, 0, 0)),
            pl.BlockSpec((None, 1, e), lambda i: (layer, 0, 0)),
        ],
        out_specs=(pl.BlockSpec((tm, d // 2), lambda i: (i, 0)), pl.BlockSpec((tm, e), lambda i: (i, 0))),
        compiler_params=_params(("arbitrary",)),
        name="norm_router",
    )(x2, g, mod3, mod3, w_router, b_router.reshape(b_router.shape[0], 1, e))


def _proj_kernel(a_ref, w_ref, o_ref):
    o_ref[...] = jnp.dot(a_ref[...], w_ref[...].astype(BF16), preferred_element_type=F32).astype(o_ref.dtype)


def _project(a, w, layer):
    t, k = a.shape
    n = w.shape[2]
    tm, tn = PROJ_TM, PROJ_TN
    return pl.pallas_call(
        _proj_kernel,
        out_shape=jax.ShapeDtypeStruct((t, n), BF16),
        grid=(n // tn, t // tm),
        in_specs=[
            pl.BlockSpec((tm, k), lambda j, i: (i, 0)),
            pl.BlockSpec((None, k, tn), lambda j, i: (layer, 0, j)),
        ],
        out_specs=pl.BlockSpec((tm, tn), lambda j, i: (i, j)),
        compiler_params=_params(("arbitrary", "arbitrary")),
        name="in_proj",
    )(a, w)


def _retention_kernel(q_ref, k_ref, v_ref, g_ref, cos_ref, sin_ref, dec_ref, qd_ref, kd_ref, cd_ref,
                      o_ref, state_ref):
    @pl.when(pl.program_id(2) == 0)
    def _():
        state_ref[...] = jnp.zeros_like(state_ref)

    cos = cos_ref[...]
    sin = sin_ref[...]
    half = RET_QK_DIM // 2
    contract_last = (((1,), (1,)), ((), ()))
    contract_rows = (((0,), (0,)), ((), ()))

    def rotary(t):
        t1, t2 = t[:, :half], t[:, half:]
        return jnp.concatenate([t1 * cos - t2 * sin, t1 * sin + t2 * cos], axis=1)

    for hh in range(RET_HEADS_PER_STEP):
        qk = slice(hh * RET_QK_DIM, (hh + 1) * RET_QK_DIM)
        vg = slice(hh * RET_V_DIM, (hh + 1) * RET_V_DIM)
        q = rotary(q_ref[:, qk].astype(F32))
        k = rotary(k_ref[:, qk].astype(F32)) * (RET_QK_DIM ** -0.5)
        v = v_ref[:, vg]
        scores = lax.dot_general(q.astype(BF16), k.astype(BF16), contract_last,
                                 preferred_element_type=F32) * dec_ref[hh]
        state = state_ref[hh]
        out = (jnp.dot(scores.astype(BF16), v, preferred_element_type=F32)
               + jnp.dot((q * qd_ref[hh]).astype(BF16), state.astype(BF16), preferred_element_type=F32))
        state_ref[hh] = state * cd_ref[hh] + lax.dot_general(
            (k * kd_ref[hh]).astype(BF16), v, contract_rows, preferred_element_type=F32)
        on = out * lax.rsqrt(jnp.mean(out * out, axis=-1, keepdims=True) + NORM_EPS)
        o_ref[:, vg] = (_silu(g_ref[:, vg].astype(F32)) * on).astype(o_ref.dtype)


def _retention_tables(seq):
    half = RET_QK_DIM // 2
    inv_freq = ROPE_BASE ** (-np.arange(half, dtype=np.float64) / half)
    ang = np.arange(seq, dtype=np.float64)[:, None] * inv_freq[None, :]
    log_gamma = np.log(1.0 - 2.0 ** (-5.0 - np.arange(RET_HEADS, dtype=np.float64)))[:, None, None]
    pos = np.arange(RET_BLOCK, dtype=np.float64)
    n, m = pos[:, None], pos[None, :]
    cn, cm = np.floor(n / CHUNK), np.floor(m / CHUNK)
    dist = np.where(cn == cm, np.abs(n - m), n - m)
    dec = np.where((cm <= cn)[None], np.exp(log_gamma * dist[None]), 0.0)
    qd = np.broadcast_to(np.exp(log_gamma * (pos[None, :, None] + 1.0)), (RET_HEADS, RET_BLOCK, RET_QK_DIM))
    kd = np.broadcast_to(np.exp(log_gamma * (RET_BLOCK - 1.0 - pos[None, :, None])),
                         (RET_HEADS, RET_BLOCK, RET_QK_DIM))
    cd = np.broadcast_to(np.exp(log_gamma * RET_BLOCK), (RET_HEADS, 1, RET_V_DIM))
    return tuple(jnp.asarray(np.ascontiguousarray(v), F32) for v in (np.cos(ang), np.sin(ang), dec, qd, kd, cd))


def _retention(proj, batch, seq, col0):
    t = proj.shape[0]
    r = RET_BLOCK
    nsb = seq // r
    cos, sin, dec, qd, kd, cd = _retention_tables(seq)
    hp = RET_HEADS_PER_STEP
    groups = RET_HEADS // hp
    qk_w, v_w = hp * RET_QK_DIM, hp * RET_V_DIM
    assert col0 % qk_w == 0 and (col0 + 2 * RET_QK_W) % v_w == 0
    qk_blk0 = col0 // qk_w
    v_blk0 = (col0 + 2 * RET_QK_W) // v_w
    row = lambda b, h, s: b * nsb + s
    head_tab = lambda shape: pl.BlockSpec((hp,) + shape, lambda b, h, s: (h, 0, 0))
    half = RET_QK_DIM // 2
    return pl.pallas_call(
        _retention_kernel,
        out_shape=jax.ShapeDtypeStruct((t, RET_V_W), BF16),
        grid=(batch, groups, nsb),
        in_specs=[
            pl.BlockSpec((r, qk_w), lambda b, h, s: (row(b, h, s), qk_blk0 + h)),
            pl.BlockSpec((r, qk_w), lambda b, h, s: (row(b, h, s), qk_blk0 + groups + h)),
            pl.BlockSpec((r, v_w), lambda b, h, s: (row(b, h, s), v_blk0 + h)),
            pl.BlockSpec((r, v_w), lambda b, h, s: (row(b, h, s), v_blk0 + groups + h)),
            pl.BlockSpec((r, half), lambda b, h, s: (s, 0)),
            pl.BlockSpec((r, half), lambda b, h, s: (s, 0)),
            head_tab((r, r)),
            head_tab((r, RET_QK_DIM)),
            head_tab((r, RET_QK_DIM)),
            head_tab((1, RET_V_DIM)),
        ],
        out_specs=pl.BlockSpec((r, v_w), lambda b, h, s: (row(b, h, s), h)),
        scratch_shapes=[pltpu.VMEM((hp, RET_QK_DIM, RET_V_DIM), F32)],
        compiler_params=_params(("arbitrary", "arbitrary", "arbitrary")),
        name="retention",
    )(proj, proj, proj, proj, cos, sin, dec, qd, kd, cd)


ATT_KEY_BLOCKS = PREV_CHUNKS * CHUNK // ATT_BLOCK + 1
NEG = float(np.finfo(np.float32).min)


def _attention_kernel(q_ref, k0_ref, k1_ref, k2_ref, v0_ref, v1_ref, v2_ref, tab_ref, gq_ref, gk_ref, o_ref):
    i = pl.program_id(1)

    def rms(t, g):
        tf = t.astype(F32)
        return (tf * lax.rsqrt(jnp.mean(tf * tf, axis=-1, keepdims=True) + NORM_EPS)) * g

    dh = ATT_HEAD_DIM
    k_all = jnp.concatenate([k0_ref[...], k1_ref[...], k2_ref[...]], axis=0)
    v_all = jnp.concatenate([v0_ref[...], v1_ref[...], v2_ref[...]], axis=0)
    outs = []
    for hh in range(ATT_HEADS_PER_STEP):
        lanes = slice(hh * dh, (hh + 1) * dh)
        q = rms(q_ref[:, lanes], gq_ref[...]).astype(BF16)
        k = rms(k_all[:, lanes], gk_ref[...]).astype(BF16)
        sc = lax.dot_general(q, k, (((1,), (1,)), ((), ())), preferred_element_type=F32)
        sc = sc * (ATT_HEAD_DIM ** -0.5) + tab_ref[hh]
        col = lax.broadcasted_iota(jnp.int32, sc.shape, 1)
        sc = jnp.where(col >= (ATT_KEY_BLOCKS - 1 - i) * ATT_BLOCK, sc, NEG)
        p = jnp.exp(sc - jnp.max(sc, axis=-1, keepdims=True))
        denom = jnp.sum(p, axis=-1, keepdims=True)
        outs.append(jnp.dot(p.astype(BF16), v_all[:, lanes], preferred_element_type=F32) / denom)
    o_ref[...] = jnp.concatenate(outs, axis=1).astype(o_ref.dtype)


def _attention_table(rel_bias):
    a = ATT_BLOCK
    w = ATT_KEY_BLOCKS * a
    pad = PREV_CHUNKS * CHUNK
    heads = rel_bias.shape[0]
    period = 1024
    assert a - 1 + w - 1 <= period - 2
    rel_of_i = pad + a - 1 - np.arange(period)
    idx = np.clip(np.minimum(rel_of_i, MAX_REL) + (CHUNK - 1), 0, rel_bias.shape[-1] - 1)
    u = rel_bias[:, idx].astype(F32)
    skew = jnp.tile(u, (1, a))[:, :a * (period - 1)].reshape(heads, a, period - 1)
    bias = skew[:, :, a - 1:a - 1 + w]
    n = np.arange(a)[:, None]
    j = np.arange(w)[None, :]
    qc = n // CHUNK
    kc = j // CHUNK - PREV_CHUNKS
    valid = (kc <= qc) & (kc >= qc - PREV_CHUNKS)
    return jnp.where(jnp.asarray(valid)[None], bias, NEG)


def _chunk_attention(proj, rel_bias, q_norm_g, k_norm_g, batch, seq, col0):
    assert ATT_KEY_BLOCKS == 3
    t = proj.shape[0]
    a = ATT_BLOCK
    nqb = seq // a
    dh = ATT_HEAD_DIM
    hw = ATT_HEADS_PER_STEP * dh
    groups = ATT_HEADS // ATT_HEADS_PER_STEP
    assert col0 % hw == 0 and ATT_W % hw == 0
    blk0 = col0 // hw
    tab = _attention_table(rel_bias)

    def kv_spec(which, jj):
        return pl.BlockSpec(
            (a, hw),
            lambda h, i, b: (b * nqb + jnp.maximum(i - (ATT_KEY_BLOCKS - 1) + jj, 0), blk0 + which * groups + h))

    return pl.pallas_call(
        _attention_kernel,
        out_shape=jax.ShapeDtypeStruct((t, ATT_W), BF16),
        grid=(groups, nqb, batch),
        in_specs=[
            pl.BlockSpec((a, hw), lambda h, i, b: (b * nqb + i, blk0 + h)),
            kv_spec(1, 0), kv_spec(1, 1), kv_spec(1, 2),
            kv_spec(2, 0), kv_spec(2, 1), kv_spec(2, 2),
            pl.BlockSpec((ATT_HEADS_PER_STEP, a, ATT_KEY_BLOCKS * a), lambda h, i, b: (h, 0, 0)),
            pl.BlockSpec((1, dh), lambda h, i, b: (0, 0)),
            pl.BlockSpec((1, dh), lambda h, i, b: (0, 0)),
        ],
        out_specs=pl.BlockSpec((a, hw), lambda h, i, b: (b * nqb + i, h)),
        compiler_params=_params(("arbitrary", "arbitrary", "arbitrary")),
        name="chunk_attention",
    )(proj, proj, proj, proj, proj, proj, proj, tab, q_norm_g.reshape(1, dh), k_norm_g.reshape(1, dh))


def _merge_kernel(r_ref, a_ref, wr_ref, wa_ref, gr_ref, ga_ref, o_ref):
    yr = jnp.dot(r_ref[...], wr_ref[...].astype(BF16), preferred_element_type=F32)
    ya = jnp.dot(a_ref[...], wa_ref[...].astype(BF16), preferred_element_type=F32)
    merged = jax.nn.sigmoid(gr_ref[...].astype(F32)) * yr + jax.nn.sigmoid(ga_ref[...].astype(F32)) * ya
    o_ref[...] = merged.astype(o_ref.dtype)


def _merge(ret, att, w_ret_out, w_att_out, proj, gate_col0, layer):
    t = ret.shape[0]
    d = w_ret_out.shape[2]
    tm, tn = MERGE_TM, MERGE_TN
    gblk = gate_col0 // tn
    return pl.pallas_call(
        _merge_kernel,
        out_shape=jax.ShapeDtypeStruct((t, d), BF16),
        grid=(d // tn, t // tm),
        in_specs=[
            pl.BlockSpec((tm, ret.shape[1]), lambda j, i: (i, 0)),
            pl.BlockSpec((tm, att.shape[1]), lambda j, i: (i, 0)),
            pl.BlockSpec((None, ret.shape[1], tn), lambda j, i: (layer, 0, j)),
            pl.BlockSpec((None, att.shape[1], tn), lambda j, i: (layer, 0, j)),
            pl.BlockSpec((tm, tn), lambda j, i: (i, gblk + j)),
            pl.BlockSpec((tm, tn), lambda j, i: (i, gblk + d // tn + j)),
        ],
        out_specs=pl.BlockSpec((tm, tn), lambda j, i: (i, j)),
        compiler_params=_params(("arbitrary", "arbitrary")),
        name="merge",
    )(ret, att, w_ret_out, w_att_out, proj, proj)


def _out_proj_kernel(a_ref, w_ref, x_ref, gate_ref, o_ref):
    acc = jnp.dot(a_ref[...], w_ref[...].astype(BF16), preferred_element_type=F32)
    o_ref[...] = x_ref[...] + gate_ref[...] * acc


def _out_proj(a, w, x2, mod3, layer, seq, gate_idx):
    t, k = a.shape
    d = w.shape[2]
    tm, tn = OUT_TM, OUT_TN
    return pl.pallas_call(
        _out_proj_kernel,
        out_shape=jax.ShapeDtypeStruct((t, d), F32),
        grid=(d // tn, t // tm),
        in_specs=[
            pl.BlockSpec((tm, k), lambda j, i: (i, 0)),
            pl.BlockSpec((None, k, tn), lambda j, i: (layer, 0, j)),
            pl.BlockSpec((tm, tn), lambda j, i: (i, j)),
            pl.BlockSpec((None, 1, tn), lambda j, i: ((i * tm // seq) * N_MOD + gate_idx, 0, j)),
        ],
        out_specs=pl.BlockSpec((tm, tn), lambda j, i: (i, j)),
        compiler_params=_params(("arbitrary", "arbitrary")),
        name="out_proj",
    )(a, w, x2, mod3)


def _routing(logits):
    t = logits.shape[0]
    bm = EXPERT_BM
    a = t * TOP_K
    top_val, top_idx = lax.top_k(logits, TOP_K)
    gate = jax.nn.softmax(top_val, axis=-1)
    onehot = top_idx[:, :, None] == jnp.arange(N_EXPERTS, dtype=jnp.int32)[None, None, :]
    picked = jnp.any(onehot, axis=1).astype(jnp.int32)
    inclusive = jnp.cumsum(picked, axis=0)
    sizes = inclusive[-1]
    padded = (sizes + bm - 1) // bm * bm
    pad_end = jnp.cumsum(padded)
    pad_start = pad_end - padded
    start = jnp.cumsum(sizes) - sizes
    slot = (pad_start[None, :] + inclusive - picked)[:, None, :]
    dest = jnp.sum(jnp.where(onehot, slot, 0), axis=-1).astype(jnp.int32).reshape(a)
    order = jnp.argsort(top_idx.reshape(a))
    n_blocks = -(-a // bm) + N_EXPERTS
    block_row0 = jnp.arange(n_blocks, dtype=jnp.int32) * bm
    block_e = jnp.minimum(jnp.searchsorted(pad_end, block_row0, side='right'), N_EXPERTS - 1).astype(jnp.int32)
    block_valid = block_row0 < pad_end[-1]
    block_rows = jnp.clip(sizes[block_e] - (block_row0 - pad_start[block_e]), 0, bm)
    block_nsub = jnp.where(block_valid, (block_rows + EXPERT_SUB - 1) // EXPERT_SUB, 0).astype(jnp.int32)
    n_chunks = -(-a // bm) + 2
    sorted_tok = jnp.zeros((n_chunks * bm,), jnp.int32).at[:a].set((order // TOP_K).astype(jnp.int32))
    block_first = jnp.where(block_valid, start[block_e] + block_row0 - pad_start[block_e], 0).astype(jnp.int32)
    return gate, dest, sorted_tok.reshape(n_chunks, 1, bm), block_first, block_e, block_nsub


def _expert_kernel(be_ref, ns_ref, first_ref, tok0_lo, tok0_hi, tokn_lo, tokn_hi, h_hbm, w1_ref, b1_ref,
                   w2a_ref, w2b_ref, b2a_ref, b2b_ref, perm_ref, o_ref, gbuf, x_scr, act_scr, sem,
                   *, nj1, issue_steps):
    b = pl.program_id(0)
    j = pl.program_id(1)
    nb = pl.num_programs(0)
    nsub = ns_ref[b]
    sub = EXPERT_SUB
    group = GATHER_GROUP
    bm, d = x_scr.shape
    half = d // 2
    tf = act_scr.shape[2]
    per_step = bm // issue_steps
    nxt = jnp.minimum(b + 1, nb - 1)
    rows_next = jnp.where(b + 1 < nb, ns_ref[nxt], 0) * sub

    def row_copy(lo_ref, hi_ref, offset, r):
        at = offset + r
        tok = jnp.where(at < bm, lo_ref[0, jnp.minimum(at, bm - 1)], hi_ref[0, jnp.maximum(at - bm, 0)])
        return pltpu.make_async_copy(h_hbm.at[pl.ds(tok, 1)], gbuf.at[pl.ds(r, 1)], sem)

    def sub_rows(s):
        return pl.ds(pl.multiple_of(s * sub, sub), sub)

    @pl.when((j == 0) & (b == 0))
    def _():
        offset = lax.rem(first_ref[0], bm)

        def start(r, carry):
            row_copy(tok0_lo, tok0_hi, offset, r).start()
            return carry
        lax.fori_loop(0, nsub * sub, start, 0)

    @pl.when(j == 0)
    def _():
        def wait_rows(s, carry):
            pltpu.make_async_copy(h_hbm.at[pl.ds(0, sub)], gbuf.at[pl.ds(0, sub)], sem).wait()
            return carry
        lax.fori_loop(0, nsub, wait_rows, 0)

        def unpack_rows(s, carry):
            xa, xb = _unpack_halves(gbuf[sub_rows(s), :])
            x_scr[sub_rows(s), pl.ds(0, half)] = xa.astype(BF16)
            x_scr[sub_rows(s), pl.ds(half, half)] = xb.astype(BF16)
            return carry
        lax.fori_loop(0, nsub, unpack_rows, 0)

    @pl.when(j < issue_steps)
    def _():
        offset = lax.rem(first_ref[nxt], bm)
        for g in range(per_step // group):
            row0 = j * per_step + g * group

            @pl.when(row0 < rows_next)
            def _():
                for i in range(group):
                    row_copy(tokn_lo, tokn_hi, offset, row0 + i).start()

    def for_real_rows(fn):
        def pair(p, carry):
            fn(pl.ds(pl.multiple_of(p * 2 * sub, 2 * sub), 2 * sub))
            return carry
        lax.fori_loop(0, nsub // 2, pair, 0)

        @pl.when(nsub % 2 == 1)
        def _():
            fn(sub_rows(nsub - 1))

    def first_matmul(rows):
        hb = jnp.dot(x_scr[rows, :], w1_ref[...].astype(BF16), preferred_element_type=F32) + b1_ref[...]
        hb = hb.astype(BF16)
        gw = perm_ref.shape[0]
        sel = [jnp.dot(hb[:, c * gw:(c + 1) * gw], perm_ref[...], preferred_element_type=F32)
               for c in range(2 * tf // gw)]
        x_glu = jnp.minimum(jnp.concatenate([s[:, :gw // 2] for s in sel], axis=1), SWIGLU_LIMIT)
        x_lin = jnp.clip(jnp.concatenate([s[:, gw // 2:] for s in sel], axis=1), -SWIGLU_LIMIT, SWIGLU_LIMIT)
        act = x_glu * jax.nn.sigmoid(SWIGLU_ALPHA * x_glu) * (x_lin + 1.0)
        act_scr[jnp.minimum(j, nj1 - 1), rows, :] = act.astype(BF16)

    def second_matmul(rows):
        act = jnp.concatenate([act_scr[t, rows, :] for t in range(nj1)], axis=1)
        w2 = jnp.concatenate([w2a_ref[...], w2b_ref[...]], axis=1).astype(BF16)
        y = jnp.dot(act, w2, preferred_element_type=F32)
        y = y + jnp.concatenate([b2a_ref[...], b2b_ref[...]], axis=1)
        o_ref[rows, :] = _pack_halves(y)

    @pl.when(j < nj1)
    def _():
        for_real_rows(first_matmul)

    @pl.when(j >= nj1)
    def _():
        for_real_rows(second_matmul)

        def clear(s, carry):
            o_ref[sub_rows(s), :] = jnp.zeros((sub, o_ref.shape[1]), o_ref.dtype)
            return carry
        lax.fori_loop(nsub, bm // sub, clear, 0)


def _deinterleave_matrix(tf):
    p = np.zeros((2 * tf, 2 * tf), np.float32)
    f = np.arange(tf)
    p[2 * f, f] = 1.0
    p[2 * f + 1, tf + f] = 1.0
    return jnp.asarray(p, BF16)


def _experts(h_packed, sorted_tok, block_first, block_e, block_nsub, w1, b1, w2, b2, layer):
    half = h_packed.shape[1]
    d = 2 * half
    f = w2.shape[2]
    bm, tf, tn = EXPERT_BM, EXPERT_TF, EXPERT_TN
    n_blocks = block_e.shape[0]
    nj1 = f // tf
    nj2 = half // tn
    issue_steps = EXPERT_ISSUE_STEPS
    assert issue_steps <= nj1 + nj2 and bm % (issue_steps * GATHER_GROUP) == 0 and bm % EXPERT_SUB == 0
    assert h_packed.shape[0] >= EXPERT_SUB

    def w1_idx(b, j, be, ns):
        nxt = jnp.minimum(b + 1, n_blocks - 1)
        to_next = (j >= nj1) & (ns[nxt] > 0)
        e = jnp.where(to_next, be[nxt], be[b])
        tile = jnp.where(to_next, 0, jnp.where(ns[b] > 0, jnp.minimum(j, nj1 - 1), nj1 - 1))
        return e, tile

    def w2_tile(b, j, ns):
        return jnp.where(ns[b] > 0, jnp.maximum(j - nj1, 0), nj2 - 1)

    def tok_chunk(which, plus):
        def index_map(b, j, be, ns, first):
            blk = 0 if which == "first" else jnp.minimum(b + 1, n_blocks - 1)
            return (first[blk] // bm + plus, 0, 0)
        return pl.BlockSpec((None, 1, bm), index_map, memory_space=pltpu.SMEM)

    b1r = b1.reshape(b1.shape[0], b1.shape[1], 1, 2 * f)
    b2r = b2.reshape(b2.shape[0], b2.shape[1], 1, d)
    return pl.pallas_call(
        functools.partial(_expert_kernel, nj1=nj1, issue_steps=issue_steps),
        out_shape=jax.ShapeDtypeStruct((n_blocks * bm, half), U32),
        grid_spec=pltpu.PrefetchScalarGridSpec(
            num_scalar_prefetch=3,
            grid=(n_blocks, nj1 + nj2),
            in_specs=[
                tok_chunk("first", 0), tok_chunk("first", 1), tok_chunk("next", 0), tok_chunk("next", 1),
                pl.BlockSpec(memory_space=pl.ANY),
                pl.BlockSpec((None, None, d, 2 * tf),
                             lambda b, j, be, ns, _: (layer, w1_idx(b, j, be, ns)[0], 0, w1_idx(b, j, be, ns)[1])),
                pl.BlockSpec((None, None, 1, 2 * tf),
                             lambda b, j, be, ns, _: (layer, be[b], 0, jnp.minimum(j, nj1 - 1))),
                pl.BlockSpec((None, None, f, tn), lambda b, j, be, ns, _: (layer, be[b], 0, w2_tile(b, j, ns))),
                pl.BlockSpec((None, None, f, tn),
                             lambda b, j, be, ns, _: (layer, be[b], 0, nj2 + w2_tile(b, j, ns))),
                pl.BlockSpec((None, None, 1, tn), lambda b, j, be, ns, _: (layer, be[b], 0, w2_tile(b, j, ns))),
                pl.BlockSpec((None, None, 1, tn),
                             lambda b, j, be, ns, _: (layer, be[b], 0, nj2 + w2_tile(b, j, ns))),
                pl.BlockSpec((EXPERT_PERM_W, EXPERT_PERM_W), lambda b, j, be, ns, _: (0, 0)),
            ],
            out_specs=pl.BlockSpec((bm, tn), lambda b, j, be, ns, _: (b, jnp.maximum(j - nj1, 0))),
            scratch_shapes=[
                pltpu.VMEM((bm, half), U32),
                pltpu.VMEM((bm, d), BF16),
                pltpu.VMEM((nj1, bm, tf), BF16),
                pltpu.SemaphoreType.DMA(()),
            ],
        ),
        compiler_params=_params(("arbitrary", "arbitrary")),
        name="experts",
    )(block_e, block_nsub, block_first, sorted_tok, sorted_tok, sorted_tok, sorted_tok, h_packed, w1, b1r, w2, w2,
      b2r, b2r, _deinterleave_matrix(EXPERT_PERM_W // 2))


def _combine_kernel(dest_ref, ys_hbm, x_ref, gate_ref, g_ref, o_ref, buf, sem):
    tm = x_ref.shape[0]
    half = buf.shape[2]

    def row_copy(src_row, k, r):
        return pltpu.make_async_copy(ys_hbm.at[pl.ds(src_row, 1)], buf.at[k, pl.ds(r, 1)], sem)

    def start(r, carry):
        for k in range(TOP_K):
            row_copy(dest_ref[0, r * TOP_K + k], k, r).start()
        return carry

    def wait(r, carry):
        for k in range(TOP_K):
            row_copy(0, k, r).wait()
        return carry

    lax.fori_loop(0, tm, start, 0)
    lax.fori_loop(0, tm, wait, 0)
    sublanes = 8

    def chunk(c, carry):
        rows = pl.ds(pl.multiple_of(c * sublanes, sublanes), sublanes)
        g = g_ref[rows, :]
        ya = jnp.zeros((sublanes, half), F32)
        yb = jnp.zeros((sublanes, half), F32)
        for k in range(TOP_K):
            a, b = _unpack_halves(buf[k, rows, :])
            ya = ya + g[:, k:k + 1] * a
            yb = yb + g[:, k:k + 1] * b
        o_ref[rows, pl.ds(0, half)] = x_ref[rows, pl.ds(0, half)] + gate_ref[:, pl.ds(0, half)] * ya
        o_ref[rows, pl.ds(half, half)] = x_ref[rows, pl.ds(half, half)] + gate_ref[:, pl.ds(half, half)] * yb
        return carry

    lax.fori_loop(0, tm // sublanes, chunk, 0)


def _combine(ys, dest, gates, x2, mod3, seq, gate_idx):
    t, d = x2.shape
    tm = COMBINE_TM
    half = ys.shape[1]
    return pl.pallas_call(
        _combine_kernel,
        out_shape=jax.ShapeDtypeStruct((t, d), F32),
        grid=(t // tm,),
        in_specs=[
            pl.BlockSpec((None, 1, tm * TOP_K), lambda i: (i, 0, 0), memory_space=pltpu.SMEM),
            pl.BlockSpec(memory_space=pl.ANY),
            pl.BlockSpec((tm, d), lambda i: (i, 0)),
            pl.BlockSpec((None, 1, d), lambda i: ((i * tm // seq) * N_MOD + gate_idx, 0, 0)),
            pl.BlockSpec((tm, TOP_K), lambda i: (i, 0)),
        ],
        out_specs=pl.BlockSpec((tm, d), lambda i: (i, 0)),
        scratch_shapes=[pltpu.VMEM((TOP_K, tm, half), U32), pltpu.SemaphoreType.DMA(())],
        compiler_params=_params(("arbitrary",)),
        name="combine",
    )(dest.reshape(t // tm, 1, tm * TOP_K), ys, x2, mod3, gates)


def kernel(x, c, w_ada, b_ada, norm1_g, w_in, q_norm_g, k_norm_g, rel_bias, w_ret_out, w_att_out, w_out,
           norm2_g, w_router, b_router, w1, b1, w2, b2):
    batch, seq, d = x.shape
    depth = w_ada.shape[0]
    t = batch * seq
    assert batch <= 8 and seq % max(RET_BLOCK, ATT_BLOCK, NORM_TM, OUT_TM, COMBINE_TM) == 0 and t % PROJ_TM == 0
    x2 = x.reshape(t, d)
    c8 = jnp.zeros((8, d), F32).at[:batch].set(c)
    ret_col0 = 0
    att_col0 = 2 * RET_QK_W + 2 * RET_V_W
    gate_col0 = att_col0 + 3 * ATT_W
    for layer in range(depth):
        mod3 = _ada_ln(c8, w_ada, b_ada, layer)[:batch].reshape(batch * N_MOD, 1, d)
        h = _norm_mod(x2, norm1_g.reshape(depth, 1, d), mod3, layer, seq, shift_idx=0, scale_idx=1)
        proj = _project(h, w_in, layer)
        ret = _retention(proj, batch, seq, ret_col0)
        att = _chunk_attention(proj, rel_bias[layer], q_norm_g[layer], k_norm_g[layer], batch, seq, att_col0)
        merged = _merge(ret, att, w_ret_out, w_att_out, proj, gate_col0, layer)
        x2 = _out_proj(merged, w_out, x2, mod3, layer, seq, gate_idx=2)
        h_packed, logits = _norm_router(x2, norm2_g.reshape(depth, 1, d), mod3, w_router, b_router, layer, seq,
                                        shift_idx=3, scale_idx=4)
        gates, dest, sorted_tok, block_first, block_e, block_nsub = _routing(logits)
        ys = _experts(h_packed, sorted_tok, block_first, block_e, block_nsub, w1, b1, w2, b2, layer)
        x2 = _combine(ys, dest, gates, x2, mod3, seq, gate_idx=5)
    return x2.reshape(batch, seq, d)
```

```python
import functools

import numpy as np
import jax
import jax.numpy as jnp
from jax import lax
from jax.experimental import pallas as pl
from jax.experimental.pallas import tpu as pltpu

F32 = jnp.float32
BF16 = jnp.bfloat16
U32 = jnp.uint32

CHUNK = 64
NORM_EPS = 1e-6
RET_HEADS = 8
RET_QK_DIM = 256
RET_V_DIM = 512
ROPE_BASE = 10000.0
ATT_HEADS = 16
ATT_HEAD_DIM = 128
PREV_CHUNKS = 8
MAX_REL = 128
N_EXPERTS = 32
TOP_K = 4
SWIGLU_ALPHA = 1.702
SWIGLU_LIMIT = 7.0
N_MOD = 6

RET_QK_W = RET_HEADS * RET_QK_DIM
RET_V_W = RET_HEADS * RET_V_DIM
ATT_W = ATT_HEADS * ATT_HEAD_DIM

V7X_VMEM_LIMIT_BYTES = 58 * 1024 * 1024
HI16 = 0xFFFF0000

ADA_TN = 1024
NORM_TM = 256
PROJ_TM, PROJ_TN = 2048, 512
RET_BLOCK = 256
RET_HEADS_PER_STEP = 2
ATT_BLOCK = 256
ATT_HEADS_PER_STEP = 2
MERGE_TM, MERGE_TN = 512, 512
OUT_TM, OUT_TN = 1024, 512
EXPERT_BM = 1280
EXPERT_SUB = 256
EXPERT_TF = 256
EXPERT_TN = 256
EXPERT_PERM_W = 256
EXPERT_ISSUE_STEPS = 10
GATHER_GROUP = 32
COMBINE_TM = 256


def _params(semantics):
    return pltpu.CompilerParams(dimension_semantics=semantics, vmem_limit_bytes=V7X_VMEM_LIMIT_BYTES)


def _silu(t):
    return t * jax.nn.sigmoid(t)


def _pack_halves(y):
    n = y.shape[1] // 2
    hi = lax.bitcast_convert_type(y[:, :n].astype(BF16).astype(F32), U32)
    lo = lax.bitcast_convert_type(y[:, n:].astype(BF16).astype(F32), U32)
    return hi | (lo >> 16)


def _unpack_halves(p):
    a = lax.bitcast_convert_type(p & jnp.uint32(HI16), F32)
    b = lax.bitcast_convert_type(p << 16, F32)
    return a, b


def _ada_kernel(c_ref, w_ref, b_ref, o_ref):
    a = _silu(c_ref[...]).astype(BF16)
    o_ref[...] = jnp.dot(a, w_ref[...].astype(BF16), preferred_element_type=F32) + b_ref[...]


def _ada_ln(c8, w_ada, b_ada, layer):
    d = c8.shape[1]
    n = w_ada.shape[2]
    return pl.pallas_call(
        _ada_kernel,
        out_shape=jax.ShapeDtypeStruct((8, n), F32),
        grid=(n // ADA_TN,),
        in_specs=[
            pl.BlockSpec((8, d), lambda j: (0, 0)),
            pl.BlockSpec((None, d, ADA_TN), lambda j: (layer, 0, j)),
            pl.BlockSpec((None, 1, ADA_TN), lambda j: (layer, 0, j)),
        ],
        out_specs=pl.BlockSpec((8, ADA_TN), lambda j: (0, j)),
        compiler_params=_params(("arbitrary",)),
        name="ada_ln",
    )(c8, w_ada, b_ada.reshape(b_ada.shape[0], 1, n))


def _modulated_norm(x, g, scale, shift):
    xn = x * lax.rsqrt(jnp.mean(x * x, axis=-1, keepdims=True) + NORM_EPS)
    return (xn * g) * (1.0 + scale) + shift


def _norm_mod_kernel(x_ref, g_ref, sc_ref, sh_ref, o_ref):
    o_ref[...] = _modulated_norm(x_ref[...], g_ref[...], sc_ref[...], sh_ref[...]).astype(o_ref.dtype)


def _norm_mod(x2, g, mod3, layer, seq, shift_idx, scale_idx):
    t, d = x2.shape
    tm = NORM_TM

    def mod_spec(idx):
        return pl.BlockSpec((None, 1, d), lambda i: ((i * tm // seq) * N_MOD + idx, 0, 0))

    return pl.pallas_call(
        _norm_mod_kernel,
        out_shape=jax.ShapeDtypeStruct((t, d), BF16),
        grid=(t // tm,),
        in_specs=[
            pl.BlockSpec((tm, d), lambda i: (i, 0)),
            pl.BlockSpec((None, 1, d), lambda i: (layer, 0, 0)),
            mod_spec(scale_idx),
            mod_spec(shift_idx),
        ],
        out_specs=pl.BlockSpec((tm, d), lambda i: (i, 0)),
        compiler_params=_params(("arbitrary",)),
        name="norm_mod",
    )(x2, g, mod3, mod3)


def _norm_router_kernel(x_ref, g_ref, sc_ref, sh_ref, wr_ref, br_ref, h_ref, lg_ref):
    h = _modulated_norm(x_ref[...], g_ref[...], sc_ref[...], sh_ref[...])
    h_ref[...] = _pack_halves(h)
    h_hi = h.astype(BF16)
    h_lo = (h - h_hi.astype(F32)).astype(BF16)
    w = wr_ref[...]
    w_hi = w.astype(BF16)
    w_lo = (w - w_hi.astype(F32)).astype(BF16)
    lg = (jnp.dot(h_hi, w_hi, preferred_element_type=F32)
          + jnp.dot(h_lo, w_hi, preferred_element_type=F32)
          + jnp.dot(h_hi, w_lo, preferred_element_type=F32))
    lg_ref[...] = lg + br_ref[...]


def _norm_router(x2, g, mod3, w_router, b_router, layer, seq, shift_idx, scale_idx):
    t, d = x2.shape
    e = w_router.shape[2]
    tm = NORM_TM

    def mod_spec(idx):
        return pl.BlockSpec((None, 1, d), lambda i: ((i * tm // seq) * N_MOD + idx, 0, 0))

    return pl.pallas_call(
        _norm_router_kernel,
        out_shape=(jax.ShapeDtypeStruct((t, d // 2), U32), jax.ShapeDtypeStruct((t, e), F32)),
        grid=(t // tm,),
        in_specs=[
            pl.BlockSpec((tm, d), lambda i: (i, 0)),
            pl.BlockSpec((None, 1, d), lambda i: (layer, 0, 0)),
            mod_spec(scale_idx),
            mod_spec(shift_idx),
            pl.BlockSpec((None, d, e), lambda i: (layer, 0, 0)),
            pl.BlockSpec((None, 1, e), lambda i: (layer, 0, 0)),
        ],
        out_specs=(pl.BlockSpec((tm, d // 2), lambda i: (i, 0)), pl.BlockSpec((tm, e), lambda i: (i, 0))),
        compiler_params=_params(("arbitrary",)),
        name="norm_router",
    )(x2, g, mod3, mod3, w_router, b_router.reshape(b_router.shape[0], 1, e))


def _proj_kernel(a_ref, w_ref, o_ref):
    o_ref[...] = jnp.dot(a_ref[...], w_ref[...].astype(BF16), preferred_element_type=F32).astype(o_ref.dtype)


def _project(a, w, layer):
    t, k = a.shape
    n = w.shape[2]
    tm, tn = PROJ_TM, PROJ_TN
    return pl.pallas_call(
        _proj_kernel,
        out_shape=jax.ShapeDtypeStruct((t, n), BF16),
        grid=(n // tn, t // tm),
        in_specs=[
            pl.BlockSpec((tm, k), lambda j, i: (i, 0)),
            pl.BlockSpec((None, k, tn), lambda j, i: (layer, 0, j)),
        ],
        out_specs=pl.BlockSpec((tm, tn), lambda j, i: (i, j)),
        compiler_params=_params(("arbitrary", "arbitrary")),
        name="in_proj",
    )(a, w)


def _retention_kernel(q_ref, k_ref, v_ref, g_ref, cos_ref, sin_ref, dec_ref, qd_ref, kd_ref, cd_ref,
                      o_ref, state_ref):
    @pl.when(pl.program_id(2) == 0)
    def _():
        state_ref[...] = jnp.zeros_like(state_ref)

    cos = cos_ref[...]
    sin = sin_ref[...]
    half = RET_QK_DIM // 2
    contract_last = (((1,), (1,)), ((), ()))
    contract_rows = (((0,), (0,)), ((), ()))

    def rotary(t):
        t1, t2 = t[:, :half], t[:, half:]
        return jnp.concatenate([t1 * cos - t2 * sin, t1 * sin + t2 * cos], axis=1)

    for hh in range(RET_HEADS_PER_STEP):
        qk = slice(hh * RET_QK_DIM, (hh + 1) * RET_QK_DIM)
        vg = slice(hh * RET_V_DIM, (hh + 1) * RET_V_DIM)
        q = rotary(q_ref[:, qk].astype(F32))
        k = rotary(k_ref[:, qk].astype(F32)) * (RET_QK_DIM ** -0.5)
        v = v_ref[:, vg]
        scores = lax.dot_general(q.astype(BF16), k.astype(BF16), contract_last,
                                 preferred_element_type=F32) * dec_ref[hh]
        state = state_ref[hh]
        out = (jnp.dot(scores.astype(BF16), v, preferred_element_type=F32)
               + jnp.dot((q * qd_ref[hh]).astype(BF16), state.astype(BF16), preferred_element_type=F32))
        state_ref[hh] = state * cd_ref[hh] + lax.dot_general(
            (k * kd_ref[hh]).astype(BF16), v, contract_rows, preferred_element_type=F32)
        on = out * lax.rsqrt(jnp.mean(out * out, axis=-1, keepdims=True) + NORM_EPS)
        o_ref[:, vg] = (_silu(g_ref[:, vg].astype(F32)) * on).astype(o_ref.dtype)


def _retention_tables(seq):
    half = RET_QK_DIM // 2
    inv_freq = ROPE_BASE ** (-np.arange(half, dtype=np.float64) / half)
    ang = np.arange(seq, dtype=np.float64)[:, None] * inv_freq[None, :]
    log_gamma = np.log(1.0 - 2.0 ** (-5.0 - np.arange(RET_HEADS, dtype=np.float64)))[:, None, None]
    pos = np.arange(RET_BLOCK, dtype=np.float64)
    n, m = pos[:, None], pos[None, :]
    cn, cm = np.floor(n / CHUNK), np.floor(m / CHUNK)
    dist = np.where(cn == cm, np.abs(n - m), n - m)
    dec = np.where((cm <= cn)[None], np.exp(log_gamma * dist[None]), 0.0)
    qd = np.broadcast_to(np.exp(log_gamma * (pos[None, :, None] + 1.0)), (RET_HEADS, RET_BLOCK, RET_QK_DIM))
    kd = np.broadcast_to(np.exp(log_gamma * (RET_BLOCK - 1.0 - pos[None, :, None])),
                         (RET_HEADS, RET_BLOCK, RET_QK_DIM))
    cd = np.broadcast_to(np.exp(log_gamma * RET_BLOCK), (RET_HEADS, 1, RET_V_DIM))
    return tuple(jnp.asarray(np.ascontiguousarray(v), F32) for v in (np.cos(ang), np.sin(ang), dec, qd, kd, cd))


def _retention(proj, batch, seq, col0):
    t = proj.shape[0]
    r = RET_BLOCK
    nsb = seq // r
    cos, sin, dec, qd, kd, cd = _retention_tables(seq)
    hp = RET_HEADS_PER_STEP
    groups = RET_HEADS // hp
    qk_w, v_w = hp * RET_QK_DIM, hp * RET_V_DIM
    assert col0 % qk_w == 0 and (col0 + 2 * RET_QK_W) % v_w == 0
    qk_blk0 = col0 // qk_w
    v_blk0 = (col0 + 2 * RET_QK_W) // v_w
    row = lambda b, h, s: b * nsb + s
    head_tab = lambda shape: pl.BlockSpec((hp,) + shape, lambda b, h, s: (h, 0, 0))
    half = RET_QK_DIM // 2
    return pl.pallas_call(
        _retention_kernel,
        out_shape=jax.ShapeDtypeStruct((t, RET_V_W), BF16),
        grid=(batch, groups, nsb),
        in_specs=[
            pl.BlockSpec((r, qk_w), lambda b, h, s: (row(b, h, s), qk_blk0 + h)),
            pl.BlockSpec((r, qk_w), lambda b, h, s: (row(b, h, s), qk_blk0 + groups + h)),
            pl.BlockSpec((r, v_w), lambda b, h, s: (row(b, h, s), v_blk0 + h)),
            pl.BlockSpec((r, v_w), lambda b, h, s: (row(b, h, s), v_blk0 + groups + h)),
            pl.BlockSpec((r, half), lambda b, h, s: (s, 0)),
            pl.BlockSpec((r, half), lambda b, h, s: (s, 0)),
            head_tab((r, r)),
            head_tab((r, RET_QK_DIM)),
            head_tab((r, RET_QK_DIM)),
            head_tab((1, RET_V_DIM)),
        ],
        out_specs=pl.BlockSpec((r, v_w), lambda b, h, s: (row(b, h, s), h)),
        scratch_shapes=[pltpu.VMEM((hp, RET_QK_DIM, RET_V_DIM), F32)],
        compiler_params=_params(("arbitrary", "arbitrary", "arbitrary")),
        name="retention",
    )(proj, proj, proj, proj, cos, sin, dec, qd, kd, cd)


ATT_KEY_BLOCKS = PREV_CHUNKS * CHUNK // ATT_BLOCK + 1
NEG = float(np.finfo(np.float32).min)


def _attention_kernel(q_ref, k0_ref, k1_ref, k2_ref, v0_ref, v1_ref, v2_ref, tab_ref, gq_ref, gk_ref, o_ref):
    i = pl.program_id(1)

    def rms(t, g):
        tf = t.astype(F32)
        return (tf * lax.rsqrt(jnp.mean(tf * tf, axis=-1, keepdims=True) + NORM_EPS)) * g

    dh = ATT_HEAD_DIM
    k_all = jnp.concatenate([k0_ref[...], k1_ref[...], k2_ref[...]], axis=0)
    v_all = jnp.concatenate([v0_ref[...], v1_ref[...], v2_ref[...]], axis=0)
    outs = []
    for hh in range(ATT_HEADS_PER_STEP):
        lanes = slice(hh * dh, (hh + 1) * dh)
        q = rms(q_ref[:, lanes], gq_ref[...]).astype(BF16)
        k = rms(k_all[:, lanes], gk_ref[...]).astype(BF16)
        sc = lax.dot_general(q, k, (((1,), (1,)), ((), ())), preferred_element_type=F32)
        sc = sc * (ATT_HEAD_DIM ** -0.5) + tab_ref[hh]
        col = lax.broadcasted_iota(jnp.int32, sc.shape, 1)
        sc = jnp.where(col >= (ATT_KEY_BLOCKS - 1 - i) * ATT_BLOCK, sc, NEG)
        p = jnp.exp(sc - jnp.max(sc, axis=-1, keepdims=True))
        denom = jnp.sum(p, axis=-1, keepdims=True)
        outs.append(jnp.dot(p.astype(BF16), v_all[:, lanes], preferred_element_type=F32) / denom)
    o_ref[...] = jnp.concatenate(outs, axis=1).astype(o_ref.dtype)


def _attention_table(rel_bias):
    a = ATT_BLOCK
    w = ATT_KEY_BLOCKS * a
    pad = PREV_CHUNKS * CHUNK
    heads = rel_bias.shape[0]
    period = 1024
    assert a - 1 + w - 1 <= period - 2
    rel_of_i = pad + a - 1 - np.arange(period)
    idx = np.clip(np.minimum(rel_of_i, MAX_REL) + (CHUNK - 1), 0, rel_bias.shape[-1] - 1)
    u = rel_bias[:, idx].astype(F32)
    skew = jnp.tile(u, (1, a))[:, :a * (period - 1)].reshape(heads, a, period - 1)
    bias = skew[:, :, a - 1:a - 1 + w]
    n = np.arange(a)[:, None]
    j = np.arange(w)[None, :]
    qc = n // CHUNK
    kc = j // CHUNK - PREV_CHUNKS
    valid = (kc <= qc) & (kc >= qc - PREV_CHUNKS)
    return jnp.where(jnp.asarray(valid)[None], bias, NEG)


def _chunk_attention(proj, rel_bias, q_norm_g, k_norm_g, batch, seq, col0):
    assert ATT_KEY_BLOCKS == 3
    t = proj.shape[0]
    a = ATT_BLOCK
    nqb = seq // a
    dh = ATT_HEAD_DIM
    hw = ATT_HEADS_PER_STEP * dh
    groups = ATT_HEADS // ATT_HEADS_PER_STEP
    assert col0 % hw == 0 and ATT_W % hw == 0
    blk0 = col0 // hw
    tab = _attention_table(rel_bias)

    def kv_spec(which, jj):
        return pl.BlockSpec(
            (a, hw),
            lambda h, i, b: (b * nqb + jnp.maximum(i - (ATT_KEY_BLOCKS - 1) + jj, 0), blk0 + which * groups + h))

    return pl.pallas_call(
        _attention_kernel,
        out_shape=jax.ShapeDtypeStruct((t, ATT_W), BF16),
        grid=(groups, nqb, batch),
        in_specs=[
            pl.BlockSpec((a, hw), lambda h, i, b: (b * nqb + i, blk0 + h)),
            kv_spec(1, 0), kv_spec(1, 1), kv_spec(1, 2),
            kv_spec(2, 0), kv_spec(2, 1), kv_spec(2, 2),
            pl.BlockSpec((ATT_HEADS_PER_STEP, a, ATT_KEY_BLOCKS * a), lambda h, i, b: (h, 0, 0)),
            pl.BlockSpec((1, dh), lambda h, i, b: (0, 0)),
            pl.BlockSpec((1, dh), lambda h, i, b: (0, 0)),
        ],
        out_specs=pl.BlockSpec((a, hw), lambda h, i, b: (b * nqb + i, h)),
        compiler_params=_params(("arbitrary", "arbitrary", "arbitrary")),
        name="chunk_attention",
    )(proj, proj, proj, proj, proj, proj, proj, tab, q_norm_g.reshape(1, dh), k_norm_g.reshape(1, dh))


def _merge_kernel(r_ref, a_ref, wr_ref, wa_ref, gr_ref, ga_ref, o_ref):
    yr = jnp.dot(r_ref[...], wr_ref[...].astype(BF16), preferred_element_type=F32)
    ya = jnp.dot(a_ref[...], wa_ref[...].astype(BF16), preferred_element_type=F32)
    merged = jax.nn.sigmoid(gr_ref[...].astype(F32)) * yr + jax.nn.sigmoid(ga_ref[...].astype(F32)) * ya
    o_ref[...] = merged.astype(o_ref.dtype)


def _merge(ret, att, w_ret_out, w_att_out, proj, gate_col0, layer):
    t = ret.shape[0]
    d = w_ret_out.shape[2]
    tm, tn = MERGE_TM, MERGE_TN
    gblk = gate_col0 // tn
    return pl.pallas_call(
        _merge_kernel,
        out_shape=jax.ShapeDtypeStruct((t, d), BF16),
        grid=(d // tn, t // tm),
        in_specs=[
            pl.BlockSpec((tm, ret.shape[1]), lambda j, i: (i, 0)),
            pl.BlockSpec((tm, att.shape[1]), lambda j, i: (i, 0)),
            pl.BlockSpec((None, ret.shape[1], tn), lambda j, i: (layer, 0, j)),
            pl.BlockSpec((None, att.shape[1], tn), lambda j, i: (layer, 0, j)),
            pl.BlockSpec((tm, tn), lambda j, i: (i, gblk + j)),
            pl.BlockSpec((tm, tn), lambda j, i: (i, gblk + d // tn + j)),
        ],
        out_specs=pl.BlockSpec((tm, tn), lambda j, i: (i, j)),
        compiler_params=_params(("arbitrary", "arbitrary")),
        name="merge",
    )(ret, att, w_ret_out, w_att_out, proj, proj)


def _out_proj_kernel(a_ref, w_ref, x_ref, gate_ref, o_ref):
    acc = jnp.dot(a_ref[...], w_ref[...].astype(BF16), preferred_element_type=F32)
    o_ref[...] = x_ref[...] + gate_ref[...] * acc


def _out_proj(a, w, x2, mod3, layer, seq, gate_idx):
    t, k = a.shape
    d = w.shape[2]
    tm, tn = OUT_TM, OUT_TN
    return pl.pallas_call(
        _out_proj_kernel,
        out_shape=jax.ShapeDtypeStruct((t, d), F32),
        grid=(d // tn, t // tm),
        in_specs=[
            pl.BlockSpec((tm, k), lambda j, i: (i, 0)),
            pl.BlockSpec((None, k, tn), lambda j, i: (layer, 0, j)),
            pl.BlockSpec((tm, tn), lambda j, i: (i, j)),
            pl.BlockSpec((None, 1, tn), lambda j, i: ((i * tm // seq) * N_MOD + gate_idx, 0, j)),
        ],
        out_specs=pl.BlockSpec((tm, tn), lambda j, i: (i, j)),
        compiler_params=_params(("arbitrary", "arbitrary")),
        name="out_proj",
    )(a, w, x2, mod3)


def _routing(logits):
    t = logits.shape[0]
    bm = EXPERT_BM
    a = t * TOP_K
    top_val, top_idx = lax.top_k(logits, TOP_K)
    gate = jax.nn.softmax(top_val, axis=-1)
    onehot = top_idx[:, :, None] == jnp.arange(N_EXPERTS, dtype=jnp.int32)[None, None, :]
    picked = jnp.any(onehot, axis=1).astype(jnp.int32)
    inclusive = jnp.cumsum(picked, axis=0)
    sizes = inclusive[-1]
    padded = (sizes + bm - 1) // bm * bm
    pad_end = jnp.cumsum(padded)
    pad_start = pad_end - padded
    start = jnp.cumsum(sizes) - sizes
    slot = (pad_start[None, :] + inclusive - picked)[:, None, :]
    dest = jnp.sum(jnp.where(onehot, slot, 0), axis=-1).astype(jnp.int32).reshape(a)
    order = jnp.argsort(top_idx.reshape(a))
    n_blocks = -(-a // bm) + N_EXPERTS
    block_row0 = jnp.arange(n_blocks, dtype=jnp.int32) * bm
    block_e = jnp.minimum(jnp.searchsorted(pad_end, block_row0, side='right'), N_EXPERTS - 1).astype(jnp.int32)
    block_valid = block_row0 < pad_end[-1]
    block_rows = jnp.clip(sizes[block_e] - (block_row0 - pad_start[block_e]), 0, bm)
    block_nsub = jnp.where(block_valid, (block_rows + EXPERT_SUB - 1) // EXPERT_SUB, 0).astype(jnp.int32)
    n_chunks = -(-a // bm) + 2
    sorted_tok = jnp.zeros((n_chunks * bm,), jnp.int32).at[:a].set((order // TOP_K).astype(jnp.int32))
    block_first = jnp.where(block_valid, start[block_e] + block_row0 - pad_start[block_e], 0).astype(jnp.int32)
    n_valid = jnp.sum(block_valid.astype(jnp.int32))
    return gate, dest, sorted_tok.reshape(n_chunks, 1, bm), block_first, block_e, block_nsub, n_valid


def _expert_kernel(be_ref, ns_ref, first_ref, tok0_lo, tok0_hi, tokn_lo, tokn_hi, h_hbm, w1_ref, b1_ref,
                   w2a_ref, w2b_ref, b2a_ref, b2b_ref, perm_ref, o_ref, gbuf, x_scr, act_scr, sem,
                   *, nj1, issue_steps):
    b = pl.program_id(0)
    j = pl.program_id(1)
    nb = pl.num_programs(0)
    nsub = ns_ref[b]
    sub = EXPERT_SUB
    group = GATHER_GROUP
    bm, d = x_scr.shape
    half = d // 2
    tf = act_scr.shape[2]
    per_step = bm // issue_steps
    nxt = jnp.minimum(b + 1, nb - 1)
    rows_next = jnp.where(b + 1 < nb, ns_ref[nxt], 0) * sub

    def row_copy(lo_ref, hi_ref, offset, r):
        at = offset + r
        tok = jnp.where(at < bm, lo_ref[0, jnp.minimum(at, bm - 1)], hi_ref[0, jnp.maximum(at - bm, 0)])
        return pltpu.make_async_copy(h_hbm.at[pl.ds(tok, 1)], gbuf.at[pl.ds(r, 1)], sem)

    def sub_rows(s):
        return pl.ds(pl.multiple_of(s * sub, sub), sub)

    @pl.when((j == 0) & (b == 0))
    def _():
        offset = lax.rem(first_ref[0], bm)

        def start(r, carry):
            row_copy(tok0_lo, tok0_hi, offset, r).start()
            return carry
        lax.fori_loop(0, nsub * sub, start, 0)

    @pl.when(j == 0)
    def _():
        def wait_rows(s, carry):
            pltpu.make_async_copy(h_hbm.at[pl.ds(0, sub)], gbuf.at[pl.ds(0, sub)], sem).wait()
            return carry
        lax.fori_loop(0, nsub, wait_rows, 0)

        def unpack_rows(s, carry):
            xa, xb = _unpack_halves(gbuf[sub_rows(s), :])
            x_scr[sub_rows(s), pl.ds(0, half)] = xa.astype(BF16)
            x_scr[sub_rows(s), pl.ds(half, half)] = xb.astype(BF16)
            return carry
        lax.fori_loop(0, nsub, unpack_rows, 0)

    @pl.when(j < issue_steps)
    def _():
        offset = lax.rem(first_ref[nxt], bm)
        for g in range(per_step // group):
            row0 = j * per_step + g * group

            @pl.when(row0 < rows_next)
            def _():
                for i in range(group):
                    row_copy(tokn_lo, tokn_hi, offset, row0 + i).start()

    def for_real_rows(fn):
        def pair(p, carry):
            fn(pl.ds(pl.multiple_of(p * 2 * sub, 2 * sub), 2 * sub))
            return carry
        lax.fori_loop(0, nsub // 2, pair, 0)

        @pl.when(nsub % 2 == 1)
        def _():
            fn(sub_rows(nsub - 1))

    def first_matmul(rows):
        hb = jnp.dot(x_scr[rows, :], w1_ref[...].astype(BF16), preferred_element_type=F32) + b1_ref[...]
        hb = hb.astype(BF16)
        gw = perm_ref.shape[0]
        sel = [jnp.dot(hb[:, c * gw:(c + 1) * gw], perm_ref[...], preferred_element_type=F32)
               for c in range(2 * tf // gw)]
        x_glu = jnp.minimum(jnp.concatenate([s[:, :gw // 2] for s in sel], axis=1), SWIGLU_LIMIT)
        x_lin = jnp.clip(jnp.concatenate([s[:, gw // 2:] for s in sel], axis=1), -SWIGLU_LIMIT, SWIGLU_LIMIT)
        act = x_glu * jax.nn.sigmoid(SWIGLU_ALPHA * x_glu) * (x_lin + 1.0)
        act_scr[jnp.minimum(j, nj1 - 1), rows, :] = act.astype(BF16)

    def second_matmul(rows):
        act = jnp.concatenate([act_scr[t, rows, :] for t in range(nj1)], axis=1)
        w2 = jnp.concatenate([w2a_ref[...], w2b_ref[...]], axis=1).astype(BF16)
        y = jnp.dot(act, w2, preferred_element_type=F32)
        y = y + jnp.concatenate([b2a_ref[...], b2b_ref[...]], axis=1)
        o_ref[rows, :] = _pack_halves(y)

    @pl.when(j < nj1)
    def _():
        for_real_rows(first_matmul)

    @pl.when(j >= nj1)
    def _():
        for_real_rows(second_matmul)

        def clear(s, carry):
            o_ref[sub_rows(s), :] = jnp.zeros((sub, o_ref.shape[1]), o_ref.dtype)
            return carry
        lax.fori_loop(nsub, bm // sub, clear, 0)


def _deinterleave_matrix(tf):
    p = np.zeros((2 * tf, 2 * tf), np.float32)
    f = np.arange(tf)
    p[2 * f, f] = 1.0
    p[2 * f + 1, tf + f] = 1.0
    return jnp.asarray(p, BF16)


def _experts(h_packed, sorted_tok, block_first, block_e, block_nsub, n_valid, w1, b1, w2, b2, layer):
    half = h_packed.shape[1]
    d = 2 * half
    f = w2.shape[2]
    bm, tf, tn = EXPERT_BM, EXPERT_TF, EXPERT_TN
    n_blocks = block_e.shape[0]
    nj1 = f // tf
    nj2 = half // tn
    issue_steps = EXPERT_ISSUE_STEPS
    assert issue_steps <= nj1 + nj2 and bm % (issue_steps * GATHER_GROUP) == 0 and bm % EXPERT_SUB == 0
    assert h_packed.shape[0] >= EXPERT_SUB

    def w1_idx(b, j, be, ns):
        nxt = jnp.minimum(b + 1, n_blocks - 1)
        to_next = (j >= nj1) & (ns[nxt] > 0)
        e = jnp.where(to_next, be[nxt], be[b])
        tile = jnp.where(to_next, 0, jnp.where(ns[b] > 0, jnp.minimum(j, nj1 - 1), nj1 - 1))
        return e, tile

    def w2_tile(b, j, ns):
        return jnp.where(ns[b] > 0, jnp.maximum(j - nj1, 0), nj2 - 1)

    def tok_chunk(which, plus):
        def index_map(b, j, be, ns, first):
            blk = 0 if which == "first" else jnp.minimum(b + 1, n_blocks - 1)
            return (first[blk] // bm + plus, 0, 0)
        return pl.BlockSpec((None, 1, bm), index_map, memory_space=pltpu.SMEM)

    b1r = b1.reshape(b1.shape[0], b1.shape[1], 1, 2 * f)
    b2r = b2.reshape(b2.shape[0], b2.shape[1], 1, d)
    return pl.pallas_call(
        functools.partial(_expert_kernel, nj1=nj1, issue_steps=issue_steps),
        out_shape=jax.ShapeDtypeStruct((n_blocks * bm, half), U32),
        grid_spec=pltpu.PrefetchScalarGridSpec(
            num_scalar_prefetch=3,
            grid=(n_valid, nj1 + nj2),
            in_specs=[
                tok_chunk("first", 0), tok_chunk("first", 1), tok_chunk("next", 0), tok_chunk("next", 1),
                pl.BlockSpec(memory_space=pl.ANY),
                pl.BlockSpec((None, None, d, 2 * tf),
                             lambda b, j, be, ns, _: (layer, w1_idx(b, j, be, ns)[0], 0, w1_idx(b, j, be, ns)[1])),
                pl.BlockSpec((None, None, 1, 2 * tf),
                             lambda b, j, be, ns, _: (layer, be[b], 0, jnp.minimum(j, nj1 - 1))),
                pl.BlockSpec((None, None, f, tn), lambda b, j, be, ns, _: (layer, be[b], 0, w2_tile(b, j, ns))),
                pl.BlockSpec((None, None, f, tn),
                             lambda b, j, be, ns, _: (layer, be[b], 0, nj2 + w2_tile(b, j, ns))),
                pl.BlockSpec((None, None, 1, tn), lambda b, j, be, ns, _: (layer, be[b], 0, w2_tile(b, j, ns))),
                pl.BlockSpec((None, None, 1, tn),
                             lambda b, j, be, ns, _: (layer, be[b], 0, nj2 + w2_tile(b, j, ns))),
                pl.BlockSpec((EXPERT_PERM_W, EXPERT_PERM_W), lambda b, j, be, ns, _: (0, 0)),
            ],
            out_specs=pl.BlockSpec((bm, tn), lambda b, j, be, ns, _: (b, jnp.maximum(j - nj1, 0))),
            scratch_shapes=[
                pltpu.VMEM((bm, half), U32),
                pltpu.VMEM((bm, d), BF16),
                pltpu.VMEM((nj1, bm, tf), BF16),
                pltpu.SemaphoreType.DMA(()),
            ],
        ),
        compiler_params=_params(("arbitrary", "arbitrary")),
        name="experts",
    )(block_e, block_nsub, block_first, sorted_tok, sorted_tok, sorted_tok, sorted_tok, h_packed, w1, b1r, w2, w2,
      b2r, b2r, _deinterleave_matrix(EXPERT_PERM_W // 2))


def _clear_kernel(nv_ref, ys_hbm, o_ref):
    del nv_ref, ys_hbm
    o_ref[...] = jnp.zeros_like(o_ref)


def _clear_unused_blocks(ys, n_valid):
    bm = EXPERT_BM
    n_blocks = ys.shape[0] // bm
    return pl.pallas_call(
        _clear_kernel,
        out_shape=jax.ShapeDtypeStruct(ys.shape, ys.dtype),
        grid_spec=pltpu.PrefetchScalarGridSpec(
            num_scalar_prefetch=1,
            grid=(n_blocks - n_valid,),
            in_specs=[pl.BlockSpec(memory_space=pl.ANY)],
            out_specs=pl.BlockSpec((bm, ys.shape[1]), lambda i, nv: (nv[0] + i, 0)),
        ),
        input_output_aliases={1: 0},
        compiler_params=_params(("arbitrary",)),
        name="clear_unused_blocks",
    )(n_valid.reshape(1), ys)


def _combine_kernel(dest_ref, ys_hbm, x_ref, gate_ref, g_ref, o_ref, buf, sem):
    tm = x_ref.shape[0]
    half = buf.shape[2]

    def row_copy(src_row, k, r):
        return pltpu.make_async_copy(ys_hbm.at[pl.ds(src_row, 1)], buf.at[k, pl.ds(r, 1)], sem)

    def start(r, carry):
        for k in range(TOP_K):
            row_copy(dest_ref[0, r * TOP_K + k], k, r).start()
        return carry

    def wait(r, carry):
        for k in range(TOP_K):
            row_copy(0, k, r).wait()
        return carry

    lax.fori_loop(0, tm, start, 0)
    lax.fori_loop(0, tm, wait, 0)
    sublanes = 8

    def chunk(c, carry):
        rows = pl.ds(pl.multiple_of(c * sublanes, sublanes), sublanes)
        g = g_ref[rows, :]
        ya = jnp.zeros((sublanes, half), F32)
        yb = jnp.zeros((sublanes, half), F32)
        for k in range(TOP_K):
            a, b = _unpack_halves(buf[k, rows, :])
            ya = ya + g[:, k:k + 1] * a
            yb = yb + g[:, k:k + 1] * b
        o_ref[rows, pl.ds(0, half)] = x_ref[rows, pl.ds(0, half)] + gate_ref[:, pl.ds(0, half)] * ya
        o_ref[rows, pl.ds(half, half)] = x_ref[rows, pl.ds(half, half)] + gate_ref[:, pl.ds(half, half)] * yb
        return carry

    lax.fori_loop(0, tm // sublanes, chunk, 0)


def _combine(ys, dest, gates, x2, mod3, seq, gate_idx):
    t, d = x2.shape
    tm = COMBINE_TM
    half = ys.shape[1]
    return pl.pallas_call(
        _combine_kernel,
        out_shape=jax.ShapeDtypeStruct((t, d), F32),
        grid=(t // tm,),
        in_specs=[
            pl.BlockSpec((None, 1, tm * TOP_K), lambda i: (i, 0, 0), memory_space=pltpu.SMEM),
            pl.BlockSpec(memory_space=pl.ANY),
            pl.BlockSpec((tm, d), lambda i: (i, 0)),
            pl.BlockSpec((None, 1, d), lambda i: ((i * tm // seq) * N_MOD + gate_idx, 0, 0)),
            pl.BlockSpec((tm, TOP_K), lambda i: (i, 0)),
        ],
        out_specs=pl.BlockSpec((tm, d), lambda i: (i, 0)),
        scratch_shapes=[pltpu.VMEM((TOP_K, tm, half), U32), pltpu.SemaphoreType.DMA(())],
        compiler_params=_params(("arbitrary",)),
        name="combine",
    )(dest.reshape(t // tm, 1, tm * TOP_K), ys, x2, mod3, gates)


def kernel(x, c, w_ada, b_ada, norm1_g, w_in, q_norm_g, k_norm_g, rel_bias, w_ret_out, w_att_out, w_out,
           norm2_g, w_router, b_router, w1, b1, w2, b2):
    batch, seq, d = x.shape
    depth = w_ada.shape[0]
    t = batch * seq
    assert batch <= 8 and seq % max(RET_BLOCK, ATT_BLOCK, NORM_TM, OUT_TM, COMBINE_TM) == 0 and t % PROJ_TM == 0
    x2 = x.reshape(t, d)
    c8 = jnp.zeros((8, d), F32).at[:batch].set(c)
    ret_col0 = 0
    att_col0 = 2 * RET_QK_W + 2 * RET_V_W
    gate_col0 = att_col0 + 3 * ATT_W
    for layer in range(depth):
        mod3 = _ada_ln(c8, w_ada, b_ada, layer)[:batch].reshape(batch * N_MOD, 1, d)
        h = _norm_mod(x2, norm1_g.reshape(depth, 1, d), mod3, layer, seq, shift_idx=0, scale_idx=1)
        proj = _project(h, w_in, layer)
        ret = _retention(proj, batch, seq, ret_col0)
        att = _chunk_attention(proj, rel_bias[layer], q_norm_g[layer], k_norm_g[layer], batch, seq, att_col0)
        merged = _merge(ret, att, w_ret_out, w_att_out, proj, gate_col0, layer)
        x2 = _out_proj(merged, w_out, x2, mod3, layer, seq, gate_idx=2)
        h_packed, logits = _norm_router(x2, norm2_g.reshape(depth, 1, d), mod3, w_router, b_router, layer, seq,
                                        shift_idx=3, scale_idx=4)
        gates, dest, sorted_tok, block_first, block_e, block_nsub, n_valid = _routing(logits)
        ys = _experts(h_packed, sorted_tok, block_first, block_e, block_nsub, n_valid, w1, b1, w2, b2, layer)
        ys = _clear_unused_blocks(ys, n_valid)
        x2 = _combine(ys, dest, gates, x2, mod3, seq, gate_idx=5)
    return x2.reshape(batch, seq, d)
```

```python
import functools

import numpy as np
import jax
import jax.numpy as jnp
from jax import lax
from jax.experimental import pallas as pl
from jax.experimental.pallas import tpu as pltpu

F32 = jnp.float32
BF16 = jnp.bfloat16
U32 = jnp.uint32

CHUNK = 64
NORM_EPS = 1e-6
RET_HEADS = 8
RET_QK_DIM = 256
RET_V_DIM = 512
ROPE_BASE = 10000.0
ATT_HEADS = 16
ATT_HEAD_DIM = 128
PREV_CHUNKS = 8
MAX_REL = 128
N_EXPERTS = 32
TOP_K = 4
SWIGLU_ALPHA = 1.702
SWIGLU_LIMIT = 7.0
N_MOD = 6

RET_QK_W = RET_HEADS * RET_QK_DIM
RET_V_W = RET_HEADS * RET_V_DIM
ATT_W = ATT_HEADS * ATT_HEAD_DIM

V7X_VMEM_LIMIT_BYTES = 58 * 1024 * 1024
HI16 = 0xFFFF0000

ADA_TN = 1024
NORM_TM = 256
PROJ_TM, PROJ_TN = 2048, 512
RET_BLOCK = 256
RET_HEADS_PER_STEP = 2
ATT_BLOCK = 256
ATT_HEADS_PER_STEP = 2
MERGE_TM, MERGE_TN = 512, 512
OUT_TM, OUT_TN = 1024, 512
EXPERT_BM = 1280
EXPERT_SUB = 256
EXPERT_TF = 256
EXPERT_TN = 256
EXPERT_PERM_W = 256
EXPERT_ISSUE_STEPS = 10
GATHER_GROUP = 32
COMBINE_TM = 256


def _params(semantics):
    return pltpu.CompilerParams(dimension_semantics=semantics, vmem_limit_bytes=V7X_VMEM_LIMIT_BYTES)


def _silu(t):
    return t * jax.nn.sigmoid(t)


def _pack_halves(y):
    n = y.shape[1] // 2
    hi = lax.bitcast_convert_type(y[:, :n].astype(BF16).astype(F32), U32)
    lo = lax.bitcast_convert_type(y[:, n:].astype(BF16).astype(F32), U32)
    return hi | (lo >> 16)


def _unpack_halves(p):
    a = lax.bitcast_convert_type(p & jnp.uint32(HI16), F32)
    b = lax.bitcast_convert_type(p << 16, F32)
    return a, b


def _ada_kernel(c_ref, w_ref, b_ref, o_ref):
    a = _silu(c_ref[...]).astype(BF16)
    o_ref[...] = jnp.dot(a, w_ref[...].astype(BF16), preferred_element_type=F32) + b_ref[...]


def _ada_ln(c8, w_ada, b_ada, layer):
    d = c8.shape[1]
    n = w_ada.shape[2]
    return pl.pallas_call(
        _ada_kernel,
        out_shape=jax.ShapeDtypeStruct((8, n), F32),
        grid=(n // ADA_TN,),
        in_specs=[
            pl.BlockSpec((8, d), lambda j: (0, 0)),
            pl.BlockSpec((None, d, ADA_TN), lambda j: (layer, 0, j)),
            pl.BlockSpec((None, 1, ADA_TN), lambda j: (layer, 0, j)),
        ],
        out_specs=pl.BlockSpec((8, ADA_TN), lambda j: (0, j)),
        compiler_params=_params(("arbitrary",)),
        name="ada_ln",
    )(c8, w_ada, b_ada.reshape(b_ada.shape[0], 1, n))


def _modulated_norm(x, g, scale, shift):
    xn = x * lax.rsqrt(jnp.mean(x * x, axis=-1, keepdims=True) + NORM_EPS)
    return (xn * g) * (1.0 + scale) + shift


def _norm_mod_kernel(x_ref, g_ref, sc_ref, sh_ref, o_ref):
    o_ref[...] = _modulated_norm(x_ref[...], g_ref[...], sc_ref[...], sh_ref[...]).astype(o_ref.dtype)


def _norm_mod(x2, g, mod3, layer, seq, shift_idx, scale_idx):
    t, d = x2.shape
    tm = NORM_TM

    def mod_spec(idx):
        return pl.BlockSpec((None, 1, d), lambda i: ((i * tm // seq) * N_MOD + idx, 0, 0))

    return pl.pallas_call(
        _norm_mod_kernel,
        out_shape=jax.ShapeDtypeStruct((t, d), BF16),
        grid=(t // tm,),
        in_specs=[
            pl.BlockSpec((tm, d), lambda i: (i, 0)),
            pl.BlockSpec((None, 1, d), lambda i: (layer, 0, 0)),
            mod_spec(scale_idx),
            mod_spec(shift_idx),
        ],
        out_specs=pl.BlockSpec((tm, d), lambda i: (i, 0)),
        compiler_params=_params(("arbitrary",)),
        name="norm_mod",
    )(x2, g, mod3, mod3)


def _norm_router_kernel(x_ref, g_ref, sc_ref, sh_ref, wr_ref, br_ref, h_ref, lg_ref):
    h = _modulated_norm(x_ref[...], g_ref[...], sc_ref[...], sh_ref[...])
    h_ref[...] = _pack_halves(h)
    h_hi = h.astype(BF16)
    h_lo = (h - h_hi.astype(F32)).astype(BF16)
    w = wr_ref[...]
    w_hi = w.astype(BF16)
    w_lo = (w - w_hi.astype(F32)).astype(BF16)
    lg = (jnp.dot(h_hi, w_hi, preferred_element_type=F32)
          + jnp.dot(h_lo, w_hi, preferred_element_type=F32)
          + jnp.dot(h_hi, w_lo, preferred_element_type=F32))
    lg_ref[...] = lg + br_ref[...]


def _norm_router(x2, g, mod3, w_router, b_router, layer, seq, shift_idx, scale_idx):
    t, d = x2.shape
    e = w_router.shape[2]
    tm = NORM_TM

    def mod_spec(idx):
        return pl.BlockSpec((None, 1, d), lambda i: ((i * tm // seq) * N_MOD + idx, 0, 0))

    return pl.pallas_call(
        _norm_router_kernel,
        out_shape=(jax.ShapeDtypeStruct((t, d // 2), U32), jax.ShapeDtypeStruct((t, e), F32)),
        grid=(t // tm,),
        in_specs=[
            pl.BlockSpec((tm, d), lambda i: (i, 0)),
            pl.BlockSpec((None, 1, d), lambda i: (layer, 0, 0)),
            mod_spec(scale_idx),
            mod_spec(shift_idx),
            pl.BlockSpec((None, d, e), lambda i: (layer, 0, 0)),
            pl.BlockSpec((None, 1, e), lambda i: (layer, 0, 0)),
        ],
        out_specs=(pl.BlockSpec((tm, d // 2), lambda i: (i, 0)), pl.BlockSpec((tm, e), lambda i: (i, 0))),
        compiler_params=_params(("arbitrary",)),
        name="norm_router",
    )(x2, g, mod3, mod3, w_router, b_router.reshape(b_router.shape[0], 1, e))


def _proj_kernel(a_ref, w_ref, o_ref):
    o_ref[...] = jnp.dot(a_ref[...], w_ref[...].astype(BF16), preferred_element_type=F32).astype(o_ref.dtype)


def _project(a, w, layer):
    t, k = a.shape
    n = w.shape[2]
    tm, tn = PROJ_TM, PROJ_TN
    return pl.pallas_call(
        _proj_kernel,
        out_shape=jax.ShapeDtypeStruct((t, n), BF16),
        grid=(n // tn, t // tm),
        in_specs=[
            pl.BlockSpec((tm, k), lambda j, i: (i, 0)),
            pl.BlockSpec((None, k, tn), lambda j, i: (layer, 0, j)),
        ],
        out_specs=pl.BlockSpec((tm, tn), lambda j, i: (i, j)),
        compiler_params=_params(("arbitrary", "arbitrary")),
        name="in_proj",
    )(a, w)


def _retention_kernel(q_ref, k_ref, v_ref, g_ref, cos_ref, sin_ref, dec_ref, qd_ref, kd_ref, cd_ref,
                      o_ref, state_ref):
    @pl.when(pl.program_id(2) == 0)
    def _():
        state_ref[...] = jnp.zeros_like(state_ref)

    cos = cos_ref[...]
    sin = sin_ref[...]
    half = RET_QK_DIM // 2
    contract_last = (((1,), (1,)), ((), ()))
    contract_rows = (((0,), (0,)), ((), ()))

    def rotary(t):
        t1, t2 = t[:, :half], t[:, half:]
        return jnp.concatenate([t1 * cos - t2 * sin, t1 * sin + t2 * cos], axis=1)

    for hh in range(RET_HEADS_PER_STEP):
        qk = slice(hh * RET_QK_DIM, (hh + 1) * RET_QK_DIM)
        vg = slice(hh * RET_V_DIM, (hh + 1) * RET_V_DIM)
        q = rotary(q_ref[:, qk].astype(F32))
        k = rotary(k_ref[:, qk].astype(F32)) * (RET_QK_DIM ** -0.5)
        v = v_ref[:, vg]
        scores = lax.dot_general(q.astype(BF16), k.astype(BF16), contract_last,
                                 preferred_element_type=F32) * dec_ref[hh]
        state = state_ref[hh]
        out = (jnp.dot(scores.astype(BF16), v, preferred_element_type=F32)
               + jnp.dot((q * qd_ref[hh]).astype(BF16), state.astype(BF16), preferred_element_type=F32))
        state_ref[hh] = state * cd_ref[hh] + lax.dot_general(
            (k * kd_ref[hh]).astype(BF16), v, contract_rows, preferred_element_type=F32)
        on = out * lax.rsqrt(jnp.mean(out * out, axis=-1, keepdims=True) + NORM_EPS)
        o_ref[:, vg] = (_silu(g_ref[:, vg].astype(F32)) * on).astype(o_ref.dtype)


def _retention_tables(seq):
    half = RET_QK_DIM // 2
    inv_freq = ROPE_BASE ** (-np.arange(half, dtype=np.float64) / half)
    ang = np.arange(seq, dtype=np.float64)[:, None] * inv_freq[None, :]
    log_gamma = np.log(1.0 - 2.0 ** (-5.0 - np.arange(RET_HEADS, dtype=np.float64)))[:, None, None]
    pos = np.arange(RET_BLOCK, dtype=np.float64)
    n, m = pos[:, None], pos[None, :]
    cn, cm = np.floor(n / CHUNK), np.floor(m / CHUNK)
    dist = np.where(cn == cm, np.abs(n - m), n - m)
    dec = np.where((cm <= cn)[None], np.exp(log_gamma * dist[None]), 0.0)
    qd = np.broadcast_to(np.exp(log_gamma * (pos[None, :, None] + 1.0)), (RET_HEADS, RET_BLOCK, RET_QK_DIM))
    kd = np.broadcast_to(np.exp(log_gamma * (RET_BLOCK - 1.0 - pos[None, :, None])),
                         (RET_HEADS, RET_BLOCK, RET_QK_DIM))
    cd = np.broadcast_to(np.exp(log_gamma * RET_BLOCK), (RET_HEADS, 1, RET_V_DIM))
    return tuple(jnp.asarray(np.ascontiguousarray(v), F32) for v in (np.cos(ang), np.sin(ang), dec, qd, kd, cd))


def _retention(proj, batch, seq, col0):
    t = proj.shape[0]
    r = RET_BLOCK
    nsb = seq // r
    cos, sin, dec, qd, kd, cd = _retention_tables(seq)
    hp = RET_HEADS_PER_STEP
    groups = RET_HEADS // hp
    qk_w, v_w = hp * RET_QK_DIM, hp * RET_V_DIM
    assert col0 % qk_w == 0 and (col0 + 2 * RET_QK_W) % v_w == 0
    qk_blk0 = col0 // qk_w
    v_blk0 = (col0 + 2 * RET_QK_W) // v_w
    row = lambda b, h, s: b * nsb + s
    head_tab = lambda shape: pl.BlockSpec((hp,) + shape, lambda b, h, s: (h, 0, 0))
    half = RET_QK_DIM // 2
    return pl.pallas_call(
        _retention_kernel,
        out_shape=jax.ShapeDtypeStruct((t, RET_V_W), BF16),
        grid=(batch, groups, nsb),
        in_specs=[
            pl.BlockSpec((r, qk_w), lambda b, h, s: (row(b, h, s), qk_blk0 + h)),
            pl.BlockSpec((r, qk_w), lambda b, h, s: (row(b, h, s), qk_blk0 + groups + h)),
            pl.BlockSpec((r, v_w), lambda b, h, s: (row(b, h, s), v_blk0 + h)),
            pl.BlockSpec((r, v_w), lambda b, h, s: (row(b, h, s), v_blk0 + groups + h)),
            pl.BlockSpec((r, half), lambda b, h, s: (s, 0)),
            pl.BlockSpec((r, half), lambda b, h, s: (s, 0)),
            head_tab((r, r)),
            head_tab((r, RET_QK_DIM)),
            head_tab((r, RET_QK_DIM)),
            head_tab((1, RET_V_DIM)),
        ],
        out_specs=pl.BlockSpec((r, v_w), lambda b, h, s: (row(b, h, s), h)),
        scratch_shapes=[pltpu.VMEM((hp, RET_QK_DIM, RET_V_DIM), F32)],
        compiler_params=_params(("arbitrary", "arbitrary", "arbitrary")),
        name="retention",
    )(proj, proj, proj, proj, cos, sin, dec, qd, kd, cd)


ATT_KEY_BLOCKS = PREV_CHUNKS * CHUNK // ATT_BLOCK + 1
NEG = float(np.finfo(np.float32).min)


def _attention_kernel(q_ref, k0_ref, k1_ref, k2_ref, v0_ref, v1_ref, v2_ref, tab_ref, gq_ref, gk_ref, o_ref):
    i = pl.program_id(1)

    def rms(t, g):
        tf = t.astype(F32)
        return (tf * lax.rsqrt(jnp.mean(tf * tf, axis=-1, keepdims=True) + NORM_EPS)) * g

    dh = ATT_HEAD_DIM
    k_all = jnp.concatenate([k0_ref[...], k1_ref[...], k2_ref[...]], axis=0)
    v_all = jnp.concatenate([v0_ref[...], v1_ref[...], v2_ref[...]], axis=0)
    outs = []
    for hh in range(ATT_HEADS_PER_STEP):
        lanes = slice(hh * dh, (hh + 1) * dh)
        q = rms(q_ref[:, lanes], gq_ref[...]).astype(BF16)
        k = rms(k_all[:, lanes], gk_ref[...]).astype(BF16)
        sc = lax.dot_general(q, k, (((1,), (1,)), ((), ())), preferred_element_type=F32)
        sc = sc * (ATT_HEAD_DIM ** -0.5) + tab_ref[hh]
        col = lax.broadcasted_iota(jnp.int32, sc.shape, 1)
        sc = jnp.where(col >= (ATT_KEY_BLOCKS - 1 - i) * ATT_BLOCK, sc, NEG)
        p = jnp.exp(sc - jnp.max(sc, axis=-1, keepdims=True))
        denom = jnp.sum(p, axis=-1, keepdims=True)
        outs.append(jnp.dot(p.astype(BF16), v_all[:, lanes], preferred_element_type=F32) / denom)
    o_ref[...] = jnp.concatenate(outs, axis=1).astype(o_ref.dtype)


def _attention_table(rel_bias):
    a = ATT_BLOCK
    w = ATT_KEY_BLOCKS * a
    pad = PREV_CHUNKS * CHUNK
    heads = rel_bias.shape[0]
    period = 1024
    assert a - 1 + w - 1 <= period - 2
    rel_of_i = pad + a - 1 - np.arange(period)
    idx = np.clip(np.minimum(rel_of_i, MAX_REL) + (CHUNK - 1), 0, rel_bias.shape[-1] - 1)
    u = rel_bias[:, idx].astype(F32)
    skew = jnp.tile(u, (1, a))[:, :a * (period - 1)].reshape(heads, a, period - 1)
    bias = skew[:, :, a - 1:a - 1 + w]
    n = np.arange(a)[:, None]
    j = np.arange(w)[None, :]
    qc = n // CHUNK
    kc = j // CHUNK - PREV_CHUNKS
    valid = (kc <= qc) & (kc >= qc - PREV_CHUNKS)
    return jnp.where(jnp.asarray(valid)[None], bias, NEG)


def _chunk_attention(proj, rel_bias, q_norm_g, k_norm_g, batch, seq, col0):
    assert ATT_KEY_BLOCKS == 3
    t = proj.shape[0]
    a = ATT_BLOCK
    nqb = seq // a
    dh = ATT_HEAD_DIM
    hw = ATT_HEADS_PER_STEP * dh
    groups = ATT_HEADS // ATT_HEADS_PER_STEP
    assert col0 % hw == 0 and ATT_W % hw == 0
    blk0 = col0 // hw
    tab = _attention_table(rel_bias)

    def kv_spec(which, jj):
        return pl.BlockSpec(
            (a, hw),
            lambda h, i, b: (b * nqb + jnp.maximum(i - (ATT_KEY_BLOCKS - 1) + jj, 0), blk0 + which * groups + h))

    return pl.pallas_call(
        _attention_kernel,
        out_shape=jax.ShapeDtypeStruct((t, ATT_W), BF16),
        grid=(groups, nqb, batch),
        in_specs=[
            pl.BlockSpec((a, hw), lambda h, i, b: (b * nqb + i, blk0 + h)),
            kv_spec(1, 0), kv_spec(1, 1), kv_spec(1, 2),
            kv_spec(2, 0), kv_spec(2, 1), kv_spec(2, 2),
            pl.BlockSpec((ATT_HEADS_PER_STEP, a, ATT_KEY_BLOCKS * a), lambda h, i, b: (h, 0, 0)),
            pl.BlockSpec((1, dh), lambda h, i, b: (0, 0)),
            pl.BlockSpec((1, dh), lambda h, i, b: (0, 0)),
        ],
        out_specs=pl.BlockSpec((a, hw), lambda h, i, b: (b * nqb + i, h)),
        compiler_params=_params(("arbitrary", "arbitrary", "arbitrary")),
        name="chunk_attention",
    )(proj, proj, proj, proj, proj, proj, proj, tab, q_norm_g.reshape(1, dh), k_norm_g.reshape(1, dh))


def _merge_kernel(r_ref, a_ref, wr_ref, wa_ref, gr_ref, ga_ref, o_ref):
    yr = jnp.dot(r_ref[...], wr_ref[...].astype(BF16), preferred_element_type=F32)
    ya = jnp.dot(a_ref[...], wa_ref[...].astype(BF16), preferred_element_type=F32)
    merged = jax.nn.sigmoid(gr_ref[...].astype(F32)) * yr + jax.nn.sigmoid(ga_ref[...].astype(F32)) * ya
    o_ref[...] = merged.astype(o_ref.dtype)


def _merge(ret, att, w_ret_out, w_att_out, proj, gate_col0, layer):
    t = ret.shape[0]
    d = w_ret_out.shape[2]
    tm, tn = MERGE_TM, MERGE_TN
    gblk = gate_col0 // tn
    return pl.pallas_call(
        _merge_kernel,
        out_shape=jax.ShapeDtypeStruct((t, d), BF16),
        grid=(d // tn, t // tm),
        in_specs=[
            pl.BlockSpec((tm, ret.shape[1]), lambda j, i: (i, 0)),
            pl.BlockSpec((tm, att.shape[1]), lambda j, i: (i, 0)),
            pl.BlockSpec((None, ret.shape[1], tn), lambda j, i: (layer, 0, j)),
            pl.BlockSpec((None, att.shape[1], tn), lambda j, i: (layer, 0, j)),
            pl.BlockSpec((tm, tn), lambda j, i: (i, gblk + j)),
            pl.BlockSpec((tm, tn), lambda j, i: (i, gblk + d // tn + j)),
        ],
        out_specs=pl.BlockSpec((tm, tn), lambda j, i: (i, j)),
        compiler_params=_params(("arbitrary", "arbitrary")),
        name="merge",
    )(ret, att, w_ret_out, w_att_out, proj, proj)


def _out_proj_kernel(a_ref, w_ref, x_ref, gate_ref, o_ref):
    acc = jnp.dot(a_ref[...], w_ref[...].astype(BF16), preferred_element_type=F32)
    o_ref[...] = x_ref[...] + gate_ref[...] * acc


def _out_proj(a, w, x2, mod3, layer, seq, gate_idx):
    t, k = a.shape
    d = w.shape[2]
    tm, tn = OUT_TM, OUT_TN
    return pl.pallas_call(
        _out_proj_kernel,
        out_shape=jax.ShapeDtypeStruct((t, d), F32),
        grid=(d // tn, t // tm),
        in_specs=[
            pl.BlockSpec((tm, k), lambda j, i: (i, 0)),
            pl.BlockSpec((None, k, tn), lambda j, i: (layer, 0, j)),
            pl.BlockSpec((tm, tn), lambda j, i: (i, j)),
            pl.BlockSpec((None, 1, tn), lambda j, i: ((i * tm // seq) * N_MOD + gate_idx, 0, j)),
        ],
        out_specs=pl.BlockSpec((tm, tn), lambda j, i: (i, j)),
        compiler_params=_params(("arbitrary", "arbitrary")),
        name="out_proj",
    )(a, w, x2, mod3)


def _routing(logits):
    t = logits.shape[0]
    bm = EXPERT_BM
    a = t * TOP_K
    top_val, top_idx = lax.top_k(logits, TOP_K)
    gate = jax.nn.softmax(top_val, axis=-1)
    onehot = top_idx[:, :, None] == jnp.arange(N_EXPERTS, dtype=jnp.int32)[None, None, :]
    picked = jnp.any(onehot, axis=1).astype(jnp.int32)
    inclusive = jnp.cumsum(picked, axis=0)
    sizes = inclusive[-1]
    sub = EXPERT_SUB
    padded = (sizes + sub - 1) // sub * sub
    pad_end = jnp.cumsum(padded)
    pad_start = pad_end - padded
    start = jnp.cumsum(sizes) - sizes
    slot = (pad_start[None, :] + inclusive - picked)[:, None, :]
    dest = jnp.sum(jnp.where(onehot, slot, 0), axis=-1).astype(jnp.int32).reshape(a)
    order = jnp.argsort(top_idx.reshape(a))
    n_blocks = a // bm + N_EXPERTS
    group_blocks = (sizes + bm - 1) // bm
    blocks_end = jnp.cumsum(group_blocks)
    n_valid = blocks_end[-1].astype(jnp.int32)
    blk = jnp.arange(n_blocks, dtype=jnp.int32)
    block_e = jnp.minimum(jnp.searchsorted(blocks_end, blk, side='right'), N_EXPERTS - 1).astype(jnp.int32)
    chunk = blk - (blocks_end - group_blocks)[block_e]
    block_valid = blk < n_valid
    block_rows = jnp.where(block_valid, jnp.clip(sizes[block_e] - chunk * bm, 0, bm), 0)
    block_nsub = ((block_rows + sub - 1) // sub).astype(jnp.int32)
    block_row0 = jnp.where(block_valid, pad_start[block_e] + chunk * bm, 0).astype(jnp.int32)
    n_chunks = -(-a // bm) + 2
    sorted_tok = jnp.zeros((n_chunks * bm,), jnp.int32).at[:a].set((order // TOP_K).astype(jnp.int32))
    block_first = jnp.where(block_valid, start[block_e] + chunk * bm, 0).astype(jnp.int32)
    tables = (block_e, block_nsub, block_first, block_row0, jnp.stack([n_valid, pad_end[-1].astype(jnp.int32)]))
    return gate, dest, sorted_tok.reshape(n_chunks, 1, bm), tables


def _expert_kernel(be_ref, ns_ref, first_ref, row0_ref, info_ref, tok_hbm, h_hbm, w1_hbm, b1_ref, w2_hbm, b2_ref,
                   perm_ref, ys_hbm, w1buf, w2buf, gbuf, x_scr, act_scr, obuf, tokbuf,
                   sem_w1, sem_w2, sem_g, sem_o, sem_t, *, layer, nj1, nj2):
    sub = EXPERT_SUB
    group = GATHER_GROUP
    bm, d = x_scr.shape
    half = d // 2
    tf = act_scr.shape[2]
    tn = obuf.shape[2]
    steps = nj1 + nj2
    nb_max = be_ref.shape[0]
    n_valid = info_ref[0]
    total = n_valid * steps
    per_step = bm // EXPERT_ISSUE_STEPS

    def for_rows(nsub, fn):
        def pair(p, carry):
            fn(pl.multiple_of(p * 2 * sub, 2 * sub), 2 * sub)
            return carry
        lax.fori_loop(0, nsub // 2, pair, 0)

        @pl.when(nsub % 2 == 1)
        def _():
            fn(pl.multiple_of((nsub - 1) * sub, sub), sub)

    def w1_copy(e, j, slot):
        cols = pl.ds(pl.multiple_of(j * 2 * tf, 2 * tf), 2 * tf)
        return pltpu.make_async_copy(w1_hbm.at[layer, e, :, cols], w1buf.at[slot], sem_w1.at[slot])

    def w2_copies(e, n, slot):
        return [pltpu.make_async_copy(w2_hbm.at[layer, e, :, pl.ds(pl.multiple_of(c0, tn), tn)],
                                      w2buf.at[slot, :, pl.ds(k * tn, tn)], sem_w2.at[slot])
                for k, c0 in enumerate((n * tn, half + n * tn))]

    def tok_copies(blk, slot):
        chunk = first_ref[blk] // bm
        return [pltpu.make_async_copy(tok_hbm.at[chunk + k], tokbuf.at[slot, k], sem_t.at[slot]) for k in (0, 1)]

    def row_copy(slot, offset, r):
        at = offset + r
        tok = jnp.where(at < bm, tokbuf[slot, 0, 0, jnp.minimum(at, bm - 1)],
                        tokbuf[slot, 1, 0, jnp.maximum(at - bm, 0)])
        return pltpu.make_async_copy(h_hbm.at[pl.ds(tok, 1)], gbuf.at[pl.ds(r, 1)], sem_g)

    def out_copy(slot, r0, m, row0, n):
        dst = ys_hbm.at[pl.ds(pl.multiple_of(row0 + r0, sub), m), pl.ds(pl.multiple_of(n * tn, tn), tn)]
        return pltpu.make_async_copy(obuf.at[slot, pl.ds(r0, m)], dst, sem_o.at[slot])

    @pl.when(n_valid > 0)
    def _():
        for c in tok_copies(0, 0):
            c.start()
        for c in tok_copies(0, 0):
            c.wait()
        offset = lax.rem(first_ref[0], bm)

        def start(r, carry):
            row_copy(0, offset, r).start()
            return carry
        lax.fori_loop(0, ns_ref[0] * sub, start, 0)
        w1_copy(be_ref[0], 0, 0).start()

    def item(it, carry):
        b = it // steps
        j = it - b * steps
        e = be_ref[b]
        nsub = ns_ref[b]
        has_next = b + 1 < n_valid
        nxt = jnp.minimum(b + 1, nb_max - 1)

        nit = it + 1
        nbk = jnp.minimum(nit // steps, nb_max - 1)
        nj = nit - (nit // steps) * steps

        @pl.when((nit < total) & (nj < nj1))
        def _():
            w1_copy(be_ref[nbk], nj, lax.rem(nj, 2)).start()

        @pl.when((nit < total) & (nj >= nj1))
        def _():
            for c in w2_copies(be_ref[nbk], nj - nj1, lax.rem(nj - nj1, 2)):
                c.start()

        @pl.when(j < nj1)
        def _():
            w1_copy(e, j, lax.rem(j, 2)).wait()

        @pl.when(j >= nj1)
        def _():
            for c in w2_copies(e, j - nj1, lax.rem(j - nj1, 2)):
                c.wait()

        @pl.when(j == 0)
        def _():
            def wait_rows(s, carry):
                pltpu.make_async_copy(h_hbm.at[pl.ds(0, sub)], gbuf.at[pl.ds(0, sub)], sem_g).wait()
                return carry
            lax.fori_loop(0, nsub, wait_rows, 0)

            def unpack_rows(s, carry):
                rows = pl.ds(pl.multiple_of(s * sub, sub), sub)
                xa, xb = _unpack_halves(gbuf[rows, :])
                x_scr[rows, pl.ds(0, half)] = xa.astype(BF16)
                x_scr[rows, pl.ds(half, half)] = xb.astype(BF16)
                return carry
            lax.fori_loop(0, nsub, unpack_rows, 0)

            @pl.when(has_next)
            def _():
                for c in tok_copies(nxt, lax.rem(b + 1, 2)):
                    c.start()

        @pl.when((j == 1) & has_next)
        def _():
            for c in tok_copies(nxt, lax.rem(b + 1, 2)):
                c.wait()

        @pl.when((j >= 1) & (j <= EXPERT_ISSUE_STEPS) & has_next)
        def _():
            slot = lax.rem(b + 1, 2)
            offset = lax.rem(first_ref[nxt], bm)
            rows_next = ns_ref[nxt] * sub
            for g in range(per_step // group):
                first_row = (j - 1) * per_step + g * group

                @pl.when(first_row < rows_next)
                def _():
                    for i in range(group):
                        row_copy(slot, offset, first_row + i).start()

        @pl.when(j < nj1)
        def _():
            wslot = lax.rem(j, 2)
            bias = b1_ref[pl.ds(e * nj1 + j, 1), :]

            def first_matmul(r0, m):
                rows = pl.ds(r0, m)
                hb = jnp.dot(x_scr[rows, :], w1buf[wslot].astype(BF16), preferred_element_type=F32) + bias
                hb = hb.astype(BF16)
                gw = perm_ref.shape[0]
                sel = [jnp.dot(hb[:, c * gw:(c + 1) * gw], perm_ref[...], preferred_element_type=F32)
                       for c in range(2 * tf // gw)]
                x_glu = jnp.minimum(jnp.concatenate([s[:, :gw // 2] for s in sel], axis=1), SWIGLU_LIMIT)
                x_lin = jnp.clip(jnp.concatenate([s[:, gw // 2:] for s in sel], axis=1),
                                 -SWIGLU_LIMIT, SWIGLU_LIMIT)
                act = x_glu * jax.nn.sigmoid(SWIGLU_ALPHA * x_glu) * (x_lin + 1.0)
                act_scr[j, rows, :] = act.astype(BF16)
            for_rows(nsub, first_matmul)

        @pl.when(j >= nj1)
        def _():
            n = j - nj1
            slot = lax.rem(n, 2)
            row0 = row0_ref[b]
            bias = jnp.concatenate([b2_ref[pl.ds(e * 2 * nj2 + n, 1), :],
                                    b2_ref[pl.ds(e * 2 * nj2 + nj2 + n, 1), :]], axis=1)
            sent = jnp.where(n >= 2, nsub, jnp.where(b > 0, ns_ref[jnp.maximum(b - 1, 0)], 0))
            for_rows(sent, lambda r0, m: out_copy(slot, r0, m, 0, 0).wait())

            def second_matmul(r0, m):
                rows = pl.ds(r0, m)
                act = jnp.concatenate([act_scr[t, rows, :] for t in range(nj1)], axis=1)
                y = jnp.dot(act, w2buf[slot].astype(BF16), preferred_element_type=F32) + bias
                obuf[slot, rows, :] = _pack_halves(y)
                out_copy(slot, r0, m, row0, n).start()
            for_rows(nsub, second_matmul)

        return carry

    lax.fori_loop(0, total, item, 0)

    @pl.when(n_valid > 0)
    def _():
        last = ns_ref[jnp.maximum(n_valid - 1, 0)]
        for slot in (0, 1):
            for_rows(last, lambda r0, m: out_copy(slot, r0, m, 0, 0).wait())

    used = info_ref[1]
    n_tail = (ys_hbm.shape[0] - used) // sub
    gbuf[pl.ds(0, sub), :] = jnp.zeros((sub, half), gbuf.dtype)

    def tail_copy(i):
        dst = ys_hbm.at[pl.ds(pl.multiple_of(used + i * sub, sub), sub)]
        return pltpu.make_async_copy(gbuf.at[pl.ds(0, sub)], dst, sem_g)

    def tail_start(i, carry):
        tail_copy(i).start()
        return carry

    def tail_wait(i, carry):
        tail_copy(i).wait()
        return carry
    lax.fori_loop(0, n_tail, tail_start, 0)
    lax.fori_loop(0, n_tail, tail_wait, 0)


def _deinterleave_matrix(tf):
    p = np.zeros((2 * tf, 2 * tf), np.float32)
    f = np.arange(tf)
    p[2 * f, f] = 1.0
    p[2 * f + 1, tf + f] = 1.0
    return jnp.asarray(p, BF16)


def _experts(h_packed, sorted_tok, tables, w1, b1, w2, b2, layer):
    t, half = h_packed.shape
    d = 2 * half
    n_experts, f = w2.shape[1], w2.shape[2]
    bm, tf, tn = EXPERT_BM, EXPERT_TF, EXPERT_TN
    nj1 = f // tf
    nj2 = half // tn
    assert nj1 % 2 == 0 and nj2 % 2 == 0
    assert 1 + EXPERT_ISSUE_STEPS <= nj1 + nj2 and bm % (EXPERT_ISSUE_STEPS * GATHER_GROUP) == 0
    assert bm % EXPERT_SUB == 0 and t >= EXPERT_SUB
    p_rows = t * TOP_K + n_experts * EXPERT_SUB
    whole = lambda shape: pl.BlockSpec(shape, lambda i, *_: (0,) * len(shape))
    hbm = pl.BlockSpec(memory_space=pl.ANY)
    return pl.pallas_call(
        functools.partial(_expert_kernel, layer=layer, nj1=nj1, nj2=nj2),
        out_shape=jax.ShapeDtypeStruct((p_rows, half), U32),
        grid_spec=pltpu.PrefetchScalarGridSpec(
            num_scalar_prefetch=5,
            grid=(1,),
            in_specs=[hbm, hbm, hbm, whole((n_experts * nj1, 2 * tf)), hbm, whole((n_experts * 2 * nj2, tn)),
                      whole((EXPERT_PERM_W, EXPERT_PERM_W))],
            out_specs=hbm,
            scratch_shapes=[
                pltpu.VMEM((2, d, 2 * tf), F32),
                pltpu.VMEM((2, f, 2 * tn), F32),
                pltpu.VMEM((bm, half), U32),
                pltpu.VMEM((bm, d), BF16),
                pltpu.VMEM((nj1, bm, tf), BF16),
                pltpu.VMEM((2, bm, tn), U32),
                pltpu.SMEM((2, 2, 1, bm), jnp.int32),
                pltpu.SemaphoreType.DMA((2,)),
                pltpu.SemaphoreType.DMA((2,)),
                pltpu.SemaphoreType.DMA(()),
                pltpu.SemaphoreType.DMA((2,)),
                pltpu.SemaphoreType.DMA((2,)),
            ],
        ),
        compiler_params=_params(("arbitrary",)),
        name="experts",
    )(*tables, sorted_tok, h_packed, w1, b1[layer].reshape(n_experts * nj1, 2 * tf), w2,
      b2[layer].reshape(n_experts * 2 * nj2, tn), _deinterleave_matrix(EXPERT_PERM_W // 2))


def _combine_kernel(dest_ref, ys_hbm, x_ref, gate_ref, g_ref, o_ref, buf, sem):
    tm = x_ref.shape[0]
    half = buf.shape[2]

    def row_copy(src_row, k, r):
        return pltpu.make_async_copy(ys_hbm.at[pl.ds(src_row, 1)], buf.at[k, pl.ds(r, 1)], sem)

    def start(r, carry):
        for k in range(TOP_K):
            row_copy(dest_ref[0, r * TOP_K + k], k, r).start()
        return carry

    def wait(r, carry):
        for k in range(TOP_K):
            row_copy(0, k, r).wait()
        return carry

    lax.fori_loop(0, tm, start, 0)
    lax.fori_loop(0, tm, wait, 0)
    sublanes = 8

    def chunk(c, carry):
        rows = pl.ds(pl.multiple_of(c * sublanes, sublanes), sublanes)
        g = g_ref[rows, :]
        ya = jnp.zeros((sublanes, half), F32)
        yb = jnp.zeros((sublanes, half), F32)
        for k in range(TOP_K):
            a, b = _unpack_halves(buf[k, rows, :])
            ya = ya + g[:, k:k + 1] * a
            yb = yb + g[:, k:k + 1] * b
        o_ref[rows, pl.ds(0, half)] = x_ref[rows, pl.ds(0, half)] + gate_ref[:, pl.ds(0, half)] * ya
        o_ref[rows, pl.ds(half, half)] = x_ref[rows, pl.ds(half, half)] + gate_ref[:, pl.ds(half, half)] * yb
        return carry

    lax.fori_loop(0, tm // sublanes, chunk, 0)


def _combine(ys, dest, gates, x2, mod3, seq, gate_idx):
    t, d = x2.shape
    tm = COMBINE_TM
    half = ys.shape[1]
    return pl.pallas_call(
        _combine_kernel,
        out_shape=jax.ShapeDtypeStruct((t, d), F32),
        grid=(t // tm,),
        in_specs=[
            pl.BlockSpec((None, 1, tm * TOP_K), lambda i: (i, 0, 0), memory_space=pltpu.SMEM),
            pl.BlockSpec(memory_space=pl.ANY),
            pl.BlockSpec((tm, d), lambda i: (i, 0)),
            pl.BlockSpec((None, 1, d), lambda i: ((i * tm // seq) * N_MOD + gate_idx, 0, 0)),
            pl.BlockSpec((tm, TOP_K), lambda i: (i, 0)),
        ],
        out_specs=pl.BlockSpec((tm, d), lambda i: (i, 0)),
        scratch_shapes=[pltpu.VMEM((TOP_K, tm, half), U32), pltpu.SemaphoreType.DMA(())],
        compiler_params=_params(("arbitrary",)),
        name="combine",
    )(dest.reshape(t // tm, 1, tm * TOP_K), ys, x2, mod3, gates)


def kernel(x, c, w_ada, b_ada, norm1_g, w_in, q_norm_g, k_norm_g, rel_bias, w_ret_out, w_att_out, w_out,
           norm2_g, w_router, b_router, w1, b1, w2, b2):
    batch, seq, d = x.shape
    depth = w_ada.shape[0]
    t = batch * seq
    assert batch <= 8 and seq % max(RET_BLOCK, ATT_BLOCK, NORM_TM, OUT_TM, COMBINE_TM) == 0 and t % PROJ_TM == 0
    x2 = x.reshape(t, d)
    c8 = jnp.zeros((8, d), F32).at[:batch].set(c)
    ret_col0 = 0
    att_col0 = 2 * RET_QK_W + 2 * RET_V_W
    gate_col0 = att_col0 + 3 * ATT_W
    for layer in range(depth):
        mod3 = _ada_ln(c8, w_ada, b_ada, layer)[:batch].reshape(batch * N_MOD, 1, d)
        h = _norm_mod(x2, norm1_g.reshape(depth, 1, d), mod3, layer, seq, shift_idx=0, scale_idx=1)
        proj = _project(h, w_in, layer)
        ret = _retention(proj, batch, seq, ret_col0)
        att = _chunk_attention(proj, rel_bias[layer], q_norm_g[layer], k_norm_g[layer], batch, seq, att_col0)
        merged = _merge(ret, att, w_ret_out, w_att_out, proj, gate_col0, layer)
        x2 = _out_proj(merged, w_out, x2, mod3, layer, seq, gate_idx=2)
        h_packed, logits = _norm_router(x2, norm2_g.reshape(depth, 1, d), mod3, w_router, b_router, layer, seq,
                                        shift_idx=3, scale_idx=4)
        gates, dest, sorted_tok, tables = _routing(logits)
        ys = _experts(h_packed, sorted_tok, tables, w1, b1, w2, b2, layer)
        x2 = _combine(ys, dest, gates, x2, mod3, seq, gate_idx=5)
    return x2.reshape(batch, seq, d)
```

```python
import functools

import numpy as np
import jax
import jax.numpy as jnp
from jax import lax
from jax.experimental import pallas as pl
from jax.experimental.pallas import tpu as pltpu

F32 = jnp.float32
BF16 = jnp.bfloat16
U32 = jnp.uint32

CHUNK = 64
NORM_EPS = 1e-6
RET_HEADS = 8
RET_QK_DIM = 256
RET_V_DIM = 512
ROPE_BASE = 10000.0
ATT_HEADS = 16
ATT_HEAD_DIM = 128
PREV_CHUNKS = 8
MAX_REL = 128
N_EXPERTS = 32
TOP_K = 4
SWIGLU_ALPHA = 1.702
SWIGLU_LIMIT = 7.0
N_MOD = 6

RET_QK_W = RET_HEADS * RET_QK_DIM
RET_V_W = RET_HEADS * RET_V_DIM
ATT_W = ATT_HEADS * ATT_HEAD_DIM

V7X_VMEM_LIMIT_BYTES = 58 * 1024 * 1024
HI16 = 0xFFFF0000

ADA_TN = 1024
NORM_TM = 256
PROJ_TM, PROJ_TN = 2048, 512
RET_BLOCK = 256
RET_HEADS_PER_STEP = 2
ATT_BLOCK = 256
ATT_HEADS_PER_STEP = 4
MERGE_TM, MERGE_TN = 512, 512
OUT_TM, OUT_TN = 1024, 512
EXPERT_BM = 1280
EXPERT_SUB = 256
EXPERT_TF = 256
EXPERT_TN = 256
EXPERT_PERM_W = 256
EXPERT_ISSUE_STEPS = 10
GATHER_GROUP = 32
COMBINE_TM = 256


def _params(semantics):
    return pltpu.CompilerParams(dimension_semantics=semantics, vmem_limit_bytes=V7X_VMEM_LIMIT_BYTES)


def _silu(t):
    return t * jax.nn.sigmoid(t)


def _pack_halves(y):
    n = y.shape[1] // 2
    hi = lax.bitcast_convert_type(y[:, :n].astype(BF16).astype(F32), U32)
    lo = lax.bitcast_convert_type(y[:, n:].astype(BF16).astype(F32), U32)
    return hi | (lo >> 16)


def _unpack_halves(p):
    a = lax.bitcast_convert_type(p & jnp.uint32(HI16), F32)
    b = lax.bitcast_convert_type(p << 16, F32)
    return a, b


def _ada_kernel(c_ref, w_ref, b_ref, o_ref):
    a = _silu(c_ref[...]).astype(BF16)
    o_ref[...] = jnp.dot(a, w_ref[...].astype(BF16), preferred_element_type=F32) + b_ref[...]


def _ada_ln(c8, w_ada, b_ada, layer):
    d = c8.shape[1]
    n = w_ada.shape[2]
    return pl.pallas_call(
        _ada_kernel,
        out_shape=jax.ShapeDtypeStruct((8, n), F32),
        grid=(n // ADA_TN,),
        in_specs=[
            pl.BlockSpec((8, d), lambda j: (0, 0)),
            pl.BlockSpec((None, d, ADA_TN), lambda j: (layer, 0, j)),
            pl.BlockSpec((None, 1, ADA_TN), lambda j: (layer, 0, j)),
        ],
        out_specs=pl.BlockSpec((8, ADA_TN), lambda j: (0, j)),
        compiler_params=_params(("arbitrary",)),
        name="ada_ln",
    )(c8, w_ada, b_ada.reshape(b_ada.shape[0], 1, n))


def _modulated_norm(x, g, scale, shift):
    xn = x * lax.rsqrt(jnp.mean(x * x, axis=-1, keepdims=True) + NORM_EPS)
    return (xn * g) * (1.0 + scale) + shift


def _norm_mod_kernel(x_ref, g_ref, sc_ref, sh_ref, o_ref):
    o_ref[...] = _modulated_norm(x_ref[...], g_ref[...], sc_ref[...], sh_ref[...]).astype(o_ref.dtype)


def _norm_mod(x2, g, mod3, layer, seq, shift_idx, scale_idx):
    t, d = x2.shape
    tm = NORM_TM

    def mod_spec(idx):
        return pl.BlockSpec((None, 1, d), lambda i: ((i * tm // seq) * N_MOD + idx, 0, 0))

    return pl.pallas_call(
        _norm_mod_kernel,
        out_shape=jax.ShapeDtypeStruct((t, d), BF16),
        grid=(t // tm,),
        in_specs=[
            pl.BlockSpec((tm, d), lambda i: (i, 0)),
            pl.BlockSpec((None, 1, d), lambda i: (layer, 0, 0)),
            mod_spec(scale_idx),
            mod_spec(shift_idx),
        ],
        out_specs=pl.BlockSpec((tm, d), lambda i: (i, 0)),
        compiler_params=_params(("arbitrary",)),
        name="norm_mod",
    )(x2, g, mod3, mod3)


def _norm_router_kernel(x_ref, g_ref, sc_ref, sh_ref, wr_ref, br_ref, h_ref, lg_ref):
    h = _modulated_norm(x_ref[...], g_ref[...], sc_ref[...], sh_ref[...])
    h_ref[...] = _pack_halves(h)
    h_hi = h.astype(BF16)
    h_lo = (h - h_hi.astype(F32)).astype(BF16)
    w = wr_ref[...]
    w_hi = w.astype(BF16)
    w_lo = (w - w_hi.astype(F32)).astype(BF16)
    lg = (jnp.dot(h_hi, w_hi, preferred_element_type=F32)
          + jnp.dot(h_lo, w_hi, preferred_element_type=F32)
          + jnp.dot(h_hi, w_lo, preferred_element_type=F32))
    lg_ref[...] = lg + br_ref[...]


def _norm_router(x2, g, mod3, w_router, b_router, layer, seq, shift_idx, scale_idx):
    t, d = x2.shape
    e = w_router.shape[2]
    tm = NORM_TM

    def mod_spec(idx):
        return pl.BlockSpec((None, 1, d), lambda i: ((i * tm // seq) * N_MOD + idx, 0, 0))

    return pl.pallas_call(
        _norm_router_kernel,
        out_shape=(jax.ShapeDtypeStruct((t, d // 2), U32), jax.ShapeDtypeStruct((t, e), F32)),
        grid=(t // tm,),
        in_specs=[
            pl.BlockSpec((tm, d), lambda i: (i, 0)),
            pl.BlockSpec((None, 1, d), lambda i: (layer, 0, 0)),
            mod_spec(scale_idx),
            mod_spec(shift_idx),
            pl.BlockSpec((None, d, e), lambda i: (layer, 0, 0)),
            pl.BlockSpec((None, 1, e), lambda i: (layer, 0, 0)),
        ],
        out_specs=(pl.BlockSpec((tm, d // 2), lambda i: (i, 0)), pl.BlockSpec((tm, e), lambda i: (i, 0))),
        compiler_params=_params(("arbitrary",)),
        name="norm_router",
    )(x2, g, mod3, mod3, w_router, b_router.reshape(b_router.shape[0], 1, e))


def _proj_kernel(a_ref, w_ref, o_ref):
    o_ref[...] = jnp.dot(a_ref[...], w_ref[...].astype(BF16), preferred_element_type=F32).astype(o_ref.dtype)


def _project(a, w, layer):
    t, k = a.shape
    n = w.shape[2]
    tm, tn = PROJ_TM, PROJ_TN
    return pl.pallas_call(
        _proj_kernel,
        out_shape=jax.ShapeDtypeStruct((t, n), BF16),
        grid=(n // tn, t // tm),
        in_specs=[
            pl.BlockSpec((tm, k), lambda j, i: (i, 0)),
            pl.BlockSpec((None, k, tn), lambda j, i: (layer, 0, j)),
        ],
        out_specs=pl.BlockSpec((tm, tn), lambda j, i: (i, j)),
        compiler_params=_params(("arbitrary", "arbitrary")),
        name="in_proj",
    )(a, w)


def _retention_kernel(q_ref, k_ref, v_ref, g_ref, cos_ref, sin_ref, dec_ref, qd_ref, kd_ref, cd_ref,
                      o_ref, state_ref):
    @pl.when(pl.program_id(2) == 0)
    def _():
        state_ref[...] = jnp.zeros_like(state_ref)

    cos = cos_ref[...]
    sin = sin_ref[...]
    half = RET_QK_DIM // 2
    contract_last = (((1,), (1,)), ((), ()))
    contract_rows = (((0,), (0,)), ((), ()))

    def rotary(t):
        t1, t2 = t[:, :half], t[:, half:]
        return jnp.concatenate([t1 * cos - t2 * sin, t1 * sin + t2 * cos], axis=1)

    for hh in range(RET_HEADS_PER_STEP):
        qk = slice(hh * RET_QK_DIM, (hh + 1) * RET_QK_DIM)
        vg = slice(hh * RET_V_DIM, (hh + 1) * RET_V_DIM)
        q = rotary(q_ref[:, qk].astype(F32))
        k = rotary(k_ref[:, qk].astype(F32)) * (RET_QK_DIM ** -0.5)
        v = v_ref[:, vg]
        scores = lax.dot_general(q.astype(BF16), k.astype(BF16), contract_last,
                                 preferred_element_type=F32) * dec_ref[hh]
        state = state_ref[hh]
        out = (jnp.dot(scores.astype(BF16), v, preferred_element_type=F32)
               + jnp.dot((q * qd_ref[hh]).astype(BF16), state.astype(BF16), preferred_element_type=F32))
        state_ref[hh] = state * cd_ref[hh] + lax.dot_general(
            (k * kd_ref[hh]).astype(BF16), v, contract_rows, preferred_element_type=F32)
        on = out * lax.rsqrt(jnp.mean(out * out, axis=-1, keepdims=True) + NORM_EPS)
        o_ref[:, vg] = (_silu(g_ref[:, vg].astype(F32)) * on).astype(o_ref.dtype)


def _retention_tables(seq):
    half = RET_QK_DIM // 2
    inv_freq = ROPE_BASE ** (-np.arange(half, dtype=np.float64) / half)
    ang = np.arange(seq, dtype=np.float64)[:, None] * inv_freq[None, :]
    log_gamma = np.log(1.0 - 2.0 ** (-5.0 - np.arange(RET_HEADS, dtype=np.float64)))[:, None, None]
    pos = np.arange(RET_BLOCK, dtype=np.float64)
    n, m = pos[:, None], pos[None, :]
    cn, cm = np.floor(n / CHUNK), np.floor(m / CHUNK)
    dist = np.where(cn == cm, np.abs(n - m), n - m)
    dec = np.where((cm <= cn)[None], np.exp(log_gamma * dist[None]), 0.0)
    qd = np.broadcast_to(np.exp(log_gamma * (pos[None, :, None] + 1.0)), (RET_HEADS, RET_BLOCK, RET_QK_DIM))
    kd = np.broadcast_to(np.exp(log_gamma * (RET_BLOCK - 1.0 - pos[None, :, None])),
                         (RET_HEADS, RET_BLOCK, RET_QK_DIM))
    cd = np.broadcast_to(np.exp(log_gamma * RET_BLOCK), (RET_HEADS, 1, RET_V_DIM))
    return tuple(jnp.asarray(np.ascontiguousarray(v), F32) for v in (np.cos(ang), np.sin(ang), dec, qd, kd, cd))


def _retention(proj, batch, seq, col0):
    t = proj.shape[0]
    r = RET_BLOCK
    nsb = seq // r
    cos, sin, dec, qd, kd, cd = _retention_tables(seq)
    hp = RET_HEADS_PER_STEP
    groups = RET_HEADS // hp
    qk_w, v_w = hp * RET_QK_DIM, hp * RET_V_DIM
    assert col0 % qk_w == 0 and (col0 + 2 * RET_QK_W) % v_w == 0
    qk_blk0 = col0 // qk_w
    v_blk0 = (col0 + 2 * RET_QK_W) // v_w
    row = lambda b, h, s: b * nsb + s
    head_tab = lambda shape: pl.BlockSpec((hp,) + shape, lambda b, h, s: (h, 0, 0))
    half = RET_QK_DIM // 2
    return pl.pallas_call(
        _retention_kernel,
        out_shape=jax.ShapeDtypeStruct((t, RET_V_W), BF16),
        grid=(batch, groups, nsb),
        in_specs=[
            pl.BlockSpec((r, qk_w), lambda b, h, s: (row(b, h, s), qk_blk0 + h)),
            pl.BlockSpec((r, qk_w), lambda b, h, s: (row(b, h, s), qk_blk0 + groups + h)),
            pl.BlockSpec((r, v_w), lambda b, h, s: (row(b, h, s), v_blk0 + h)),
            pl.BlockSpec((r, v_w), lambda b, h, s: (row(b, h, s), v_blk0 + groups + h)),
            pl.BlockSpec((r, half), lambda b, h, s: (s, 0)),
            pl.BlockSpec((r, half), lambda b, h, s: (s, 0)),
            head_tab((r, r)),
            head_tab((r, RET_QK_DIM)),
            head_tab((r, RET_QK_DIM)),
            head_tab((1, RET_V_DIM)),
        ],
        out_specs=pl.BlockSpec((r, v_w), lambda b, h, s: (row(b, h, s), h)),
        scratch_shapes=[pltpu.VMEM((hp, RET_QK_DIM, RET_V_DIM), F32)],
        compiler_params=_params(("arbitrary", "arbitrary", "arbitrary")),
        name="retention",
    )(proj, proj, proj, proj, cos, sin, dec, qd, kd, cd)


ATT_KEY_BLOCKS = PREV_CHUNKS * CHUNK // ATT_BLOCK + 1
NEG = float(np.finfo(np.float32).min)


def _attention_kernel(q_ref, k0_ref, k1_ref, k2_ref, v0_ref, v1_ref, v2_ref, tab_ref, gq_ref, gk_ref, o_ref):
    i = pl.program_id(1)

    def rms(t, g):
        tf = t.astype(F32)
        return (tf * lax.rsqrt(jnp.mean(tf * tf, axis=-1, keepdims=True) + NORM_EPS)) * g

    dh = ATT_HEAD_DIM
    k_all = jnp.concatenate([k0_ref[...], k1_ref[...], k2_ref[...]], axis=0)
    v_all = jnp.concatenate([v0_ref[...], v1_ref[...], v2_ref[...]], axis=0)
    outs = []
    for hh in range(ATT_HEADS_PER_STEP):
        lanes = slice(hh * dh, (hh + 1) * dh)
        q = rms(q_ref[:, lanes], gq_ref[...]).astype(BF16)
        k = rms(k_all[:, lanes], gk_ref[...]).astype(BF16)
        sc = lax.dot_general(q, k, (((1,), (1,)), ((), ())), preferred_element_type=F32)
        sc = sc * (ATT_HEAD_DIM ** -0.5) + tab_ref[hh]
        col = lax.broadcasted_iota(jnp.int32, sc.shape, 1)
        sc = jnp.where(col >= (ATT_KEY_BLOCKS - 1 - i) * ATT_BLOCK, sc, NEG)
        p = jnp.exp(sc - jnp.max(sc, axis=-1, keepdims=True))
        denom = jnp.sum(p, axis=-1, keepdims=True)
        outs.append(jnp.dot(p.astype(BF16), v_all[:, lanes], preferred_element_type=F32) / denom)
    o_ref[...] = jnp.concatenate(outs, axis=1).astype(o_ref.dtype)


def _attention_table(rel_bias):
    a = ATT_BLOCK
    w = ATT_KEY_BLOCKS * a
    pad = PREV_CHUNKS * CHUNK
    heads = rel_bias.shape[0]
    period = 1024
    assert a - 1 + w - 1 <= period - 2
    rel_of_i = pad + a - 1 - np.arange(period)
    idx = np.clip(np.minimum(rel_of_i, MAX_REL) + (CHUNK - 1), 0, rel_bias.shape[-1] - 1)
    u = rel_bias[:, idx].astype(F32)
    skew = jnp.tile(u, (1, a))[:, :a * (period - 1)].reshape(heads, a, period - 1)
    bias = skew[:, :, a - 1:a - 1 + w]
    n = np.arange(a)[:, None]
    j = np.arange(w)[None, :]
    qc = n // CHUNK
    kc = j // CHUNK - PREV_CHUNKS
    valid = (kc <= qc) & (kc >= qc - PREV_CHUNKS)
    return jnp.where(jnp.asarray(valid)[None], bias, NEG)


def _chunk_attention(proj, rel_bias, q_norm_g, k_norm_g, batch, seq, col0):
    assert ATT_KEY_BLOCKS == 3
    t = proj.shape[0]
    a = ATT_BLOCK
    nqb = seq // a
    dh = ATT_HEAD_DIM
    hw = ATT_HEADS_PER_STEP * dh
    groups = ATT_HEADS // ATT_HEADS_PER_STEP
    assert col0 % hw == 0 and ATT_W % hw == 0
    blk0 = col0 // hw
    tab = _attention_table(rel_bias)

    def kv_spec(which, jj):
        return pl.BlockSpec(
            (a, hw),
            lambda h, i, b: (b * nqb + jnp.maximum(i - (ATT_KEY_BLOCKS - 1) + jj, 0), blk0 + which * groups + h))

    return pl.pallas_call(
        _attention_kernel,
        out_shape=jax.ShapeDtypeStruct((t, ATT_W), BF16),
        grid=(groups, nqb, batch),
        in_specs=[
            pl.BlockSpec((a, hw), lambda h, i, b: (b * nqb + i, blk0 + h)),
            kv_spec(1, 0), kv_spec(1, 1), kv_spec(1, 2),
            kv_spec(2, 0), kv_spec(2, 1), kv_spec(2, 2),
            pl.BlockSpec((ATT_HEADS_PER_STEP, a, ATT_KEY_BLOCKS * a), lambda h, i, b: (h, 0, 0)),
            pl.BlockSpec((1, dh), lambda h, i, b: (0, 0)),
            pl.BlockSpec((1, dh), lambda h, i, b: (0, 0)),
        ],
        out_specs=pl.BlockSpec((a, hw), lambda h, i, b: (b * nqb + i, h)),
        compiler_params=_params(("arbitrary", "arbitrary", "arbitrary")),
        name="chunk_attention",
    )(proj, proj, proj, proj, proj, proj, proj, tab, q_norm_g.reshape(1, dh), k_norm_g.reshape(1, dh))


def _merge_kernel(r_ref, a_ref, wr_ref, wa_ref, gr_ref, ga_ref, o_ref):
    yr = jnp.dot(r_ref[...], wr_ref[...].astype(BF16), preferred_element_type=F32)
    ya = jnp.dot(a_ref[...], wa_ref[...].astype(BF16), preferred_element_type=F32)
    merged = jax.nn.sigmoid(gr_ref[...].astype(F32)) * yr + jax.nn.sigmoid(ga_ref[...].astype(F32)) * ya
    o_ref[...] = merged.astype(o_ref.dtype)


def _merge(ret, att, w_ret_out, w_att_out, proj, gate_col0, layer):
    t = ret.shape[0]
    d = w_ret_out.shape[2]
    tm, tn = MERGE_TM, MERGE_TN
    gblk = gate_col0 // tn
    return pl.pallas_call(
        _merge_kernel,
        out_shape=jax.ShapeDtypeStruct((t, d), BF16),
        grid=(d // tn, t // tm),
        in_specs=[
            pl.BlockSpec((tm, ret.shape[1]), lambda j, i: (i, 0)),
            pl.BlockSpec((tm, att.shape[1]), lambda j, i: (i, 0)),
            pl.BlockSpec((None, ret.shape[1], tn), lambda j, i: (layer, 0, j)),
            pl.BlockSpec((None, att.shape[1], tn), lambda j, i: (layer, 0, j)),
            pl.BlockSpec((tm, tn), lambda j, i: (i, gblk + j)),
            pl.BlockSpec((tm, tn), lambda j, i: (i, gblk + d // tn + j)),
        ],
        out_specs=pl.BlockSpec((tm, tn), lambda j, i: (i, j)),
        compiler_params=_params(("arbitrary", "arbitrary")),
        name="merge",
    )(ret, att, w_ret_out, w_att_out, proj, proj)


def _out_proj_kernel(a_ref, w_ref, x_ref, gate_ref, o_ref):
    acc = jnp.dot(a_ref[...], w_ref[...].astype(BF16), preferred_element_type=F32)
    o_ref[...] = x_ref[...] + gate_ref[...] * acc


def _out_proj(a, w, x2, mod3, layer, seq, gate_idx):
    t, k = a.shape
    d = w.shape[2]
    tm, tn = OUT_TM, OUT_TN
    return pl.pallas_call(
        _out_proj_kernel,
        out_shape=jax.ShapeDtypeStruct((t, d), F32),
        grid=(d // tn, t // tm),
        in_specs=[
            pl.BlockSpec((tm, k), lambda j, i: (i, 0)),
            pl.BlockSpec((None, k, tn), lambda j, i: (layer, 0, j)),
            pl.BlockSpec((tm, tn), lambda j, i: (i, j)),
            pl.BlockSpec((None, 1, tn), lambda j, i: ((i * tm // seq) * N_MOD + gate_idx, 0, j)),
        ],
        out_specs=pl.BlockSpec((tm, tn), lambda j, i: (i, j)),
        compiler_params=_params(("arbitrary", "arbitrary")),
        name="out_proj",
    )(a, w, x2, mod3)


def _routing(logits):
    t = logits.shape[0]
    bm = EXPERT_BM
    a = t * TOP_K
    top_val, top_idx = lax.top_k(logits, TOP_K)
    gate = jax.nn.softmax(top_val, axis=-1)
    onehot = top_idx[:, :, None] == jnp.arange(N_EXPERTS, dtype=jnp.int32)[None, None, :]
    picked = jnp.any(onehot, axis=1).astype(jnp.int32)
    inclusive = jnp.cumsum(picked, axis=0)
    sizes = inclusive[-1]
    sub = EXPERT_SUB
    padded = (sizes + sub - 1) // sub * sub
    pad_end = jnp.cumsum(padded)
    pad_start = pad_end - padded
    start = jnp.cumsum(sizes) - sizes
    slot = (pad_start[None, :] + inclusive - picked)[:, None, :]
    dest = jnp.sum(jnp.where(onehot, slot, 0), axis=-1).astype(jnp.int32).reshape(a)
    order = jnp.argsort(top_idx.reshape(a))
    n_blocks = a // bm + N_EXPERTS
    group_blocks = (sizes + bm - 1) // bm
    blocks_end = jnp.cumsum(group_blocks)
    n_valid = blocks_end[-1].astype(jnp.int32)
    blk = jnp.arange(n_blocks, dtype=jnp.int32)
    block_e = jnp.minimum(jnp.searchsorted(blocks_end, blk, side='right'), N_EXPERTS - 1).astype(jnp.int32)
    chunk = blk - (blocks_end - group_blocks)[block_e]
    block_valid = blk < n_valid
    block_rows = jnp.where(block_valid, jnp.clip(sizes[block_e] - chunk * bm, 0, bm), 0)
    block_nsub = ((block_rows + sub - 1) // sub).astype(jnp.int32)
    block_row0 = jnp.where(block_valid, pad_start[block_e] + chunk * bm, 0).astype(jnp.int32)
    n_chunks = -(-a // bm) + 2
    sorted_tok = jnp.zeros((n_chunks * bm,), jnp.int32).at[:a].set((order // TOP_K).astype(jnp.int32))
    block_first = jnp.where(block_valid, start[block_e] + chunk * bm, 0).astype(jnp.int32)
    tables = (block_e, block_nsub, block_first, block_row0, jnp.stack([n_valid, pad_end[-1].astype(jnp.int32)]))
    return gate, dest, sorted_tok.reshape(n_chunks, 1, bm), tables


def _expert_kernel(be_ref, ns_ref, first_ref, row0_ref, info_ref, tok_hbm, h_hbm, w1_hbm, b1_ref, w2_hbm, b2_ref,
                   perm_ref, ys_hbm, w1buf, w2buf, gbuf, x_scr, act_scr, obuf, tokbuf,
                   sem_w1, sem_w2, sem_g, sem_o, sem_t, *, layer, nj1, nj2):
    sub = EXPERT_SUB
    group = GATHER_GROUP
    bm, d = x_scr.shape
    half = d // 2
    tf = act_scr.shape[2]
    tn = obuf.shape[2]
    steps = nj1 + nj2
    nb_max = be_ref.shape[0]
    n_valid = info_ref[0]
    total = n_valid * steps
    per_step = bm // EXPERT_ISSUE_STEPS

    def for_rows(nsub, fn):
        def pair(p, carry):
            fn(pl.multiple_of(p * 2 * sub, 2 * sub), 2 * sub)
            return carry
        lax.fori_loop(0, nsub // 2, pair, 0)

        @pl.when(nsub % 2 == 1)
        def _():
            fn(pl.multiple_of((nsub - 1) * sub, sub), sub)

    def w1_copy(e, j, slot):
        cols = pl.ds(pl.multiple_of(j * 2 * tf, 2 * tf), 2 * tf)
        return pltpu.make_async_copy(w1_hbm.at[layer, e, :, cols], w1buf.at[slot], sem_w1.at[slot])

    def w2_copies(e, n, slot):
        return [pltpu.make_async_copy(w2_hbm.at[layer, e, :, pl.ds(pl.multiple_of(c0, tn), tn)],
                                      w2buf.at[slot, :, pl.ds(k * tn, tn)], sem_w2.at[slot])
                for k, c0 in enumerate((n * tn, half + n * tn))]

    def tok_copies(blk, slot):
        chunk = first_ref[blk] // bm
        return [pltpu.make_async_copy(tok_hbm.at[chunk + k], tokbuf.at[slot, :, pl.ds(k * bm, bm)], sem_t.at[slot])
                for k in (0, 1)]

    def row_copy(slot, offset, r):
        tok = tokbuf[slot, 0, offset + r]
        return pltpu.make_async_copy(h_hbm.at[pl.ds(tok, 1)], gbuf.at[pl.ds(r, 1)], sem_g)

    def out_copy(slot, r0, m, row0, n):
        dst = ys_hbm.at[pl.ds(pl.multiple_of(row0 + r0, sub), m), pl.ds(pl.multiple_of(n * tn, tn), tn)]
        return pltpu.make_async_copy(obuf.at[slot, pl.ds(r0, m)], dst, sem_o.at[slot])

    @pl.when(n_valid > 0)
    def _():
        for c in tok_copies(0, 0):
            c.start()
        for c in tok_copies(0, 0):
            c.wait()
        offset = lax.rem(first_ref[0], bm)

        def start(r, carry):
            row_copy(0, offset, r).start()
            return carry
        lax.fori_loop(0, ns_ref[0] * sub, start, 0)
        w1_copy(be_ref[0], 0, 0).start()

    def item(it, carry):
        b = it // steps
        j = it - b * steps
        e = be_ref[b]
        nsub = ns_ref[b]
        has_next = b + 1 < n_valid
        nxt = jnp.minimum(b + 1, nb_max - 1)

        nit = it + 1
        nbk = jnp.minimum(nit // steps, nb_max - 1)
        nj = nit - (nit // steps) * steps

        @pl.when((nit < total) & (nj < nj1))
        def _():
            w1_copy(be_ref[nbk], nj, lax.rem(nj, 2)).start()

        @pl.when((nit < total) & (nj >= nj1))
        def _():
            for c in w2_copies(be_ref[nbk], nj - nj1, lax.rem(nj - nj1, 2)):
                c.start()

        @pl.when(j < nj1)
        def _():
            w1_copy(e, j, lax.rem(j, 2)).wait()

        @pl.when(j >= nj1)
        def _():
            for c in w2_copies(e, j - nj1, lax.rem(j - nj1, 2)):
                c.wait()

        @pl.when(j == 0)
        def _():
            def wait_rows(s, carry):
                pltpu.make_async_copy(h_hbm.at[pl.ds(0, sub)], gbuf.at[pl.ds(0, sub)], sem_g).wait()
                return carry
            lax.fori_loop(0, nsub, wait_rows, 0)

            def unpack_rows(s, carry):
                rows = pl.ds(pl.multiple_of(s * sub, sub), sub)
                xa, xb = _unpack_halves(gbuf[rows, :])
                x_scr[rows, pl.ds(0, half)] = xa.astype(BF16)
                x_scr[rows, pl.ds(half, half)] = xb.astype(BF16)
                return carry
            lax.fori_loop(0, nsub, unpack_rows, 0)

            @pl.when(has_next)
            def _():
                for c in tok_copies(nxt, lax.rem(b + 1, 2)):
                    c.start()

        @pl.when((j == 1) & has_next)
        def _():
            for c in tok_copies(nxt, lax.rem(b + 1, 2)):
                c.wait()

        @pl.when((j >= 1) & (j <= EXPERT_ISSUE_STEPS) & has_next)
        def _():
            slot = lax.rem(b + 1, 2)
            offset = lax.rem(first_ref[nxt], bm)
            rows_next = ns_ref[nxt] * sub
            for g in range(per_step // group):
                first_row = (j - 1) * per_step + g * group

                @pl.when(first_row < rows_next)
                def _():
                    for i in range(group):
                        row_copy(slot, offset, first_row + i).start()

        @pl.when(j < nj1)
        def _():
            wslot = lax.rem(j, 2)
            bias = b1_ref[pl.ds(e * nj1 + j, 1), :]

            def first_matmul(r0, m):
                rows = pl.ds(r0, m)
                hb = jnp.dot(x_scr[rows, :], w1buf[wslot].astype(BF16), preferred_element_type=F32) + bias
                hb = hb.astype(BF16)
                gw = perm_ref.shape[0]
                sel = [jnp.dot(hb[:, c * gw:(c + 1) * gw], perm_ref[...], preferred_element_type=F32)
                       for c in range(2 * tf // gw)]
                x_glu = jnp.minimum(jnp.concatenate([s[:, :gw // 2] for s in sel], axis=1), SWIGLU_LIMIT)
                x_lin = jnp.clip(jnp.concatenate([s[:, gw // 2:] for s in sel], axis=1),
                                 -SWIGLU_LIMIT, SWIGLU_LIMIT)
                act = x_glu * jax.nn.sigmoid(SWIGLU_ALPHA * x_glu) * (x_lin + 1.0)
                act_scr[j, rows, :] = act.astype(BF16)
            for_rows(nsub, first_matmul)

        @pl.when(j >= nj1)
        def _():
            n = j - nj1
            slot = lax.rem(n, 2)
            row0 = row0_ref[b]
            bias = jnp.concatenate([b2_ref[pl.ds(e * 2 * nj2 + n, 1), :],
                                    b2_ref[pl.ds(e * 2 * nj2 + nj2 + n, 1), :]], axis=1)
            sent = jnp.where(n >= 2, nsub, jnp.where(b > 0, ns_ref[jnp.maximum(b - 1, 0)], 0))
            for_rows(sent, lambda r0, m: out_copy(slot, r0, m, 0, 0).wait())

            def second_matmul(r0, m):
                rows = pl.ds(r0, m)
                act = jnp.concatenate([act_scr[t, rows, :] for t in range(nj1)], axis=1)
                y = jnp.dot(act, w2buf[slot].astype(BF16), preferred_element_type=F32) + bias
                obuf[slot, rows, :] = _pack_halves(y)
                out_copy(slot, r0, m, row0, n).start()
            for_rows(nsub, second_matmul)

        return carry

    lax.fori_loop(0, total, item, 0)

    @pl.when(n_valid > 0)
    def _():
        last = ns_ref[jnp.maximum(n_valid - 1, 0)]
        for slot in (0, 1):
            for_rows(last, lambda r0, m: out_copy(slot, r0, m, 0, 0).wait())

    used = info_ref[1]
    n_tail = (ys_hbm.shape[0] - used) // sub
    gbuf[pl.ds(0, sub), :] = jnp.zeros((sub, half), gbuf.dtype)

    def tail_copy(i):
        dst = ys_hbm.at[pl.ds(pl.multiple_of(used + i * sub, sub), sub)]
        return pltpu.make_async_copy(gbuf.at[pl.ds(0, sub)], dst, sem_g)

    def tail_start(i, carry):
        tail_copy(i).start()
        return carry

    def tail_wait(i, carry):
        tail_copy(i).wait()
        return carry
    lax.fori_loop(0, n_tail, tail_start, 0)
    lax.fori_loop(0, n_tail, tail_wait, 0)


def _deinterleave_matrix(tf):
    p = np.zeros((2 * tf, 2 * tf), np.float32)
    f = np.arange(tf)
    p[2 * f, f] = 1.0
    p[2 * f + 1, tf + f] = 1.0
    return jnp.asarray(p, BF16)


def _experts(h_packed, sorted_tok, tables, w1, b1, w2, b2, layer):
    t, half = h_packed.shape
    d = 2 * half
    n_experts, f = w2.shape[1], w2.shape[2]
    bm, tf, tn = EXPERT_BM, EXPERT_TF, EXPERT_TN
    nj1 = f // tf
    nj2 = half // tn
    assert nj1 % 2 == 0 and nj2 % 2 == 0
    assert 1 + EXPERT_ISSUE_STEPS <= nj1 + nj2 and bm % (EXPERT_ISSUE_STEPS * GATHER_GROUP) == 0
    assert bm % EXPERT_SUB == 0 and t >= EXPERT_SUB
    p_rows = t * TOP_K + n_experts * EXPERT_SUB
    whole = lambda shape: pl.BlockSpec(shape, lambda i, *_: (0,) * len(shape))
    hbm = pl.BlockSpec(memory_space=pl.ANY)
    return pl.pallas_call(
        functools.partial(_expert_kernel, layer=layer, nj1=nj1, nj2=nj2),
        out_shape=jax.ShapeDtypeStruct((p_rows, half), U32),
        grid_spec=pltpu.PrefetchScalarGridSpec(
            num_scalar_prefetch=5,
            grid=(1,),
            in_specs=[hbm, hbm, hbm, whole((n_experts * nj1, 2 * tf)), hbm, whole((n_experts * 2 * nj2, tn)),
                      whole((EXPERT_PERM_W, EXPERT_PERM_W))],
            out_specs=hbm,
            scratch_shapes=[
                pltpu.VMEM((2, d, 2 * tf), F32),
                pltpu.VMEM((2, f, 2 * tn), F32),
                pltpu.VMEM((bm, half), U32),
                pltpu.VMEM((bm, d), BF16),
                pltpu.VMEM((nj1, bm, tf), BF16),
                pltpu.VMEM((2, bm, tn), U32),
                pltpu.SMEM((2, 1, 2 * bm), jnp.int32),
                pltpu.SemaphoreType.DMA((2,)),
                pltpu.SemaphoreType.DMA((2,)),
                pltpu.SemaphoreType.DMA(()),
                pltpu.SemaphoreType.DMA((2,)),
                pltpu.SemaphoreType.DMA((2,)),
            ],
        ),
        compiler_params=_params(("arbitrary",)),
        name="experts",
    )(*tables, sorted_tok, h_packed, w1, b1[layer].reshape(n_experts * nj1, 2 * tf), w2,
      b2[layer].reshape(n_experts * 2 * nj2, tn), _deinterleave_matrix(EXPERT_PERM_W // 2))


def _combine_kernel(dest_ref, ys_hbm, x_ref, gate_ref, g_ref, o_ref, buf, sem):
    tm = x_ref.shape[0]
    half = buf.shape[2]

    def row_copy(src_row, k, r):
        return pltpu.make_async_copy(ys_hbm.at[pl.ds(src_row, 1)], buf.at[k, pl.ds(r, 1)], sem)

    def start(r, carry):
        for k in range(TOP_K):
            row_copy(dest_ref[0, r * TOP_K + k], k, r).start()
        return carry

    def wait(r, carry):
        for k in range(TOP_K):
            row_copy(0, k, r).wait()
        return carry

    lax.fori_loop(0, tm, start, 0)
    lax.fori_loop(0, tm, wait, 0)
    sublanes = 8

    def chunk(c, carry):
        rows = pl.ds(pl.multiple_of(c * sublanes, sublanes), sublanes)
        g = g_ref[rows, :]
        ya = jnp.zeros((sublanes, half), F32)
        yb = jnp.zeros((sublanes, half), F32)
        for k in range(TOP_K):
            a, b = _unpack_halves(buf[k, rows, :])
            ya = ya + g[:, k:k + 1] * a
            yb = yb + g[:, k:k + 1] * b
        o_ref[rows, pl.ds(0, half)] = x_ref[rows, pl.ds(0, half)] + gate_ref[:, pl.ds(0, half)] * ya
        o_ref[rows, pl.ds(half, half)] = x_ref[rows, pl.ds(half, half)] + gate_ref[:, pl.ds(half, half)] * yb
        return carry

    lax.fori_loop(0, tm // sublanes, chunk, 0)


def _combine(ys, dest, gates, x2, mod3, seq, gate_idx):
    t, d = x2.shape
    tm = COMBINE_TM
    half = ys.shape[1]
    return pl.pallas_call(
        _combine_kernel,
        out_shape=jax.ShapeDtypeStruct((t, d), F32),
        grid=(t // tm,),
        in_specs=[
            pl.BlockSpec((None, 1, tm * TOP_K), lambda i: (i, 0, 0), memory_space=pltpu.SMEM),
            pl.BlockSpec(memory_space=pl.ANY),
            pl.BlockSpec((tm, d), lambda i: (i, 0)),
            pl.BlockSpec((None, 1, d), lambda i: ((i * tm // seq) * N_MOD + gate_idx, 0, 0)),
            pl.BlockSpec((tm, TOP_K), lambda i: (i, 0)),
        ],
        out_specs=pl.BlockSpec((tm, d), lambda i: (i, 0)),
        scratch_shapes=[pltpu.VMEM((TOP_K, tm, half), U32), pltpu.SemaphoreType.DMA(())],
        compiler_params=_params(("arbitrary",)),
        name="combine",
    )(dest.reshape(t // tm, 1, tm * TOP_K), ys, x2, mod3, gates)


def kernel(x, c, w_ada, b_ada, norm1_g, w_in, q_norm_g, k_norm_g, rel_bias, w_ret_out, w_att_out, w_out,
           norm2_g, w_router, b_router, w1, b1, w2, b2):
    batch, seq, d = x.shape
    depth = w_ada.shape[0]
    t = batch * seq
    assert batch <= 8 and seq % max(RET_BLOCK, ATT_BLOCK, NORM_TM, OUT_TM, COMBINE_TM) == 0 and t % PROJ_TM == 0
    x2 = x.reshape(t, d)
    c8 = jnp.zeros((8, d), F32).at[:batch].set(c)
    ret_col0 = 0
    att_col0 = 2 * RET_QK_W + 2 * RET_V_W
    gate_col0 = att_col0 + 3 * ATT_W
    for layer in range(depth):
        mod3 = _ada_ln(c8, w_ada, b_ada, layer)[:batch].reshape(batch * N_MOD, 1, d)
        h = _norm_mod(x2, norm1_g.reshape(depth, 1, d), mod3, layer, seq, shift_idx=0, scale_idx=1)
        proj = _project(h, w_in, layer)
        ret = _retention(proj, batch, seq, ret_col0)
        att = _chunk_attention(proj, rel_bias[layer], q_norm_g[layer], k_norm_g[layer], batch, seq, att_col0)
        merged = _merge(ret, att, w_ret_out, w_att_out, proj, gate_col0, layer)
        x2 = _out_proj(merged, w_out, x2, mod3, layer, seq, gate_idx=2)
        h_packed, logits = _norm_router(x2, norm2_g.reshape(depth, 1, d), mod3, w_router, b_router, layer, seq,
                                        shift_idx=3, scale_idx=4)
        gates, dest, sorted_tok, tables = _routing(logits)
        ys = _experts(h_packed, sorted_tok, tables, w1, b1, w2, b2, layer)
        x2 = _combine(ys, dest, gates, x2, mod3, seq, gate_idx=5)
    return x2.reshape(batch, seq, d)
```

```python
import functools

import numpy as np
import jax
import jax.numpy as jnp
from jax import lax
from jax.experimental import pallas as pl
from jax.experimental.pallas import tpu as pltpu

F32 = jnp.float32
BF16 = jnp.bfloat16
U32 = jnp.uint32

CHUNK = 64
NORM_EPS = 1e-6
RET_HEADS = 8
RET_QK_DIM = 256
RET_V_DIM = 512
ROPE_BASE = 10000.0
ATT_HEADS = 16
ATT_HEAD_DIM = 128
PREV_CHUNKS = 8
MAX_REL = 128
N_EXPERTS = 32
TOP_K = 4
SWIGLU_ALPHA = 1.702
SWIGLU_LIMIT = 7.0
N_MOD = 6

RET_QK_W = RET_HEADS * RET_QK_DIM
RET_V_W = RET_HEADS * RET_V_DIM
ATT_W = ATT_HEADS * ATT_HEAD_DIM

V7X_VMEM_LIMIT_BYTES = 58 * 1024 * 1024
HI16 = 0xFFFF0000

ADA_TN = 1024
NORM_TM = 256
PROJ_TM, PROJ_TN = 2048, 512
RET_BLOCK = 256
RET_HEADS_PER_STEP = 4
ATT_BLOCK = 256
ATT_HEADS_PER_STEP = 4
MERGE_TM, MERGE_TN = 512, 512
OUT_TM, OUT_TN = 1024, 512
EXPERT_BM = 1280
EXPERT_SUB = 256
EXPERT_TF = 256
EXPERT_TN = 256
EXPERT_PERM_W = 256
EXPERT_ISSUE_STEPS = 10
GATHER_GROUP = 32
TOK_ALIGN = 1024
TOK_WINDOW = 3 * TOK_ALIGN
SUBLANES = 8
COMBINE_TM = 256


def _params(semantics):
    return pltpu.CompilerParams(dimension_semantics=semantics, vmem_limit_bytes=V7X_VMEM_LIMIT_BYTES)


def _silu(t):
    return t * jax.nn.sigmoid(t)


def _pack_halves(y):
    n = y.shape[1] // 2
    hi = lax.bitcast_convert_type(y[:, :n].astype(BF16).astype(F32), U32)
    lo = lax.bitcast_convert_type(y[:, n:].astype(BF16).astype(F32), U32)
    return hi | (lo >> 16)


def _unpack_halves(p):
    a = lax.bitcast_convert_type(p & jnp.uint32(HI16), F32)
    b = lax.bitcast_convert_type(p << 16, F32)
    return a, b


def _ada_kernel(c_ref, w_ref, b_ref, o_ref):
    a = _silu(c_ref[...]).astype(BF16)
    o_ref[...] = jnp.dot(a, w_ref[...].astype(BF16), preferred_element_type=F32) + b_ref[...]


def _ada_ln(c8, w_ada, b_ada, layer):
    d = c8.shape[1]
    n = w_ada.shape[2]
    return pl.pallas_call(
        _ada_kernel,
        out_shape=jax.ShapeDtypeStruct((8, n), F32),
        grid=(n // ADA_TN,),
        in_specs=[
            pl.BlockSpec((8, d), lambda j: (0, 0)),
            pl.BlockSpec((None, d, ADA_TN), lambda j: (layer, 0, j)),
            pl.BlockSpec((None, 1, ADA_TN), lambda j: (layer, 0, j)),
        ],
        out_specs=pl.BlockSpec((8, ADA_TN), lambda j: (0, j)),
        compiler_params=_params(("arbitrary",)),
        name="ada_ln",
    )(c8, w_ada, b_ada.reshape(b_ada.shape[0], 1, n))


def _modulated_norm(x, g, scale, shift):
    xn = x * lax.rsqrt(jnp.mean(x * x, axis=-1, keepdims=True) + NORM_EPS)
    return (xn * g) * (1.0 + scale) + shift


def _norm_mod_kernel(x_ref, g_ref, sc_ref, sh_ref, o_ref):
    o_ref[...] = _modulated_norm(x_ref[...], g_ref[...], sc_ref[...], sh_ref[...]).astype(o_ref.dtype)


def _norm_mod(x2, g, mod3, layer, seq, shift_idx, scale_idx):
    t, d = x2.shape
    tm = NORM_TM

    def mod_spec(idx):
        return pl.BlockSpec((None, 1, d), lambda i: ((i * tm // seq) * N_MOD + idx, 0, 0))

    return pl.pallas_call(
        _norm_mod_kernel,
        out_shape=jax.ShapeDtypeStruct((t, d), BF16),
        grid=(t // tm,),
        in_specs=[
            pl.BlockSpec((tm, d), lambda i: (i, 0)),
            pl.BlockSpec((None, 1, d), lambda i: (layer, 0, 0)),
            mod_spec(scale_idx),
            mod_spec(shift_idx),
        ],
        out_specs=pl.BlockSpec((tm, d), lambda i: (i, 0)),
        compiler_params=_params(("arbitrary",)),
        name="norm_mod",
    )(x2, g, mod3, mod3)


def _norm_router_kernel(x_ref, g_ref, sc_ref, sh_ref, wr_ref, br_ref, h_ref, lg_ref):
    h = _modulated_norm(x_ref[...], g_ref[...], sc_ref[...], sh_ref[...])
    h_ref[...] = _pack_halves(h)
    h_hi = h.astype(BF16)
    h_lo = (h - h_hi.astype(F32)).astype(BF16)
    w = wr_ref[...]
    w_hi = w.astype(BF16)
    w_lo = (w - w_hi.astype(F32)).astype(BF16)
    lg = (jnp.dot(h_hi, w_hi, preferred_element_type=F32)
          + jnp.dot(h_lo, w_hi, preferred_element_type=F32)
          + jnp.dot(h_hi, w_lo, preferred_element_type=F32))
    lg_ref[...] = lg + br_ref[...]


def _norm_router(x2, g, mod3, w_router, b_router, layer, seq, shift_idx, scale_idx):
    t, d = x2.shape
    e = w_router.shape[2]
    tm = NORM_TM

    def mod_spec(idx):
        return pl.BlockSpec((None, 1, d), lambda i: ((i * tm // seq) * N_MOD + idx, 0, 0))

    return pl.pallas_call(
        _norm_router_kernel,
        out_shape=(jax.ShapeDtypeStruct((t, d // 2), U32), jax.ShapeDtypeStruct((t, e), F32)),
        grid=(t // tm,),
        in_specs=[
            pl.BlockSpec((tm, d), lambda i: (i, 0)),
            pl.BlockSpec((None, 1, d), lambda i: (layer, 0, 0)),
            mod_spec(scale_idx),
            mod_spec(shift_idx),
            pl.BlockSpec((None, d, e), lambda i: (layer, 0, 0)),
            pl.BlockSpec((None, 1, e), lambda i: (layer, 0, 0)),
        ],
        out_specs=(pl.BlockSpec((tm, d // 2), lambda i: (i, 0)), pl.BlockSpec((tm, e), lambda i: (i, 0))),
        compiler_params=_params(("arbitrary",)),
        name="norm_router",
    )(x2, g, mod3, mod3, w_router, b_router.reshape(b_router.shape[0], 1, e))


def _proj_kernel(a_ref, w_ref, o_ref):
    o_ref[...] = jnp.dot(a_ref[...], w_ref[...].astype(BF16), preferred_element_type=F32).astype(o_ref.dtype)


def _project(a, w, layer):
    t, k = a.shape
    n = w.shape[2]
    tm, tn = PROJ_TM, PROJ_TN
    return pl.pallas_call(
        _proj_kernel,
        out_shape=jax.ShapeDtypeStruct((t, n), BF16),
        grid=(n // tn, t // tm),
        in_specs=[
            pl.BlockSpec((tm, k), lambda j, i: (i, 0)),
            pl.BlockSpec((None, k, tn), lambda j, i: (layer, 0, j)),
        ],
        out_specs=pl.BlockSpec((tm, tn), lambda j, i: (i, j)),
        compiler_params=_params(("arbitrary", "arbitrary")),
        name="in_proj",
    )(a, w)


def _retention_kernel(q_ref, k_ref, v_ref, g_ref, cos_ref, sin_ref, dec_ref, qd_ref, kd_ref, cd_ref,
                      o_ref, state_ref):
    @pl.when(pl.program_id(2) == 0)
    def _():
        state_ref[...] = jnp.zeros_like(state_ref)

    cos = cos_ref[...]
    sin = sin_ref[...]
    half = RET_QK_DIM // 2
    contract_last = (((1,), (1,)), ((), ()))
    contract_rows = (((0,), (0,)), ((), ()))

    def rotary(t):
        t1, t2 = t[:, :half], t[:, half:]
        return jnp.concatenate([t1 * cos - t2 * sin, t1 * sin + t2 * cos], axis=1)

    for hh in range(RET_HEADS_PER_STEP):
        qk = slice(hh * RET_QK_DIM, (hh + 1) * RET_QK_DIM)
        vg = slice(hh * RET_V_DIM, (hh + 1) * RET_V_DIM)
        q = rotary(q_ref[:, qk].astype(F32))
        k = rotary(k_ref[:, qk].astype(F32)) * (RET_QK_DIM ** -0.5)
        v = v_ref[:, vg]
        scores = lax.dot_general(q.astype(BF16), k.astype(BF16), contract_last,
                                 preferred_element_type=F32) * dec_ref[hh]
        state = state_ref[hh]
        out = (jnp.dot(scores.astype(BF16), v, preferred_element_type=F32)
               + jnp.dot((q * qd_ref[hh]).astype(BF16), state.astype(BF16), preferred_element_type=F32))
        state_ref[hh] = state * cd_ref[hh] + lax.dot_general(
            (k * kd_ref[hh]).astype(BF16), v, contract_rows, preferred_element_type=F32)
        on = out * lax.rsqrt(jnp.mean(out * out, axis=-1, keepdims=True) + NORM_EPS)
        o_ref[:, vg] = (_silu(g_ref[:, vg].astype(F32)) * on).astype(o_ref.dtype)


def _retention_tables(seq):
    half = RET_QK_DIM // 2
    inv_freq = ROPE_BASE ** (-np.arange(half, dtype=np.float64) / half)
    ang = np.arange(seq, dtype=np.float64)[:, None] * inv_freq[None, :]
    log_gamma = np.log(1.0 - 2.0 ** (-5.0 - np.arange(RET_HEADS, dtype=np.float64)))[:, None, None]
    pos = np.arange(RET_BLOCK, dtype=np.float64)
    n, m = pos[:, None], pos[None, :]
    cn, cm = np.floor(n / CHUNK), np.floor(m / CHUNK)
    dist = np.where(cn == cm, np.abs(n - m), n - m)
    dec = np.where((cm <= cn)[None], np.exp(log_gamma * dist[None]), 0.0)
    qd = np.broadcast_to(np.exp(log_gamma * (pos[None, :, None] + 1.0)), (RET_HEADS, RET_BLOCK, RET_QK_DIM))
    kd = np.broadcast_to(np.exp(log_gamma * (RET_BLOCK - 1.0 - pos[None, :, None])),
                         (RET_HEADS, RET_BLOCK, RET_QK_DIM))
    cd = np.broadcast_to(np.exp(log_gamma * RET_BLOCK), (RET_HEADS, 1, RET_V_DIM))
    return tuple(jnp.asarray(np.ascontiguousarray(v), F32) for v in (np.cos(ang), np.sin(ang), dec, qd, kd, cd))


def _retention(proj, batch, seq, col0):
    t = proj.shape[0]
    r = RET_BLOCK
    nsb = seq // r
    cos, sin, dec, qd, kd, cd = _retention_tables(seq)
    hp = RET_HEADS_PER_STEP
    groups = RET_HEADS // hp
    qk_w, v_w = hp * RET_QK_DIM, hp * RET_V_DIM
    assert col0 % qk_w == 0 and (col0 + 2 * RET_QK_W) % v_w == 0
    qk_blk0 = col0 // qk_w
    v_blk0 = (col0 + 2 * RET_QK_W) // v_w
    row = lambda b, h, s: b * nsb + s
    head_tab = lambda shape: pl.BlockSpec((hp,) + shape, lambda b, h, s: (h, 0, 0))
    half = RET_QK_DIM // 2
    return pl.pallas_call(
        _retention_kernel,
        out_shape=jax.ShapeDtypeStruct((t, RET_V_W), BF16),
        grid=(batch, groups, nsb),
        in_specs=[
            pl.BlockSpec((r, qk_w), lambda b, h, s: (row(b, h, s), qk_blk0 + h)),
            pl.BlockSpec((r, qk_w), lambda b, h, s: (row(b, h, s), qk_blk0 + groups + h)),
            pl.BlockSpec((r, v_w), lambda b, h, s: (row(b, h, s), v_blk0 + h)),
            pl.BlockSpec((r, v_w), lambda b, h, s: (row(b, h, s), v_blk0 + groups + h)),
            pl.BlockSpec((r, half), lambda b, h, s: (s, 0)),
            pl.BlockSpec((r, half), lambda b, h, s: (s, 0)),
            head_tab((r, r)),
            head_tab((r, RET_QK_DIM)),
            head_tab((r, RET_QK_DIM)),
            head_tab((1, RET_V_DIM)),
        ],
        out_specs=pl.BlockSpec((r, v_w), lambda b, h, s: (row(b, h, s), h)),
        scratch_shapes=[pltpu.VMEM((hp, RET_QK_DIM, RET_V_DIM), F32)],
        compiler_params=_params(("arbitrary", "arbitrary", "arbitrary")),
        name="retention",
    )(proj, proj, proj, proj, cos, sin, dec, qd, kd, cd)


ATT_KEY_BLOCKS = PREV_CHUNKS * CHUNK // ATT_BLOCK + 1
NEG = float(np.finfo(np.float32).min)


def _attention_kernel(q_ref, k0_ref, k1_ref, k2_ref, v0_ref, v1_ref, v2_ref, tab_ref, gq_ref, gk_ref, o_ref):
    i = pl.program_id(1)

    def rms(t, g):
        tf = t.astype(F32)
        return (tf * lax.rsqrt(jnp.mean(tf * tf, axis=-1, keepdims=True) + NORM_EPS)) * g

    dh = ATT_HEAD_DIM
    k_all = jnp.concatenate([k0_ref[...], k1_ref[...], k2_ref[...]], axis=0)
    v_all = jnp.concatenate([v0_ref[...], v1_ref[...], v2_ref[...]], axis=0)
    outs = []
    for hh in range(ATT_HEADS_PER_STEP):
        lanes = slice(hh * dh, (hh + 1) * dh)
        q = rms(q_ref[:, lanes], gq_ref[...]).astype(BF16)
        k = rms(k_all[:, lanes], gk_ref[...]).astype(BF16)
        sc = lax.dot_general(q, k, (((1,), (1,)), ((), ())), preferred_element_type=F32)
        sc = sc * (ATT_HEAD_DIM ** -0.5) + tab_ref[hh]
        col = lax.broadcasted_iota(jnp.int32, sc.shape, 1)
        sc = jnp.where(col >= (ATT_KEY_BLOCKS - 1 - i) * ATT_BLOCK, sc, NEG)
        p = jnp.exp(sc - jnp.max(sc, axis=-1, keepdims=True))
        denom = jnp.sum(p, axis=-1, keepdims=True)
        outs.append(jnp.dot(p.astype(BF16), v_all[:, lanes], preferred_element_type=F32) / denom)
    o_ref[...] = jnp.concatenate(outs, axis=1).astype(o_ref.dtype)


def _attention_table(rel_bias):
    a = ATT_BLOCK
    w = ATT_KEY_BLOCKS * a
    pad = PREV_CHUNKS * CHUNK
    heads = rel_bias.shape[0]
    period = 1024
    assert a - 1 + w - 1 <= period - 2
    rel_of_i = pad + a - 1 - np.arange(period)
    idx = np.clip(np.minimum(rel_of_i, MAX_REL) + (CHUNK - 1), 0, rel_bias.shape[-1] - 1)
    u = rel_bias[:, idx].astype(F32)
    skew = jnp.tile(u, (1, a))[:, :a * (period - 1)].reshape(heads, a, period - 1)
    bias = skew[:, :, a - 1:a - 1 + w]
    n = np.arange(a)[:, None]
    j = np.arange(w)[None, :]
    qc = n // CHUNK
    kc = j // CHUNK - PREV_CHUNKS
    valid = (kc <= qc) & (kc >= qc - PREV_CHUNKS)
    return jnp.where(jnp.asarray(valid)[None], bias, NEG)


def _chunk_attention(proj, rel_bias, q_norm_g, k_norm_g, batch, seq, col0):
    assert ATT_KEY_BLOCKS == 3
    t = proj.shape[0]
    a = ATT_BLOCK
    nqb = seq // a
    dh = ATT_HEAD_DIM
    hw = ATT_HEADS_PER_STEP * dh
    groups = ATT_HEADS // ATT_HEADS_PER_STEP
    assert col0 % hw == 0 and ATT_W % hw == 0
    blk0 = col0 // hw
    tab = _attention_table(rel_bias)

    def kv_spec(which, jj):
        return pl.BlockSpec(
            (a, hw),
            lambda h, i, b: (b * nqb + jnp.maximum(i - (ATT_KEY_BLOCKS - 1) + jj, 0), blk0 + which * groups + h))

    return pl.pallas_call(
        _attention_kernel,
        out_shape=jax.ShapeDtypeStruct((t, ATT_W), BF16),
        grid=(groups, nqb, batch),
        in_specs=[
            pl.BlockSpec((a, hw), lambda h, i, b: (b * nqb + i, blk0 + h)),
            kv_spec(1, 0), kv_spec(1, 1), kv_spec(1, 2),
            kv_spec(2, 0), kv_spec(2, 1), kv_spec(2, 2),
            pl.BlockSpec((ATT_HEADS_PER_STEP, a, ATT_KEY_BLOCKS * a), lambda h, i, b: (h, 0, 0)),
            pl.BlockSpec((1, dh), lambda h, i, b: (0, 0)),
            pl.BlockSpec((1, dh), lambda h, i, b: (0, 0)),
        ],
        out_specs=pl.BlockSpec((a, hw), lambda h, i, b: (b * nqb + i, h)),
        compiler_params=_params(("arbitrary", "arbitrary", "arbitrary")),
        name="chunk_attention",
    )(proj, proj, proj, proj, proj, proj, proj, tab, q_norm_g.reshape(1, dh), k_norm_g.reshape(1, dh))


def _merge_kernel(r_ref, a_ref, wr_ref, wa_ref, gr_ref, ga_ref, o_ref):
    yr = jnp.dot(r_ref[...], wr_ref[...].astype(BF16), preferred_element_type=F32)
    ya = jnp.dot(a_ref[...], wa_ref[...].astype(BF16), preferred_element_type=F32)
    merged = jax.nn.sigmoid(gr_ref[...].astype(F32)) * yr + jax.nn.sigmoid(ga_ref[...].astype(F32)) * ya
    o_ref[...] = merged.astype(o_ref.dtype)


def _merge(ret, att, w_ret_out, w_att_out, proj, gate_col0, layer):
    t = ret.shape[0]
    d = w_ret_out.shape[2]
    tm, tn = MERGE_TM, MERGE_TN
    gblk = gate_col0 // tn
    return pl.pallas_call(
        _merge_kernel,
        out_shape=jax.ShapeDtypeStruct((t, d), BF16),
        grid=(d // tn, t // tm),
        in_specs=[
            pl.BlockSpec((tm, ret.shape[1]), lambda j, i: (i, 0)),
            pl.BlockSpec((tm, att.shape[1]), lambda j, i: (i, 0)),
            pl.BlockSpec((None, ret.shape[1], tn), lambda j, i: (layer, 0, j)),
            pl.BlockSpec((None, att.shape[1], tn), lambda j, i: (layer, 0, j)),
            pl.BlockSpec((tm, tn), lambda j, i: (i, gblk + j)),
            pl.BlockSpec((tm, tn), lambda j, i: (i, gblk + d // tn + j)),
        ],
        out_specs=pl.BlockSpec((tm, tn), lambda j, i: (i, j)),
        compiler_params=_params(("arbitrary", "arbitrary")),
        name="merge",
    )(ret, att, w_ret_out, w_att_out, proj, proj)


def _out_proj_kernel(a_ref, w_ref, x_ref, gate_ref, o_ref):
    acc = jnp.dot(a_ref[...], w_ref[...].astype(BF16), preferred_element_type=F32)
    o_ref[...] = x_ref[...] + gate_ref[...] * acc


def _out_proj(a, w, x2, mod3, layer, seq, gate_idx):
    t, k = a.shape
    d = w.shape[2]
    tm, tn = OUT_TM, OUT_TN
    return pl.pallas_call(
        _out_proj_kernel,
        out_shape=jax.ShapeDtypeStruct((t, d), F32),
        grid=(d // tn, t // tm),
        in_specs=[
            pl.BlockSpec((tm, k), lambda j, i: (i, 0)),
            pl.BlockSpec((None, k, tn), lambda j, i: (layer, 0, j)),
            pl.BlockSpec((tm, tn), lambda j, i: (i, j)),
            pl.BlockSpec((None, 1, tn), lambda j, i: ((i * tm // seq) * N_MOD + gate_idx, 0, j)),
        ],
        out_specs=pl.BlockSpec((tm, tn), lambda j, i: (i, j)),
        compiler_params=_params(("arbitrary", "arbitrary")),
        name="out_proj",
    )(a, w, x2, mod3)


def _routing(logits):
    t = logits.shape[0]
    bm = EXPERT_BM
    a = t * TOP_K
    top_val, top_idx = lax.top_k(logits, TOP_K)
    gate = jax.nn.softmax(top_val, axis=-1)
    onehot = top_idx[:, :, None] == jnp.arange(N_EXPERTS, dtype=jnp.int32)[None, None, :]
    picked = jnp.any(onehot, axis=1).astype(jnp.int32)
    inclusive = jnp.cumsum(picked, axis=0)
    sizes = inclusive[-1]
    sub = EXPERT_SUB
    padded = (sizes + sub - 1) // sub * sub
    pad_end = jnp.cumsum(padded)
    pad_start = pad_end - padded
    start = jnp.cumsum(sizes) - sizes
    slot = (pad_start[None, :] + inclusive - picked)[:, None, :]
    dest = jnp.sum(jnp.where(onehot, slot, 0), axis=-1).astype(jnp.int32).reshape(a)
    order = jnp.argsort(top_idx.reshape(a))
    n_blocks = a // bm + N_EXPERTS
    group_blocks = (sizes + bm - 1) // bm
    blocks_end = jnp.cumsum(group_blocks)
    n_valid = blocks_end[-1].astype(jnp.int32)
    blk = jnp.arange(n_blocks, dtype=jnp.int32)
    block_e = jnp.minimum(jnp.searchsorted(blocks_end, blk, side='right'), N_EXPERTS - 1).astype(jnp.int32)
    chunk = blk - (blocks_end - group_blocks)[block_e]
    block_valid = blk < n_valid
    block_rows = jnp.where(block_valid, jnp.clip(sizes[block_e] - chunk * bm, 0, bm), 0)
    block_nsub = ((block_rows + sub - 1) // sub).astype(jnp.int32)
    block_row0 = jnp.where(block_valid, pad_start[block_e] + chunk * bm, 0).astype(jnp.int32)
    padded_len = (a // TOK_ALIGN) * TOK_ALIGN + TOK_WINDOW
    sorted_tok = jnp.zeros((padded_len,), jnp.int32).at[:a].set((order // TOP_K).astype(jnp.int32))
    block_first = jnp.where(block_valid, start[block_e] + chunk * bm, 0).astype(jnp.int32)
    tables = (block_e, block_nsub, block_first, block_row0, jnp.stack([n_valid, pad_end[-1].astype(jnp.int32)]))
    return gate, dest, sorted_tok, tables


def _expert_kernel(be_ref, ns_ref, first_ref, row0_ref, info_ref, tok_hbm, h_hbm, w1_hbm, b1_ref, w2_hbm, b2_ref,
                   perm_ref, ys_hbm, w1buf, w2buf, gbuf, x_scr, act_scr, obuf, zbuf, tokbuf,
                   sem_w1, sem_w2, sem_g, sem_o, sem_t, *, layer, nj1, nj2):
    sub = EXPERT_SUB
    group = GATHER_GROUP
    bm, d = x_scr.shape
    half = d // 2
    tf = act_scr.shape[2]
    tn = obuf.shape[2]
    steps = nj1 + nj2
    nb_max = be_ref.shape[0]
    n_valid = info_ref[0]
    total = n_valid * steps
    per_step = bm // EXPERT_ISSUE_STEPS

    def for_rows(nsub, fn):
        def pair(p, carry):
            fn(pl.multiple_of(p * 2 * sub, 2 * sub), 2 * sub)
            return carry
        lax.fori_loop(0, nsub // 2, pair, 0)

        @pl.when(nsub % 2 == 1)
        def _():
            fn(pl.multiple_of((nsub - 1) * sub, sub), sub)

    def w1_copy(e, j, slot):
        cols = pl.ds(pl.multiple_of(j * 2 * tf, 2 * tf), 2 * tf)
        return pltpu.make_async_copy(w1_hbm.at[layer, e, :, cols], w1buf.at[slot], sem_w1.at[slot])

    def w2_copies(e, n, slot):
        return [pltpu.make_async_copy(w2_hbm.at[layer, e, :, pl.ds(pl.multiple_of(c0, tn), tn)],
                                      w2buf.at[slot, :, pl.ds(k * tn, tn)], sem_w2.at[slot])
                for k, c0 in enumerate((n * tn, half + n * tn))]

    def tok_copy(blk, slot):
        src = tok_hbm.at[pl.ds(pl.multiple_of(first_ref[blk] // TOK_ALIGN * TOK_ALIGN, TOK_ALIGN), TOK_WINDOW)]
        dst = tokbuf.at[pl.ds(pl.multiple_of(slot * TOK_WINDOW, TOK_ALIGN), TOK_WINDOW)]
        return pltpu.make_async_copy(src, dst, sem_t.at[slot])

    def tok_base(blk, slot):
        return slot * TOK_WINDOW + lax.rem(first_ref[blk], TOK_ALIGN)

    def row_copy(tok_at, tile_row, sublane):
        src = h_hbm.at[pl.ds(tokbuf[tok_at], 1)]
        return pltpu.make_async_copy(src, gbuf.at[tile_row, pl.ds(sublane, 1)], sem_g)

    def out_copy(slot, r0, m, row0, n):
        dst = ys_hbm.at[pl.ds(pl.multiple_of(row0 + r0, sub), m), pl.ds(pl.multiple_of(n * tn, tn), tn)]
        return pltpu.make_async_copy(obuf.at[slot, pl.ds(r0, m)], dst, sem_o.at[slot])

    @pl.when(n_valid > 0)
    def _():
        tok_copy(0, 0).start()
        tok_copy(0, 0).wait()
        base = tok_base(0, 0)

        def start(q, carry):
            for s in range(SUBLANES):
                row_copy(base + q * SUBLANES + s, q, s).start()
            return carry
        lax.fori_loop(0, ns_ref[0] * (sub // SUBLANES), start, 0)
        w1_copy(be_ref[0], 0, 0).start()

    def item(it, carry):
        b = it // steps
        j = it - b * steps
        e = be_ref[b]
        nsub = ns_ref[b]
        has_next = b + 1 < n_valid
        nxt = jnp.minimum(b + 1, nb_max - 1)

        nit = it + 1
        nbk = jnp.minimum(nit // steps, nb_max - 1)
        nj = nit - (nit // steps) * steps

        @pl.when((nit < total) & (nj < nj1))
        def _():
            w1_copy(be_ref[nbk], nj, lax.rem(nj, 2)).start()

        @pl.when((nit < total) & (nj >= nj1))
        def _():
            for c in w2_copies(be_ref[nbk], nj - nj1, lax.rem(nj - nj1, 2)):
                c.start()

        @pl.when(j < nj1)
        def _():
            w1_copy(e, j, lax.rem(j, 2)).wait()

        @pl.when(j >= nj1)
        def _():
            for c in w2_copies(e, j - nj1, lax.rem(j - nj1, 2)):
                c.wait()

        @pl.when(j == 0)
        def _():
            def wait_rows(s, carry):
                pltpu.make_async_copy(h_hbm.at[pl.ds(0, sub)], zbuf, sem_g).wait()
                return carry
            lax.fori_loop(0, nsub, wait_rows, 0)

            def unpack_rows(s, carry):
                rows = pl.ds(pl.multiple_of(s * sub, sub), sub)
                tiles = pl.ds(pl.multiple_of(s * (sub // SUBLANES), sub // SUBLANES), sub // SUBLANES)
                xa, xb = _unpack_halves(gbuf[tiles].reshape(sub, half))
                x_scr[rows, pl.ds(0, half)] = xa.astype(BF16)
                x_scr[rows, pl.ds(half, half)] = xb.astype(BF16)
                return carry
            lax.fori_loop(0, nsub, unpack_rows, 0)

            @pl.when(has_next)
            def _():
                tok_copy(nxt, lax.rem(b + 1, 2)).start()

        @pl.when((j == 1) & has_next)
        def _():
            tok_copy(nxt, lax.rem(b + 1, 2)).wait()

        @pl.when((j >= 1) & (j <= EXPERT_ISSUE_STEPS) & has_next)
        def _():
            base = tok_base(nxt, lax.rem(b + 1, 2))
            rows_next = ns_ref[nxt] * sub
            for g in range(per_step // group):
                first_row = (j - 1) * per_step + g * group
                first_tile = (j - 1) * (per_step // SUBLANES) + g * (group // SUBLANES)

                @pl.when(first_row < rows_next)
                def _():
                    for i in range(group):
                        row_copy(base + first_row + i, first_tile + i // SUBLANES, i % SUBLANES).start()

        @pl.when(j < nj1)
        def _():
            wslot = lax.rem(j, 2)
            bias = b1_ref[pl.ds(e * nj1 + j, 1), :]

            def first_matmul(r0, m):
                rows = pl.ds(r0, m)
                hb = jnp.dot(x_scr[rows, :], w1buf[wslot].astype(BF16), preferred_element_type=F32) + bias
                hb = hb.astype(BF16)
                gw = perm_ref.shape[0]
                sel = [jnp.dot(hb[:, c * gw:(c + 1) * gw], perm_ref[...], preferred_element_type=F32)
                       for c in range(2 * tf // gw)]
                x_glu = jnp.minimum(jnp.concatenate([s[:, :gw // 2] for s in sel], axis=1), SWIGLU_LIMIT)
                x_lin = jnp.clip(jnp.concatenate([s[:, gw // 2:] for s in sel], axis=1),
                                 -SWIGLU_LIMIT, SWIGLU_LIMIT)
                act = x_glu * jax.nn.sigmoid(SWIGLU_ALPHA * x_glu) * (x_lin + 1.0)
                act_scr[j, rows, :] = act.astype(BF16)
            for_rows(nsub, first_matmul)

        @pl.when(j >= nj1)
        def _():
            n = j - nj1
            slot = lax.rem(n, 2)
            row0 = row0_ref[b]
            bias = jnp.concatenate([b2_ref[pl.ds(e * 2 * nj2 + n, 1), :],
                                    b2_ref[pl.ds(e * 2 * nj2 + nj2 + n, 1), :]], axis=1)
            sent = jnp.where(n >= 2, nsub, jnp.where(b > 0, ns_ref[jnp.maximum(b - 1, 0)], 0))
            for_rows(sent, lambda r0, m: out_copy(slot, r0, m, 0, 0).wait())

            def second_matmul(r0, m):
                rows = pl.ds(r0, m)
                act = jnp.concatenate([act_scr[t, rows, :] for t in range(nj1)], axis=1)
                y = jnp.dot(act, w2buf[slot].astype(BF16), preferred_element_type=F32) + bias
                obuf[slot, rows, :] = _pack_halves(y)
                out_copy(slot, r0, m, row0, n).start()
            for_rows(nsub, second_matmul)

        return carry

    lax.fori_loop(0, total, item, 0)

    @pl.when(n_valid > 0)
    def _():
        last = ns_ref[jnp.maximum(n_valid - 1, 0)]
        for slot in (0, 1):
            for_rows(last, lambda r0, m: out_copy(slot, r0, m, 0, 0).wait())

    used = info_ref[1]
    n_tail = (ys_hbm.shape[0] - used) // sub
    zbuf[...] = jnp.zeros_like(zbuf)

    def tail_copy(i):
        dst = ys_hbm.at[pl.ds(pl.multiple_of(used + i * sub, sub), sub)]
        return pltpu.make_async_copy(zbuf, dst, sem_g)

    def tail_start(i, carry):
        tail_copy(i).start()
        return carry

    def tail_wait(i, carry):
        tail_copy(i).wait()
        return carry
    lax.fori_loop(0, n_tail, tail_start, 0)
    lax.fori_loop(0, n_tail, tail_wait, 0)


def _deinterleave_matrix(tf):
    p = np.zeros((2 * tf, 2 * tf), np.float32)
    f = np.arange(tf)
    p[2 * f, f] = 1.0
    p[2 * f + 1, tf + f] = 1.0
    return jnp.asarray(p, BF16)


def _experts(h_packed, sorted_tok, tables, w1, b1, w2, b2, layer):
    t, half = h_packed.shape
    d = 2 * half
    n_experts, f = w2.shape[1], w2.shape[2]
    bm, tf, tn = EXPERT_BM, EXPERT_TF, EXPERT_TN
    nj1 = f // tf
    nj2 = half // tn
    assert nj1 % 2 == 0 and nj2 % 2 == 0
    assert 1 + EXPERT_ISSUE_STEPS <= nj1 + nj2 and bm % (EXPERT_ISSUE_STEPS * GATHER_GROUP) == 0
    assert bm % EXPERT_SUB == 0 and t >= EXPERT_SUB and GATHER_GROUP % SUBLANES == 0
    assert bm + TOK_ALIGN - 1 <= TOK_WINDOW
    p_rows = t * TOP_K + n_experts * EXPERT_SUB
    whole = lambda shape: pl.BlockSpec(shape, lambda i, *_: (0,) * len(shape))
    hbm = pl.BlockSpec(memory_space=pl.ANY)
    return pl.pallas_call(
        functools.partial(_expert_kernel, layer=layer, nj1=nj1, nj2=nj2),
        out_shape=jax.ShapeDtypeStruct((p_rows, half), U32),
        grid_spec=pltpu.PrefetchScalarGridSpec(
            num_scalar_prefetch=5,
            grid=(1,),
            in_specs=[hbm, hbm, hbm, whole((n_experts * nj1, 2 * tf)), hbm, whole((n_experts * 2 * nj2, tn)),
                      whole((EXPERT_PERM_W, EXPERT_PERM_W))],
            out_specs=hbm,
            scratch_shapes=[
                pltpu.VMEM((2, d, 2 * tf), F32),
                pltpu.VMEM((2, f, 2 * tn), F32),
                pltpu.VMEM((bm // SUBLANES, SUBLANES, half), U32),
                pltpu.VMEM((bm, d), BF16),
                pltpu.VMEM((nj1, bm, tf), BF16),
                pltpu.VMEM((2, bm, tn), U32),
                pltpu.VMEM((EXPERT_SUB, half), U32),
                pltpu.SMEM((2 * TOK_WINDOW,), jnp.int32),
                pltpu.SemaphoreType.DMA((2,)),
                pltpu.SemaphoreType.DMA((2,)),
                pltpu.SemaphoreType.DMA(()),
                pltpu.SemaphoreType.DMA((2,)),
                pltpu.SemaphoreType.DMA((2,)),
            ],
        ),
        compiler_params=_params(("arbitrary",)),
        name="experts",
    )(*tables, sorted_tok, h_packed, w1, b1[layer].reshape(n_experts * nj1, 2 * tf), w2,
      b2[layer].reshape(n_experts * 2 * nj2, tn), _deinterleave_matrix(EXPERT_PERM_W // 2))


def _combine_kernel(dest_ref, ys_hbm, x_ref, gate_ref, g_ref, o_ref, buf, sem):
    tm = x_ref.shape[0]
    half = buf.shape[2]

    def row_copy(src_row, k, r):
        return pltpu.make_async_copy(ys_hbm.at[pl.ds(src_row, 1)], buf.at[k, pl.ds(r, 1)], sem)

    def start(r, carry):
        for k in range(TOP_K):
            row_copy(dest_ref[0, r * TOP_K + k], k, r).start()
        return carry

    def wait(r, carry):
        for k in range(TOP_K):
            row_copy(0, k, r).wait()
        return carry

    lax.fori_loop(0, tm, start, 0)
    lax.fori_loop(0, tm, wait, 0)
    sublanes = SUBLANES

    def chunk(c, carry):
        rows = pl.ds(pl.multiple_of(c * sublanes, sublanes), sublanes)
        g = g_ref[rows, :]
        ya = jnp.zeros((sublanes, half), F32)
        yb = jnp.zeros((sublanes, half), F32)
        for k in range(TOP_K):
            a, b = _unpack_halves(buf[k, rows, :])
            ya = ya + g[:, k:k + 1] * a
            yb = yb + g[:, k:k + 1] * b
        o_ref[rows, pl.ds(0, half)] = x_ref[rows, pl.ds(0, half)] + gate_ref[:, pl.ds(0, half)] * ya
        o_ref[rows, pl.ds(half, half)] = x_ref[rows, pl.ds(half, half)] + gate_ref[:, pl.ds(half, half)] * yb
        return carry

    lax.fori_loop(0, tm // sublanes, chunk, 0)


def _combine(ys, dest, gates, x2, mod3, seq, gate_idx):
    t, d = x2.shape
    tm = COMBINE_TM
    half = ys.shape[1]
    return pl.pallas_call(
        _combine_kernel,
        out_shape=jax.ShapeDtypeStruct((t, d), F32),
        grid=(t // tm,),
        in_specs=[
            pl.BlockSpec((None, 1, tm * TOP_K), lambda i: (i, 0, 0), memory_space=pltpu.SMEM),
            pl.BlockSpec(memory_space=pl.ANY),
            pl.BlockSpec((tm, d), lambda i: (i, 0)),
            pl.BlockSpec((None, 1, d), lambda i: ((i * tm // seq) * N_MOD + gate_idx, 0, 0)),
            pl.BlockSpec((tm, TOP_K), lambda i: (i, 0)),
        ],
        out_specs=pl.BlockSpec((tm, d), lambda i: (i, 0)),
        scratch_shapes=[pltpu.VMEM((TOP_K, tm, half), U32), pltpu.SemaphoreType.DMA(())],
        compiler_params=_params(("arbitrary",)),
        name="combine",
    )(dest.reshape(t // tm, 1, tm * TOP_K), ys, x2, mod3, gates)


def kernel(x, c, w_ada, b_ada, norm1_g, w_in, q_norm_g, k_norm_g, rel_bias, w_ret_out, w_att_out, w_out,
           norm2_g, w_router, b_router, w1, b1, w2, b2):
    batch, seq, d = x.shape
    depth = w_ada.shape[0]
    t = batch * seq
    assert batch <= 8 and seq % max(RET_BLOCK, ATT_BLOCK, NORM_TM, OUT_TM, COMBINE_TM) == 0 and t % PROJ_TM == 0
    x2 = x.reshape(t, d)
    c8 = jnp.zeros((8, d), F32).at[:batch].set(c)
    ret_col0 = 0
    att_col0 = 2 * RET_QK_W + 2 * RET_V_W
    gate_col0 = att_col0 + 3 * ATT_W
    for layer in range(depth):
        mod3 = _ada_ln(c8, w_ada, b_ada, layer)[:batch].reshape(batch * N_MOD, 1, d)
        h = _norm_mod(x2, norm1_g.reshape(depth, 1, d), mod3, layer, seq, shift_idx=0, scale_idx=1)
        proj = _project(h, w_in, layer)
        ret = _retention(proj, batch, seq, ret_col0)
        att = _chunk_attention(proj, rel_bias[layer], q_norm_g[layer], k_norm_g[layer], batch, seq, att_col0)
        merged = _merge(ret, att, w_ret_out, w_att_out, proj, gate_col0, layer)
        x2 = _out_proj(merged, w_out, x2, mod3, layer, seq, gate_idx=2)
        h_packed, logits = _norm_router(x2, norm2_g.reshape(depth, 1, d), mod3, w_router, b_router, layer, seq,
                                        shift_idx=3, scale_idx=4)
        gates, dest, sorted_tok, tables = _routing(logits)
        ys = _experts(h_packed, sorted_tok, tables, w1, b1, w2, b2, layer)
        x2 = _combine(ys, dest, gates, x2, mod3, seq, gate_idx=5)
    return x2.reshape(batch, seq, d)
```

```python
import functools

import numpy as np
import jax
import jax.numpy as jnp
from jax import lax
from jax.experimental import pallas as pl
from jax.experimental.pallas import tpu as pltpu

F32 = jnp.float32
BF16 = jnp.bfloat16
U32 = jnp.uint32

CHUNK = 64
NORM_EPS = 1e-6
RET_HEADS = 8
RET_QK_DIM = 256
RET_V_DIM = 512
ROPE_BASE = 10000.0
ATT_HEADS = 16
ATT_HEAD_DIM = 128
PREV_CHUNKS = 8
MAX_REL = 128
N_EXPERTS = 32
TOP_K = 4
SWIGLU_ALPHA = 1.702
SWIGLU_LIMIT = 7.0
N_MOD = 6

RET_QK_W = RET_HEADS * RET_QK_DIM
RET_V_W = RET_HEADS * RET_V_DIM
ATT_W = ATT_HEADS * ATT_HEAD_DIM

V7X_VMEM_LIMIT_BYTES = 58 * 1024 * 1024
HI16 = 0xFFFF0000

ADA_TN = 1024
NORM_TM = 512
PROJ_TM, PROJ_TN = 2048, 512
RET_BLOCK = 256
RET_HEADS_PER_STEP = 4
ATT_BLOCK = 256
ATT_HEADS_PER_STEP = 4
MERGE_TM, MERGE_TN = 1024, 512
OUT_TM, OUT_TN = 1024, 512
EXPERT_BM = 1280
EXPERT_SUB = 256
EXPERT_TF = 256
EXPERT_TN = 256
EXPERT_PERM_W = 256
EXPERT_ISSUE_STEPS = 10
GATHER_GROUP = 32
TOK_ALIGN = 1024
TOK_WINDOW = 3 * TOK_ALIGN
SUBLANES = 8
COMBINE_TM = 256


def _params(semantics):
    return pltpu.CompilerParams(dimension_semantics=semantics, vmem_limit_bytes=V7X_VMEM_LIMIT_BYTES)


def _silu(t):
    return t * jax.nn.sigmoid(t)


def _pack_halves(y):
    n = y.shape[1] // 2
    hi = lax.bitcast_convert_type(y[:, :n].astype(BF16).astype(F32), U32)
    lo = lax.bitcast_convert_type(y[:, n:].astype(BF16).astype(F32), U32)
    return hi | (lo >> 16)


def _unpack_halves(p):
    a = lax.bitcast_convert_type(p & jnp.uint32(HI16), F32)
    b = lax.bitcast_convert_type(p << 16, F32)
    return a, b


def _ada_kernel(c_ref, w_ref, b_ref, o_ref):
    a = _silu(c_ref[...]).astype(BF16)
    o_ref[...] = jnp.dot(a, w_ref[...].astype(BF16), preferred_element_type=F32) + b_ref[...]


def _ada_ln(c8, w_ada, b_ada, layer):
    rows, d = c8.shape
    n = w_ada.shape[2]
    return pl.pallas_call(
        _ada_kernel,
        out_shape=jax.ShapeDtypeStruct((rows, n), F32),
        grid=(n // ADA_TN,),
        in_specs=[
            pl.BlockSpec((rows, d), lambda j: (0, 0)),
            pl.BlockSpec((None, d, ADA_TN), lambda j: (layer, 0, j)),
            pl.BlockSpec((None, 1, ADA_TN), lambda j: (layer, 0, j)),
        ],
        out_specs=pl.BlockSpec((rows, ADA_TN), lambda j: (0, j)),
        compiler_params=_params(("arbitrary",)),
        name="ada_ln",
    )(c8, w_ada, b_ada.reshape(b_ada.shape[0], 1, n))


def _modulated_norm(x, g, scale, shift):
    xn = x * lax.rsqrt(jnp.mean(x * x, axis=-1, keepdims=True) + NORM_EPS)
    return (xn * g) * (1.0 + scale) + shift


def _norm_mod_kernel(x_ref, g_ref, sc_ref, sh_ref, o_ref):
    o_ref[...] = _modulated_norm(x_ref[...], g_ref[...], sc_ref[...], sh_ref[...]).astype(o_ref.dtype)


def _norm_mod(x2, g, mod3, layer, seq, shift_idx, scale_idx):
    t, d = x2.shape
    tm = NORM_TM

    def mod_spec(idx):
        return pl.BlockSpec((None, 1, d), lambda i: ((i * tm // seq) * N_MOD + idx, 0, 0))

    return pl.pallas_call(
        _norm_mod_kernel,
        out_shape=jax.ShapeDtypeStruct((t, d), BF16),
        grid=(t // tm,),
        in_specs=[
            pl.BlockSpec((tm, d), lambda i: (i, 0)),
            pl.BlockSpec((None, 1, d), lambda i: (layer, 0, 0)),
            mod_spec(scale_idx),
            mod_spec(shift_idx),
        ],
        out_specs=pl.BlockSpec((tm, d), lambda i: (i, 0)),
        compiler_params=_params(("arbitrary",)),
        name="norm_mod",
    )(x2, g, mod3, mod3)


def _norm_router_kernel(x_ref, g_ref, sc_ref, sh_ref, wr_ref, br_ref, h_ref, lg_ref):
    h = _modulated_norm(x_ref[...], g_ref[...], sc_ref[...], sh_ref[...])
    h_ref[...] = _pack_halves(h)
    h_hi = h.astype(BF16)
    h_lo = (h - h_hi.astype(F32)).astype(BF16)
    w = wr_ref[...]
    w_hi = w.astype(BF16)
    w_lo = (w - w_hi.astype(F32)).astype(BF16)
    lg = (jnp.dot(h_hi, w_hi, preferred_element_type=F32)
          + jnp.dot(h_lo, w_hi, preferred_element_type=F32)
          + jnp.dot(h_hi, w_lo, preferred_element_type=F32))
    lg_ref[...] = lg + br_ref[...]


def _norm_router(x2, g, mod3, w_router, b_router, layer, seq, shift_idx, scale_idx):
    t, d = x2.shape
    e = w_router.shape[2]
    tm = NORM_TM

    def mod_spec(idx):
        return pl.BlockSpec((None, 1, d), lambda i: ((i * tm // seq) * N_MOD + idx, 0, 0))

    return pl.pallas_call(
        _norm_router_kernel,
        out_shape=(jax.ShapeDtypeStruct((t, d // 2), U32), jax.ShapeDtypeStruct((t, e), F32)),
        grid=(t // tm,),
        in_specs=[
            pl.BlockSpec((tm, d), lambda i: (i, 0)),
            pl.BlockSpec((None, 1, d), lambda i: (layer, 0, 0)),
            mod_spec(scale_idx),
            mod_spec(shift_idx),
            pl.BlockSpec((None, d, e), lambda i: (layer, 0, 0)),
            pl.BlockSpec((None, 1, e), lambda i: (layer, 0, 0)),
        ],
        out_specs=(pl.BlockSpec((tm, d // 2), lambda i: (i, 0)), pl.BlockSpec((tm, e), lambda i: (i, 0))),
        compiler_params=_params(("arbitrary",)),
        name="norm_router",
    )(x2, g, mod3, mod3, w_router, b_router.reshape(b_router.shape[0], 1, e))


def _proj_kernel(a_ref, w_ref, o_ref):
    o_ref[...] = jnp.dot(a_ref[...], w_ref[...].astype(BF16), preferred_element_type=F32).astype(o_ref.dtype)


def _project(a, w, layer):
    t, k = a.shape
    n = w.shape[2]
    tm, tn = PROJ_TM, PROJ_TN
    return pl.pallas_call(
        _proj_kernel,
        out_shape=jax.ShapeDtypeStruct((t, n), BF16),
        grid=(n // tn, t // tm),
        in_specs=[
            pl.BlockSpec((tm, k), lambda j, i: (i, 0)),
            pl.BlockSpec((None, k, tn), lambda j, i: (layer, 0, j)),
        ],
        out_specs=pl.BlockSpec((tm, tn), lambda j, i: (i, j)),
        compiler_params=_params(("arbitrary", "arbitrary")),
        name="in_proj",
    )(a, w)


def _retention_kernel(q_ref, k_ref, v_ref, g_ref, cos_ref, sin_ref, dec_ref, qd_ref, kd_ref, cd_ref,
                      o_ref, state_ref):
    @pl.when(pl.program_id(2) == 0)
    def _():
        state_ref[...] = jnp.zeros_like(state_ref)

    cos = cos_ref[...]
    sin = sin_ref[...]
    half = RET_QK_DIM // 2
    contract_last = (((1,), (1,)), ((), ()))
    contract_rows = (((0,), (0,)), ((), ()))

    def rotary(t):
        t1, t2 = t[:, :half], t[:, half:]
        return jnp.concatenate([t1 * cos - t2 * sin, t1 * sin + t2 * cos], axis=1)

    for hh in range(RET_HEADS_PER_STEP):
        qk = slice(hh * RET_QK_DIM, (hh + 1) * RET_QK_DIM)
        vg = slice(hh * RET_V_DIM, (hh + 1) * RET_V_DIM)
        q = rotary(q_ref[:, qk].astype(F32))
        k = rotary(k_ref[:, qk].astype(F32)) * (RET_QK_DIM ** -0.5)
        v = v_ref[:, vg]
        scores = lax.dot_general(q.astype(BF16), k.astype(BF16), contract_last,
                                 preferred_element_type=F32) * dec_ref[hh]
        state = state_ref[hh]
        out = (jnp.dot(scores.astype(BF16), v, preferred_element_type=F32)
               + jnp.dot((q * qd_ref[hh]).astype(BF16), state.astype(BF16), preferred_element_type=F32))
        state_ref[hh] = state * cd_ref[hh] + lax.dot_general(
            (k * kd_ref[hh]).astype(BF16), v, contract_rows, preferred_element_type=F32)
        on = out * lax.rsqrt(jnp.mean(out * out, axis=-1, keepdims=True) + NORM_EPS)
        o_ref[:, vg] = (_silu(g_ref[:, vg].astype(F32)) * on).astype(o_ref.dtype)


def _retention_tables(seq):
    half = RET_QK_DIM // 2
    inv_freq = ROPE_BASE ** (-np.arange(half, dtype=np.float64) / half)
    ang = np.arange(seq, dtype=np.float64)[:, None] * inv_freq[None, :]
    log_gamma = np.log(1.0 - 2.0 ** (-5.0 - np.arange(RET_HEADS, dtype=np.float64)))[:, None, None]
    pos = np.arange(RET_BLOCK, dtype=np.float64)
    n, m = pos[:, None], pos[None, :]
    cn, cm = np.floor(n / CHUNK), np.floor(m / CHUNK)
    dist = np.where(cn == cm, np.abs(n - m), n - m)
    dec = np.where((cm <= cn)[None], np.exp(log_gamma * dist[None]), 0.0)
    qd = np.broadcast_to(np.exp(log_gamma * (pos[None, :, None] + 1.0)), (RET_HEADS, RET_BLOCK, RET_QK_DIM))
    kd = np.broadcast_to(np.exp(log_gamma * (RET_BLOCK - 1.0 - pos[None, :, None])),
                         (RET_HEADS, RET_BLOCK, RET_QK_DIM))
    cd = np.broadcast_to(np.exp(log_gamma * RET_BLOCK), (RET_HEADS, 1, RET_V_DIM))
    return tuple(jnp.asarray(np.ascontiguousarray(v), F32) for v in (np.cos(ang), np.sin(ang), dec, qd, kd, cd))


def _retention(proj, batch, seq, col0):
    t = proj.shape[0]
    r = RET_BLOCK
    nsb = seq // r
    cos, sin, dec, qd, kd, cd = _retention_tables(seq)
    hp = RET_HEADS_PER_STEP
    groups = RET_HEADS // hp
    qk_w, v_w = hp * RET_QK_DIM, hp * RET_V_DIM
    assert col0 % qk_w == 0 and (col0 + 2 * RET_QK_W) % v_w == 0
    qk_blk0 = col0 // qk_w
    v_blk0 = (col0 + 2 * RET_QK_W) // v_w
    row = lambda b, h, s: b * nsb + s
    head_tab = lambda shape: pl.BlockSpec((hp,) + shape, lambda b, h, s: (h, 0, 0))
    half = RET_QK_DIM // 2
    return pl.pallas_call(
        _retention_kernel,
        out_shape=jax.ShapeDtypeStruct((t, RET_V_W), BF16),
        grid=(batch, groups, nsb),
        in_specs=[
            pl.BlockSpec((r, qk_w), lambda b, h, s: (row(b, h, s), qk_blk0 + h)),
            pl.BlockSpec((r, qk_w), lambda b, h, s: (row(b, h, s), qk_blk0 + groups + h)),
            pl.BlockSpec((r, v_w), lambda b, h, s: (row(b, h, s), v_blk0 + h)),
            pl.BlockSpec((r, v_w), lambda b, h, s: (row(b, h, s), v_blk0 + groups + h)),
            pl.BlockSpec((r, half), lambda b, h, s: (s, 0)),
            pl.BlockSpec((r, half), lambda b, h, s: (s, 0)),
            head_tab((r, r)),
            head_tab((r, RET_QK_DIM)),
            head_tab((r, RET_QK_DIM)),
            head_tab((1, RET_V_DIM)),
        ],
        out_specs=pl.BlockSpec((r, v_w), lambda b, h, s: (row(b, h, s), h)),
        scratch_shapes=[pltpu.VMEM((hp, RET_QK_DIM, RET_V_DIM), F32)],
        compiler_params=_params(("arbitrary", "arbitrary", "arbitrary")),
        name="retention",
    )(proj, proj, proj, proj, cos, sin, dec, qd, kd, cd)


ATT_KEY_BLOCKS = PREV_CHUNKS * CHUNK // ATT_BLOCK + 1
NEG = float(np.finfo(np.float32).min)


def _attention_kernel(q_ref, k0_ref, k1_ref, k2_ref, v0_ref, v1_ref, v2_ref, tab_ref, gq_ref, gk_ref, o_ref):
    i = pl.program_id(1)

    def rms(t, g):
        tf = t.astype(F32)
        return (tf * lax.rsqrt(jnp.mean(tf * tf, axis=-1, keepdims=True) + NORM_EPS)) * g

    dh = ATT_HEAD_DIM
    k_all = jnp.concatenate([k0_ref[...], k1_ref[...], k2_ref[...]], axis=0)
    v_all = jnp.concatenate([v0_ref[...], v1_ref[...], v2_ref[...]], axis=0)
    outs = []
    for hh in range(ATT_HEADS_PER_STEP):
        lanes = slice(hh * dh, (hh + 1) * dh)
        q = rms(q_ref[:, lanes], gq_ref[...]).astype(BF16)
        k = rms(k_all[:, lanes], gk_ref[...]).astype(BF16)
        sc = lax.dot_general(q, k, (((1,), (1,)), ((), ())), preferred_element_type=F32)
        sc = sc * (ATT_HEAD_DIM ** -0.5) + tab_ref[hh]
        col = lax.broadcasted_iota(jnp.int32, sc.shape, 1)
        sc = jnp.where(col >= (ATT_KEY_BLOCKS - 1 - i) * ATT_BLOCK, sc, NEG)
        p = jnp.exp(sc - jnp.max(sc, axis=-1, keepdims=True))
        denom = jnp.sum(p, axis=-1, keepdims=True)
        outs.append(jnp.dot(p.astype(BF16), v_all[:, lanes], preferred_element_type=F32) / denom)
    o_ref[...] = jnp.concatenate(outs, axis=1).astype(o_ref.dtype)


def _attention_table(rel_bias):
    a = ATT_BLOCK
    w = ATT_KEY_BLOCKS * a
    pad = PREV_CHUNKS * CHUNK
    heads = rel_bias.shape[0]
    period = 1024
    assert a - 1 + w - 1 <= period - 2
    rel_of_i = pad + a - 1 - np.arange(period)
    idx = np.clip(np.minimum(rel_of_i, MAX_REL) + (CHUNK - 1), 0, rel_bias.shape[-1] - 1)
    u = rel_bias[:, idx].astype(F32)
    skew = jnp.tile(u, (1, a))[:, :a * (period - 1)].reshape(heads, a, period - 1)
    bias = skew[:, :, a - 1:a - 1 + w]
    n = np.arange(a)[:, None]
    j = np.arange(w)[None, :]
    qc = n // CHUNK
    kc = j // CHUNK - PREV_CHUNKS
    valid = (kc <= qc) & (kc >= qc - PREV_CHUNKS)
    return jnp.where(jnp.asarray(valid)[None], bias, NEG)


def _chunk_attention(proj, rel_bias, q_norm_g, k_norm_g, batch, seq, col0):
    assert ATT_KEY_BLOCKS == 3
    t = proj.shape[0]
    a = ATT_BLOCK
    nqb = seq // a
    dh = ATT_HEAD_DIM
    hw = ATT_HEADS_PER_STEP * dh
    groups = ATT_HEADS // ATT_HEADS_PER_STEP
    assert col0 % hw == 0 and ATT_W % hw == 0
    blk0 = col0 // hw
    tab = _attention_table(rel_bias)

    def kv_spec(which, jj):
        return pl.BlockSpec(
            (a, hw),
            lambda h, i, b: (b * nqb + jnp.maximum(i - (ATT_KEY_BLOCKS - 1) + jj, 0), blk0 + which * groups + h))

    return pl.pallas_call(
        _attention_kernel,
        out_shape=jax.ShapeDtypeStruct((t, ATT_W), BF16),
        grid=(groups, nqb, batch),
        in_specs=[
            pl.BlockSpec((a, hw), lambda h, i, b: (b * nqb + i, blk0 + h)),
            kv_spec(1, 0), kv_spec(1, 1), kv_spec(1, 2),
            kv_spec(2, 0), kv_spec(2, 1), kv_spec(2, 2),
            pl.BlockSpec((ATT_HEADS_PER_STEP, a, ATT_KEY_BLOCKS * a), lambda h, i, b: (h, 0, 0)),
            pl.BlockSpec((1, dh), lambda h, i, b: (0, 0)),
            pl.BlockSpec((1, dh), lambda h, i, b: (0, 0)),
        ],
        out_specs=pl.BlockSpec((a, hw), lambda h, i, b: (b * nqb + i, h)),
        compiler_params=_params(("arbitrary", "arbitrary", "arbitrary")),
        name="chunk_attention",
    )(proj, proj, proj, proj, proj, proj, proj, tab, q_norm_g.reshape(1, dh), k_norm_g.reshape(1, dh))


def _merge_kernel(r_ref, a_ref, wr_ref, wa_ref, gr_ref, ga_ref, o_ref):
    yr = jnp.dot(r_ref[...], wr_ref[...].astype(BF16), preferred_element_type=F32)
    ya = jnp.dot(a_ref[...], wa_ref[...].astype(BF16), preferred_element_type=F32)
    merged = jax.nn.sigmoid(gr_ref[...].astype(F32)) * yr + jax.nn.sigmoid(ga_ref[...].astype(F32)) * ya
    o_ref[...] = merged.astype(o_ref.dtype)


def _merge(ret, att, w_ret_out, w_att_out, proj, gate_col0, layer):
    t = ret.shape[0]
    d = w_ret_out.shape[2]
    tm, tn = MERGE_TM, MERGE_TN
    gblk = gate_col0 // tn
    return pl.pallas_call(
        _merge_kernel,
        out_shape=jax.ShapeDtypeStruct((t, d), BF16),
        grid=(d // tn, t // tm),
        in_specs=[
            pl.BlockSpec((tm, ret.shape[1]), lambda j, i: (i, 0)),
            pl.BlockSpec((tm, att.shape[1]), lambda j, i: (i, 0)),
            pl.BlockSpec((None, ret.shape[1], tn), lambda j, i: (layer, 0, j)),
            pl.BlockSpec((None, att.shape[1], tn), lambda j, i: (layer, 0, j)),
            pl.BlockSpec((tm, tn), lambda j, i: (i, gblk + j)),
            pl.BlockSpec((tm, tn), lambda j, i: (i, gblk + d // tn + j)),
        ],
        out_specs=pl.BlockSpec((tm, tn), lambda j, i: (i, j)),
        compiler_params=_params(("arbitrary", "arbitrary")),
        name="merge",
    )(ret, att, w_ret_out, w_att_out, proj, proj)


def _out_proj_kernel(a_ref, w_ref, x_ref, gate_ref, o_ref):
    acc = jnp.dot(a_ref[...], w_ref[...].astype(BF16), preferred_element_type=F32)
    o_ref[...] = x_ref[...] + gate_ref[...] * acc


def _out_proj(a, w, x2, mod3, layer, seq, gate_idx):
    t, k = a.shape
    d = w.shape[2]
    tm, tn = OUT_TM, OUT_TN
    return pl.pallas_call(
        _out_proj_kernel,
        out_shape=jax.ShapeDtypeStruct((t, d), F32),
        grid=(d // tn, t // tm),
        in_specs=[
            pl.BlockSpec((tm, k), lambda j, i: (i, 0)),
            pl.BlockSpec((None, k, tn), lambda j, i: (layer, 0, j)),
            pl.BlockSpec((tm, tn), lambda j, i: (i, j)),
            pl.BlockSpec((None, 1, tn), lambda j, i: ((i * tm // seq) * N_MOD + gate_idx, 0, j)),
        ],
        out_specs=pl.BlockSpec((tm, tn), lambda j, i: (i, j)),
        compiler_params=_params(("arbitrary", "arbitrary")),
        name="out_proj",
    )(a, w, x2, mod3)


def _routing(logits):
    t = logits.shape[0]
    bm = EXPERT_BM
    a = t * TOP_K
    top_val, top_idx = lax.top_k(logits, TOP_K)
    gate = jax.nn.softmax(top_val, axis=-1)
    onehot = top_idx[:, :, None] == jnp.arange(N_EXPERTS, dtype=jnp.int32)[None, None, :]
    picked = jnp.any(onehot, axis=1).astype(jnp.int32)
    inclusive = jnp.cumsum(picked, axis=0)
    sizes = inclusive[-1]
    sub = EXPERT_SUB
    padded = (sizes + sub - 1) // sub * sub
    pad_end = jnp.cumsum(padded)
    pad_start = pad_end - padded
    start = jnp.cumsum(sizes) - sizes
    slot = (pad_start[None, :] + inclusive - picked)[:, None, :]
    dest = jnp.sum(jnp.where(onehot, slot, 0), axis=-1).astype(jnp.int32).reshape(a)
    order = jnp.argsort(top_idx.reshape(a))
    n_blocks = a // bm + N_EXPERTS
    group_blocks = (sizes + bm - 1) // bm
    blocks_end = jnp.cumsum(group_blocks)
    n_valid = blocks_end[-1].astype(jnp.int32)
    blk = jnp.arange(n_blocks, dtype=jnp.int32)
    block_e = jnp.minimum(jnp.searchsorted(blocks_end, blk, side='right'), N_EXPERTS - 1).astype(jnp.int32)
    chunk = blk - (blocks_end - group_blocks)[block_e]
    block_valid = blk < n_valid
    block_rows = jnp.where(block_valid, jnp.clip(sizes[block_e] - chunk * bm, 0, bm), 0)
    block_nsub = ((block_rows + sub - 1) // sub).astype(jnp.int32)
    block_row0 = jnp.where(block_valid, pad_start[block_e] + chunk * bm, 0).astype(jnp.int32)
    padded_len = (a // TOK_ALIGN) * TOK_ALIGN + TOK_WINDOW
    sorted_tok = jnp.zeros((padded_len,), jnp.int32).at[:a].set((order // TOP_K).astype(jnp.int32))
    block_first = jnp.where(block_valid, start[block_e] + chunk * bm, 0).astype(jnp.int32)
    tables = (block_e, block_nsub, block_first, block_row0, jnp.stack([n_valid, pad_end[-1].astype(jnp.int32)]))
    return gate, dest, sorted_tok, tables


def _expert_kernel(be_ref, ns_ref, first_ref, row0_ref, info_ref, tok_hbm, h_hbm, w1_hbm, b1_ref, w2_hbm, b2_ref,
                   perm_ref, ys_hbm, w1buf, w2buf, gbuf, x_scr, act_scr, obuf, zbuf, tokbuf,
                   sem_w1, sem_w2, sem_g, sem_o, sem_t, *, layer, nj1, nj2):
    sub = EXPERT_SUB
    group = GATHER_GROUP
    bm, d = x_scr.shape
    half = d // 2
    tf = act_scr.shape[2]
    tn = obuf.shape[2]
    steps = nj1 + nj2
    nb_max = be_ref.shape[0]
    n_valid = info_ref[0]
    total = n_valid * steps
    per_step = bm // EXPERT_ISSUE_STEPS

    def for_rows(nsub, fn):
        assert bm // sub < 8
        for size in (4, 2, 1):
            before = (nsub // (2 * size)) * (2 * size)

            @pl.when((nsub & size) != 0)
            def _():
                fn(pl.multiple_of(before * sub, size * sub), size * sub)

    def w1_copy(e, j, slot):
        cols = pl.ds(pl.multiple_of(j * 2 * tf, 2 * tf), 2 * tf)
        return pltpu.make_async_copy(w1_hbm.at[layer, e, :, cols], w1buf.at[slot], sem_w1.at[slot])

    def w2_copies(e, n, slot):
        return [pltpu.make_async_copy(w2_hbm.at[layer, e, :, pl.ds(pl.multiple_of(c0, tn), tn)],
                                      w2buf.at[slot, :, pl.ds(k * tn, tn)], sem_w2.at[slot])
                for k, c0 in enumerate((n * tn, half + n * tn))]

    def tok_copy(blk, slot):
        src = tok_hbm.at[pl.ds(pl.multiple_of(first_ref[blk] // TOK_ALIGN * TOK_ALIGN, TOK_ALIGN), TOK_WINDOW)]
        dst = tokbuf.at[pl.ds(pl.multiple_of(slot * TOK_WINDOW, TOK_ALIGN), TOK_WINDOW)]
        return pltpu.make_async_copy(src, dst, sem_t.at[slot])

    def tok_base(blk, slot):
        return slot * TOK_WINDOW + lax.rem(first_ref[blk], TOK_ALIGN)

    def row_copy(tok_at, tile_row, sublane):
        src = h_hbm.at[pl.ds(tokbuf[tok_at], 1)]
        return pltpu.make_async_copy(src, gbuf.at[tile_row, pl.ds(sublane, 1)], sem_g)

    def out_copy(slot, r0, m, row0, n):
        dst = ys_hbm.at[pl.ds(pl.multiple_of(row0 + r0, sub), m), pl.ds(pl.multiple_of(n * tn, tn), tn)]
        return pltpu.make_async_copy(obuf.at[slot, pl.ds(r0, m)], dst, sem_o.at[slot])

    @pl.when(n_valid > 0)
    def _():
        tok_copy(0, 0).start()
        tok_copy(0, 0).wait()
        base = tok_base(0, 0)

        def start(q, carry):
            for s in range(SUBLANES):
                row_copy(base + q * SUBLANES + s, q, s).start()
            return carry
        lax.fori_loop(0, ns_ref[0] * (sub // SUBLANES), start, 0)
        w1_copy(be_ref[0], 0, 0).start()

    def item(it, carry):
        b = it // steps
        j = it - b * steps
        e = be_ref[b]
        nsub = ns_ref[b]
        has_next = b + 1 < n_valid
        nxt = jnp.minimum(b + 1, nb_max - 1)

        nit = it + 1
        nbk = jnp.minimum(nit // steps, nb_max - 1)
        nj = nit - (nit // steps) * steps

        @pl.when((nit < total) & (nj < nj1))
        def _():
            w1_copy(be_ref[nbk], nj, lax.rem(nj, 2)).start()

        @pl.when((nit < total) & (nj >= nj1))
        def _():
            for c in w2_copies(be_ref[nbk], nj - nj1, lax.rem(nj - nj1, 2)):
                c.start()

        @pl.when(j < nj1)
        def _():
            w1_copy(e, j, lax.rem(j, 2)).wait()

        @pl.when(j >= nj1)
        def _():
            for c in w2_copies(e, j - nj1, lax.rem(j - nj1, 2)):
                c.wait()

        @pl.when(j == 0)
        def _():
            def wait_rows(s, carry):
                pltpu.make_async_copy(h_hbm.at[pl.ds(0, sub)], zbuf, sem_g).wait()
                return carry
            lax.fori_loop(0, nsub, wait_rows, 0)

            def unpack_rows(s, carry):
                rows = pl.ds(pl.multiple_of(s * sub, sub), sub)
                tiles = pl.ds(pl.multiple_of(s * (sub // SUBLANES), sub // SUBLANES), sub // SUBLANES)
                xa, xb = _unpack_halves(gbuf[tiles].reshape(sub, half))
                x_scr[rows, pl.ds(0, half)] = xa.astype(BF16)
                x_scr[rows, pl.ds(half, half)] = xb.astype(BF16)
                return carry
            lax.fori_loop(0, nsub, unpack_rows, 0)

            @pl.when(has_next)
            def _():
                tok_copy(nxt, lax.rem(b + 1, 2)).start()

        @pl.when((j == 1) & has_next)
        def _():
            tok_copy(nxt, lax.rem(b + 1, 2)).wait()

        @pl.when((j >= 1) & (j <= EXPERT_ISSUE_STEPS) & has_next)
        def _():
            base = tok_base(nxt, lax.rem(b + 1, 2))
            rows_next = ns_ref[nxt] * sub
            for g in range(per_step // group):
                first_row = (j - 1) * per_step + g * group
                first_tile = (j - 1) * (per_step // SUBLANES) + g * (group // SUBLANES)

                @pl.when(first_row < rows_next)
                def _():
                    for i in range(group):
                        row_copy(base + first_row + i, first_tile + i // SUBLANES, i % SUBLANES).start()

        @pl.when(j < nj1)
        def _():
            wslot = lax.rem(j, 2)
            bias = b1_ref[pl.ds(e * nj1 + j, 1), :]

            def first_matmul(r0, m):
                rows = pl.ds(r0, m)
                hb = jnp.dot(x_scr[rows, :], w1buf[wslot].astype(BF16), preferred_element_type=F32) + bias
                hb = hb.astype(BF16)
                gw = perm_ref.shape[0]
                sel = [jnp.dot(hb[:, c * gw:(c + 1) * gw], perm_ref[...], preferred_element_type=F32)
                       for c in range(2 * tf // gw)]
                x_glu = jnp.minimum(jnp.concatenate([s[:, :gw // 2] for s in sel], axis=1), SWIGLU_LIMIT)
                x_lin = jnp.clip(jnp.concatenate([s[:, gw // 2:] for s in sel], axis=1),
                                 -SWIGLU_LIMIT, SWIGLU_LIMIT)
                act = x_glu * jax.nn.sigmoid(SWIGLU_ALPHA * x_glu) * (x_lin + 1.0)
                act_scr[j, rows, :] = act.astype(BF16)
            for_rows(nsub, first_matmul)

        @pl.when(j >= nj1)
        def _():
            n = j - nj1
            slot = lax.rem(n, 2)
            row0 = row0_ref[b]
            bias = jnp.concatenate([b2_ref[pl.ds(e * 2 * nj2 + n, 1), :],
                                    b2_ref[pl.ds(e * 2 * nj2 + nj2 + n, 1), :]], axis=1)
            sent = jnp.where(n >= 2, nsub, jnp.where(b > 0, ns_ref[jnp.maximum(b - 1, 0)], 0))
            for_rows(sent, lambda r0, m: out_copy(slot, r0, m, 0, 0).wait())

            def second_matmul(r0, m):
                rows = pl.ds(r0, m)
                act = jnp.concatenate([act_scr[t, rows, :] for t in range(nj1)], axis=1)
                y = jnp.dot(act, w2buf[slot].astype(BF16), preferred_element_type=F32) + bias
                obuf[slot, rows, :] = _pack_halves(y)
                out_copy(slot, r0, m, row0, n).start()
            for_rows(nsub, second_matmul)

        return carry

    lax.fori_loop(0, total, item, 0)

    @pl.when(n_valid > 0)
    def _():
        last = ns_ref[jnp.maximum(n_valid - 1, 0)]
        for slot in (0, 1):
            for_rows(last, lambda r0, m: out_copy(slot, r0, m, 0, 0).wait())

    used = info_ref[1]
    n_tail = (ys_hbm.shape[0] - used) // sub
    zbuf[...] = jnp.zeros_like(zbuf)

    def tail_copy(i):
        dst = ys_hbm.at[pl.ds(pl.multiple_of(used + i * sub, sub), sub)]
        return pltpu.make_async_copy(zbuf, dst, sem_g)

    def tail_start(i, carry):
        tail_copy(i).start()
        return carry

    def tail_wait(i, carry):
        tail_copy(i).wait()
        return carry
    lax.fori_loop(0, n_tail, tail_start, 0)
    lax.fori_loop(0, n_tail, tail_wait, 0)


def _deinterleave_matrix(tf):
    p = np.zeros((2 * tf, 2 * tf), np.float32)
    f = np.arange(tf)
    p[2 * f, f] = 1.0
    p[2 * f + 1, tf + f] = 1.0
    return jnp.asarray(p, BF16)


def _experts(h_packed, sorted_tok, tables, w1, b1, w2, b2, layer):
    t, half = h_packed.shape
    d = 2 * half
    n_experts, f = w2.shape[1], w2.shape[2]
    bm, tf, tn = EXPERT_BM, EXPERT_TF, EXPERT_TN
    nj1 = f // tf
    nj2 = half // tn
    assert nj1 % 2 == 0 and nj2 % 2 == 0
    assert 1 + EXPERT_ISSUE_STEPS <= nj1 + nj2 and bm % (EXPERT_ISSUE_STEPS * GATHER_GROUP) == 0
    assert bm % EXPERT_SUB == 0 and t >= EXPERT_SUB and GATHER_GROUP % SUBLANES == 0
    assert bm + TOK_ALIGN - 1 <= TOK_WINDOW
    p_rows = t * TOP_K + n_experts * EXPERT_SUB
    whole = lambda shape: pl.BlockSpec(shape, lambda i, *_: (0,) * len(shape))
    hbm = pl.BlockSpec(memory_space=pl.ANY)
    return pl.pallas_call(
        functools.partial(_expert_kernel, layer=layer, nj1=nj1, nj2=nj2),
        out_shape=jax.ShapeDtypeStruct((p_rows, half), U32),
        grid_spec=pltpu.PrefetchScalarGridSpec(
            num_scalar_prefetch=5,
            grid=(1,),
            in_specs=[hbm, hbm, hbm, whole((n_experts * nj1, 2 * tf)), hbm, whole((n_experts * 2 * nj2, tn)),
                      whole((EXPERT_PERM_W, EXPERT_PERM_W))],
            out_specs=hbm,
            scratch_shapes=[
                pltpu.VMEM((2, d, 2 * tf), F32),
                pltpu.VMEM((2, f, 2 * tn), F32),
                pltpu.VMEM((bm // SUBLANES, SUBLANES, half), U32),
                pltpu.VMEM((bm, d), BF16),
                pltpu.VMEM((nj1, bm, tf), BF16),
                pltpu.VMEM((2, bm, tn), U32),
                pltpu.VMEM((EXPERT_SUB, half), U32),
                pltpu.SMEM((2 * TOK_WINDOW,), jnp.int32),
                pltpu.SemaphoreType.DMA((2,)),
                pltpu.SemaphoreType.DMA((2,)),
                pltpu.SemaphoreType.DMA(()),
                pltpu.SemaphoreType.DMA((2,)),
                pltpu.SemaphoreType.DMA((2,)),
            ],
        ),
        compiler_params=_params(("arbitrary",)),
        name="experts",
    )(*tables, sorted_tok, h_packed, w1, b1[layer].reshape(n_experts * nj1, 2 * tf), w2,
      b2[layer].reshape(n_experts * 2 * nj2, tn), _deinterleave_matrix(EXPERT_PERM_W // 2))


def _combine_kernel(dest_ref, ys_hbm, x_ref, gate_ref, g_ref, o_ref, buf, sem):
    tm = x_ref.shape[0]
    half = buf.shape[2]

    def row_copy(src_row, k, r):
        return pltpu.make_async_copy(ys_hbm.at[pl.ds(src_row, 1)], buf.at[k, pl.ds(r, 1)], sem)

    def start(r, carry):
        for k in range(TOP_K):
            row_copy(dest_ref[0, r * TOP_K + k], k, r).start()
        return carry

    def wait(r, carry):
        for k in range(TOP_K):
            row_copy(0, k, r).wait()
        return carry

    lax.fori_loop(0, tm, start, 0)
    lax.fori_loop(0, tm, wait, 0)
    sublanes = SUBLANES

    def chunk(c, carry):
        rows = pl.ds(pl.multiple_of(c * sublanes, sublanes), sublanes)
        g = g_ref[rows, :]
        ya = jnp.zeros((sublanes, half), F32)
        yb = jnp.zeros((sublanes, half), F32)
        for k in range(TOP_K):
            a, b = _unpack_halves(buf[k, rows, :])
            ya = ya + g[:, k:k + 1] * a
            yb = yb + g[:, k:k + 1] * b
        o_ref[rows, pl.ds(0, half)] = x_ref[rows, pl.ds(0, half)] + gate_ref[:, pl.ds(0, half)] * ya
        o_ref[rows, pl.ds(half, half)] = x_ref[rows, pl.ds(half, half)] + gate_ref[:, pl.ds(half, half)] * yb
        return carry

    lax.fori_loop(0, tm // sublanes, chunk, 0)


def _combine(ys, dest, gates, x2, mod3, seq, gate_idx):
    t, d = x2.shape
    tm = COMBINE_TM
    half = ys.shape[1]
    return pl.pallas_call(
        _combine_kernel,
        out_shape=jax.ShapeDtypeStruct((t, d), F32),
        grid=(t // tm,),
        in_specs=[
            pl.BlockSpec((None, 1, tm * TOP_K), lambda i: (i, 0, 0), memory_space=pltpu.SMEM),
            pl.BlockSpec(memory_space=pl.ANY),
            pl.BlockSpec((tm, d), lambda i: (i, 0)),
            pl.BlockSpec((None, 1, d), lambda i: ((i * tm // seq) * N_MOD + gate_idx, 0, 0)),
            pl.BlockSpec((tm, TOP_K), lambda i: (i, 0)),
        ],
        out_specs=pl.BlockSpec((tm, d), lambda i: (i, 0)),
        scratch_shapes=[pltpu.VMEM((TOP_K, tm, half), U32), pltpu.SemaphoreType.DMA(())],
        compiler_params=_params(("arbitrary",)),
        name="combine",
    )(dest.reshape(t // tm, 1, tm * TOP_K), ys, x2, mod3, gates)


def kernel(x, c, w_ada, b_ada, norm1_g, w_in, q_norm_g, k_norm_g, rel_bias, w_ret_out, w_att_out, w_out,
           norm2_g, w_router, b_router, w1, b1, w2, b2):
    batch, seq, d = x.shape
    depth = w_ada.shape[0]
    t = batch * seq
    assert batch <= SUBLANES and t % PROJ_TM == 0
    assert seq % max(RET_BLOCK, ATT_BLOCK, NORM_TM, MERGE_TM, OUT_TM, COMBINE_TM) == 0
    x2 = x.reshape(t, d)
    c8 = jnp.zeros((SUBLANES, d), F32).at[:batch].set(c)
    ret_col0 = 0
    att_col0 = 2 * RET_QK_W + 2 * RET_V_W
    gate_col0 = att_col0 + 3 * ATT_W
    for layer in range(depth):
        mod3 = _ada_ln(c8, w_ada, b_ada, layer)[:batch].reshape(batch * N_MOD, 1, d)
        h = _norm_mod(x2, norm1_g.reshape(depth, 1, d), mod3, layer, seq, shift_idx=0, scale_idx=1)
        proj = _project(h, w_in, layer)
        ret = _retention(proj, batch, seq, ret_col0)
        att = _chunk_attention(proj, rel_bias[layer], q_norm_g[layer], k_norm_g[layer], batch, seq, att_col0)
        merged = _merge(ret, att, w_ret_out, w_att_out, proj, gate_col0, layer)
        x2 = _out_proj(merged, w_out, x2, mod3, layer, seq, gate_idx=2)
        h_packed, logits = _norm_router(x2, norm2_g.reshape(depth, 1, d), mod3, w_router, b_router, layer, seq,
                                        shift_idx=3, scale_idx=4)
        gates, dest, sorted_tok, tables = _routing(logits)
        ys = _experts(h_packed, sorted_tok, tables, w1, b1, w2, b2, layer)
        x2 = _combine(ys, dest, gates, x2, mod3, seq, gate_idx=5)
    return x2.reshape(batch, seq, d)
```

```python
import functools

import numpy as np
import jax
import jax.numpy as jnp
from jax import lax
from jax.experimental import pallas as pl
from jax.experimental.pallas import tpu as pltpu

F32 = jnp.float32
BF16 = jnp.bfloat16
U32 = jnp.uint32

CHUNK = 64
NORM_EPS = 1e-6
RET_HEADS = 8
RET_QK_DIM = 256
RET_V_DIM = 512
ROPE_BASE = 10000.0
ATT_HEADS = 16
ATT_HEAD_DIM = 128
PREV_CHUNKS = 8
MAX_REL = 128
N_EXPERTS = 32
TOP_K = 4
SWIGLU_ALPHA = 1.702
SWIGLU_LIMIT = 7.0
N_MOD = 6

RET_QK_W = RET_HEADS * RET_QK_DIM
RET_V_W = RET_HEADS * RET_V_DIM
ATT_W = ATT_HEADS * ATT_HEAD_DIM

V7X_VMEM_LIMIT_BYTES = 58 * 1024 * 1024
HI16 = 0xFFFF0000

ADA_TN = 1024
NORM_TM = 512
PROJ_TM, PROJ_TN = 2048, 512
RET_BLOCK = 256
RET_HEADS_PER_STEP = 4
ATT_BLOCK = 256
ATT_HEADS_PER_STEP = 4
MERGE_TM, MERGE_TN = 1024, 512
OUT_TM, OUT_TN = 1024, 512
EXPERT_BM = 1280
EXPERT_SUB = 256
EXPERT_TF = 256
EXPERT_TN = 256
EXPERT_PERM_W = 256
EXPERT_ISSUE_STEPS = 10
GATHER_GROUP = 32
TOK_ALIGN = 1024
TOK_WINDOW = 3 * TOK_ALIGN
SUBLANES = 8
COMBINE_TM = 256


def _params(semantics):
    return pltpu.CompilerParams(dimension_semantics=semantics, vmem_limit_bytes=V7X_VMEM_LIMIT_BYTES)


def _silu(t):
    return t * jax.nn.sigmoid(t)


def _pack_halves(y):
    n = y.shape[1] // 2
    hi = lax.bitcast_convert_type(y[:, :n].astype(BF16).astype(F32), U32)
    lo = lax.bitcast_convert_type(y[:, n:].astype(BF16).astype(F32), U32)
    return hi | (lo >> 16)


def _unpack_halves(p):
    a = lax.bitcast_convert_type(p & jnp.uint32(HI16), F32)
    b = lax.bitcast_convert_type(p << 16, F32)
    return a, b


def _ada_kernel(c_ref, w_ref, b_ref, o_ref):
    a = _silu(c_ref[...]).astype(BF16)
    o_ref[...] = jnp.dot(a, w_ref[...].astype(BF16), preferred_element_type=F32) + b_ref[...]


def _ada_ln(c8, w_ada, b_ada, layer):
    rows, d = c8.shape
    n = w_ada.shape[2]
    return pl.pallas_call(
        _ada_kernel,
        out_shape=jax.ShapeDtypeStruct((rows, n), F32),
        grid=(n // ADA_TN,),
        in_specs=[
            pl.BlockSpec((rows, d), lambda j: (0, 0)),
            pl.BlockSpec((None, d, ADA_TN), lambda j: (layer, 0, j)),
            pl.BlockSpec((None, 1, ADA_TN), lambda j: (layer, 0, j)),
        ],
        out_specs=pl.BlockSpec((rows, ADA_TN), lambda j: (0, j)),
        compiler_params=_params(("arbitrary",)),
        name="ada_ln",
    )(c8, w_ada, b_ada.reshape(b_ada.shape[0], 1, n))


def _modulated_norm(x, g, scale, shift):
    xn = x * lax.rsqrt(jnp.mean(x * x, axis=-1, keepdims=True) + NORM_EPS)
    return (xn * g) * (1.0 + scale) + shift


def _norm_mod_kernel(x_ref, g_ref, sc_ref, sh_ref, o_ref):
    o_ref[...] = _modulated_norm(x_ref[...], g_ref[...], sc_ref[...], sh_ref[...]).astype(o_ref.dtype)


def _norm_mod(x2, g, mod3, layer, seq, shift_idx, scale_idx):
    t, d = x2.shape
    tm = NORM_TM

    def mod_spec(idx):
        return pl.BlockSpec((None, 1, d), lambda i: ((i * tm // seq) * N_MOD + idx, 0, 0))

    return pl.pallas_call(
        _norm_mod_kernel,
        out_shape=jax.ShapeDtypeStruct((t, d), BF16),
        grid=(t // tm,),
        in_specs=[
            pl.BlockSpec((tm, d), lambda i: (i, 0)),
            pl.BlockSpec((None, 1, d), lambda i: (layer, 0, 0)),
            mod_spec(scale_idx),
            mod_spec(shift_idx),
        ],
        out_specs=pl.BlockSpec((tm, d), lambda i: (i, 0)),
        compiler_params=_params(("arbitrary",)),
        name="norm_mod",
    )(x2, g, mod3, mod3)


def _norm_router_kernel(x_ref, g_ref, sc_ref, sh_ref, wr_ref, br_ref, h_ref, lg_ref):
    h = _modulated_norm(x_ref[...], g_ref[...], sc_ref[...], sh_ref[...])
    h_ref[...] = _pack_halves(h)
    h_hi = h.astype(BF16)
    h_lo = (h - h_hi.astype(F32)).astype(BF16)
    w = wr_ref[...]
    w_hi = w.astype(BF16)
    w_lo = (w - w_hi.astype(F32)).astype(BF16)
    lg = (jnp.dot(h_hi, w_hi, preferred_element_type=F32)
          + jnp.dot(h_lo, w_hi, preferred_element_type=F32)
          + jnp.dot(h_hi, w_lo, preferred_element_type=F32))
    lg_ref[...] = lg + br_ref[...]


def _norm_router(x2, g, mod3, w_router, b_router, layer, seq, shift_idx, scale_idx):
    t, d = x2.shape
    e = w_router.shape[2]
    tm = NORM_TM

    def mod_spec(idx):
        return pl.BlockSpec((None, 1, d), lambda i: ((i * tm // seq) * N_MOD + idx, 0, 0))

    return pl.pallas_call(
        _norm_router_kernel,
        out_shape=(jax.ShapeDtypeStruct((t, d // 2), U32), jax.ShapeDtypeStruct((t, e), F32)),
        grid=(t // tm,),
        in_specs=[
            pl.BlockSpec((tm, d), lambda i: (i, 0)),
            pl.BlockSpec((None, 1, d), lambda i: (layer, 0, 0)),
            mod_spec(scale_idx),
            mod_spec(shift_idx),
            pl.BlockSpec((None, d, e), lambda i: (layer, 0, 0)),
            pl.BlockSpec((None, 1, e), lambda i: (layer, 0, 0)),
        ],
        out_specs=(pl.BlockSpec((tm, d // 2), lambda i: (i, 0)), pl.BlockSpec((tm, e), lambda i: (i, 0))),
        compiler_params=_params(("arbitrary",)),
        name="norm_router",
    )(x2, g, mod3, mod3, w_router, b_router.reshape(b_router.shape[0], 1, e))


def _proj_kernel(a_ref, w_ref, o_ref):
    o_ref[...] = jnp.dot(a_ref[...], w_ref[...].astype(BF16), preferred_element_type=F32).astype(o_ref.dtype)


def _project(a, w, layer):
    t, k = a.shape
    n = w.shape[2]
    tm, tn = PROJ_TM, PROJ_TN
    return pl.pallas_call(
        _proj_kernel,
        out_shape=jax.ShapeDtypeStruct((t, n), BF16),
        grid=(n // tn, t // tm),
        in_specs=[
            pl.BlockSpec((tm, k), lambda j, i: (i, 0)),
            pl.BlockSpec((None, k, tn), lambda j, i: (layer, 0, j)),
        ],
        out_specs=pl.BlockSpec((tm, tn), lambda j, i: (i, j)),
        compiler_params=_params(("arbitrary", "arbitrary")),
        name="in_proj",
    )(a, w)


def _retention_kernel(q_ref, k_ref, v_ref, g_ref, cos_ref, sin_ref, dec_ref, qd_ref, kd_ref, cd_ref,
                      o_ref, state_ref):
    @pl.when(pl.program_id(2) == 0)
    def _():
        state_ref[...] = jnp.zeros_like(state_ref)

    cos = cos_ref[...]
    sin = sin_ref[...]
    half = RET_QK_DIM // 2
    contract_last = (((1,), (1,)), ((), ()))
    contract_rows = (((0,), (0,)), ((), ()))

    def rotary(t):
        t1, t2 = t[:, :half], t[:, half:]
        return jnp.concatenate([t1 * cos - t2 * sin, t1 * sin + t2 * cos], axis=1)

    for hh in range(RET_HEADS_PER_STEP):
        qk = slice(hh * RET_QK_DIM, (hh + 1) * RET_QK_DIM)
        vg = slice(hh * RET_V_DIM, (hh + 1) * RET_V_DIM)
        q = rotary(q_ref[:, qk].astype(F32))
        k = rotary(k_ref[:, qk].astype(F32)) * (RET_QK_DIM ** -0.5)
        v = v_ref[:, vg]
        scores = lax.dot_general(q.astype(BF16), k.astype(BF16), contract_last,
                                 preferred_element_type=F32) * dec_ref[hh]
        state = state_ref[hh]
        out = (jnp.dot(scores.astype(BF16), v, preferred_element_type=F32)
               + jnp.dot((q * qd_ref[hh]).astype(BF16), state.astype(BF16), preferred_element_type=F32))
        state_ref[hh] = state * cd_ref[hh] + lax.dot_general(
            (k * kd_ref[hh]).astype(BF16), v, contract_rows, preferred_element_type=F32)
        on = out * lax.rsqrt(jnp.mean(out * out, axis=-1, keepdims=True) + NORM_EPS)
        o_ref[:, vg] = (_silu(g_ref[:, vg].astype(F32)) * on).astype(o_ref.dtype)


def _retention_tables(seq):
    half = RET_QK_DIM // 2
    inv_freq = ROPE_BASE ** (-np.arange(half, dtype=np.float64) / half)
    ang = np.arange(seq, dtype=np.float64)[:, None] * inv_freq[None, :]
    log_gamma = np.log(1.0 - 2.0 ** (-5.0 - np.arange(RET_HEADS, dtype=np.float64)))[:, None, None]
    pos = np.arange(RET_BLOCK, dtype=np.float64)
    n, m = pos[:, None], pos[None, :]
    cn, cm = np.floor(n / CHUNK), np.floor(m / CHUNK)
    dist = np.where(cn == cm, np.abs(n - m), n - m)
    dec = np.where((cm <= cn)[None], np.exp(log_gamma * dist[None]), 0.0)
    qd = np.broadcast_to(np.exp(log_gamma * (pos[None, :, None] + 1.0)), (RET_HEADS, RET_BLOCK, RET_QK_DIM))
    kd = np.broadcast_to(np.exp(log_gamma * (RET_BLOCK - 1.0 - pos[None, :, None])),
                         (RET_HEADS, RET_BLOCK, RET_QK_DIM))
    cd = np.broadcast_to(np.exp(log_gamma * RET_BLOCK), (RET_HEADS, 1, RET_V_DIM))
    return tuple(jnp.asarray(np.ascontiguousarray(v), F32) for v in (np.cos(ang), np.sin(ang), dec, qd, kd, cd))


def _retention(proj, batch, seq, col0):
    t = proj.shape[0]
    r = RET_BLOCK
    nsb = seq // r
    cos, sin, dec, qd, kd, cd = _retention_tables(seq)
    hp = RET_HEADS_PER_STEP
    groups = RET_HEADS // hp
    qk_w, v_w = hp * RET_QK_DIM, hp * RET_V_DIM
    assert col0 % qk_w == 0 and (col0 + 2 * RET_QK_W) % v_w == 0
    qk_blk0 = col0 // qk_w
    v_blk0 = (col0 + 2 * RET_QK_W) // v_w
    row = lambda b, h, s: b * nsb + s
    head_tab = lambda shape: pl.BlockSpec((hp,) + shape, lambda b, h, s: (h, 0, 0))
    half = RET_QK_DIM // 2
    return pl.pallas_call(
        _retention_kernel,
        out_shape=jax.ShapeDtypeStruct((t, RET_V_W), BF16),
        grid=(batch, groups, nsb),
        in_specs=[
            pl.BlockSpec((r, qk_w), lambda b, h, s: (row(b, h, s), qk_blk0 + h)),
            pl.BlockSpec((r, qk_w), lambda b, h, s: (row(b, h, s), qk_blk0 + groups + h)),
            pl.BlockSpec((r, v_w), lambda b, h, s: (row(b, h, s), v_blk0 + h)),
            pl.BlockSpec((r, v_w), lambda b, h, s: (row(b, h, s), v_blk0 + groups + h)),
            pl.BlockSpec((r, half), lambda b, h, s: (s, 0)),
            pl.BlockSpec((r, half), lambda b, h, s: (s, 0)),
            head_tab((r, r)),
            head_tab((r, RET_QK_DIM)),
            head_tab((r, RET_QK_DIM)),
            head_tab((1, RET_V_DIM)),
        ],
        out_specs=pl.BlockSpec((r, v_w), lambda b, h, s: (row(b, h, s), h)),
        scratch_shapes=[pltpu.VMEM((hp, RET_QK_DIM, RET_V_DIM), F32)],
        compiler_params=_params(("arbitrary", "arbitrary", "arbitrary")),
        name="retention",
    )(proj, proj, proj, proj, cos, sin, dec, qd, kd, cd)


ATT_KEY_BLOCKS = PREV_CHUNKS * CHUNK // ATT_BLOCK + 1
NEG = float(np.finfo(np.float32).min)


def _attention_kernel(q_ref, k0_ref, k1_ref, k2_ref, v0_ref, v1_ref, v2_ref, tab_ref, gq_ref, gk_ref, o_ref):
    i = pl.program_id(1)

    def rms(t, g):
        tf = t.astype(F32)
        return (tf * lax.rsqrt(jnp.mean(tf * tf, axis=-1, keepdims=True) + NORM_EPS)) * g

    dh = ATT_HEAD_DIM
    k_all = jnp.concatenate([k0_ref[...], k1_ref[...], k2_ref[...]], axis=0)
    v_all = jnp.concatenate([v0_ref[...], v1_ref[...], v2_ref[...]], axis=0)
    outs = []
    for hh in range(ATT_HEADS_PER_STEP):
        lanes = slice(hh * dh, (hh + 1) * dh)
        q = rms(q_ref[:, lanes], gq_ref[...]).astype(BF16)
        k = rms(k_all[:, lanes], gk_ref[...]).astype(BF16)
        sc = lax.dot_general(q, k, (((1,), (1,)), ((), ())), preferred_element_type=F32)
        sc = sc * (ATT_HEAD_DIM ** -0.5) + tab_ref[hh]
        col = lax.broadcasted_iota(jnp.int32, sc.shape, 1)
        sc = jnp.where(col >= (ATT_KEY_BLOCKS - 1 - i) * ATT_BLOCK, sc, NEG)
        p = jnp.exp(sc - jnp.max(sc, axis=-1, keepdims=True))
        denom = jnp.sum(p, axis=-1, keepdims=True)
        outs.append(jnp.dot(p.astype(BF16), v_all[:, lanes], preferred_element_type=F32) / denom)
    o_ref[...] = jnp.concatenate(outs, axis=1).astype(o_ref.dtype)


def _attention_table(rel_bias):
    a = ATT_BLOCK
    w = ATT_KEY_BLOCKS * a
    pad = PREV_CHUNKS * CHUNK
    heads = rel_bias.shape[0]
    period = 1024
    assert w + a - 1 <= period
    i = np.arange(period)
    rel_of_i = np.where(i < w, pad - i, pad + period - i)
    idx = np.clip(np.minimum(rel_of_i, MAX_REL) + (CHUNK - 1), 0, rel_bias.shape[-1] - 1)
    u = rel_bias[:, idx].astype(F32).reshape(heads, 1, period)
    return pl.pallas_call(
        _attention_table_kernel,
        out_shape=jax.ShapeDtypeStruct((heads, a, w), F32),
        grid=(heads,),
        in_specs=[pl.BlockSpec((None, 1, period), lambda h: (h, 0, 0))],
        out_specs=pl.BlockSpec((None, a, w), lambda h: (h, 0, 0)),
        compiler_params=_params(("arbitrary",)),
        name="attention_table",
    )(u)


def _attention_table_kernel(u_ref, o_ref):
    a, w = o_ref.shape
    rows = jnp.broadcast_to(u_ref[...], (a, u_ref.shape[1]))
    skew = pltpu.roll(rows, 0, 1, stride=1, stride_axis=0)
    chunk_bits = CHUNK.bit_length() - 1
    assert CHUNK == 1 << chunk_bits
    qc = lax.broadcasted_iota(jnp.int32, (a, w), 0) >> chunk_bits
    kc = (lax.broadcasted_iota(jnp.int32, (a, w), 1) >> chunk_bits) - PREV_CHUNKS
    valid = (kc <= qc) & (kc >= qc - PREV_CHUNKS)
    o_ref[...] = jnp.where(valid, skew[:, :w], NEG)


def _chunk_attention(proj, rel_bias, q_norm_g, k_norm_g, batch, seq, col0):
    assert ATT_KEY_BLOCKS == 3
    t = proj.shape[0]
    a = ATT_BLOCK
    nqb = seq // a
    dh = ATT_HEAD_DIM
    hw = ATT_HEADS_PER_STEP * dh
    groups = ATT_HEADS // ATT_HEADS_PER_STEP
    assert col0 % hw == 0 and ATT_W % hw == 0
    blk0 = col0 // hw
    tab = _attention_table(rel_bias)

    def kv_spec(which, jj):
        return pl.BlockSpec(
            (a, hw),
            lambda h, i, b: (b * nqb + jnp.maximum(i - (ATT_KEY_BLOCKS - 1) + jj, 0), blk0 + which * groups + h))

    return pl.pallas_call(
        _attention_kernel,
        out_shape=jax.ShapeDtypeStruct((t, ATT_W), BF16),
        grid=(groups, nqb, batch),
        in_specs=[
            pl.BlockSpec((a, hw), lambda h, i, b: (b * nqb + i, blk0 + h)),
            kv_spec(1, 0), kv_spec(1, 1), kv_spec(1, 2),
            kv_spec(2, 0), kv_spec(2, 1), kv_spec(2, 2),
            pl.BlockSpec((ATT_HEADS_PER_STEP, a, ATT_KEY_BLOCKS * a), lambda h, i, b: (h, 0, 0)),
            pl.BlockSpec((1, dh), lambda h, i, b: (0, 0)),
            pl.BlockSpec((1, dh), lambda h, i, b: (0, 0)),
        ],
        out_specs=pl.BlockSpec((a, hw), lambda h, i, b: (b * nqb + i, h)),
        compiler_params=_params(("arbitrary", "arbitrary", "arbitrary")),
        name="chunk_attention",
    )(proj, proj, proj, proj, proj, proj, proj, tab, q_norm_g.reshape(1, dh), k_norm_g.reshape(1, dh))


def _merge_kernel(r_ref, a_ref, wr_ref, wa_ref, gr_ref, ga_ref, o_ref):
    yr = jnp.dot(r_ref[...], wr_ref[...].astype(BF16), preferred_element_type=F32)
    ya = jnp.dot(a_ref[...], wa_ref[...].astype(BF16), preferred_element_type=F32)
    merged = jax.nn.sigmoid(gr_ref[...].astype(F32)) * yr + jax.nn.sigmoid(ga_ref[...].astype(F32)) * ya
    o_ref[...] = merged.astype(o_ref.dtype)


def _merge(ret, att, w_ret_out, w_att_out, proj, gate_col0, layer):
    t = ret.shape[0]
    d = w_ret_out.shape[2]
    tm, tn = MERGE_TM, MERGE_TN
    gblk = gate_col0 // tn
    return pl.pallas_call(
        _merge_kernel,
        out_shape=jax.ShapeDtypeStruct((t, d), BF16),
        grid=(d // tn, t // tm),
        in_specs=[
            pl.BlockSpec((tm, ret.shape[1]), lambda j, i: (i, 0)),
            pl.BlockSpec((tm, att.shape[1]), lambda j, i: (i, 0)),
            pl.BlockSpec((None, ret.shape[1], tn), lambda j, i: (layer, 0, j)),
            pl.BlockSpec((None, att.shape[1], tn), lambda j, i: (layer, 0, j)),
            pl.BlockSpec((tm, tn), lambda j, i: (i, gblk + j)),
            pl.BlockSpec((tm, tn), lambda j, i: (i, gblk + d // tn + j)),
        ],
        out_specs=pl.BlockSpec((tm, tn), lambda j, i: (i, j)),
        compiler_params=_params(("arbitrary", "arbitrary")),
        name="merge",
    )(ret, att, w_ret_out, w_att_out, proj, proj)


def _out_proj_kernel(a_ref, w_ref, x_ref, gate_ref, o_ref):
    acc = jnp.dot(a_ref[...], w_ref[...].astype(BF16), preferred_element_type=F32)
    o_ref[...] = x_ref[...] + gate_ref[...] * acc


def _out_proj(a, w, x2, mod3, layer, seq, gate_idx):
    t, k = a.shape
    d = w.shape[2]
    tm, tn = OUT_TM, OUT_TN
    return pl.pallas_call(
        _out_proj_kernel,
        out_shape=jax.ShapeDtypeStruct((t, d), F32),
        grid=(d // tn, t // tm),
        in_specs=[
            pl.BlockSpec((tm, k), lambda j, i: (i, 0)),
            pl.BlockSpec((None, k, tn), lambda j, i: (layer, 0, j)),
            pl.BlockSpec((tm, tn), lambda j, i: (i, j)),
            pl.BlockSpec((None, 1, tn), lambda j, i: ((i * tm // seq) * N_MOD + gate_idx, 0, j)),
        ],
        out_specs=pl.BlockSpec((tm, tn), lambda j, i: (i, j)),
        compiler_params=_params(("arbitrary", "arbitrary")),
        name="out_proj",
    )(a, w, x2, mod3)


def _routing(logits):
    t = logits.shape[0]
    bm = EXPERT_BM
    a = t * TOP_K
    top_val, top_idx = lax.top_k(logits, TOP_K)
    gate = jax.nn.softmax(top_val, axis=-1)
    onehot = top_idx[:, :, None] == jnp.arange(N_EXPERTS, dtype=jnp.int32)[None, None, :]
    picked = jnp.any(onehot, axis=1).astype(jnp.int32)
    inclusive = jnp.cumsum(picked, axis=0)
    sizes = inclusive[-1]
    sub = EXPERT_SUB
    padded = (sizes + sub - 1) // sub * sub
    pad_end = jnp.cumsum(padded)
    pad_start = pad_end - padded
    start = jnp.cumsum(sizes) - sizes
    slot = (pad_start[None, :] + inclusive - picked)[:, None, :]
    dest = jnp.sum(jnp.where(onehot, slot, 0), axis=-1).astype(jnp.int32).reshape(a)
    order = jnp.argsort(top_idx.reshape(a))
    n_blocks = a // bm + N_EXPERTS
    group_blocks = (sizes + bm - 1) // bm
    blocks_end = jnp.cumsum(group_blocks)
    n_valid = blocks_end[-1].astype(jnp.int32)
    blk = jnp.arange(n_blocks, dtype=jnp.int32)
    block_e = jnp.minimum(jnp.searchsorted(blocks_end, blk, side='right'), N_EXPERTS - 1).astype(jnp.int32)
    chunk = blk - (blocks_end - group_blocks)[block_e]
    block_valid = blk < n_valid
    block_rows = jnp.where(block_valid, jnp.clip(sizes[block_e] - chunk * bm, 0, bm), 0)
    block_nsub = ((block_rows + sub - 1) // sub).astype(jnp.int32)
    block_row0 = jnp.where(block_valid, pad_start[block_e] + chunk * bm, 0).astype(jnp.int32)
    padded_len = (a // TOK_ALIGN) * TOK_ALIGN + TOK_WINDOW
    sorted_tok = jnp.zeros((padded_len,), jnp.int32).at[:a].set((order // TOP_K).astype(jnp.int32))
    block_first = jnp.where(block_valid, start[block_e] + chunk * bm, 0).astype(jnp.int32)
    tables = (block_e, block_nsub, block_first, block_row0, jnp.stack([n_valid, pad_end[-1].astype(jnp.int32)]))
    return gate, dest, sorted_tok, tables


def _expert_kernel(be_ref, ns_ref, first_ref, row0_ref, info_ref, tok_hbm, h_hbm, w1_hbm, b1_ref, w2_hbm, b2_ref,
                   perm_ref, ys_hbm, w1buf, w2buf, gbuf, x_scr, act_scr, obuf, zbuf, tokbuf,
                   sem_w1, sem_w2, sem_g, sem_o, sem_t, *, layer, nj1, nj2):
    sub = EXPERT_SUB
    group = GATHER_GROUP
    bm, d = x_scr.shape
    half = d // 2
    tf = act_scr.shape[2]
    tn = obuf.shape[2]
    steps = nj1 + nj2
    nb_max = be_ref.shape[0]
    n_valid = info_ref[0]
    total = n_valid * steps
    per_step = bm // EXPERT_ISSUE_STEPS

    def for_rows(nsub, fn):
        assert bm // sub < 8
        for size in (4, 2, 1):
            before = (nsub // (2 * size)) * (2 * size)

            @pl.when((nsub & size) != 0)
            def _():
                fn(pl.multiple_of(before * sub, size * sub), size * sub)

    def w1_copy(e, j, slot):
        cols = pl.ds(pl.multiple_of(j * 2 * tf, 2 * tf), 2 * tf)
        return pltpu.make_async_copy(w1_hbm.at[layer, e, :, cols], w1buf.at[slot], sem_w1.at[slot])

    def w2_copies(e, n, slot):
        return [pltpu.make_async_copy(w2_hbm.at[layer, e, :, pl.ds(pl.multiple_of(c0, tn), tn)],
                                      w2buf.at[slot, :, pl.ds(k * tn, tn)], sem_w2.at[slot])
                for k, c0 in enumerate((n * tn, half + n * tn))]

    def tok_copy(blk, slot):
        src = tok_hbm.at[pl.ds(pl.multiple_of(first_ref[blk] // TOK_ALIGN * TOK_ALIGN, TOK_ALIGN), TOK_WINDOW)]
        dst = tokbuf.at[pl.ds(pl.multiple_of(slot * TOK_WINDOW, TOK_ALIGN), TOK_WINDOW)]
        return pltpu.make_async_copy(src, dst, sem_t.at[slot])

    def tok_base(blk, slot):
        return slot * TOK_WINDOW + lax.rem(first_ref[blk], TOK_ALIGN)

    def row_copy(tok_at, tile_row, sublane):
        src = h_hbm.at[pl.ds(tokbuf[tok_at], 1)]
        return pltpu.make_async_copy(src, gbuf.at[tile_row, pl.ds(sublane, 1)], sem_g)

    def out_copy(slot, r0, m, row0, n):
        dst = ys_hbm.at[pl.ds(pl.multiple_of(row0 + r0, sub), m), pl.ds(pl.multiple_of(n * tn, tn), tn)]
        return pltpu.make_async_copy(obuf.at[slot, pl.ds(r0, m)], dst, sem_o.at[slot])

    @pl.when(n_valid > 0)
    def _():
        tok_copy(0, 0).start()
        tok_copy(0, 0).wait()
        base = tok_base(0, 0)

        def start(q, carry):
            for s in range(SUBLANES):
                row_copy(base + q * SUBLANES + s, q, s).start()
            return carry
        lax.fori_loop(0, ns_ref[0] * (sub // SUBLANES), start, 0)
        w1_copy(be_ref[0], 0, 0).start()

    def item(it, carry):
        b = it // steps
        j = it - b * steps
        e = be_ref[b]
        nsub = ns_ref[b]
        has_next = b + 1 < n_valid
        nxt = jnp.minimum(b + 1, nb_max - 1)

        nit = it + 1
        nbk = jnp.minimum(nit // steps, nb_max - 1)
        nj = nit - (nit // steps) * steps

        @pl.when((nit < total) & (nj < nj1))
        def _():
            w1_copy(be_ref[nbk], nj, lax.rem(nj, 2)).start()

        @pl.when((nit < total) & (nj >= nj1))
        def _():
            for c in w2_copies(be_ref[nbk], nj - nj1, lax.rem(nj - nj1, 2)):
                c.start()

        @pl.when(j < nj1)
        def _():
            w1_copy(e, j, lax.rem(j, 2)).wait()

        @pl.when(j >= nj1)
        def _():
            for c in w2_copies(e, j - nj1, lax.rem(j - nj1, 2)):
                c.wait()

        @pl.when(j == 0)
        def _():
            def wait_rows(s, carry):
                pltpu.make_async_copy(h_hbm.at[pl.ds(0, sub)], zbuf, sem_g).wait()
                return carry
            lax.fori_loop(0, nsub, wait_rows, 0)

            def unpack_rows(s, carry):
                rows = pl.ds(pl.multiple_of(s * sub, sub), sub)
                tiles = pl.ds(pl.multiple_of(s * (sub // SUBLANES), sub // SUBLANES), sub // SUBLANES)
                xa, xb = _unpack_halves(gbuf[tiles].reshape(sub, half))
                x_scr[rows, pl.ds(0, half)] = xa.astype(BF16)
                x_scr[rows, pl.ds(half, half)] = xb.astype(BF16)
                return carry
            lax.fori_loop(0, nsub, unpack_rows, 0)

            @pl.when(has_next)
            def _():
                tok_copy(nxt, lax.rem(b + 1, 2)).start()

        @pl.when((j == 1) & has_next)
        def _():
            tok_copy(nxt, lax.rem(b + 1, 2)).wait()

        @pl.when((j >= 1) & (j <= EXPERT_ISSUE_STEPS) & has_next)
        def _():
            base = tok_base(nxt, lax.rem(b + 1, 2))
            rows_next = ns_ref[nxt] * sub
            for g in range(per_step // group):
                first_row = (j - 1) * per_step + g * group
                first_tile = (j - 1) * (per_step // SUBLANES) + g * (group // SUBLANES)

                @pl.when(first_row < rows_next)
                def _():
                    for i in range(group):
                        row_copy(base + first_row + i, first_tile + i // SUBLANES, i % SUBLANES).start()

        @pl.when(j < nj1)
        def _():
            wslot = lax.rem(j, 2)
            bias = b1_ref[pl.ds(e * nj1 + j, 1), :]

            def first_matmul(r0, m):
                rows = pl.ds(r0, m)
                hb = jnp.dot(x_scr[rows, :], w1buf[wslot].astype(BF16), preferred_element_type=F32) + bias
                hb = hb.astype(BF16)
                gw = perm_ref.shape[0]
                sel = [jnp.dot(hb[:, c * gw:(c + 1) * gw], perm_ref[...], preferred_element_type=F32)
                       for c in range(2 * tf // gw)]
                x_glu = jnp.minimum(jnp.concatenate([s[:, :gw // 2] for s in sel], axis=1), SWIGLU_LIMIT)
                x_lin = jnp.clip(jnp.concatenate([s[:, gw // 2:] for s in sel], axis=1),
                                 -SWIGLU_LIMIT, SWIGLU_LIMIT)
                act = x_glu * jax.nn.sigmoid(SWIGLU_ALPHA * x_glu) * (x_lin + 1.0)
                act_scr[j, rows, :] = act.astype(BF16)
            for_rows(nsub, first_matmul)

        @pl.when(j >= nj1)
        def _():
            n = j - nj1
            slot = lax.rem(n, 2)
            row0 = row0_ref[b]
            bias = jnp.concatenate([b2_ref[pl.ds(e * 2 * nj2 + n, 1), :],
                                    b2_ref[pl.ds(e * 2 * nj2 + nj2 + n, 1), :]], axis=1)
            sent = jnp.where(n >= 2, nsub, jnp.where(b > 0, ns_ref[jnp.maximum(b - 1, 0)], 0))
            for_rows(sent, lambda r0, m: out_copy(slot, r0, m, 0, 0).wait())

            def second_matmul(r0, m):
                rows = pl.ds(r0, m)
                act = jnp.concatenate([act_scr[t, rows, :] for t in range(nj1)], axis=1)
                y = jnp.dot(act, w2buf[slot].astype(BF16), preferred_element_type=F32) + bias
                obuf[slot, rows, :] = _pack_halves(y)
                out_copy(slot, r0, m, row0, n).start()
            for_rows(nsub, second_matmul)

        return carry

    lax.fori_loop(0, total, item, 0)

    @pl.when(n_valid > 0)
    def _():
        last = ns_ref[jnp.maximum(n_valid - 1, 0)]
        for slot in (0, 1):
            for_rows(last, lambda r0, m: out_copy(slot, r0, m, 0, 0).wait())

    used = info_ref[1]
    n_tail = (ys_hbm.shape[0] - used) // sub
    zbuf[...] = jnp.zeros_like(zbuf)

    def tail_copy(i):
        dst = ys_hbm.at[pl.ds(pl.multiple_of(used + i * sub, sub), sub)]
        return pltpu.make_async_copy(zbuf, dst, sem_g)

    def tail_start(i, carry):
        tail_copy(i).start()
        return carry

    def tail_wait(i, carry):
        tail_copy(i).wait()
        return carry
    lax.fori_loop(0, n_tail, tail_start, 0)
    lax.fori_loop(0, n_tail, tail_wait, 0)


def _deinterleave_matrix(tf):
    p = np.zeros((2 * tf, 2 * tf), np.float32)
    f = np.arange(tf)
    p[2 * f, f] = 1.0
    p[2 * f + 1, tf + f] = 1.0
    return jnp.asarray(p, BF16)


def _experts(h_packed, sorted_tok, tables, w1, b1, w2, b2, layer):
    t, half = h_packed.shape
    d = 2 * half
    n_experts, f = w2.shape[1], w2.shape[2]
    bm, tf, tn = EXPERT_BM, EXPERT_TF, EXPERT_TN
    nj1 = f // tf
    nj2 = half // tn
    assert nj1 % 2 == 0 and nj2 % 2 == 0
    assert 1 + EXPERT_ISSUE_STEPS <= nj1 + nj2 and bm % (EXPERT_ISSUE_STEPS * GATHER_GROUP) == 0
    assert bm % EXPERT_SUB == 0 and t >= EXPERT_SUB and GATHER_GROUP % SUBLANES == 0
    assert bm + TOK_ALIGN - 1 <= TOK_WINDOW
    p_rows = t * TOP_K + n_experts * EXPERT_SUB
    whole = lambda shape: pl.BlockSpec(shape, lambda i, *_: (0,) * len(shape))
    hbm = pl.BlockSpec(memory_space=pl.ANY)
    return pl.pallas_call(
        functools.partial(_expert_kernel, layer=layer, nj1=nj1, nj2=nj2),
        out_shape=jax.ShapeDtypeStruct((p_rows, half), U32),
        grid_spec=pltpu.PrefetchScalarGridSpec(
            num_scalar_prefetch=5,
            grid=(1,),
            in_specs=[hbm, hbm, hbm, whole((n_experts * nj1, 2 * tf)), hbm, whole((n_experts * 2 * nj2, tn)),
                      whole((EXPERT_PERM_W, EXPERT_PERM_W))],
            out_specs=hbm,
            scratch_shapes=[
                pltpu.VMEM((2, d, 2 * tf), F32),
                pltpu.VMEM((2, f, 2 * tn), F32),
                pltpu.VMEM((bm // SUBLANES, SUBLANES, half), U32),
                pltpu.VMEM((bm, d), BF16),
                pltpu.VMEM((nj1, bm, tf), BF16),
                pltpu.VMEM((2, bm, tn), U32),
                pltpu.VMEM((EXPERT_SUB, half), U32),
                pltpu.SMEM((2 * TOK_WINDOW,), jnp.int32),
                pltpu.SemaphoreType.DMA((2,)),
                pltpu.SemaphoreType.DMA((2,)),
                pltpu.SemaphoreType.DMA(()),
                pltpu.SemaphoreType.DMA((2,)),
                pltpu.SemaphoreType.DMA((2,)),
            ],
        ),
        compiler_params=_params(("arbitrary",)),
        name="experts",
    )(*tables, sorted_tok, h_packed, w1, b1[layer].reshape(n_experts * nj1, 2 * tf), w2,
      b2[layer].reshape(n_experts * 2 * nj2, tn), _deinterleave_matrix(EXPERT_PERM_W // 2))


def _combine_kernel(dest_hbm, ys_hbm, x_ref, gate_ref, g_ref, o_ref, buf, idx, sem, sem_idx):
    i = pl.program_id(0)
    tm = x_ref.shape[0]
    half = buf.shape[3]
    sublanes = SUBLANES
    per_step = tm * TOP_K
    slot = lax.rem(i, 2)

    def idx_copy(step, s):
        src = dest_hbm.at[pl.ds(pl.multiple_of(step * per_step, per_step), per_step)]
        return pltpu.make_async_copy(src, idx.at[pl.ds(pl.multiple_of(s * per_step, per_step), per_step)],
                                     sem_idx.at[s])

    @pl.when(i == 0)
    def _():
        idx_copy(0, 0).start()

    idx_copy(i, slot).wait()

    @pl.when(i + 1 < pl.num_programs(0))
    def _():
        idx_copy(i + 1, 1 - slot).start()

    def row_copy(src_row, k, tile, s):
        return pltpu.make_async_copy(ys_hbm.at[pl.ds(src_row, 1)], buf.at[k, tile, pl.ds(s, 1)], sem)

    def start(tile, carry):
        base = slot * per_step + tile * (sublanes * TOP_K)
        for s in range(sublanes):
            for k in range(TOP_K):
                row_copy(idx[base + s * TOP_K + k], k, tile, s).start()
        return carry

    def wait(tile, carry):
        for _ in range(sublanes * TOP_K):
            row_copy(0, 0, 0, 0).wait()
        return carry

    lax.fori_loop(0, tm // sublanes, start, 0)
    lax.fori_loop(0, tm // sublanes, wait, 0)

    def chunk(c, carry):
        rows = pl.ds(pl.multiple_of(c * sublanes, sublanes), sublanes)
        g = g_ref[rows, :]
        ya = jnp.zeros((sublanes, half), F32)
        yb = jnp.zeros((sublanes, half), F32)
        for k in range(TOP_K):
            a, b = _unpack_halves(buf[k, c])
            ya = ya + g[:, k:k + 1] * a
            yb = yb + g[:, k:k + 1] * b
        o_ref[rows, pl.ds(0, half)] = x_ref[rows, pl.ds(0, half)] + gate_ref[:, pl.ds(0, half)] * ya
        o_ref[rows, pl.ds(half, half)] = x_ref[rows, pl.ds(half, half)] + gate_ref[:, pl.ds(half, half)] * yb
        return carry

    lax.fori_loop(0, tm // sublanes, chunk, 0)


def _combine(ys, dest, gates, x2, mod3, seq, gate_idx):
    t, d = x2.shape
    tm = COMBINE_TM
    half = ys.shape[1]
    assert (tm * TOP_K) % TOK_ALIGN == 0
    return pl.pallas_call(
        _combine_kernel,
        out_shape=jax.ShapeDtypeStruct((t, d), F32),
        grid=(t // tm,),
        in_specs=[
            pl.BlockSpec(memory_space=pl.ANY),
            pl.BlockSpec(memory_space=pl.ANY),
            pl.BlockSpec((tm, d), lambda i: (i, 0)),
            pl.BlockSpec((None, 1, d), lambda i: ((i * tm // seq) * N_MOD + gate_idx, 0, 0)),
            pl.BlockSpec((tm, TOP_K), lambda i: (i, 0)),
        ],
        out_specs=pl.BlockSpec((tm, d), lambda i: (i, 0)),
        scratch_shapes=[
            pltpu.VMEM((TOP_K, tm // SUBLANES, SUBLANES, half), U32),
            pltpu.SMEM((2 * tm * TOP_K,), jnp.int32),
            pltpu.SemaphoreType.DMA(()),
            pltpu.SemaphoreType.DMA((2,)),
        ],
        compiler_params=_params(("arbitrary",)),
        name="combine",
    )(dest, ys, x2, mod3, gates)


def kernel(x, c, w_ada, b_ada, norm1_g, w_in, q_norm_g, k_norm_g, rel_bias, w_ret_out, w_att_out, w_out,
           norm2_g, w_router, b_router, w1, b1, w2, b2):
    batch, seq, d = x.shape
    depth = w_ada.shape[0]
    t = batch * seq
    assert batch <= SUBLANES and t % PROJ_TM == 0
    assert seq % max(RET_BLOCK, ATT_BLOCK, NORM_TM, MERGE_TM, OUT_TM, COMBINE_TM) == 0
    x2 = x.reshape(t, d)
    c8 = jnp.zeros((SUBLANES, d), F32).at[:batch].set(c)
    ret_col0 = 0
    att_col0 = 2 * RET_QK_W + 2 * RET_V_W
    gate_col0 = att_col0 + 3 * ATT_W
    for layer in range(depth):
        mod3 = _ada_ln(c8, w_ada, b_ada, layer)[:batch].reshape(batch * N_MOD, 1, d)
        h = _norm_mod(x2, norm1_g.reshape(depth, 1, d), mod3, layer, seq, shift_idx=0, scale_idx=1)
        proj = _project(h, w_in, layer)
        ret = _retention(proj, batch, seq, ret_col0)
        att = _chunk_attention(proj, rel_bias[layer], q_norm_g[layer], k_norm_g[layer], batch, seq, att_col0)
        merged = _merge(ret, att, w_ret_out, w_att_out, proj, gate_col0, layer)
        x2 = _out_proj(merged, w_out, x2, mod3, layer, seq, gate_idx=2)
        h_packed, logits = _norm_router(x2, norm2_g.reshape(depth, 1, d), mod3, w_router, b_router, layer, seq,
                                        shift_idx=3, scale_idx=4)
        gates, dest, sorted_tok, tables = _routing(logits)
        ys = _experts(h_packed, sorted_tok, tables, w1, b1, w2, b2, layer)
        x2 = _combine(ys, dest, gates, x2, mod3, seq, gate_idx=5)
    return x2.reshape(batch, seq, d)
```

```python
import functools

import numpy as np
import jax
import jax.numpy as jnp
from jax import lax
from jax.experimental import pallas as pl
from jax.experimental.pallas import tpu as pltpu

F32 = jnp.float32
BF16 = jnp.bfloat16
U32 = jnp.uint32

CHUNK = 64
NORM_EPS = 1e-6
RET_HEADS = 8
RET_QK_DIM = 256
RET_V_DIM = 512
ROPE_BASE = 10000.0
ATT_HEADS = 16
ATT_HEAD_DIM = 128
PREV_CHUNKS = 8
MAX_REL = 128
N_EXPERTS = 32
TOP_K = 4
SWIGLU_ALPHA = 1.702
SWIGLU_LIMIT = 7.0
N_MOD = 6

RET_QK_W = RET_HEADS * RET_QK_DIM
RET_V_W = RET_HEADS * RET_V_DIM
ATT_W = ATT_HEADS * ATT_HEAD_DIM

V7X_VMEM_LIMIT_BYTES = 58 * 1024 * 1024
HI16 = 0xFFFF0000

ADA_TN = 1024
NORM_TM = 512
PROJ_TM, PROJ_TN = 2048, 512
RET_BLOCK = 256
RET_HEADS_PER_STEP = 4
ATT_BLOCK = 256
ATT_HEADS_PER_STEP = 4
MERGE_TM, MERGE_TN = 1024, 512
OUT_TM, OUT_TN = 1024, 512
EXPERT_BM = 1280
EXPERT_SUB = 128
EXPERT_TF = 256
EXPERT_TN = 256
EXPERT_PERM_W = 256
EXPERT_ISSUE_STEPS = 10
GATHER_GROUP = 32
TOK_ALIGN = 1024
TOK_WINDOW = 3 * TOK_ALIGN
SUBLANES = 8
COMBINE_TM = 256


def _params(semantics):
    return pltpu.CompilerParams(dimension_semantics=semantics, vmem_limit_bytes=V7X_VMEM_LIMIT_BYTES)


def _silu(t):
    return t * jax.nn.sigmoid(t)


def _pack_halves(y):
    n = y.shape[1] // 2
    hi = lax.bitcast_convert_type(y[:, :n].astype(BF16).astype(F32), U32)
    lo = lax.bitcast_convert_type(y[:, n:].astype(BF16).astype(F32), U32)
    return hi | (lo >> 16)


def _unpack_halves(p):
    a = lax.bitcast_convert_type(p & jnp.uint32(HI16), F32)
    b = lax.bitcast_convert_type(p << 16, F32)
    return a, b


def _ada_kernel(c_ref, w_ref, b_ref, o_ref):
    a = _silu(c_ref[...]).astype(BF16)
    o_ref[...] = jnp.dot(a, w_ref[...].astype(BF16), preferred_element_type=F32) + b_ref[...]


def _ada_ln(c8, w_ada, b_ada, layer):
    rows, d = c8.shape
    n = w_ada.shape[2]
    return pl.pallas_call(
        _ada_kernel,
        out_shape=jax.ShapeDtypeStruct((rows, n), F32),
        grid=(n // ADA_TN,),
        in_specs=[
            pl.BlockSpec((rows, d), lambda j: (0, 0)),
            pl.BlockSpec((None, d, ADA_TN), lambda j: (layer, 0, j)),
            pl.BlockSpec((None, 1, ADA_TN), lambda j: (layer, 0, j)),
        ],
        out_specs=pl.BlockSpec((rows, ADA_TN), lambda j: (0, j)),
        compiler_params=_params(("arbitrary",)),
        name="ada_ln",
    )(c8, w_ada, b_ada.reshape(b_ada.shape[0], 1, n))


def _modulated_norm(x, g, scale, shift):
    xn = x * lax.rsqrt(jnp.mean(x * x, axis=-1, keepdims=True) + NORM_EPS)
    return (xn * g) * (1.0 + scale) + shift


def _norm_mod_kernel(x_ref, g_ref, sc_ref, sh_ref, o_ref):
    o_ref[...] = _modulated_norm(x_ref[...], g_ref[...], sc_ref[...], sh_ref[...]).astype(o_ref.dtype)


def _norm_mod(x2, g, mod3, layer, seq, shift_idx, scale_idx):
    t, d = x2.shape
    tm = NORM_TM

    def mod_spec(idx):
        return pl.BlockSpec((None, 1, d), lambda i: ((i * tm // seq) * N_MOD + idx, 0, 0))

    return pl.pallas_call(
        _norm_mod_kernel,
        out_shape=jax.ShapeDtypeStruct((t, d), BF16),
        grid=(t // tm,),
        in_specs=[
            pl.BlockSpec((tm, d), lambda i: (i, 0)),
            pl.BlockSpec((None, 1, d), lambda i: (layer, 0, 0)),
            mod_spec(scale_idx),
            mod_spec(shift_idx),
        ],
        out_specs=pl.BlockSpec((tm, d), lambda i: (i, 0)),
        compiler_params=_params(("arbitrary",)),
        name="norm_mod",
    )(x2, g, mod3, mod3)


def _norm_router_kernel(x_ref, g_ref, sc_ref, sh_ref, wr_ref, br_ref, h_ref, lg_ref):
    h = _modulated_norm(x_ref[...], g_ref[...], sc_ref[...], sh_ref[...])
    h_ref[...] = _pack_halves(h)
    h_hi = h.astype(BF16)
    h_lo = (h - h_hi.astype(F32)).astype(BF16)
    w = wr_ref[...]
    w_hi = w.astype(BF16)
    w_lo = (w - w_hi.astype(F32)).astype(BF16)
    lg = (jnp.dot(h_hi, w_hi, preferred_element_type=F32)
          + jnp.dot(h_lo, w_hi, preferred_element_type=F32)
          + jnp.dot(h_hi, w_lo, preferred_element_type=F32))
    lg_ref[...] = lg + br_ref[...]


def _norm_router(x2, g, mod3, w_router, b_router, layer, seq, shift_idx, scale_idx):
    t, d = x2.shape
    e = w_router.shape[2]
    tm = NORM_TM

    def mod_spec(idx):
        return pl.BlockSpec((None, 1, d), lambda i: ((i * tm // seq) * N_MOD + idx, 0, 0))

    return pl.pallas_call(
        _norm_router_kernel,
        out_shape=(jax.ShapeDtypeStruct((t, d // 2), U32), jax.ShapeDtypeStruct((t, e), F32)),
        grid=(t // tm,),
        in_specs=[
            pl.BlockSpec((tm, d), lambda i: (i, 0)),
            pl.BlockSpec((None, 1, d), lambda i: (layer, 0, 0)),
            mod_spec(scale_idx),
            mod_spec(shift_idx),
            pl.BlockSpec((None, d, e), lambda i: (layer, 0, 0)),
            pl.BlockSpec((None, 1, e), lambda i: (layer, 0, 0)),
        ],
        out_specs=(pl.BlockSpec((tm, d // 2), lambda i: (i, 0)), pl.BlockSpec((tm, e), lambda i: (i, 0))),
        compiler_params=_params(("arbitrary",)),
        name="norm_router",
    )(x2, g, mod3, mod3, w_router, b_router.reshape(b_router.shape[0], 1, e))


def _proj_kernel(a_ref, w_ref, o_ref):
    o_ref[...] = jnp.dot(a_ref[...], w_ref[...].astype(BF16), preferred_element_type=F32).astype(o_ref.dtype)


def _project(a, w, layer):
    t, k = a.shape
    n = w.shape[2]
    tm, tn = PROJ_TM, PROJ_TN
    return pl.pallas_call(
        _proj_kernel,
        out_shape=jax.ShapeDtypeStruct((t, n), BF16),
        grid=(n // tn, t // tm),
        in_specs=[
            pl.BlockSpec((tm, k), lambda j, i: (i, 0)),
            pl.BlockSpec((None, k, tn), lambda j, i: (layer, 0, j)),
        ],
        out_specs=pl.BlockSpec((tm, tn), lambda j, i: (i, j)),
        compiler_params=_params(("arbitrary", "arbitrary")),
        name="in_proj",
    )(a, w)


def _retention_kernel(q_ref, k_ref, v_ref, g_ref, cos_ref, sin_ref, dec_ref, qd_ref, kd_ref, cd_ref,
                      o_ref, state_ref):
    @pl.when(pl.program_id(2) == 0)
    def _():
        state_ref[...] = jnp.zeros_like(state_ref)

    cos = cos_ref[...]
    sin = sin_ref[...]
    half = RET_QK_DIM // 2
    contract_last = (((1,), (1,)), ((), ()))
    contract_rows = (((0,), (0,)), ((), ()))

    def rotary(t):
        t1, t2 = t[:, :half], t[:, half:]
        return jnp.concatenate([t1 * cos - t2 * sin, t1 * sin + t2 * cos], axis=1)

    for hh in range(RET_HEADS_PER_STEP):
        qk = slice(hh * RET_QK_DIM, (hh + 1) * RET_QK_DIM)
        vg = slice(hh * RET_V_DIM, (hh + 1) * RET_V_DIM)
        q = rotary(q_ref[:, qk].astype(F32))
        k = rotary(k_ref[:, qk].astype(F32)) * (RET_QK_DIM ** -0.5)
        v = v_ref[:, vg]
        scores = lax.dot_general(q.astype(BF16), k.astype(BF16), contract_last,
                                 preferred_element_type=F32) * dec_ref[hh]
        state = state_ref[hh]
        out = (jnp.dot(scores.astype(BF16), v, preferred_element_type=F32)
               + jnp.dot((q * qd_ref[hh]).astype(BF16), state.astype(BF16), preferred_element_type=F32))
        state_ref[hh] = state * cd_ref[hh] + lax.dot_general(
            (k * kd_ref[hh]).astype(BF16), v, contract_rows, preferred_element_type=F32)
        on = out * lax.rsqrt(jnp.mean(out * out, axis=-1, keepdims=True) + NORM_EPS)
        o_ref[:, vg] = (_silu(g_ref[:, vg].astype(F32)) * on).astype(o_ref.dtype)


def _retention_tables(seq):
    half = RET_QK_DIM // 2
    inv_freq = ROPE_BASE ** (-np.arange(half, dtype=np.float64) / half)
    ang = np.arange(seq, dtype=np.float64)[:, None] * inv_freq[None, :]
    log_gamma = np.log(1.0 - 2.0 ** (-5.0 - np.arange(RET_HEADS, dtype=np.float64)))[:, None, None]
    pos = np.arange(RET_BLOCK, dtype=np.float64)
    n, m = pos[:, None], pos[None, :]
    cn, cm = np.floor(n / CHUNK), np.floor(m / CHUNK)
    dist = np.where(cn == cm, np.abs(n - m), n - m)
    dec = np.where((cm <= cn)[None], np.exp(log_gamma * dist[None]), 0.0)
    qd = np.broadcast_to(np.exp(log_gamma * (pos[None, :, None] + 1.0)), (RET_HEADS, RET_BLOCK, RET_QK_DIM))
    kd = np.broadcast_to(np.exp(log_gamma * (RET_BLOCK - 1.0 - pos[None, :, None])),
                         (RET_HEADS, RET_BLOCK, RET_QK_DIM))
    cd = np.broadcast_to(np.exp(log_gamma * RET_BLOCK), (RET_HEADS, 1, RET_V_DIM))
    return tuple(jnp.asarray(np.ascontiguousarray(v), F32) for v in (np.cos(ang), np.sin(ang), dec, qd, kd, cd))


def _retention(proj, batch, seq, col0):
    t = proj.shape[0]
    r = RET_BLOCK
    nsb = seq // r
    cos, sin, dec, qd, kd, cd = _retention_tables(seq)
    hp = RET_HEADS_PER_STEP
    groups = RET_HEADS // hp
    qk_w, v_w = hp * RET_QK_DIM, hp * RET_V_DIM
    assert col0 % qk_w == 0 and (col0 + 2 * RET_QK_W) % v_w == 0
    qk_blk0 = col0 // qk_w
    v_blk0 = (col0 + 2 * RET_QK_W) // v_w
    row = lambda b, h, s: b * nsb + s
    head_tab = lambda shape: pl.BlockSpec((hp,) + shape, lambda b, h, s: (h, 0, 0))
    half = RET_QK_DIM // 2
    return pl.pallas_call(
        _retention_kernel,
        out_shape=jax.ShapeDtypeStruct((t, RET_V_W), BF16),
        grid=(batch, groups, nsb),
        in_specs=[
            pl.BlockSpec((r, qk_w), lambda b, h, s: (row(b, h, s), qk_blk0 + h)),
            pl.BlockSpec((r, qk_w), lambda b, h, s: (row(b, h, s), qk_blk0 + groups + h)),
            pl.BlockSpec((r, v_w), lambda b, h, s: (row(b, h, s), v_blk0 + h)),
            pl.BlockSpec((r, v_w), lambda b, h, s: (row(b, h, s), v_blk0 + groups + h)),
            pl.BlockSpec((r, half), lambda b, h, s: (s, 0)),
            pl.BlockSpec((r, half), lambda b, h, s: (s, 0)),
            head_tab((r, r)),
            head_tab((r, RET_QK_DIM)),
            head_tab((r, RET_QK_DIM)),
            head_tab((1, RET_V_DIM)),
        ],
        out_specs=pl.BlockSpec((r, v_w), lambda b, h, s: (row(b, h, s), h)),
        scratch_shapes=[pltpu.VMEM((hp, RET_QK_DIM, RET_V_DIM), F32)],
        compiler_params=_params(("arbitrary", "arbitrary", "arbitrary")),
        name="retention",
    )(proj, proj, proj, proj, cos, sin, dec, qd, kd, cd)


ATT_KEY_BLOCKS = PREV_CHUNKS * CHUNK // ATT_BLOCK + 1
NEG = float(np.finfo(np.float32).min)


def _attention_kernel(q_ref, k0_ref, k1_ref, k2_ref, v0_ref, v1_ref, v2_ref, tab_ref, gq_ref, gk_ref, o_ref):
    i = pl.program_id(1)

    def rms(t, g):
        tf = t.astype(F32)
        return (tf * lax.rsqrt(jnp.mean(tf * tf, axis=-1, keepdims=True) + NORM_EPS)) * g

    dh = ATT_HEAD_DIM
    k_all = jnp.concatenate([k0_ref[...], k1_ref[...], k2_ref[...]], axis=0)
    v_all = jnp.concatenate([v0_ref[...], v1_ref[...], v2_ref[...]], axis=0)
    outs = []
    for hh in range(ATT_HEADS_PER_STEP):
        lanes = slice(hh * dh, (hh + 1) * dh)
        q = rms(q_ref[:, lanes], gq_ref[...]).astype(BF16)
        k = rms(k_all[:, lanes], gk_ref[...]).astype(BF16)
        sc = lax.dot_general(q, k, (((1,), (1,)), ((), ())), preferred_element_type=F32)
        sc = sc * (ATT_HEAD_DIM ** -0.5) + tab_ref[hh]
        col = lax.broadcasted_iota(jnp.int32, sc.shape, 1)
        sc = jnp.where(col >= (ATT_KEY_BLOCKS - 1 - i) * ATT_BLOCK, sc, NEG)
        p = jnp.exp(sc - jnp.max(sc, axis=-1, keepdims=True))
        denom = jnp.sum(p, axis=-1, keepdims=True)
        outs.append(jnp.dot(p.astype(BF16), v_all[:, lanes], preferred_element_type=F32) / denom)
    o_ref[...] = jnp.concatenate(outs, axis=1).astype(o_ref.dtype)


def _attention_table(rel_bias):
    a = ATT_BLOCK
    w = ATT_KEY_BLOCKS * a
    pad = PREV_CHUNKS * CHUNK
    heads = rel_bias.shape[0]
    period = 1024
    assert w + a - 1 <= period
    i = np.arange(period)
    rel_of_i = np.where(i < w, pad - i, pad + period - i)
    idx = np.clip(np.minimum(rel_of_i, MAX_REL) + (CHUNK - 1), 0, rel_bias.shape[-1] - 1)
    u = rel_bias[:, idx].astype(F32).reshape(heads, 1, period)
    return pl.pallas_call(
        _attention_table_kernel,
        out_shape=jax.ShapeDtypeStruct((heads, a, w), F32),
        grid=(heads,),
        in_specs=[pl.BlockSpec((None, 1, period), lambda h: (h, 0, 0))],
        out_specs=pl.BlockSpec((None, a, w), lambda h: (h, 0, 0)),
        compiler_params=_params(("arbitrary",)),
        name="attention_table",
    )(u)


def _attention_table_kernel(u_ref, o_ref):
    a, w = o_ref.shape
    rows = jnp.broadcast_to(u_ref[...], (a, u_ref.shape[1]))
    skew = pltpu.roll(rows, 0, 1, stride=1, stride_axis=0)
    chunk_bits = CHUNK.bit_length() - 1
    assert CHUNK == 1 << chunk_bits
    qc = lax.broadcasted_iota(jnp.int32, (a, w), 0) >> chunk_bits
    kc = (lax.broadcasted_iota(jnp.int32, (a, w), 1) >> chunk_bits) - PREV_CHUNKS
    valid = (kc <= qc) & (kc >= qc - PREV_CHUNKS)
    o_ref[...] = jnp.where(valid, skew[:, :w], NEG)


def _chunk_attention(proj, rel_bias, q_norm_g, k_norm_g, batch, seq, col0):
    assert ATT_KEY_BLOCKS == 3
    t = proj.shape[0]
    a = ATT_BLOCK
    nqb = seq // a
    dh = ATT_HEAD_DIM
    hw = ATT_HEADS_PER_STEP * dh
    groups = ATT_HEADS // ATT_HEADS_PER_STEP
    assert col0 % hw == 0 and ATT_W % hw == 0
    blk0 = col0 // hw
    tab = _attention_table(rel_bias)

    def kv_spec(which, jj):
        return pl.BlockSpec(
            (a, hw),
            lambda h, i, b: (b * nqb + jnp.maximum(i - (ATT_KEY_BLOCKS - 1) + jj, 0), blk0 + which * groups + h))

    return pl.pallas_call(
        _attention_kernel,
        out_shape=jax.ShapeDtypeStruct((t, ATT_W), BF16),
        grid=(groups, nqb, batch),
        in_specs=[
            pl.BlockSpec((a, hw), lambda h, i, b: (b * nqb + i, blk0 + h)),
            kv_spec(1, 0), kv_spec(1, 1), kv_spec(1, 2),
            kv_spec(2, 0), kv_spec(2, 1), kv_spec(2, 2),
            pl.BlockSpec((ATT_HEADS_PER_STEP, a, ATT_KEY_BLOCKS * a), lambda h, i, b: (h, 0, 0)),
            pl.BlockSpec((1, dh), lambda h, i, b: (0, 0)),
            pl.BlockSpec((1, dh), lambda h, i, b: (0, 0)),
        ],
        out_specs=pl.BlockSpec((a, hw), lambda h, i, b: (b * nqb + i, h)),
        compiler_params=_params(("arbitrary", "arbitrary", "arbitrary")),
        name="chunk_attention",
    )(proj, proj, proj, proj, proj, proj, proj, tab, q_norm_g.reshape(1, dh), k_norm_g.reshape(1, dh))


def _merge_kernel(r_ref, a_ref, wr_ref, wa_ref, gr_ref, ga_ref, o_ref):
    yr = jnp.dot(r_ref[...], wr_ref[...].astype(BF16), preferred_element_type=F32)
    ya = jnp.dot(a_ref[...], wa_ref[...].astype(BF16), preferred_element_type=F32)
    merged = jax.nn.sigmoid(gr_ref[...].astype(F32)) * yr + jax.nn.sigmoid(ga_ref[...].astype(F32)) * ya
    o_ref[...] = merged.astype(o_ref.dtype)


def _merge(ret, att, w_ret_out, w_att_out, proj, gate_col0, layer):
    t = ret.shape[0]
    d = w_ret_out.shape[2]
    tm, tn = MERGE_TM, MERGE_TN
    gblk = gate_col0 // tn
    return pl.pallas_call(
        _merge_kernel,
        out_shape=jax.ShapeDtypeStruct((t, d), BF16),
        grid=(d // tn, t // tm),
        in_specs=[
            pl.BlockSpec((tm, ret.shape[1]), lambda j, i: (i, 0)),
            pl.BlockSpec((tm, att.shape[1]), lambda j, i: (i, 0)),
            pl.BlockSpec((None, ret.shape[1], tn), lambda j, i: (layer, 0, j)),
            pl.BlockSpec((None, att.shape[1], tn), lambda j, i: (layer, 0, j)),
            pl.BlockSpec((tm, tn), lambda j, i: (i, gblk + j)),
            pl.BlockSpec((tm, tn), lambda j, i: (i, gblk + d // tn + j)),
        ],
        out_specs=pl.BlockSpec((tm, tn), lambda j, i: (i, j)),
        compiler_params=_params(("arbitrary", "arbitrary")),
        name="merge",
    )(ret, att, w_ret_out, w_att_out, proj, proj)


def _out_proj_kernel(a_ref, w_ref, x_ref, gate_ref, o_ref):
    acc = jnp.dot(a_ref[...], w_ref[...].astype(BF16), preferred_element_type=F32)
    o_ref[...] = x_ref[...] + gate_ref[...] * acc


def _out_proj(a, w, x2, mod3, layer, seq, gate_idx):
    t, k = a.shape
    d = w.shape[2]
    tm, tn = OUT_TM, OUT_TN
    return pl.pallas_call(
        _out_proj_kernel,
        out_shape=jax.ShapeDtypeStruct((t, d), F32),
        grid=(d // tn, t // tm),
        in_specs=[
            pl.BlockSpec((tm, k), lambda j, i: (i, 0)),
            pl.BlockSpec((None, k, tn), lambda j, i: (layer, 0, j)),
            pl.BlockSpec((tm, tn), lambda j, i: (i, j)),
            pl.BlockSpec((None, 1, tn), lambda j, i: ((i * tm // seq) * N_MOD + gate_idx, 0, j)),
        ],
        out_specs=pl.BlockSpec((tm, tn), lambda j, i: (i, j)),
        compiler_params=_params(("arbitrary", "arbitrary")),
        name="out_proj",
    )(a, w, x2, mod3)


def _routing(logits):
    t = logits.shape[0]
    bm = EXPERT_BM
    a = t * TOP_K
    top_val, top_idx = lax.top_k(logits, TOP_K)
    gate = jax.nn.softmax(top_val, axis=-1)
    onehot = top_idx[:, :, None] == jnp.arange(N_EXPERTS, dtype=jnp.int32)[None, None, :]
    picked = jnp.any(onehot, axis=1).astype(jnp.int32)
    inclusive = jnp.cumsum(picked, axis=0)
    sizes = inclusive[-1]
    sub = EXPERT_SUB
    padded = (sizes + sub - 1) // sub * sub
    pad_end = jnp.cumsum(padded)
    pad_start = pad_end - padded
    start = jnp.cumsum(sizes) - sizes
    slot = (pad_start[None, :] + inclusive - picked)[:, None, :]
    dest = jnp.sum(jnp.where(onehot, slot, 0), axis=-1).astype(jnp.int32).reshape(a)
    order = jnp.argsort(top_idx.reshape(a))
    n_blocks = a // bm + N_EXPERTS
    group_blocks = (sizes + bm - 1) // bm
    blocks_end = jnp.cumsum(group_blocks)
    n_valid = blocks_end[-1].astype(jnp.int32)
    blk = jnp.arange(n_blocks, dtype=jnp.int32)
    block_e = jnp.minimum(jnp.searchsorted(blocks_end, blk, side='right'), N_EXPERTS - 1).astype(jnp.int32)
    chunk = blk - (blocks_end - group_blocks)[block_e]
    block_valid = blk < n_valid
    block_rows = jnp.where(block_valid, jnp.clip(sizes[block_e] - chunk * bm, 0, bm), 0)
    block_nsub = ((block_rows + sub - 1) // sub).astype(jnp.int32)
    block_row0 = jnp.where(block_valid, pad_start[block_e] + chunk * bm, 0).astype(jnp.int32)
    padded_len = (a // TOK_ALIGN) * TOK_ALIGN + TOK_WINDOW
    sorted_tok = jnp.zeros((padded_len,), jnp.int32).at[:a].set((order // TOP_K).astype(jnp.int32))
    block_first = jnp.where(block_valid, start[block_e] + chunk * bm, 0).astype(jnp.int32)
    tables = (block_e, block_nsub, block_first, block_row0, jnp.stack([n_valid, pad_end[-1].astype(jnp.int32)]))
    return gate, dest, sorted_tok, tables


def _expert_kernel(be_ref, ns_ref, first_ref, row0_ref, info_ref, tok_hbm, h_hbm, w1_hbm, b1_ref, w2_hbm, b2_ref,
                   perm_ref, ys_hbm, w1buf, w2buf, gbuf, x_scr, act_scr, obuf, zbuf, tokbuf,
                   sem_w1, sem_w2, sem_g, sem_o, sem_t, *, layer, nj1, nj2):
    sub = EXPERT_SUB
    group = GATHER_GROUP
    bm, d = x_scr.shape
    half = d // 2
    tf = act_scr.shape[2]
    tn = obuf.shape[2]
    steps = nj1 + nj2
    nb_max = be_ref.shape[0]
    n_valid = info_ref[0]
    total = n_valid * steps
    per_step = bm // EXPERT_ISSUE_STEPS

    def for_rows(nsub, fn):
        assert bm // sub < 16
        for size in (8, 4, 2, 1):
            before = (nsub // (2 * size)) * (2 * size)

            @pl.when((nsub & size) != 0)
            def _():
                fn(pl.multiple_of(before * sub, size * sub), size * sub)

    def w1_copy(e, j, slot):
        cols = pl.ds(pl.multiple_of(j * 2 * tf, 2 * tf), 2 * tf)
        return pltpu.make_async_copy(w1_hbm.at[layer, e, :, cols], w1buf.at[slot], sem_w1.at[slot])

    def w2_copies(e, n, slot):
        return [pltpu.make_async_copy(w2_hbm.at[layer, e, :, pl.ds(pl.multiple_of(c0, tn), tn)],
                                      w2buf.at[slot, :, pl.ds(k * tn, tn)], sem_w2.at[slot])
                for k, c0 in enumerate((n * tn, half + n * tn))]

    def tok_copy(blk, slot):
        src = tok_hbm.at[pl.ds(pl.multiple_of(first_ref[blk] // TOK_ALIGN * TOK_ALIGN, TOK_ALIGN), TOK_WINDOW)]
        dst = tokbuf.at[pl.ds(pl.multiple_of(slot * TOK_WINDOW, TOK_ALIGN), TOK_WINDOW)]
        return pltpu.make_async_copy(src, dst, sem_t.at[slot])

    def tok_base(blk, slot):
        return slot * TOK_WINDOW + lax.rem(first_ref[blk], TOK_ALIGN)

    def row_copy(tok_at, tile_row, sublane):
        src = h_hbm.at[pl.ds(tokbuf[tok_at], 1)]
        return pltpu.make_async_copy(src, gbuf.at[tile_row, pl.ds(sublane, 1)], sem_g)

    def out_copy(slot, r0, m, row0, n):
        dst = ys_hbm.at[pl.ds(pl.multiple_of(row0 + r0, sub), m), pl.ds(pl.multiple_of(n * tn, tn), tn)]
        return pltpu.make_async_copy(obuf.at[slot, pl.ds(r0, m)], dst, sem_o.at[slot])

    @pl.when(n_valid > 0)
    def _():
        tok_copy(0, 0).start()
        tok_copy(0, 0).wait()
        base = tok_base(0, 0)

        def start(q, carry):
            for s in range(SUBLANES):
                row_copy(base + q * SUBLANES + s, q, s).start()
            return carry
        lax.fori_loop(0, ns_ref[0] * (sub // SUBLANES), start, 0)
        w1_copy(be_ref[0], 0, 0).start()

    def item(it, carry):
        b = it // steps
        j = it - b * steps
        e = be_ref[b]
        nsub = ns_ref[b]
        has_next = b + 1 < n_valid
        nxt = jnp.minimum(b + 1, nb_max - 1)

        nit = it + 1
        nbk = jnp.minimum(nit // steps, nb_max - 1)
        nj = nit - (nit // steps) * steps

        @pl.when((nit < total) & (nj < nj1))
        def _():
            w1_copy(be_ref[nbk], nj, lax.rem(nj, 2)).start()

        @pl.when((nit < total) & (nj >= nj1))
        def _():
            for c in w2_copies(be_ref[nbk], nj - nj1, lax.rem(nj - nj1, 2)):
                c.start()

        @pl.when(j < nj1)
        def _():
            w1_copy(e, j, lax.rem(j, 2)).wait()

        @pl.when(j >= nj1)
        def _():
            for c in w2_copies(e, j - nj1, lax.rem(j - nj1, 2)):
                c.wait()

        @pl.when(j == 0)
        def _():
            def wait_rows(s, carry):
                pltpu.make_async_copy(h_hbm.at[pl.ds(0, sub)], zbuf, sem_g).wait()
                return carry
            lax.fori_loop(0, nsub, wait_rows, 0)

            def unpack_rows(s, carry):
                rows = pl.ds(pl.multiple_of(s * sub, sub), sub)
                tiles = pl.ds(pl.multiple_of(s * (sub // SUBLANES), sub // SUBLANES), sub // SUBLANES)
                xa, xb = _unpack_halves(gbuf[tiles].reshape(sub, half))
                x_scr[rows, pl.ds(0, half)] = xa.astype(BF16)
                x_scr[rows, pl.ds(half, half)] = xb.astype(BF16)
                return carry
            lax.fori_loop(0, nsub, unpack_rows, 0)

            @pl.when(has_next)
            def _():
                tok_copy(nxt, lax.rem(b + 1, 2)).start()

        @pl.when((j == 1) & has_next)
        def _():
            tok_copy(nxt, lax.rem(b + 1, 2)).wait()

        @pl.when((j >= 1) & (j <= EXPERT_ISSUE_STEPS) & has_next)
        def _():
            base = tok_base(nxt, lax.rem(b + 1, 2))
            rows_next = ns_ref[nxt] * sub
            for g in range(per_step // group):
                first_row = (j - 1) * per_step + g * group
                first_tile = (j - 1) * (per_step // SUBLANES) + g * (group // SUBLANES)

                @pl.when(first_row < rows_next)
                def _():
                    for i in range(group):
                        row_copy(base + first_row + i, first_tile + i // SUBLANES, i % SUBLANES).start()

        @pl.when(j < nj1)
        def _():
            wslot = lax.rem(j, 2)
            bias = b1_ref[pl.ds(e * nj1 + j, 1), :]

            def first_matmul(r0, m):
                rows = pl.ds(r0, m)
                hb = jnp.dot(x_scr[rows, :], w1buf[wslot].astype(BF16), preferred_element_type=F32) + bias
                hb = hb.astype(BF16)
                gw = perm_ref.shape[0]
                sel = [jnp.dot(hb[:, c * gw:(c + 1) * gw], perm_ref[...], preferred_element_type=F32)
                       for c in range(2 * tf // gw)]
                x_glu = jnp.minimum(jnp.concatenate([s[:, :gw // 2] for s in sel], axis=1), SWIGLU_LIMIT)
                x_lin = jnp.clip(jnp.concatenate([s[:, gw // 2:] for s in sel], axis=1),
                                 -SWIGLU_LIMIT, SWIGLU_LIMIT)
                act = x_glu * jax.nn.sigmoid(SWIGLU_ALPHA * x_glu) * (x_lin + 1.0)
                act_scr[j, rows, :] = act.astype(BF16)
            for_rows(nsub, first_matmul)

        @pl.when(j >= nj1)
        def _():
            n = j - nj1
            slot = lax.rem(n, 2)
            row0 = row0_ref[b]
            bias = jnp.concatenate([b2_ref[pl.ds(e * 2 * nj2 + n, 1), :],
                                    b2_ref[pl.ds(e * 2 * nj2 + nj2 + n, 1), :]], axis=1)
            sent = jnp.where(n >= 2, nsub, jnp.where(b > 0, ns_ref[jnp.maximum(b - 1, 0)], 0))
            for_rows(sent, lambda r0, m: out_copy(slot, r0, m, 0, 0).wait())

            def second_matmul(r0, m):
                rows = pl.ds(r0, m)
                act = jnp.concatenate([act_scr[t, rows, :] for t in range(nj1)], axis=1)
                y = jnp.dot(act, w2buf[slot].astype(BF16), preferred_element_type=F32) + bias
                obuf[slot, rows, :] = _pack_halves(y)
                out_copy(slot, r0, m, row0, n).start()
            for_rows(nsub, second_matmul)

        return carry

    lax.fori_loop(0, total, item, 0)

    @pl.when(n_valid > 0)
    def _():
        last = ns_ref[jnp.maximum(n_valid - 1, 0)]
        for slot in (0, 1):
            for_rows(last, lambda r0, m: out_copy(slot, r0, m, 0, 0).wait())

    used = info_ref[1]
    n_tail = (ys_hbm.shape[0] - used) // sub
    zbuf[...] = jnp.zeros_like(zbuf)

    def tail_copy(i):
        dst = ys_hbm.at[pl.ds(pl.multiple_of(used + i * sub, sub), sub)]
        return pltpu.make_async_copy(zbuf, dst, sem_g)

    def tail_start(i, carry):
        tail_copy(i).start()
        return carry

    def tail_wait(i, carry):
        tail_copy(i).wait()
        return carry
    lax.fori_loop(0, n_tail, tail_start, 0)
    lax.fori_loop(0, n_tail, tail_wait, 0)


def _deinterleave_matrix(tf):
    p = np.zeros((2 * tf, 2 * tf), np.float32)
    f = np.arange(tf)
    p[2 * f, f] = 1.0
    p[2 * f + 1, tf + f] = 1.0
    return jnp.asarray(p, BF16)


def _experts(h_packed, sorted_tok, tables, w1, b1, w2, b2, layer):
    t, half = h_packed.shape
    d = 2 * half
    n_experts, f = w2.shape[1], w2.shape[2]
    bm, tf, tn = EXPERT_BM, EXPERT_TF, EXPERT_TN
    nj1 = f // tf
    nj2 = half // tn
    assert nj1 % 2 == 0 and nj2 % 2 == 0
    assert 1 + EXPERT_ISSUE_STEPS <= nj1 + nj2 and bm % (EXPERT_ISSUE_STEPS * GATHER_GROUP) == 0
    assert bm % EXPERT_SUB == 0 and t >= EXPERT_SUB and GATHER_GROUP % SUBLANES == 0
    assert bm + TOK_ALIGN - 1 <= TOK_WINDOW
    p_rows = t * TOP_K + n_experts * EXPERT_SUB
    whole = lambda shape: pl.BlockSpec(shape, lambda i, *_: (0,) * len(shape))
    hbm = pl.BlockSpec(memory_space=pl.ANY)
    return pl.pallas_call(
        functools.partial(_expert_kernel, layer=layer, nj1=nj1, nj2=nj2),
        out_shape=jax.ShapeDtypeStruct((p_rows, half), U32),
        grid_spec=pltpu.PrefetchScalarGridSpec(
            num_scalar_prefetch=5,
            grid=(1,),
            in_specs=[hbm, hbm, hbm, whole((n_experts * nj1, 2 * tf)), hbm, whole((n_experts * 2 * nj2, tn)),
                      whole((EXPERT_PERM_W, EXPERT_PERM_W))],
            out_specs=hbm,
            scratch_shapes=[
                pltpu.VMEM((2, d, 2 * tf), F32),
                pltpu.VMEM((2, f, 2 * tn), F32),
                pltpu.VMEM((bm // SUBLANES, SUBLANES, half), U32),
                pltpu.VMEM((bm, d), BF16),
                pltpu.VMEM((nj1, bm, tf), BF16),
                pltpu.VMEM((2, bm, tn), U32),
                pltpu.VMEM((EXPERT_SUB, half), U32),
                pltpu.SMEM((2 * TOK_WINDOW,), jnp.int32),
                pltpu.SemaphoreType.DMA((2,)),
                pltpu.SemaphoreType.DMA((2,)),
                pltpu.SemaphoreType.DMA(()),
                pltpu.SemaphoreType.DMA((2,)),
                pltpu.SemaphoreType.DMA((2,)),
            ],
        ),
        compiler_params=_params(("arbitrary",)),
        name="experts",
    )(*tables, sorted_tok, h_packed, w1, b1[layer].reshape(n_experts * nj1, 2 * tf), w2,
      b2[layer].reshape(n_experts * 2 * nj2, tn), _deinterleave_matrix(EXPERT_PERM_W // 2))


def _combine_kernel(dest_hbm, ys_hbm, x_ref, gate_ref, g_ref, o_ref, buf, idx, sem, sem_idx):
    i = pl.program_id(0)
    tm = x_ref.shape[0]
    half = buf.shape[3]
    sublanes = SUBLANES
    per_step = tm * TOP_K
    slot = lax.rem(i, 2)

    def idx_copy(step, s):
        src = dest_hbm.at[pl.ds(pl.multiple_of(step * per_step, per_step), per_step)]
        return pltpu.make_async_copy(src, idx.at[pl.ds(pl.multiple_of(s * per_step, per_step), per_step)],
                                     sem_idx.at[s])

    @pl.when(i == 0)
    def _():
        idx_copy(0, 0).start()

    idx_copy(i, slot).wait()

    @pl.when(i + 1 < pl.num_programs(0))
    def _():
        idx_copy(i + 1, 1 - slot).start()

    def row_copy(src_row, k, tile, s):
        return pltpu.make_async_copy(ys_hbm.at[pl.ds(src_row, 1)], buf.at[k, tile, pl.ds(s, 1)], sem)

    def start(tile, carry):
        base = slot * per_step + tile * (sublanes * TOP_K)
        for s in range(sublanes):
            for k in range(TOP_K):
                row_copy(idx[base + s * TOP_K + k], k, tile, s).start()
        return carry

    def wait(tile, carry):
        for _ in range(sublanes * TOP_K):
            row_copy(0, 0, 0, 0).wait()
        return carry

    lax.fori_loop(0, tm // sublanes, start, 0)
    lax.fori_loop(0, tm // sublanes, wait, 0)

    def chunk(c, carry):
        rows = pl.ds(pl.multiple_of(c * sublanes, sublanes), sublanes)
        g = g_ref[rows, :]
        ya = jnp.zeros((sublanes, half), F32)
        yb = jnp.zeros((sublanes, half), F32)
        for k in range(TOP_K):
            a, b = _unpack_halves(buf[k, c])
            ya = ya + g[:, k:k + 1] * a
            yb = yb + g[:, k:k + 1] * b
        o_ref[rows, pl.ds(0, half)] = x_ref[rows, pl.ds(0, half)] + gate_ref[:, pl.ds(0, half)] * ya
        o_ref[rows, pl.ds(half, half)] = x_ref[rows, pl.ds(half, half)] + gate_ref[:, pl.ds(half, half)] * yb
        return carry

    lax.fori_loop(0, tm // sublanes, chunk, 0)


def _combine(ys, dest, gates, x2, mod3, seq, gate_idx):
    t, d = x2.shape
    tm = COMBINE_TM
    half = ys.shape[1]
    assert (tm * TOP_K) % TOK_ALIGN == 0
    return pl.pallas_call(
        _combine_kernel,
        out_shape=jax.ShapeDtypeStruct((t, d), F32),
        grid=(t // tm,),
        in_specs=[
            pl.BlockSpec(memory_space=pl.ANY),
            pl.BlockSpec(memory_space=pl.ANY),
            pl.BlockSpec((tm, d), lambda i: (i, 0)),
            pl.BlockSpec((None, 1, d), lambda i: ((i * tm // seq) * N_MOD + gate_idx, 0, 0)),
            pl.BlockSpec((tm, TOP_K), lambda i: (i, 0)),
        ],
        out_specs=pl.BlockSpec((tm, d), lambda i: (i, 0)),
        scratch_shapes=[
            pltpu.VMEM((TOP_K, tm // SUBLANES, SUBLANES, half), U32),
            pltpu.SMEM((2 * tm * TOP_K,), jnp.int32),
            pltpu.SemaphoreType.DMA(()),
            pltpu.SemaphoreType.DMA((2,)),
        ],
        compiler_params=_params(("arbitrary",)),
        name="combine",
    )(dest, ys, x2, mod3, gates)


def kernel(x, c, w_ada, b_ada, norm1_g, w_in, q_norm_g, k_norm_g, rel_bias, w_ret_out, w_att_out, w_out,
           norm2_g, w_router, b_router, w1, b1, w2, b2):
    batch, seq, d = x.shape
    depth = w_ada.shape[0]
    t = batch * seq
    assert batch <= SUBLANES and t % PROJ_TM == 0
    assert seq % max(RET_BLOCK, ATT_BLOCK, NORM_TM, MERGE_TM, OUT_TM, COMBINE_TM) == 0
    x2 = x.reshape(t, d)
    c8 = jnp.zeros((SUBLANES, d), F32).at[:batch].set(c)
    ret_col0 = 0
    att_col0 = 2 * RET_QK_W + 2 * RET_V_W
    gate_col0 = att_col0 + 3 * ATT_W
    for layer in range(depth):
        mod3 = _ada_ln(c8, w_ada, b_ada, layer)[:batch].reshape(batch * N_MOD, 1, d)
        h = _norm_mod(x2, norm1_g.reshape(depth, 1, d), mod3, layer, seq, shift_idx=0, scale_idx=1)
        proj = _project(h, w_in, layer)
        ret = _retention(proj, batch, seq, ret_col0)
        att = _chunk_attention(proj, rel_bias[layer], q_norm_g[layer], k_norm_g[layer], batch, seq, att_col0)
        merged = _merge(ret, att, w_ret_out, w_att_out, proj, gate_col0, layer)
        x2 = _out_proj(merged, w_out, x2, mod3, layer, seq, gate_idx=2)
        h_packed, logits = _norm_router(x2, norm2_g.reshape(depth, 1, d), mod3, w_router, b_router, layer, seq,
                                        shift_idx=3, scale_idx=4)
        gates, dest, sorted_tok, tables = _routing(logits)
        ys = _experts(h_packed, sorted_tok, tables, w1, b1, w2, b2, layer)
        x2 = _combine(ys, dest, gates, x2, mod3, seq, gate_idx=5)
    return x2.reshape(batch, seq, d)
```

```python
import functools

import numpy as np
import jax
import jax.numpy as jnp
from jax import lax
from jax.experimental import pallas as pl
from jax.experimental.pallas import tpu as pltpu

F32 = jnp.float32
BF16 = jnp.bfloat16
U32 = jnp.uint32

CHUNK = 64
NORM_EPS = 1e-6
RET_HEADS = 8
RET_QK_DIM = 256
RET_V_DIM = 512
ROPE_BASE = 10000.0
ATT_HEADS = 16
ATT_HEAD_DIM = 128
PREV_CHUNKS = 8
MAX_REL = 128
N_EXPERTS = 32
TOP_K = 4
SWIGLU_ALPHA = 1.702
SWIGLU_LIMIT = 7.0
N_MOD = 6

RET_QK_W = RET_HEADS * RET_QK_DIM
RET_V_W = RET_HEADS * RET_V_DIM
ATT_W = ATT_HEADS * ATT_HEAD_DIM

V7X_VMEM_LIMIT_BYTES = 58 * 1024 * 1024
HI16 = 0xFFFF0000

ADA_TN = 1024
NORM_TM = 512
PROJ_TM, PROJ_TN = 2048, 512
RET_BLOCK = 256
RET_HEADS_PER_STEP = 8
ATT_BLOCK = 256
ATT_HEADS_PER_STEP = 8
MERGE_TM, MERGE_TN = 1024, 512
OUT_TM, OUT_TN = 1024, 512
EXPERT_BM = 1280
EXPERT_SUB = 128
EXPERT_TF = 256
EXPERT_TN = 256
EXPERT_PERM_W = 256
EXPERT_ISSUE_STEPS = 10
GATHER_GROUP = 32
TOK_ALIGN = 1024
TOK_WINDOW = 3 * TOK_ALIGN
SUBLANES = 8
COMBINE_TM = 256


def _params(semantics):
    return pltpu.CompilerParams(dimension_semantics=semantics, vmem_limit_bytes=V7X_VMEM_LIMIT_BYTES)


def _silu(t):
    return t * jax.nn.sigmoid(t)


def _pack_halves(y):
    n = y.shape[1] // 2
    hi = lax.bitcast_convert_type(y[:, :n].astype(BF16).astype(F32), U32)
    lo = lax.bitcast_convert_type(y[:, n:].astype(BF16).astype(F32), U32)
    return hi | (lo >> 16)


def _unpack_halves(p):
    a = lax.bitcast_convert_type(p & jnp.uint32(HI16), F32)
    b = lax.bitcast_convert_type(p << 16, F32)
    return a, b


def _ada_kernel(c_ref, w_ref, b_ref, o_ref):
    a = _silu(c_ref[...]).astype(BF16)
    o_ref[...] = jnp.dot(a, w_ref[...].astype(BF16), preferred_element_type=F32) + b_ref[...]


def _ada_ln(c8, w_ada, b_ada, layer):
    rows, d = c8.shape
    n = w_ada.shape[2]
    return pl.pallas_call(
        _ada_kernel,
        out_shape=jax.ShapeDtypeStruct((rows, n), F32),
        grid=(n // ADA_TN,),
        in_specs=[
            pl.BlockSpec((rows, d), lambda j: (0, 0)),
            pl.BlockSpec((None, d, ADA_TN), lambda j: (layer, 0, j)),
            pl.BlockSpec((None, 1, ADA_TN), lambda j: (layer, 0, j)),
        ],
        out_specs=pl.BlockSpec((rows, ADA_TN), lambda j: (0, j)),
        compiler_params=_params(("arbitrary",)),
        name="ada_ln",
    )(c8, w_ada, b_ada.reshape(b_ada.shape[0], 1, n))


def _modulated_norm(x, g, scale, shift):
    xn = x * lax.rsqrt(jnp.mean(x * x, axis=-1, keepdims=True) + NORM_EPS)
    return (xn * g) * (1.0 + scale) + shift


def _norm_mod_kernel(x_ref, g_ref, sc_ref, sh_ref, o_ref):
    o_ref[...] = _modulated_norm(x_ref[...], g_ref[...], sc_ref[...], sh_ref[...]).astype(o_ref.dtype)


def _norm_mod(x2, g, mod3, layer, seq, shift_idx, scale_idx):
    t, d = x2.shape
    tm = NORM_TM

    def mod_spec(idx):
        return pl.BlockSpec((None, 1, d), lambda i: ((i * tm // seq) * N_MOD + idx, 0, 0))

    return pl.pallas_call(
        _norm_mod_kernel,
        out_shape=jax.ShapeDtypeStruct((t, d), BF16),
        grid=(t // tm,),
        in_specs=[
            pl.BlockSpec((tm, d), lambda i: (i, 0)),
            pl.BlockSpec((None, 1, d), lambda i: (layer, 0, 0)),
            mod_spec(scale_idx),
            mod_spec(shift_idx),
        ],
        out_specs=pl.BlockSpec((tm, d), lambda i: (i, 0)),
        compiler_params=_params(("arbitrary",)),
        name="norm_mod",
    )(x2, g, mod3, mod3)


def _norm_router_kernel(x_ref, g_ref, sc_ref, sh_ref, wr_ref, br_ref, h_ref, lg_ref):
    h = _modulated_norm(x_ref[...], g_ref[...], sc_ref[...], sh_ref[...])
    h_ref[...] = _pack_halves(h)
    h_hi = h.astype(BF16)
    h_lo = (h - h_hi.astype(F32)).astype(BF16)
    w = wr_ref[...]
    w_hi = w.astype(BF16)
    w_lo = (w - w_hi.astype(F32)).astype(BF16)
    lg = (jnp.dot(h_hi, w_hi, preferred_element_type=F32)
          + jnp.dot(h_lo, w_hi, preferred_element_type=F32)
          + jnp.dot(h_hi, w_lo, preferred_element_type=F32))
    lg_ref[...] = lg + br_ref[...]


def _norm_router(x2, g, mod3, w_router, b_router, layer, seq, shift_idx, scale_idx):
    t, d = x2.shape
    e = w_router.shape[2]
    tm = NORM_TM

    def mod_spec(idx):
        return pl.BlockSpec((None, 1, d), lambda i: ((i * tm // seq) * N_MOD + idx, 0, 0))

    return pl.pallas_call(
        _norm_router_kernel,
        out_shape=(jax.ShapeDtypeStruct((t, d // 2), U32), jax.ShapeDtypeStruct((t, e), F32)),
        grid=(t // tm,),
        in_specs=[
            pl.BlockSpec((tm, d), lambda i: (i, 0)),
            pl.BlockSpec((None, 1, d), lambda i: (layer, 0, 0)),
            mod_spec(scale_idx),
            mod_spec(shift_idx),
            pl.BlockSpec((None, d, e), lambda i: (layer, 0, 0)),
            pl.BlockSpec((None, 1, e), lambda i: (layer, 0, 0)),
        ],
        out_specs=(pl.BlockSpec((tm, d // 2), lambda i: (i, 0)), pl.BlockSpec((tm, e), lambda i: (i, 0))),
        compiler_params=_params(("arbitrary",)),
        name="norm_router",
    )(x2, g, mod3, mod3, w_router, b_router.reshape(b_router.shape[0], 1, e))


def _proj_kernel(a_ref, w_ref, o_ref):
    o_ref[...] = jnp.dot(a_ref[...], w_ref[...].astype(BF16), preferred_element_type=F32).astype(o_ref.dtype)


def _project(a, w, layer):
    t, k = a.shape
    n = w.shape[2]
    tm, tn = PROJ_TM, PROJ_TN
    return pl.pallas_call(
        _proj_kernel,
        out_shape=jax.ShapeDtypeStruct((t, n), BF16),
        grid=(n // tn, t // tm),
        in_specs=[
            pl.BlockSpec((tm, k), lambda j, i: (i, 0)),
            pl.BlockSpec((None, k, tn), lambda j, i: (layer, 0, j)),
        ],
        out_specs=pl.BlockSpec((tm, tn), lambda j, i: (i, j)),
        compiler_params=_params(("arbitrary", "arbitrary")),
        name="in_proj",
    )(a, w)


def _retention_kernel(q_ref, k_ref, v_ref, g_ref, cos_ref, sin_ref, dec_ref, qd_ref, kd_ref, cd_ref,
                      o_ref, state_ref):
    @pl.when(pl.program_id(2) == 0)
    def _():
        state_ref[...] = jnp.zeros_like(state_ref)

    cos = cos_ref[...]
    sin = sin_ref[...]
    half = RET_QK_DIM // 2
    contract_last = (((1,), (1,)), ((), ()))
    contract_rows = (((0,), (0,)), ((), ()))

    def rotary(t):
        t1, t2 = t[:, :half], t[:, half:]
        return jnp.concatenate([t1 * cos - t2 * sin, t1 * sin + t2 * cos], axis=1)

    for hh in range(RET_HEADS_PER_STEP):
        qk = slice(hh * RET_QK_DIM, (hh + 1) * RET_QK_DIM)
        vg = slice(hh * RET_V_DIM, (hh + 1) * RET_V_DIM)
        q = rotary(q_ref[:, qk].astype(F32))
        k = rotary(k_ref[:, qk].astype(F32)) * (RET_QK_DIM ** -0.5)
        v = v_ref[:, vg]
        scores = lax.dot_general(q.astype(BF16), k.astype(BF16), contract_last,
                                 preferred_element_type=F32) * dec_ref[hh]
        state = state_ref[hh]
        out = (jnp.dot(scores.astype(BF16), v, preferred_element_type=F32)
               + jnp.dot((q * qd_ref[hh]).astype(BF16), state.astype(BF16), preferred_element_type=F32))
        state_ref[hh] = state * cd_ref[hh] + lax.dot_general(
            (k * kd_ref[hh]).astype(BF16), v, contract_rows, preferred_element_type=F32)
        on = out * lax.rsqrt(jnp.mean(out * out, axis=-1, keepdims=True) + NORM_EPS)
        o_ref[:, vg] = (_silu(g_ref[:, vg].astype(F32)) * on).astype(o_ref.dtype)


def _retention_tables(seq):
    half = RET_QK_DIM // 2
    inv_freq = ROPE_BASE ** (-np.arange(half, dtype=np.float64) / half)
    ang = np.arange(seq, dtype=np.float64)[:, None] * inv_freq[None, :]
    log_gamma = np.log(1.0 - 2.0 ** (-5.0 - np.arange(RET_HEADS, dtype=np.float64)))[:, None, None]
    pos = np.arange(RET_BLOCK, dtype=np.float64)
    n, m = pos[:, None], pos[None, :]
    cn, cm = np.floor(n / CHUNK), np.floor(m / CHUNK)
    dist = np.where(cn == cm, np.abs(n - m), n - m)
    dec = np.where((cm <= cn)[None], np.exp(log_gamma * dist[None]), 0.0)
    qd = np.broadcast_to(np.exp(log_gamma * (pos[None, :, None] + 1.0)), (RET_HEADS, RET_BLOCK, RET_QK_DIM))
    kd = np.broadcast_to(np.exp(log_gamma * (RET_BLOCK - 1.0 - pos[None, :, None])),
                         (RET_HEADS, RET_BLOCK, RET_QK_DIM))
    cd = np.broadcast_to(np.exp(log_gamma * RET_BLOCK), (RET_HEADS, 1, RET_V_DIM))
    return tuple(jnp.asarray(np.ascontiguousarray(v), F32) for v in (np.cos(ang), np.sin(ang), dec, qd, kd, cd))


def _retention(proj, batch, seq, col0):
    t = proj.shape[0]
    r = RET_BLOCK
    nsb = seq // r
    cos, sin, dec, qd, kd, cd = _retention_tables(seq)
    hp = RET_HEADS_PER_STEP
    groups = RET_HEADS // hp
    qk_w, v_w = hp * RET_QK_DIM, hp * RET_V_DIM
    assert col0 % qk_w == 0 and (col0 + 2 * RET_QK_W) % v_w == 0
    qk_blk0 = col0 // qk_w
    v_blk0 = (col0 + 2 * RET_QK_W) // v_w
    row = lambda b, h, s: b * nsb + s
    head_tab = lambda shape: pl.BlockSpec((hp,) + shape, lambda b, h, s: (h, 0, 0))
    half = RET_QK_DIM // 2
    return pl.pallas_call(
        _retention_kernel,
        out_shape=jax.ShapeDtypeStruct((t, RET_V_W), BF16),
        grid=(batch, groups, nsb),
        in_specs=[
            pl.BlockSpec((r, qk_w), lambda b, h, s: (row(b, h, s), qk_blk0 + h)),
            pl.BlockSpec((r, qk_w), lambda b, h, s: (row(b, h, s), qk_blk0 + groups + h)),
            pl.BlockSpec((r, v_w), lambda b, h, s: (row(b, h, s), v_blk0 + h)),
            pl.BlockSpec((r, v_w), lambda b, h, s: (row(b, h, s), v_blk0 + groups + h)),
            pl.BlockSpec((r, half), lambda b, h, s: (s, 0)),
            pl.BlockSpec((r, half), lambda b, h, s: (s, 0)),
            head_tab((r, r)),
            head_tab((r, RET_QK_DIM)),
            head_tab((r, RET_QK_DIM)),
            head_tab((1, RET_V_DIM)),
        ],
        out_specs=pl.BlockSpec((r, v_w), lambda b, h, s: (row(b, h, s), h)),
        scratch_shapes=[pltpu.VMEM((hp, RET_QK_DIM, RET_V_DIM), F32)],
        compiler_params=_params(("arbitrary", "arbitrary", "arbitrary")),
        name="retention",
    )(proj, proj, proj, proj, cos, sin, dec, qd, kd, cd)


ATT_KEY_BLOCKS = PREV_CHUNKS * CHUNK // ATT_BLOCK + 1
NEG = float(np.finfo(np.float32).min)


def _attention_kernel(q_ref, k0_ref, k1_ref, k2_ref, v0_ref, v1_ref, v2_ref, tab_ref, gq_ref, gk_ref, o_ref):
    i = pl.program_id(1)

    def rms(t, g):
        tf = t.astype(F32)
        return (tf * lax.rsqrt(jnp.mean(tf * tf, axis=-1, keepdims=True) + NORM_EPS)) * g

    dh = ATT_HEAD_DIM
    k_all = jnp.concatenate([k0_ref[...], k1_ref[...], k2_ref[...]], axis=0)
    v_all = jnp.concatenate([v0_ref[...], v1_ref[...], v2_ref[...]], axis=0)
    outs = []
    for hh in range(ATT_HEADS_PER_STEP):
        lanes = slice(hh * dh, (hh + 1) * dh)
        q = rms(q_ref[:, lanes], gq_ref[...]).astype(BF16)
        k = rms(k_all[:, lanes], gk_ref[...]).astype(BF16)
        sc = lax.dot_general(q, k, (((1,), (1,)), ((), ())), preferred_element_type=F32)
        sc = sc * (ATT_HEAD_DIM ** -0.5) + tab_ref[hh]
        col = lax.broadcasted_iota(jnp.int32, sc.shape, 1)
        sc = jnp.where(col >= (ATT_KEY_BLOCKS - 1 - i) * ATT_BLOCK, sc, NEG)
        p = jnp.exp(sc - jnp.max(sc, axis=-1, keepdims=True))
        denom = jnp.sum(p, axis=-1, keepdims=True)
        outs.append(jnp.dot(p.astype(BF16), v_all[:, lanes], preferred_element_type=F32) / denom)
    o_ref[...] = jnp.concatenate(outs, axis=1).astype(o_ref.dtype)


def _attention_table(rel_bias):
    a = ATT_BLOCK
    w = ATT_KEY_BLOCKS * a
    pad = PREV_CHUNKS * CHUNK
    heads = rel_bias.shape[0]
    period = 1024
    assert w + a - 1 <= period
    i = np.arange(period)
    rel_of_i = np.where(i < w, pad - i, pad + period - i)
    idx = np.clip(np.minimum(rel_of_i, MAX_REL) + (CHUNK - 1), 0, rel_bias.shape[-1] - 1)
    u = rel_bias[:, idx].astype(F32).reshape(heads, 1, period)
    return pl.pallas_call(
        _attention_table_kernel,
        out_shape=jax.ShapeDtypeStruct((heads, a, w), F32),
        grid=(heads,),
        in_specs=[pl.BlockSpec((None, 1, period), lambda h: (h, 0, 0))],
        out_specs=pl.BlockSpec((None, a, w), lambda h: (h, 0, 0)),
        compiler_params=_params(("arbitrary",)),
        name="attention_table",
    )(u)


def _attention_table_kernel(u_ref, o_ref):
    a, w = o_ref.shape
    rows = jnp.broadcast_to(u_ref[...], (a, u_ref.shape[1]))
    skew = pltpu.roll(rows, 0, 1, stride=1, stride_axis=0)
    chunk_bits = CHUNK.bit_length() - 1
    assert CHUNK == 1 << chunk_bits
    qc = lax.broadcasted_iota(jnp.int32, (a, w), 0) >> chunk_bits
    kc = (lax.broadcasted_iota(jnp.int32, (a, w), 1) >> chunk_bits) - PREV_CHUNKS
    valid = (kc <= qc) & (kc >= qc - PREV_CHUNKS)
    o_ref[...] = jnp.where(valid, skew[:, :w], NEG)


def _chunk_attention(proj, rel_bias, q_norm_g, k_norm_g, batch, seq, col0):
    assert ATT_KEY_BLOCKS == 3
    t = proj.shape[0]
    a = ATT_BLOCK
    nqb = seq // a
    dh = ATT_HEAD_DIM
    hw = ATT_HEADS_PER_STEP * dh
    groups = ATT_HEADS // ATT_HEADS_PER_STEP
    assert col0 % hw == 0 and ATT_W % hw == 0
    blk0 = col0 // hw
    tab = _attention_table(rel_bias)

    def kv_spec(which, jj):
        return pl.BlockSpec(
            (a, hw),
            lambda h, i, b: (b * nqb + jnp.maximum(i - (ATT_KEY_BLOCKS - 1) + jj, 0), blk0 + which * groups + h))

    return pl.pallas_call(
        _attention_kernel,
        out_shape=jax.ShapeDtypeStruct((t, ATT_W), BF16),
        grid=(groups, nqb, batch),
        in_specs=[
            pl.BlockSpec((a, hw), lambda h, i, b: (b * nqb + i, blk0 + h)),
            kv_spec(1, 0), kv_spec(1, 1), kv_spec(1, 2),
            kv_spec(2, 0), kv_spec(2, 1), kv_spec(2, 2),
            pl.BlockSpec((ATT_HEADS_PER_STEP, a, ATT_KEY_BLOCKS * a), lambda h, i, b: (h, 0, 0)),
            pl.BlockSpec((1, dh), lambda h, i, b: (0, 0)),
            pl.BlockSpec((1, dh), lambda h, i, b: (0, 0)),
        ],
        out_specs=pl.BlockSpec((a, hw), lambda h, i, b: (b * nqb + i, h)),
        compiler_params=_params(("arbitrary", "arbitrary", "arbitrary")),
        name="chunk_attention",
    )(proj, proj, proj, proj, proj, proj, proj, tab, q_norm_g.reshape(1, dh), k_norm_g.reshape(1, dh))


def _merge_kernel(r_ref, a_ref, wr_ref, wa_ref, gr_ref, ga_ref, o_ref):
    yr = jnp.dot(r_ref[...], wr_ref[...].astype(BF16), preferred_element_type=F32)
    ya = jnp.dot(a_ref[...], wa_ref[...].astype(BF16), preferred_element_type=F32)
    merged = jax.nn.sigmoid(gr_ref[...].astype(F32)) * yr + jax.nn.sigmoid(ga_ref[...].astype(F32)) * ya
    o_ref[...] = merged.astype(o_ref.dtype)


def _merge(ret, att, w_ret_out, w_att_out, proj, gate_col0, layer):
    t = ret.shape[0]
    d = w_ret_out.shape[2]
    tm, tn = MERGE_TM, MERGE_TN
    gblk = gate_col0 // tn
    return pl.pallas_call(
        _merge_kernel,
        out_shape=jax.ShapeDtypeStruct((t, d), BF16),
        grid=(d // tn, t // tm),
        in_specs=[
            pl.BlockSpec((tm, ret.shape[1]), lambda j, i: (i, 0)),
            pl.BlockSpec((tm, att.shape[1]), lambda j, i: (i, 0)),
            pl.BlockSpec((None, ret.shape[1], tn), lambda j, i: (layer, 0, j)),
            pl.BlockSpec((None, att.shape[1], tn), lambda j, i: (layer, 0, j)),
            pl.BlockSpec((tm, tn), lambda j, i: (i, gblk + j)),
            pl.BlockSpec((tm, tn), lambda j, i: (i, gblk + d // tn + j)),
        ],
        out_specs=pl.BlockSpec((tm, tn), lambda j, i: (i, j)),
        compiler_params=_params(("arbitrary", "arbitrary")),
        name="merge",
    )(ret, att, w_ret_out, w_att_out, proj, proj)


def _out_proj_kernel(a_ref, w_ref, x_ref, gate_ref, o_ref):
    acc = jnp.dot(a_ref[...], w_ref[...].astype(BF16), preferred_element_type=F32)
    o_ref[...] = x_ref[...] + gate_ref[...] * acc


def _out_proj(a, w, x2, mod3, layer, seq, gate_idx):
    t, k = a.shape
    d = w.shape[2]
    tm, tn = OUT_TM, OUT_TN
    return pl.pallas_call(
        _out_proj_kernel,
        out_shape=jax.ShapeDtypeStruct((t, d), F32),
        grid=(d // tn, t // tm),
        in_specs=[
            pl.BlockSpec((tm, k), lambda j, i: (i, 0)),
            pl.BlockSpec((None, k, tn), lambda j, i: (layer, 0, j)),
            pl.BlockSpec((tm, tn), lambda j, i: (i, j)),
            pl.BlockSpec((None, 1, tn), lambda j, i: ((i * tm // seq) * N_MOD + gate_idx, 0, j)),
        ],
        out_specs=pl.BlockSpec((tm, tn), lambda j, i: (i, j)),
        compiler_params=_params(("arbitrary", "arbitrary")),
        name="out_proj",
    )(a, w, x2, mod3)


def _routing(logits):
    t = logits.shape[0]
    bm = EXPERT_BM
    a = t * TOP_K
    top_val, top_idx = lax.top_k(logits, TOP_K)
    gate = jax.nn.softmax(top_val, axis=-1)
    onehot = top_idx[:, :, None] == jnp.arange(N_EXPERTS, dtype=jnp.int32)[None, None, :]
    picked = jnp.any(onehot, axis=1).astype(jnp.int32)
    inclusive = jnp.cumsum(picked, axis=0)
    sizes = inclusive[-1]
    sub = EXPERT_SUB
    padded = (sizes + sub - 1) // sub * sub
    pad_end = jnp.cumsum(padded)
    pad_start = pad_end - padded
    start = jnp.cumsum(sizes) - sizes
    slot = (pad_start[None, :] + inclusive - picked)[:, None, :]
    dest = jnp.sum(jnp.where(onehot, slot, 0), axis=-1).astype(jnp.int32).reshape(a)
    order = jnp.argsort(top_idx.reshape(a))
    n_blocks = a // bm + N_EXPERTS
    group_blocks = (sizes + bm - 1) // bm
    blocks_end = jnp.cumsum(group_blocks)
    n_valid = blocks_end[-1].astype(jnp.int32)
    blk = jnp.arange(n_blocks, dtype=jnp.int32)
    block_e = jnp.minimum(jnp.searchsorted(blocks_end, blk, side='right'), N_EXPERTS - 1).astype(jnp.int32)
    chunk = blk - (blocks_end - group_blocks)[block_e]
    block_valid = blk < n_valid
    block_rows = jnp.where(block_valid, jnp.clip(sizes[block_e] - chunk * bm, 0, bm), 0)
    block_nsub = ((block_rows + sub - 1) // sub).astype(jnp.int32)
    block_row0 = jnp.where(block_valid, pad_start[block_e] + chunk * bm, 0).astype(jnp.int32)
    padded_len = (a // TOK_ALIGN) * TOK_ALIGN + TOK_WINDOW
    sorted_tok = jnp.zeros((padded_len,), jnp.int32).at[:a].set((order // TOP_K).astype(jnp.int32))
    block_first = jnp.where(block_valid, start[block_e] + chunk * bm, 0).astype(jnp.int32)
    tables = (block_e, block_nsub, block_first, block_row0, jnp.stack([n_valid, pad_end[-1].astype(jnp.int32)]))
    return gate, dest, sorted_tok, tables


def _expert_kernel(be_ref, ns_ref, first_ref, row0_ref, info_ref, tok_hbm, h_hbm, w1_hbm, b1_ref, w2_hbm, b2_ref,
                   perm_ref, ys_hbm, w1buf, w2buf, gbuf, x_scr, act_scr, obuf, zbuf, tokbuf,
                   sem_w1, sem_w2, sem_g, sem_o, sem_t, *, layer, nj1, nj2):
    sub = EXPERT_SUB
    group = GATHER_GROUP
    bm, d = x_scr.shape
    half = d // 2
    tf = act_scr.shape[2]
    tn = obuf.shape[2]
    steps = nj1 + nj2
    nb_max = be_ref.shape[0]
    n_valid = info_ref[0]
    total = n_valid * steps
    per_step = bm // EXPERT_ISSUE_STEPS

    def for_rows(nsub, fn):
        assert bm // sub < 16
        for size in (8, 4, 2, 1):
            before = (nsub // (2 * size)) * (2 * size)

            @pl.when((nsub & size) != 0)
            def _():
                fn(pl.multiple_of(before * sub, size * sub), size * sub)

    def w1_copy(e, j, slot):
        cols = pl.ds(pl.multiple_of(j * 2 * tf, 2 * tf), 2 * tf)
        return pltpu.make_async_copy(w1_hbm.at[layer, e, :, cols], w1buf.at[slot], sem_w1.at[slot])

    def w2_copies(e, n, slot):
        return [pltpu.make_async_copy(w2_hbm.at[layer, e, :, pl.ds(pl.multiple_of(c0, tn), tn)],
                                      w2buf.at[slot, :, pl.ds(k * tn, tn)], sem_w2.at[slot])
                for k, c0 in enumerate((n * tn, half + n * tn))]

    def tok_copy(blk, slot):
        src = tok_hbm.at[pl.ds(pl.multiple_of(first_ref[blk] // TOK_ALIGN * TOK_ALIGN, TOK_ALIGN), TOK_WINDOW)]
        dst = tokbuf.at[pl.ds(pl.multiple_of(slot * TOK_WINDOW, TOK_ALIGN), TOK_WINDOW)]
        return pltpu.make_async_copy(src, dst, sem_t.at[slot])

    def tok_base(blk, slot):
        return slot * TOK_WINDOW + lax.rem(first_ref[blk], TOK_ALIGN)

    def row_copy(tok_at, tile_row, sublane):
        src = h_hbm.at[pl.ds(tokbuf[tok_at], 1)]
        return pltpu.make_async_copy(src, gbuf.at[tile_row, pl.ds(sublane, 1)], sem_g)

    def out_copy(slot, r0, m, row0, n):
        dst = ys_hbm.at[pl.ds(pl.multiple_of(row0 + r0, sub), m), pl.ds(pl.multiple_of(n * tn, tn), tn)]
        return pltpu.make_async_copy(obuf.at[slot, pl.ds(r0, m)], dst, sem_o.at[slot])

    @pl.when(n_valid > 0)
    def _():
        tok_copy(0, 0).start()
        tok_copy(0, 0).wait()
        base = tok_base(0, 0)

        def start(q, carry):
            for s in range(SUBLANES):
                row_copy(base + q * SUBLANES + s, q, s).start()
            return carry
        lax.fori_loop(0, ns_ref[0] * (sub // SUBLANES), start, 0)
        w1_copy(be_ref[0], 0, 0).start()

    def item(it, carry):
        b = it // steps
        j = it - b * steps
        e = be_ref[b]
        nsub = ns_ref[b]
        has_next = b + 1 < n_valid
        nxt = jnp.minimum(b + 1, nb_max - 1)

        nit = it + 1
        nbk = jnp.minimum(nit // steps, nb_max - 1)
        nj = nit - (nit // steps) * steps

        @pl.when((nit < total) & (nj < nj1))
        def _():
            w1_copy(be_ref[nbk], nj, lax.rem(nj, 2)).start()

        @pl.when((nit < total) & (nj >= nj1))
        def _():
            for c in w2_copies(be_ref[nbk], nj - nj1, lax.rem(nj - nj1, 2)):
                c.start()

        @pl.when(j < nj1)
        def _():
            w1_copy(e, j, lax.rem(j, 2)).wait()

        @pl.when(j >= nj1)
        def _():
            for c in w2_copies(e, j - nj1, lax.rem(j - nj1, 2)):
                c.wait()

        @pl.when(j == 0)
        def _():
            def wait_rows(s, carry):
                pltpu.make_async_copy(h_hbm.at[pl.ds(0, sub)], zbuf, sem_g).wait()
                return carry
            lax.fori_loop(0, nsub, wait_rows, 0)

            def unpack_rows(s, carry):
                rows = pl.ds(pl.multiple_of(s * sub, sub), sub)
                tiles = pl.ds(pl.multiple_of(s * (sub // SUBLANES), sub // SUBLANES), sub // SUBLANES)
                xa, xb = _unpack_halves(gbuf[tiles].reshape(sub, half))
                x_scr[rows, pl.ds(0, half)] = xa.astype(BF16)
                x_scr[rows, pl.ds(half, half)] = xb.astype(BF16)
                return carry
            lax.fori_loop(0, nsub, unpack_rows, 0)

            @pl.when(has_next)
            def _():
                tok_copy(nxt, lax.rem(b + 1, 2)).start()

        @pl.when((j == 1) & has_next)
        def _():
            tok_copy(nxt, lax.rem(b + 1, 2)).wait()

        @pl.when((j >= 1) & (j <= EXPERT_ISSUE_STEPS) & has_next)
        def _():
            base = tok_base(nxt, lax.rem(b + 1, 2))
            rows_next = ns_ref[nxt] * sub
            for g in range(per_step // group):
                first_row = (j - 1) * per_step + g * group
                first_tile = (j - 1) * (per_step // SUBLANES) + g * (group // SUBLANES)

                @pl.when(first_row < rows_next)
                def _():
                    for i in range(group):
                        row_copy(base + first_row + i, first_tile + i // SUBLANES, i % SUBLANES).start()

        @pl.when(j < nj1)
        def _():
            wslot = lax.rem(j, 2)
            bias = b1_ref[pl.ds(e * nj1 + j, 1), :]

            def first_matmul(r0, m):
                rows = pl.ds(r0, m)
                hb = jnp.dot(x_scr[rows, :], w1buf[wslot].astype(BF16), preferred_element_type=F32) + bias
                hb = hb.astype(BF16)
                gw = perm_ref.shape[0]
                sel = [jnp.dot(hb[:, c * gw:(c + 1) * gw], perm_ref[...], preferred_element_type=F32)
                       for c in range(2 * tf // gw)]
                x_glu = jnp.minimum(jnp.concatenate([s[:, :gw // 2] for s in sel], axis=1), SWIGLU_LIMIT)
                x_lin = jnp.clip(jnp.concatenate([s[:, gw // 2:] for s in sel], axis=1),
                                 -SWIGLU_LIMIT, SWIGLU_LIMIT)
                act = x_glu * jax.nn.sigmoid(SWIGLU_ALPHA * x_glu) * (x_lin + 1.0)
                act_scr[j, rows, :] = act.astype(BF16)
            for_rows(nsub, first_matmul)

        @pl.when(j >= nj1)
        def _():
            n = j - nj1
            slot = lax.rem(n, 2)
            row0 = row0_ref[b]
            bias = jnp.concatenate([b2_ref[pl.ds(e * 2 * nj2 + n, 1), :],
                                    b2_ref[pl.ds(e * 2 * nj2 + nj2 + n, 1), :]], axis=1)
            sent = jnp.where(n >= 2, nsub, jnp.where(b > 0, ns_ref[jnp.maximum(b - 1, 0)], 0))
            for_rows(sent, lambda r0, m: out_copy(slot, r0, m, 0, 0).wait())

            def second_matmul(r0, m):
                rows = pl.ds(r0, m)
                act = jnp.concatenate([act_scr[t, rows, :] for t in range(nj1)], axis=1)
                y = jnp.dot(act, w2buf[slot].astype(BF16), preferred_element_type=F32) + bias
                obuf[slot, rows, :] = _pack_halves(y)
                out_copy(slot, r0, m, row0, n).start()
            for_rows(nsub, second_matmul)

        return carry

    lax.fori_loop(0, total, item, 0)

    @pl.when(n_valid > 0)
    def _():
        last = ns_ref[jnp.maximum(n_valid - 1, 0)]
        for slot in (0, 1):
            for_rows(last, lambda r0, m: out_copy(slot, r0, m, 0, 0).wait())

    used = info_ref[1]
    n_tail = (ys_hbm.shape[0] - used) // sub
    zbuf[...] = jnp.zeros_like(zbuf)

    def tail_copy(i):
        dst = ys_hbm.at[pl.ds(pl.multiple_of(used + i * sub, sub), sub)]
        return pltpu.make_async_copy(zbuf, dst, sem_g)

    def tail_start(i, carry):
        tail_copy(i).start()
        return carry

    def tail_wait(i, carry):
        tail_copy(i).wait()
        return carry
    lax.fori_loop(0, n_tail, tail_start, 0)
    lax.fori_loop(0, n_tail, tail_wait, 0)


def _deinterleave_matrix(tf):
    p = np.zeros((2 * tf, 2 * tf), np.float32)
    f = np.arange(tf)
    p[2 * f, f] = 1.0
    p[2 * f + 1, tf + f] = 1.0
    return jnp.asarray(p, BF16)


def _experts(h_packed, sorted_tok, tables, w1, b1, w2, b2, layer):
    t, half = h_packed.shape
    d = 2 * half
    n_experts, f = w2.shape[1], w2.shape[2]
    bm, tf, tn = EXPERT_BM, EXPERT_TF, EXPERT_TN
    nj1 = f // tf
    nj2 = half // tn
    assert nj1 % 2 == 0 and nj2 % 2 == 0
    assert 1 + EXPERT_ISSUE_STEPS <= nj1 + nj2 and bm % (EXPERT_ISSUE_STEPS * GATHER_GROUP) == 0
    assert bm % EXPERT_SUB == 0 and t >= EXPERT_SUB and GATHER_GROUP % SUBLANES == 0
    assert bm + TOK_ALIGN - 1 <= TOK_WINDOW
    p_rows = t * TOP_K + n_experts * EXPERT_SUB
    whole = lambda shape: pl.BlockSpec(shape, lambda i, *_: (0,) * len(shape))
    hbm = pl.BlockSpec(memory_space=pl.ANY)
    return pl.pallas_call(
        functools.partial(_expert_kernel, layer=layer, nj1=nj1, nj2=nj2),
        out_shape=jax.ShapeDtypeStruct((p_rows, half), U32),
        grid_spec=pltpu.PrefetchScalarGridSpec(
            num_scalar_prefetch=5,
            grid=(1,),
            in_specs=[hbm, hbm, hbm, whole((n_experts * nj1, 2 * tf)), hbm, whole((n_experts * 2 * nj2, tn)),
                      whole((EXPERT_PERM_W, EXPERT_PERM_W))],
            out_specs=hbm,
            scratch_shapes=[
                pltpu.VMEM((2, d, 2 * tf), F32),
                pltpu.VMEM((2, f, 2 * tn), F32),
                pltpu.VMEM((bm // SUBLANES, SUBLANES, half), U32),
                pltpu.VMEM((bm, d), BF16),
                pltpu.VMEM((nj1, bm, tf), BF16),
                pltpu.VMEM((2, bm, tn), U32),
                pltpu.VMEM((EXPERT_SUB, half), U32),
                pltpu.SMEM((2 * TOK_WINDOW,), jnp.int32),
                pltpu.SemaphoreType.DMA((2,)),
                pltpu.SemaphoreType.DMA((2,)),
                pltpu.SemaphoreType.DMA(()),
                pltpu.SemaphoreType.DMA((2,)),
                pltpu.SemaphoreType.DMA((2,)),
            ],
        ),
        compiler_params=_params(("arbitrary",)),
        name="experts",
    )(*tables, sorted_tok, h_packed, w1, b1[layer].reshape(n_experts * nj1, 2 * tf), w2,
      b2[layer].reshape(n_experts * 2 * nj2, tn), _deinterleave_matrix(EXPERT_PERM_W // 2))


def _combine_kernel(dest_hbm, ys_hbm, x_ref, gate_ref, g_ref, o_ref, buf, idx, sem, sem_idx):
    i = pl.program_id(0)
    tm = x_ref.shape[0]
    half = buf.shape[3]
    sublanes = SUBLANES
    per_step = tm * TOP_K
    slot = lax.rem(i, 2)

    def idx_copy(step, s):
        src = dest_hbm.at[pl.ds(pl.multiple_of(step * per_step, per_step), per_step)]
        return pltpu.make_async_copy(src, idx.at[pl.ds(pl.multiple_of(s * per_step, per_step), per_step)],
                                     sem_idx.at[s])

    @pl.when(i == 0)
    def _():
        idx_copy(0, 0).start()

    idx_copy(i, slot).wait()

    @pl.when(i + 1 < pl.num_programs(0))
    def _():
        idx_copy(i + 1, 1 - slot).start()

    def row_copy(src_row, k, tile, s):
        return pltpu.make_async_copy(ys_hbm.at[pl.ds(src_row, 1)], buf.at[k, tile, pl.ds(s, 1)], sem)

    def start(tile, carry):
        base = slot * per_step + tile * (sublanes * TOP_K)
        for s in range(sublanes):
            for k in range(TOP_K):
                row_copy(idx[base + s * TOP_K + k], k, tile, s).start()
        return carry

    def wait(tile, carry):
        for _ in range(sublanes * TOP_K):
            row_copy(0, 0, 0, 0).wait()
        return carry

    lax.fori_loop(0, tm // sublanes, start, 0)
    lax.fori_loop(0, tm // sublanes, wait, 0)

    def chunk(c, carry):
        rows = pl.ds(pl.multiple_of(c * sublanes, sublanes), sublanes)
        g = g_ref[rows, :]
        ya = jnp.zeros((sublanes, half), F32)
        yb = jnp.zeros((sublanes, half), F32)
        for k in range(TOP_K):
            a, b = _unpack_halves(buf[k, c])
            ya = ya + g[:, k:k + 1] * a
            yb = yb + g[:, k:k + 1] * b
        o_ref[rows, pl.ds(0, half)] = x_ref[rows, pl.ds(0, half)] + gate_ref[:, pl.ds(0, half)] * ya
        o_ref[rows, pl.ds(half, half)] = x_ref[rows, pl.ds(half, half)] + gate_ref[:, pl.ds(half, half)] * yb
        return carry

    lax.fori_loop(0, tm // sublanes, chunk, 0)


def _combine(ys, dest, gates, x2, mod3, seq, gate_idx):
    t, d = x2.shape
    tm = COMBINE_TM
    half = ys.shape[1]
    assert (tm * TOP_K) % TOK_ALIGN == 0
    return pl.pallas_call(
        _combine_kernel,
        out_shape=jax.ShapeDtypeStruct((t, d), F32),
        grid=(t // tm,),
        in_specs=[
            pl.BlockSpec(memory_space=pl.ANY),
            pl.BlockSpec(memory_space=pl.ANY),
            pl.BlockSpec((tm, d), lambda i: (i, 0)),
            pl.BlockSpec((None, 1, d), lambda i: ((i * tm // seq) * N_MOD + gate_idx, 0, 0)),
            pl.BlockSpec((tm, TOP_K), lambda i: (i, 0)),
        ],
        out_specs=pl.BlockSpec((tm, d), lambda i: (i, 0)),
        scratch_shapes=[
            pltpu.VMEM((TOP_K, tm // SUBLANES, SUBLANES, half), U32),
            pltpu.SMEM((2 * tm * TOP_K,), jnp.int32),
            pltpu.SemaphoreType.DMA(()),
            pltpu.SemaphoreType.DMA((2,)),
        ],
        compiler_params=_params(("arbitrary",)),
        name="combine",
    )(dest, ys, x2, mod3, gates)


def kernel(x, c, w_ada, b_ada, norm1_g, w_in, q_norm_g, k_norm_g, rel_bias, w_ret_out, w_att_out, w_out,
           norm2_g, w_router, b_router, w1, b1, w2, b2):
    batch, seq, d = x.shape
    depth = w_ada.shape[0]
    t = batch * seq
    assert batch <= SUBLANES and t % PROJ_TM == 0
    assert seq % max(RET_BLOCK, ATT_BLOCK, NORM_TM, MERGE_TM, OUT_TM, COMBINE_TM) == 0
    x2 = x.reshape(t, d)
    c8 = jnp.zeros((SUBLANES, d), F32).at[:batch].set(c)
    ret_col0 = 0
    att_col0 = 2 * RET_QK_W + 2 * RET_V_W
    gate_col0 = att_col0 + 3 * ATT_W
    for layer in range(depth):
        mod3 = _ada_ln(c8, w_ada, b_ada, layer)[:batch].reshape(batch * N_MOD, 1, d)
        h = _norm_mod(x2, norm1_g.reshape(depth, 1, d), mod3, layer, seq, shift_idx=0, scale_idx=1)
        proj = _project(h, w_in, layer)
        ret = _retention(proj, batch, seq, ret_col0)
        att = _chunk_attention(proj, rel_bias[layer], q_norm_g[layer], k_norm_g[layer], batch, seq, att_col0)
        merged = _merge(ret, att, w_ret_out, w_att_out, proj, gate_col0, layer)
        x2 = _out_proj(merged, w_out, x2, mod3, layer, seq, gate_idx=2)
        h_packed, logits = _norm_router(x2, norm2_g.reshape(depth, 1, d), mod3, w_router, b_router, layer, seq,
                                        shift_idx=3, scale_idx=4)
        gates, dest, sorted_tok, tables = _routing(logits)
        ys = _experts(h_packed, sorted_tok, tables, w1, b1, w2, b2, layer)
        x2 = _combine(ys, dest, gates, x2, mod3, seq, gate_idx=5)
    return x2.reshape(batch, seq, d)
```

```python
import functools

import numpy as np
import jax
import jax.numpy as jnp
from jax import lax
from jax.experimental import pallas as pl
from jax.experimental.pallas import tpu as pltpu

F32 = jnp.float32
BF16 = jnp.bfloat16
U32 = jnp.uint32

CHUNK = 64
NORM_EPS = 1e-6
RET_HEADS = 8
RET_QK_DIM = 256
RET_V_DIM = 512
ROPE_BASE = 10000.0
ATT_HEADS = 16
ATT_HEAD_DIM = 128
PREV_CHUNKS = 8
MAX_REL = 128
N_EXPERTS = 32
TOP_K = 4
SWIGLU_ALPHA = 1.702
SWIGLU_LIMIT = 7.0
N_MOD = 6

RET_QK_W = RET_HEADS * RET_QK_DIM
RET_V_W = RET_HEADS * RET_V_DIM
ATT_W = ATT_HEADS * ATT_HEAD_DIM

V7X_VMEM_LIMIT_BYTES = 58 * 1024 * 1024
HI16 = 0xFFFF0000

ADA_TN = 1024
NORM_TM = 512
PROJ_TM, PROJ_TN = 2048, 512
RET_BLOCK = 256
RET_HEADS_PER_STEP = 8
ATT_BLOCK = 256
ATT_HEADS_PER_STEP = 8
MERGE_TM, MERGE_TN = 1024, 512
OUT_TM, OUT_TN = 1024, 512
EXPERT_BM = 1280
EXPERT_SUB = 128
EXPERT_TF = 256
EXPERT_TN = 256
EXPERT_PERM_W = 256
EXPERT_ISSUE_STEPS = 10
GATHER_GROUP = 32
TOK_ALIGN = 1024
TOK_WINDOW = 3 * TOK_ALIGN
SUBLANES = 8
COMBINE_TM = 256


def _params(semantics):
    return pltpu.CompilerParams(dimension_semantics=semantics, vmem_limit_bytes=V7X_VMEM_LIMIT_BYTES)


def _silu(t):
    return t * jax.nn.sigmoid(t)


def _pack_halves(y):
    n = y.shape[1] // 2
    hi = lax.bitcast_convert_type(y[:, :n].astype(BF16).astype(F32), U32)
    lo = lax.bitcast_convert_type(y[:, n:].astype(BF16).astype(F32), U32)
    return hi | (lo >> 16)


def _unpack_halves(p):
    a = lax.bitcast_convert_type(p & jnp.uint32(HI16), F32)
    b = lax.bitcast_convert_type(p << 16, F32)
    return a, b


def _ada_kernel(c_ref, w_ref, b_ref, o_ref):
    a = _silu(c_ref[...]).astype(BF16)
    o_ref[...] = jnp.dot(a, w_ref[...].astype(BF16), preferred_element_type=F32) + b_ref[...]


def _ada_ln(c8, w_ada, b_ada, layer):
    rows, d = c8.shape
    n = w_ada.shape[2]
    return pl.pallas_call(
        _ada_kernel,
        out_shape=jax.ShapeDtypeStruct((rows, n), F32),
        grid=(n // ADA_TN,),
        in_specs=[
            pl.BlockSpec((rows, d), lambda j: (0, 0)),
            pl.BlockSpec((None, d, ADA_TN), lambda j: (layer, 0, j)),
            pl.BlockSpec((None, 1, ADA_TN), lambda j: (layer, 0, j)),
        ],
        out_specs=pl.BlockSpec((rows, ADA_TN), lambda j: (0, j)),
        compiler_params=_params(("arbitrary",)),
        name="ada_ln",
    )(c8, w_ada, b_ada.reshape(b_ada.shape[0], 1, n))


def _modulated_norm(x, g, scale, shift):
    xn = x * lax.rsqrt(jnp.mean(x * x, axis=-1, keepdims=True) + NORM_EPS)
    return (xn * g) * (1.0 + scale) + shift


def _norm_mod_kernel(x_ref, g_ref, sc_ref, sh_ref, o_ref):
    o_ref[...] = _modulated_norm(x_ref[...], g_ref[...], sc_ref[...], sh_ref[...]).astype(o_ref.dtype)


def _norm_mod(x2, g, mod3, layer, seq, shift_idx, scale_idx):
    t, d = x2.shape
    tm = NORM_TM

    def mod_spec(idx):
        return pl.BlockSpec((None, 1, d), lambda i: ((i * tm // seq) * N_MOD + idx, 0, 0))

    return pl.pallas_call(
        _norm_mod_kernel,
        out_shape=jax.ShapeDtypeStruct((t, d), BF16),
        grid=(t // tm,),
        in_specs=[
            pl.BlockSpec((tm, d), lambda i: (i, 0)),
            pl.BlockSpec((None, 1, d), lambda i: (layer, 0, 0)),
            mod_spec(scale_idx),
            mod_spec(shift_idx),
        ],
        out_specs=pl.BlockSpec((tm, d), lambda i: (i, 0)),
        compiler_params=_params(("arbitrary",)),
        name="norm_mod",
    )(x2, g, mod3, mod3)


def _norm_router_kernel(x_ref, g_ref, sc_ref, sh_ref, wr_ref, br_ref, h_ref, lg_ref):
    h = _modulated_norm(x_ref[...], g_ref[...], sc_ref[...], sh_ref[...])
    h_ref[...] = _pack_halves(h)
    h_hi = h.astype(BF16)
    h_lo = (h - h_hi.astype(F32)).astype(BF16)
    w = wr_ref[...]
    w_hi = w.astype(BF16)
    w_lo = (w - w_hi.astype(F32)).astype(BF16)
    e = w.shape[1]
    both = jnp.dot(h_hi, jnp.concatenate([w_hi, w_lo], axis=1), preferred_element_type=F32)
    lg = both[:, :e] + jnp.dot(h_lo, w_hi, preferred_element_type=F32) + both[:, e:]
    lg_ref[...] = lg + br_ref[...]


def _norm_router(x2, g, mod3, w_router, b_router, layer, seq, shift_idx, scale_idx):
    t, d = x2.shape
    e = w_router.shape[2]
    tm = NORM_TM

    def mod_spec(idx):
        return pl.BlockSpec((None, 1, d), lambda i: ((i * tm // seq) * N_MOD + idx, 0, 0))

    return pl.pallas_call(
        _norm_router_kernel,
        out_shape=(jax.ShapeDtypeStruct((t, d // 2), U32), jax.ShapeDtypeStruct((t, e), F32)),
        grid=(t // tm,),
        in_specs=[
            pl.BlockSpec((tm, d), lambda i: (i, 0)),
            pl.BlockSpec((None, 1, d), lambda i: (layer, 0, 0)),
            mod_spec(scale_idx),
            mod_spec(shift_idx),
            pl.BlockSpec((None, d, e), lambda i: (layer, 0, 0)),
            pl.BlockSpec((None, 1, e), lambda i: (layer, 0, 0)),
        ],
        out_specs=(pl.BlockSpec((tm, d // 2), lambda i: (i, 0)), pl.BlockSpec((tm, e), lambda i: (i, 0))),
        compiler_params=_params(("arbitrary",)),
        name="norm_router",
    )(x2, g, mod3, mod3, w_router, b_router.reshape(b_router.shape[0], 1, e))


def _proj_kernel(a_ref, w_ref, o_ref):
    o_ref[...] = jnp.dot(a_ref[...], w_ref[...].astype(BF16), preferred_element_type=F32).astype(o_ref.dtype)


def _project(a, w, layer):
    t, k = a.shape
    n = w.shape[2]
    tm, tn = PROJ_TM, PROJ_TN
    return pl.pallas_call(
        _proj_kernel,
        out_shape=jax.ShapeDtypeStruct((t, n), BF16),
        grid=(n // tn, t // tm),
        in_specs=[
            pl.BlockSpec((tm, k), lambda j, i: (i, 0)),
            pl.BlockSpec((None, k, tn), lambda j, i: (layer, 0, j)),
        ],
        out_specs=pl.BlockSpec((tm, tn), lambda j, i: (i, j)),
        compiler_params=_params(("arbitrary", "arbitrary")),
        name="in_proj",
    )(a, w)


def _retention_kernel(q_ref, k_ref, v_ref, g_ref, cos_ref, sin_ref, dec_ref, qd_ref, kd_ref, cd_ref,
                      o_ref, state_ref):
    @pl.when(pl.program_id(2) == 0)
    def _():
        state_ref[...] = jnp.zeros_like(state_ref)

    cos = cos_ref[...]
    sin = sin_ref[...]
    half = RET_QK_DIM // 2
    contract_last = (((1,), (1,)), ((), ()))
    contract_rows = (((0,), (0,)), ((), ()))

    def rotary(t):
        t1, t2 = t[:, :half], t[:, half:]
        return jnp.concatenate([t1 * cos - t2 * sin, t1 * sin + t2 * cos], axis=1)

    for hh in range(RET_HEADS_PER_STEP):
        qk = slice(hh * RET_QK_DIM, (hh + 1) * RET_QK_DIM)
        vg = slice(hh * RET_V_DIM, (hh + 1) * RET_V_DIM)
        q = rotary(q_ref[:, qk].astype(F32))
        k = rotary(k_ref[:, qk].astype(F32)) * (RET_QK_DIM ** -0.5)
        v = v_ref[:, vg]
        scores = lax.dot_general(q.astype(BF16), k.astype(BF16), contract_last,
                                 preferred_element_type=F32) * dec_ref[hh]
        state = state_ref[hh]
        out = (jnp.dot(scores.astype(BF16), v, preferred_element_type=F32)
               + jnp.dot((q * qd_ref[hh]).astype(BF16), state.astype(BF16), preferred_element_type=F32))
        state_ref[hh] = state * cd_ref[hh] + lax.dot_general(
            (k * kd_ref[hh]).astype(BF16), v, contract_rows, preferred_element_type=F32)
        on = out * lax.rsqrt(jnp.mean(out * out, axis=-1, keepdims=True) + NORM_EPS)
        o_ref[:, vg] = (_silu(g_ref[:, vg].astype(F32)) * on).astype(o_ref.dtype)


def _retention_tables(seq):
    half = RET_QK_DIM // 2
    inv_freq = ROPE_BASE ** (-np.arange(half, dtype=np.float64) / half)
    ang = np.arange(seq, dtype=np.float64)[:, None] * inv_freq[None, :]
    log_gamma = np.log(1.0 - 2.0 ** (-5.0 - np.arange(RET_HEADS, dtype=np.float64)))[:, None, None]
    pos = np.arange(RET_BLOCK, dtype=np.float64)
    n, m = pos[:, None], pos[None, :]
    cn, cm = np.floor(n / CHUNK), np.floor(m / CHUNK)
    dist = np.where(cn == cm, np.abs(n - m), n - m)
    dec = np.where((cm <= cn)[None], np.exp(log_gamma * dist[None]), 0.0)
    qd = np.broadcast_to(np.exp(log_gamma * (pos[None, :, None] + 1.0)), (RET_HEADS, RET_BLOCK, RET_QK_DIM))
    kd = np.broadcast_to(np.exp(log_gamma * (RET_BLOCK - 1.0 - pos[None, :, None])),
                         (RET_HEADS, RET_BLOCK, RET_QK_DIM))
    cd = np.broadcast_to(np.exp(log_gamma * RET_BLOCK), (RET_HEADS, 1, RET_V_DIM))
    return tuple(jnp.asarray(np.ascontiguousarray(v), F32) for v in (np.cos(ang), np.sin(ang), dec, qd, kd, cd))


def _retention(proj, batch, seq, col0):
    t = proj.shape[0]
    r = RET_BLOCK
    nsb = seq // r
    cos, sin, dec, qd, kd, cd = _retention_tables(seq)
    hp = RET_HEADS_PER_STEP
    groups = RET_HEADS // hp
    qk_w, v_w = hp * RET_QK_DIM, hp * RET_V_DIM
    assert col0 % qk_w == 0 and (col0 + 2 * RET_QK_W) % v_w == 0
    qk_blk0 = col0 // qk_w
    v_blk0 = (col0 + 2 * RET_QK_W) // v_w
    row = lambda b, h, s: b * nsb + s
    head_tab = lambda shape: pl.BlockSpec((hp,) + shape, lambda b, h, s: (h, 0, 0))
    half = RET_QK_DIM // 2
    return pl.pallas_call(
        _retention_kernel,
        out_shape=jax.ShapeDtypeStruct((t, RET_V_W), BF16),
        grid=(batch, groups, nsb),
        in_specs=[
            pl.BlockSpec((r, qk_w), lambda b, h, s: (row(b, h, s), qk_blk0 + h)),
            pl.BlockSpec((r, qk_w), lambda b, h, s: (row(b, h, s), qk_blk0 + groups + h)),
            pl.BlockSpec((r, v_w), lambda b, h, s: (row(b, h, s), v_blk0 + h)),
            pl.BlockSpec((r, v_w), lambda b, h, s: (row(b, h, s), v_blk0 + groups + h)),
            pl.BlockSpec((r, half), lambda b, h, s: (s, 0)),
            pl.BlockSpec((r, half), lambda b, h, s: (s, 0)),
            head_tab((r, r)),
            head_tab((r, RET_QK_DIM)),
            head_tab((r, RET_QK_DIM)),
            head_tab((1, RET_V_DIM)),
        ],
        out_specs=pl.BlockSpec((r, v_w), lambda b, h, s: (row(b, h, s), h)),
        scratch_shapes=[pltpu.VMEM((hp, RET_QK_DIM, RET_V_DIM), F32)],
        compiler_params=_params(("arbitrary", "arbitrary", "arbitrary")),
        name="retention",
    )(proj, proj, proj, proj, cos, sin, dec, qd, kd, cd)


ATT_KEY_BLOCKS = PREV_CHUNKS * CHUNK // ATT_BLOCK + 1
NEG = float(np.finfo(np.float32).min)


def _attention_kernel(q_ref, k0_ref, k1_ref, k2_ref, v0_ref, v1_ref, v2_ref, tab_ref, gq_ref, gk_ref, o_ref):
    i = pl.program_id(1)

    def rms(t, g):
        tf = t.astype(F32)
        return (tf * lax.rsqrt(jnp.mean(tf * tf, axis=-1, keepdims=True) + NORM_EPS)) * g

    dh = ATT_HEAD_DIM
    k_all = jnp.concatenate([k0_ref[...], k1_ref[...], k2_ref[...]], axis=0)
    v_all = jnp.concatenate([v0_ref[...], v1_ref[...], v2_ref[...]], axis=0)
    outs = []
    for hh in range(ATT_HEADS_PER_STEP):
        lanes = slice(hh * dh, (hh + 1) * dh)
        q = rms(q_ref[:, lanes], gq_ref[...]).astype(BF16)
        k = rms(k_all[:, lanes], gk_ref[...]).astype(BF16)
        sc = lax.dot_general(q, k, (((1,), (1,)), ((), ())), preferred_element_type=F32)
        sc = sc * (ATT_HEAD_DIM ** -0.5) + tab_ref[hh]
        col = lax.broadcasted_iota(jnp.int32, sc.shape, 1)
        sc = jnp.where(col >= (ATT_KEY_BLOCKS - 1 - i) * ATT_BLOCK, sc, NEG)
        p = jnp.exp(sc - jnp.max(sc, axis=-1, keepdims=True))
        denom = jnp.sum(p, axis=-1, keepdims=True)
        outs.append(jnp.dot(p.astype(BF16), v_all[:, lanes], preferred_element_type=F32) / denom)
    o_ref[...] = jnp.concatenate(outs, axis=1).astype(o_ref.dtype)


def _attention_table(rel_bias):
    a = ATT_BLOCK
    w = ATT_KEY_BLOCKS * a
    pad = PREV_CHUNKS * CHUNK
    heads = rel_bias.shape[0]
    period = 1024
    assert w + a - 1 <= period
    i = np.arange(period)
    rel_of_i = np.where(i < w, pad - i, pad + period - i)
    idx = np.clip(np.minimum(rel_of_i, MAX_REL) + (CHUNK - 1), 0, rel_bias.shape[-1] - 1)
    u = rel_bias[:, idx].astype(F32).reshape(heads, 1, period)
    return pl.pallas_call(
        _attention_table_kernel,
        out_shape=jax.ShapeDtypeStruct((heads, a, w), F32),
        grid=(heads,),
        in_specs=[pl.BlockSpec((None, 1, period), lambda h: (h, 0, 0))],
        out_specs=pl.BlockSpec((None, a, w), lambda h: (h, 0, 0)),
        compiler_params=_params(("arbitrary",)),
        name="attention_table",
    )(u)


def _attention_table_kernel(u_ref, o_ref):
    a, w = o_ref.shape
    rows = jnp.broadcast_to(u_ref[...], (a, u_ref.shape[1]))
    skew = pltpu.roll(rows, 0, 1, stride=1, stride_axis=0)
    chunk_bits = CHUNK.bit_length() - 1
    assert CHUNK == 1 << chunk_bits
    qc = lax.broadcasted_iota(jnp.int32, (a, w), 0) >> chunk_bits
    kc = (lax.broadcasted_iota(jnp.int32, (a, w), 1) >> chunk_bits) - PREV_CHUNKS
    valid = (kc <= qc) & (kc >= qc - PREV_CHUNKS)
    o_ref[...] = jnp.where(valid, skew[:, :w], NEG)


def _chunk_attention(proj, rel_bias, q_norm_g, k_norm_g, batch, seq, col0):
    assert ATT_KEY_BLOCKS == 3
    t = proj.shape[0]
    a = ATT_BLOCK
    nqb = seq // a
    dh = ATT_HEAD_DIM
    hw = ATT_HEADS_PER_STEP * dh
    groups = ATT_HEADS // ATT_HEADS_PER_STEP
    assert col0 % hw == 0 and ATT_W % hw == 0
    blk0 = col0 // hw
    tab = _attention_table(rel_bias)

    def kv_spec(which, jj):
        return pl.BlockSpec(
            (a, hw),
            lambda h, i, b: (b * nqb + jnp.maximum(i - (ATT_KEY_BLOCKS - 1) + jj, 0), blk0 + which * groups + h))

    return pl.pallas_call(
        _attention_kernel,
        out_shape=jax.ShapeDtypeStruct((t, ATT_W), BF16),
        grid=(groups, nqb, batch),
        in_specs=[
            pl.BlockSpec((a, hw), lambda h, i, b: (b * nqb + i, blk0 + h)),
            kv_spec(1, 0), kv_spec(1, 1), kv_spec(1, 2),
            kv_spec(2, 0), kv_spec(2, 1), kv_spec(2, 2),
            pl.BlockSpec((ATT_HEADS_PER_STEP, a, ATT_KEY_BLOCKS * a), lambda h, i, b: (h, 0, 0)),
            pl.BlockSpec((1, dh), lambda h, i, b: (0, 0)),
            pl.BlockSpec((1, dh), lambda h, i, b: (0, 0)),
        ],
        out_specs=pl.BlockSpec((a, hw), lambda h, i, b: (b * nqb + i, h)),
        compiler_params=_params(("arbitrary", "arbitrary", "arbitrary")),
        name="chunk_attention",
    )(proj, proj, proj, proj, proj, proj, proj, tab, q_norm_g.reshape(1, dh), k_norm_g.reshape(1, dh))


def _merge_kernel(r_ref, a_ref, wr_ref, wa_ref, gr_ref, ga_ref, o_ref):
    yr = jnp.dot(r_ref[...], wr_ref[...].astype(BF16), preferred_element_type=F32)
    ya = jnp.dot(a_ref[...], wa_ref[...].astype(BF16), preferred_element_type=F32)
    merged = jax.nn.sigmoid(gr_ref[...].astype(F32)) * yr + jax.nn.sigmoid(ga_ref[...].astype(F32)) * ya
    o_ref[...] = merged.astype(o_ref.dtype)


def _merge(ret, att, w_ret_out, w_att_out, proj, gate_col0, layer):
    t = ret.shape[0]
    d = w_ret_out.shape[2]
    tm, tn = MERGE_TM, MERGE_TN
    gblk = gate_col0 // tn
    return pl.pallas_call(
        _merge_kernel,
        out_shape=jax.ShapeDtypeStruct((t, d), BF16),
        grid=(d // tn, t // tm),
        in_specs=[
            pl.BlockSpec((tm, ret.shape[1]), lambda j, i: (i, 0)),
            pl.BlockSpec((tm, att.shape[1]), lambda j, i: (i, 0)),
            pl.BlockSpec((None, ret.shape[1], tn), lambda j, i: (layer, 0, j)),
            pl.BlockSpec((None, att.shape[1], tn), lambda j, i: (layer, 0, j)),
            pl.BlockSpec((tm, tn), lambda j, i: (i, gblk + j)),
            pl.BlockSpec((tm, tn), lambda j, i: (i, gblk + d // tn + j)),
        ],
        out_specs=pl.BlockSpec((tm, tn), lambda j, i: (i, j)),
        compiler_params=_params(("arbitrary", "arbitrary")),
        name="merge",
    )(ret, att, w_ret_out, w_att_out, proj, proj)


def _out_proj_kernel(a_ref, w_ref, x_ref, gate_ref, o_ref):
    acc = jnp.dot(a_ref[...], w_ref[...].astype(BF16), preferred_element_type=F32)
    o_ref[...] = x_ref[...] + gate_ref[...] * acc


def _out_proj(a, w, x2, mod3, layer, seq, gate_idx):
    t, k = a.shape
    d = w.shape[2]
    tm, tn = OUT_TM, OUT_TN
    return pl.pallas_call(
        _out_proj_kernel,
        out_shape=jax.ShapeDtypeStruct((t, d), F32),
        grid=(d // tn, t // tm),
        in_specs=[
            pl.BlockSpec((tm, k), lambda j, i: (i, 0)),
            pl.BlockSpec((None, k, tn), lambda j, i: (layer, 0, j)),
            pl.BlockSpec((tm, tn), lambda j, i: (i, j)),
            pl.BlockSpec((None, 1, tn), lambda j, i: ((i * tm // seq) * N_MOD + gate_idx, 0, j)),
        ],
        out_specs=pl.BlockSpec((tm, tn), lambda j, i: (i, j)),
        compiler_params=_params(("arbitrary", "arbitrary")),
        name="out_proj",
    )(a, w, x2, mod3)


def _routing(logits):
    t = logits.shape[0]
    bm = EXPERT_BM
    a = t * TOP_K
    top_val, top_idx = lax.top_k(logits, TOP_K)
    gate = jax.nn.softmax(top_val, axis=-1)
    onehot = top_idx[:, :, None] == jnp.arange(N_EXPERTS, dtype=jnp.int32)[None, None, :]
    picked = jnp.any(onehot, axis=1).astype(jnp.int32)
    inclusive = jnp.cumsum(picked, axis=0)
    sizes = inclusive[-1]
    sub = EXPERT_SUB
    padded = (sizes + sub - 1) // sub * sub
    pad_end = jnp.cumsum(padded)
    pad_start = pad_end - padded
    start = jnp.cumsum(sizes) - sizes
    slot = (pad_start[None, :] + inclusive - picked)[:, None, :]
    dest = jnp.sum(jnp.where(onehot, slot, 0), axis=-1).astype(jnp.int32).reshape(a)
    order = jnp.argsort(top_idx.reshape(a))
    n_blocks = a // bm + N_EXPERTS
    group_blocks = (sizes + bm - 1) // bm
    blocks_end = jnp.cumsum(group_blocks)
    n_valid = blocks_end[-1].astype(jnp.int32)
    blk = jnp.arange(n_blocks, dtype=jnp.int32)
    block_e = jnp.minimum(jnp.searchsorted(blocks_end, blk, side='right'), N_EXPERTS - 1).astype(jnp.int32)
    chunk = blk - (blocks_end - group_blocks)[block_e]
    block_valid = blk < n_valid
    block_rows = jnp.where(block_valid, jnp.clip(sizes[block_e] - chunk * bm, 0, bm), 0)
    block_nsub = ((block_rows + sub - 1) // sub).astype(jnp.int32)
    block_row0 = jnp.where(block_valid, pad_start[block_e] + chunk * bm, 0).astype(jnp.int32)
    padded_len = (a // TOK_ALIGN) * TOK_ALIGN + TOK_WINDOW
    sorted_tok = jnp.zeros((padded_len,), jnp.int32).at[:a].set((order // TOP_K).astype(jnp.int32))
    block_first = jnp.where(block_valid, start[block_e] + chunk * bm, 0).astype(jnp.int32)
    tables = (block_e, block_nsub, block_first, block_row0, jnp.stack([n_valid, pad_end[-1].astype(jnp.int32)]))
    return gate, dest, sorted_tok, tables


def _expert_kernel(be_ref, ns_ref, first_ref, row0_ref, info_ref, tok_hbm, h_hbm, w1_hbm, b1_ref, w2_hbm, b2_ref,
                   perm_ref, ys_hbm, w1buf, w2buf, gbuf, x_scr, act_scr, obuf, zbuf, tokbuf,
                   sem_w1, sem_w2, sem_g, sem_o, sem_t, *, layer, nj1, nj2):
    sub = EXPERT_SUB
    group = GATHER_GROUP
    bm, d = x_scr.shape
    half = d // 2
    tf = act_scr.shape[2]
    tn = obuf.shape[2]
    steps = nj1 + nj2
    nb_max = be_ref.shape[0]
    n_valid = info_ref[0]
    total = n_valid * steps
    per_step = bm // EXPERT_ISSUE_STEPS

    def for_rows(nsub, fn):
        assert bm // sub < 16
        for size in (8, 4, 2, 1):
            before = (nsub // (2 * size)) * (2 * size)

            @pl.when((nsub & size) != 0)
            def _():
                fn(pl.multiple_of(before * sub, size * sub), size * sub)

    def w1_copy(e, j, slot):
        cols = pl.ds(pl.multiple_of(j * 2 * tf, 2 * tf), 2 * tf)
        return pltpu.make_async_copy(w1_hbm.at[layer, e, :, cols], w1buf.at[slot], sem_w1.at[slot])

    def w2_copies(e, n, slot):
        return [pltpu.make_async_copy(w2_hbm.at[layer, e, :, pl.ds(pl.multiple_of(c0, tn), tn)],
                                      w2buf.at[slot, :, pl.ds(k * tn, tn)], sem_w2.at[slot])
                for k, c0 in enumerate((n * tn, half + n * tn))]

    def tok_copy(blk, slot):
        src = tok_hbm.at[pl.ds(pl.multiple_of(first_ref[blk] // TOK_ALIGN * TOK_ALIGN, TOK_ALIGN), TOK_WINDOW)]
        dst = tokbuf.at[pl.ds(pl.multiple_of(slot * TOK_WINDOW, TOK_ALIGN), TOK_WINDOW)]
        return pltpu.make_async_copy(src, dst, sem_t.at[slot])

    def tok_base(blk, slot):
        return slot * TOK_WINDOW + lax.rem(first_ref[blk], TOK_ALIGN)

    def row_copy(tok_at, tile_row, sublane):
        src = h_hbm.at[pl.ds(tokbuf[tok_at], 1)]
        return pltpu.make_async_copy(src, gbuf.at[tile_row, pl.ds(sublane, 1)], sem_g)

    def out_copy(slot, r0, m, row0, n):
        dst = ys_hbm.at[pl.ds(pl.multiple_of(row0 + r0, sub), m), pl.ds(pl.multiple_of(n * tn, tn), tn)]
        return pltpu.make_async_copy(obuf.at[slot, pl.ds(r0, m)], dst, sem_o.at[slot])

    @pl.when(n_valid > 0)
    def _():
        tok_copy(0, 0).start()
        tok_copy(0, 0).wait()
        base = tok_base(0, 0)

        def start(q, carry):
            for s in range(SUBLANES):
                row_copy(base + q * SUBLANES + s, q, s).start()
            return carry
        lax.fori_loop(0, ns_ref[0] * (sub // SUBLANES), start, 0)
        w1_copy(be_ref[0], 0, 0).start()

    def item(it, carry):
        b = it // steps
        j = it - b * steps
        e = be_ref[b]
        nsub = ns_ref[b]
        has_next = b + 1 < n_valid
        nxt = jnp.minimum(b + 1, nb_max - 1)

        nit = it + 1
        nbk = jnp.minimum(nit // steps, nb_max - 1)
        nj = nit - (nit // steps) * steps

        @pl.when((nit < total) & (nj < nj1) & (nj > 0))
        def _():
            w1_copy(be_ref[nbk], nj, lax.rem(nj, 2)).start()

        @pl.when((j == nj1) & has_next)
        def _():
            w1_copy(be_ref[nxt], 0, 0).start()

        @pl.when((nit < total) & (nj >= nj1))
        def _():
            for c in w2_copies(be_ref[nbk], nj - nj1, lax.rem(nj - nj1, 2)):
                c.start()

        @pl.when(j < nj1)
        def _():
            w1_copy(e, j, lax.rem(j, 2)).wait()

        @pl.when(j >= nj1)
        def _():
            for c in w2_copies(e, j - nj1, lax.rem(j - nj1, 2)):
                c.wait()

        @pl.when(j == 0)
        def _():
            def wait_rows(s, carry):
                pltpu.make_async_copy(h_hbm.at[pl.ds(0, sub)], zbuf, sem_g).wait()
                return carry
            lax.fori_loop(0, nsub, wait_rows, 0)

            def unpack_rows(s, carry):
                rows = pl.ds(pl.multiple_of(s * sub, sub), sub)
                tiles = pl.ds(pl.multiple_of(s * (sub // SUBLANES), sub // SUBLANES), sub // SUBLANES)
                xa, xb = _unpack_halves(gbuf[tiles].reshape(sub, half))
                x_scr[rows, pl.ds(0, half)] = xa.astype(BF16)
                x_scr[rows, pl.ds(half, half)] = xb.astype(BF16)
                return carry
            lax.fori_loop(0, nsub, unpack_rows, 0)

            @pl.when(has_next)
            def _():
                tok_copy(nxt, lax.rem(b + 1, 2)).start()

        @pl.when((j == 1) & has_next)
        def _():
            tok_copy(nxt, lax.rem(b + 1, 2)).wait()

        @pl.when((j >= 1) & (j <= EXPERT_ISSUE_STEPS) & has_next)
        def _():
            base = tok_base(nxt, lax.rem(b + 1, 2))
            rows_next = ns_ref[nxt] * sub
            for g in range(per_step // group):
                first_row = (j - 1) * per_step + g * group
                first_tile = (j - 1) * (per_step // SUBLANES) + g * (group // SUBLANES)

                @pl.when(first_row < rows_next)
                def _():
                    for i in range(group):
                        row_copy(base + first_row + i, first_tile + i // SUBLANES, i % SUBLANES).start()

        @pl.when(j < nj1)
        def _():
            wslot = lax.rem(j, 2)
            bias = b1_ref[pl.ds(e * nj1 + j, 1), :]

            def first_matmul(r0, m):
                rows = pl.ds(r0, m)
                hb = jnp.dot(x_scr[rows, :], w1buf[wslot].astype(BF16), preferred_element_type=F32) + bias
                hb = hb.astype(BF16)
                gw = perm_ref.shape[0]
                sel = [jnp.dot(hb[:, c * gw:(c + 1) * gw], perm_ref[...], preferred_element_type=F32)
                       for c in range(2 * tf // gw)]
                x_glu = jnp.minimum(jnp.concatenate([s[:, :gw // 2] for s in sel], axis=1), SWIGLU_LIMIT)
                x_lin = jnp.clip(jnp.concatenate([s[:, gw // 2:] for s in sel], axis=1),
                                 -SWIGLU_LIMIT, SWIGLU_LIMIT)
                act = x_glu * jax.nn.sigmoid(SWIGLU_ALPHA * x_glu) * (x_lin + 1.0)
                act_scr[j, rows, :] = act.astype(BF16)
            for_rows(nsub, first_matmul)

        @pl.when(j >= nj1)
        def _():
            n = j - nj1
            slot = lax.rem(n, 2)
            row0 = row0_ref[b]
            bias = jnp.concatenate([b2_ref[pl.ds(e * 2 * nj2 + n, 1), :],
                                    b2_ref[pl.ds(e * 2 * nj2 + nj2 + n, 1), :]], axis=1)
            sent = jnp.where(n >= 2, nsub, jnp.where(b > 0, ns_ref[jnp.maximum(b - 1, 0)], 0))
            for_rows(sent, lambda r0, m: out_copy(slot, r0, m, 0, 0).wait())

            def second_matmul(r0, m):
                rows = pl.ds(r0, m)
                act = jnp.concatenate([act_scr[t, rows, :] for t in range(nj1)], axis=1)
                y = jnp.dot(act, w2buf[slot].astype(BF16), preferred_element_type=F32) + bias
                obuf[slot, rows, :] = _pack_halves(y)
                out_copy(slot, r0, m, row0, n).start()
            for_rows(nsub, second_matmul)

        return carry

    lax.fori_loop(0, total, item, 0)

    @pl.when(n_valid > 0)
    def _():
        last = ns_ref[jnp.maximum(n_valid - 1, 0)]
        for slot in (0, 1):
            for_rows(last, lambda r0, m: out_copy(slot, r0, m, 0, 0).wait())

    used = info_ref[1]
    n_tail = (ys_hbm.shape[0] - used) // sub
    zbuf[...] = jnp.zeros_like(zbuf)

    def tail_copy(i):
        dst = ys_hbm.at[pl.ds(pl.multiple_of(used + i * sub, sub), sub)]
        return pltpu.make_async_copy(zbuf, dst, sem_g)

    def tail_start(i, carry):
        tail_copy(i).start()
        return carry

    def tail_wait(i, carry):
        tail_copy(i).wait()
        return carry
    lax.fori_loop(0, n_tail, tail_start, 0)
    lax.fori_loop(0, n_tail, tail_wait, 0)


def _deinterleave_matrix(tf):
    p = np.zeros((2 * tf, 2 * tf), np.float32)
    f = np.arange(tf)
    p[2 * f, f] = 1.0
    p[2 * f + 1, tf + f] = 1.0
    return jnp.asarray(p, BF16)


def _experts(h_packed, sorted_tok, tables, w1, b1, w2, b2, layer):
    t, half = h_packed.shape
    d = 2 * half
    n_experts, f = w2.shape[1], w2.shape[2]
    bm, tf, tn = EXPERT_BM, EXPERT_TF, EXPERT_TN
    nj1 = f // tf
    nj2 = half // tn
    assert nj1 % 2 == 0 and nj2 % 2 == 0
    assert 1 + EXPERT_ISSUE_STEPS <= nj1 + nj2 and bm % (EXPERT_ISSUE_STEPS * GATHER_GROUP) == 0
    assert bm % EXPERT_SUB == 0 and t >= EXPERT_SUB and GATHER_GROUP % SUBLANES == 0
    assert bm + TOK_ALIGN - 1 <= TOK_WINDOW
    p_rows = t * TOP_K + n_experts * EXPERT_SUB
    whole = lambda shape: pl.BlockSpec(shape, lambda i, *_: (0,) * len(shape))
    hbm = pl.BlockSpec(memory_space=pl.ANY)
    return pl.pallas_call(
        functools.partial(_expert_kernel, layer=layer, nj1=nj1, nj2=nj2),
        out_shape=jax.ShapeDtypeStruct((p_rows, half), U32),
        grid_spec=pltpu.PrefetchScalarGridSpec(
            num_scalar_prefetch=5,
            grid=(1,),
            in_specs=[hbm, hbm, hbm, whole((n_experts * nj1, 2 * tf)), hbm, whole((n_experts * 2 * nj2, tn)),
                      whole((EXPERT_PERM_W, EXPERT_PERM_W))],
            out_specs=hbm,
            scratch_shapes=[
                pltpu.VMEM((2, d, 2 * tf), F32),
                pltpu.VMEM((2, f, 2 * tn), F32),
                pltpu.VMEM((bm // SUBLANES, SUBLANES, half), U32),
                pltpu.VMEM((bm, d), BF16),
                pltpu.VMEM((nj1, bm, tf), BF16),
                pltpu.VMEM((2, bm, tn), U32),
                pltpu.VMEM((EXPERT_SUB, half), U32),
                pltpu.SMEM((2 * TOK_WINDOW,), jnp.int32),
                pltpu.SemaphoreType.DMA((2,)),
                pltpu.SemaphoreType.DMA((2,)),
                pltpu.SemaphoreType.DMA(()),
                pltpu.SemaphoreType.DMA((2,)),
                pltpu.SemaphoreType.DMA((2,)),
            ],
        ),
        compiler_params=_params(("arbitrary",)),
        name="experts",
    )(*tables, sorted_tok, h_packed, w1, b1[layer].reshape(n_experts * nj1, 2 * tf), w2,
      b2[layer].reshape(n_experts * 2 * nj2, tn), _deinterleave_matrix(EXPERT_PERM_W // 2))


def _combine_kernel(dest_hbm, ys_hbm, x_ref, gate_ref, g_ref, o_ref, buf, idx, sem, sem_idx):
    i = pl.program_id(0)
    tm = x_ref.shape[0]
    half = buf.shape[3]
    sublanes = SUBLANES
    per_step = tm * TOP_K
    slot = lax.rem(i, 2)

    def idx_copy(step, s):
        src = dest_hbm.at[pl.ds(pl.multiple_of(step * per_step, per_step), per_step)]
        return pltpu.make_async_copy(src, idx.at[pl.ds(pl.multiple_of(s * per_step, per_step), per_step)],
                                     sem_idx.at[s])

    @pl.when(i == 0)
    def _():
        idx_copy(0, 0).start()

    idx_copy(i, slot).wait()

    @pl.when(i + 1 < pl.num_programs(0))
    def _():
        idx_copy(i + 1, 1 - slot).start()

    def row_copy(src_row, k, tile, s):
        return pltpu.make_async_copy(ys_hbm.at[pl.ds(src_row, 1)], buf.at[k, tile, pl.ds(s, 1)], sem)

    def start(tile, carry):
        base = slot * per_step + tile * (sublanes * TOP_K)
        for s in range(sublanes):
            for k in range(TOP_K):
                row_copy(idx[base + s * TOP_K + k], k, tile, s).start()
        return carry

    def wait(tile, carry):
        for _ in range(sublanes * TOP_K):
            row_copy(0, 0, 0, 0).wait()
        return carry

    lax.fori_loop(0, tm // sublanes, start, 0)
    lax.fori_loop(0, tm // sublanes, wait, 0)

    def chunk(c, carry):
        rows = pl.ds(pl.multiple_of(c * sublanes, sublanes), sublanes)
        g = g_ref[rows, :]
        ya = jnp.zeros((sublanes, half), F32)
        yb = jnp.zeros((sublanes, half), F32)
        for k in range(TOP_K):
            a, b = _unpack_halves(buf[k, c])
            ya = ya + g[:, k:k + 1] * a
            yb = yb + g[:, k:k + 1] * b
        o_ref[rows, pl.ds(0, half)] = x_ref[rows, pl.ds(0, half)] + gate_ref[:, pl.ds(0, half)] * ya
        o_ref[rows, pl.ds(half, half)] = x_ref[rows, pl.ds(half, half)] + gate_ref[:, pl.ds(half, half)] * yb
        return carry

    lax.fori_loop(0, tm // sublanes, chunk, 0)


def _combine(ys, dest, gates, x2, mod3, seq, gate_idx):
    t, d = x2.shape
    tm = COMBINE_TM
    half = ys.shape[1]
    assert (tm * TOP_K) % TOK_ALIGN == 0
    return pl.pallas_call(
        _combine_kernel,
        out_shape=jax.ShapeDtypeStruct((t, d), F32),
        grid=(t // tm,),
        in_specs=[
            pl.BlockSpec(memory_space=pl.ANY),
            pl.BlockSpec(memory_space=pl.ANY),
            pl.BlockSpec((tm, d), lambda i: (i, 0)),
            pl.BlockSpec((None, 1, d), lambda i: ((i * tm // seq) * N_MOD + gate_idx, 0, 0)),
            pl.BlockSpec((tm, TOP_K), lambda i: (i, 0)),
        ],
        out_specs=pl.BlockSpec((tm, d), lambda i: (i, 0)),
        scratch_shapes=[
            pltpu.VMEM((TOP_K, tm // SUBLANES, SUBLANES, half), U32),
            pltpu.SMEM((2 * tm * TOP_K,), jnp.int32),
            pltpu.SemaphoreType.DMA(()),
            pltpu.SemaphoreType.DMA((2,)),
        ],
        compiler_params=_params(("arbitrary",)),
        name="combine",
    )(dest, ys, x2, mod3, gates)


def kernel(x, c, w_ada, b_ada, norm1_g, w_in, q_norm_g, k_norm_g, rel_bias, w_ret_out, w_att_out, w_out,
           norm2_g, w_router, b_router, w1, b1, w2, b2):
    batch, seq, d = x.shape
    depth = w_ada.shape[0]
    t = batch * seq
    assert batch <= SUBLANES and t % PROJ_TM == 0
    assert seq % max(RET_BLOCK, ATT_BLOCK, NORM_TM, MERGE_TM, OUT_TM, COMBINE_TM) == 0
    x2 = x.reshape(t, d)
    c8 = jnp.zeros((SUBLANES, d), F32).at[:batch].set(c)
    ret_col0 = 0
    att_col0 = 2 * RET_QK_W + 2 * RET_V_W
    gate_col0 = att_col0 + 3 * ATT_W
    for layer in range(depth):
        mod3 = _ada_ln(c8, w_ada, b_ada, layer)[:batch].reshape(batch * N_MOD, 1, d)
        h = _norm_mod(x2, norm1_g.reshape(depth, 1, d), mod3, layer, seq, shift_idx=0, scale_idx=1)
        proj = _project(h, w_in, layer)
        ret = _retention(proj, batch, seq, ret_col0)
        att = _chunk_attention(proj, rel_bias[layer], q_norm_g[layer], k_norm_g[layer], batch, seq, att_col0)
        merged = _merge(ret, att, w_ret_out, w_att_out, proj, gate_col0, layer)
        x2 = _out_proj(merged, w_out, x2, mod3, layer, seq, gate_idx=2)
        h_packed, logits = _norm_router(x2, norm2_g.reshape(depth, 1, d), mod3, w_router, b_router, layer, seq,
                                        shift_idx=3, scale_idx=4)
        gates, dest, sorted_tok, tables = _routing(logits)
        ys = _experts(h_packed, sorted_tok, tables, w1, b1, w2, b2, layer)
        x2 = _combine(ys, dest, gates, x2, mod3, seq, gate_idx=5)
    return x2.reshape(batch, seq, d)
```

```python
import functools

import numpy as np
import jax
import jax.numpy as jnp
from jax import lax
from jax.experimental import pallas as pl
from jax.experimental.pallas import tpu as pltpu

F32 = jnp.float32
BF16 = jnp.bfloat16
U32 = jnp.uint32

CHUNK = 64
NORM_EPS = 1e-6
RET_HEADS = 8
RET_QK_DIM = 256
RET_V_DIM = 512
ROPE_BASE = 10000.0
ATT_HEADS = 16
ATT_HEAD_DIM = 128
PREV_CHUNKS = 8
MAX_REL = 128
N_EXPERTS = 32
TOP_K = 4
SWIGLU_ALPHA = 1.702
SWIGLU_LIMIT = 7.0
N_MOD = 6

RET_QK_W = RET_HEADS * RET_QK_DIM
RET_V_W = RET_HEADS * RET_V_DIM
ATT_W = ATT_HEADS * ATT_HEAD_DIM

V7X_VMEM_LIMIT_BYTES = 58 * 1024 * 1024
HI16 = 0xFFFF0000

ADA_TN = 1024
NORM_TM = 512
PROJ_TM, PROJ_TN = 2048, 512
RET_BLOCK = 256
RET_HEADS_PER_STEP = 8
ATT_BLOCK = 256
ATT_HEADS_PER_STEP = 8
MERGE_TM, MERGE_TN = 1024, 512
OUT_TM, OUT_TN = 1024, 512
EXPERT_BM = 1280
EXPERT_SUB = 128
EXPERT_TF = 256
EXPERT_TN = 256
EXPERT_PERM_W = 256
EXPERT_ISSUE_STEPS = 10
GATHER_GROUP = 32
TOK_ALIGN = 1024
TOK_WINDOW = 3 * TOK_ALIGN
SUBLANES = 8
COMBINE_TM = 256


def _params(semantics):
    return pltpu.CompilerParams(dimension_semantics=semantics, vmem_limit_bytes=V7X_VMEM_LIMIT_BYTES)


def _silu(t):
    return t * jax.nn.sigmoid(t)


def _pack_halves(y):
    n = y.shape[1] // 2
    hi = lax.bitcast_convert_type(y[:, :n].astype(BF16).astype(F32), U32)
    lo = lax.bitcast_convert_type(y[:, n:].astype(BF16).astype(F32), U32)
    return hi | (lo >> 16)


def _unpack_halves(p):
    a = lax.bitcast_convert_type(p & jnp.uint32(HI16), F32)
    b = lax.bitcast_convert_type(p << 16, F32)
    return a, b


def _ada_kernel(c_ref, w_ref, b_ref, o_ref):
    a = _silu(c_ref[...]).astype(BF16)
    o_ref[...] = jnp.dot(a, w_ref[...].astype(BF16), preferred_element_type=F32) + b_ref[...]


def _ada_ln(c8, w_ada, b_ada, layer):
    rows, d = c8.shape
    n = w_ada.shape[2]
    return pl.pallas_call(
        _ada_kernel,
        out_shape=jax.ShapeDtypeStruct((rows, n), F32),
        grid=(n // ADA_TN,),
        in_specs=[
            pl.BlockSpec((rows, d), lambda j: (0, 0)),
            pl.BlockSpec((None, d, ADA_TN), lambda j: (layer, 0, j)),
            pl.BlockSpec((None, 1, ADA_TN), lambda j: (layer, 0, j)),
        ],
        out_specs=pl.BlockSpec((rows, ADA_TN), lambda j: (0, j)),
        compiler_params=_params(("arbitrary",)),
        name="ada_ln",
    )(c8, w_ada, b_ada.reshape(b_ada.shape[0], 1, n))


def _modulated_norm(x, g, scale, shift):
    xn = x * lax.rsqrt(jnp.mean(x * x, axis=-1, keepdims=True) + NORM_EPS)
    return (xn * g) * (1.0 + scale) + shift


def _norm_mod_kernel(x_ref, g_ref, sc_ref, sh_ref, o_ref):
    o_ref[...] = _modulated_norm(x_ref[...], g_ref[...], sc_ref[...], sh_ref[...]).astype(o_ref.dtype)


def _norm_mod(x2, g, mod3, layer, seq, shift_idx, scale_idx):
    t, d = x2.shape
    tm = NORM_TM

    def mod_spec(idx):
        return pl.BlockSpec((None, 1, d), lambda i: ((i * tm // seq) * N_MOD + idx, 0, 0))

    return pl.pallas_call(
        _norm_mod_kernel,
        out_shape=jax.ShapeDtypeStruct((t, d), BF16),
        grid=(t // tm,),
        in_specs=[
            pl.BlockSpec((tm, d), lambda i: (i, 0)),
            pl.BlockSpec((None, 1, d), lambda i: (layer, 0, 0)),
            mod_spec(scale_idx),
            mod_spec(shift_idx),
        ],
        out_specs=pl.BlockSpec((tm, d), lambda i: (i, 0)),
        compiler_params=_params(("arbitrary",)),
        name="norm_mod",
    )(x2, g, mod3, mod3)


def _norm_router_kernel(x_ref, g_ref, sc_ref, sh_ref, wr_ref, br_ref, h_ref, lg_ref):
    h = _modulated_norm(x_ref[...], g_ref[...], sc_ref[...], sh_ref[...])
    h_ref[...] = _pack_halves(h)
    h_hi = h.astype(BF16)
    h_lo = (h - h_hi.astype(F32)).astype(BF16)
    w = wr_ref[...]
    w_hi = w.astype(BF16)
    w_lo = (w - w_hi.astype(F32)).astype(BF16)
    e = w.shape[1]
    both = jnp.dot(h_hi, jnp.concatenate([w_hi, w_lo], axis=1), preferred_element_type=F32)
    lg = both[:, :e] + jnp.dot(h_lo, w_hi, preferred_element_type=F32) + both[:, e:]
    lg_ref[...] = lg + br_ref[...]


def _norm_router(x2, g, mod3, w_router, b_router, layer, seq, shift_idx, scale_idx):
    t, d = x2.shape
    e = w_router.shape[2]
    tm = NORM_TM

    def mod_spec(idx):
        return pl.BlockSpec((None, 1, d), lambda i: ((i * tm // seq) * N_MOD + idx, 0, 0))

    return pl.pallas_call(
        _norm_router_kernel,
        out_shape=(jax.ShapeDtypeStruct((t, d // 2), U32), jax.ShapeDtypeStruct((t, e), F32)),
        grid=(t // tm,),
        in_specs=[
            pl.BlockSpec((tm, d), lambda i: (i, 0)),
            pl.BlockSpec((None, 1, d), lambda i: (layer, 0, 0)),
            mod_spec(scale_idx),
            mod_spec(shift_idx),
            pl.BlockSpec((None, d, e), lambda i: (layer, 0, 0)),
            pl.BlockSpec((None, 1, e), lambda i: (layer, 0, 0)),
        ],
        out_specs=(pl.BlockSpec((tm, d // 2), lambda i: (i, 0)), pl.BlockSpec((tm, e), lambda i: (i, 0))),
        compiler_params=_params(("arbitrary",)),
        name="norm_router",
    )(x2, g, mod3, mod3, w_router, b_router.reshape(b_router.shape[0], 1, e))


def _proj_kernel(a_ref, w_ref, o_ref):
    o_ref[...] = jnp.dot(a_ref[...], w_ref[...].astype(BF16), preferred_element_type=F32).astype(o_ref.dtype)


def _project(a, w, layer):
    t, k = a.shape
    n = w.shape[2]
    tm, tn = PROJ_TM, PROJ_TN
    return pl.pallas_call(
        _proj_kernel,
        out_shape=jax.ShapeDtypeStruct((t, n), BF16),
        grid=(n // tn, t // tm),
        in_specs=[
            pl.BlockSpec((tm, k), lambda j, i: (i, 0)),
            pl.BlockSpec((None, k, tn), lambda j, i: (layer, 0, j)),
        ],
        out_specs=pl.BlockSpec((tm, tn), lambda j, i: (i, j)),
        compiler_params=_params(("arbitrary", "arbitrary")),
        name="in_proj",
    )(a, w)


def _retention_kernel(q_ref, k_ref, v_ref, g_ref, cos_ref, sin_ref, dec_ref, qd_ref, kd_ref, cd_ref,
                      o_ref, state_ref):
    @pl.when(pl.program_id(2) == 0)
    def _():
        state_ref[...] = jnp.zeros_like(state_ref)

    cos = cos_ref[...]
    sin = sin_ref[...]
    half = RET_QK_DIM // 2
    contract_last = (((1,), (1,)), ((), ()))
    contract_rows = (((0,), (0,)), ((), ()))

    def rotary(t):
        t1, t2 = t[:, :half], t[:, half:]
        return jnp.concatenate([t1 * cos - t2 * sin, t1 * sin + t2 * cos], axis=1)

    for hh in range(RET_HEADS_PER_STEP):
        qk = slice(hh * RET_QK_DIM, (hh + 1) * RET_QK_DIM)
        vg = slice(hh * RET_V_DIM, (hh + 1) * RET_V_DIM)
        q = rotary(q_ref[:, qk].astype(F32))
        k = rotary(k_ref[:, qk].astype(F32)) * (RET_QK_DIM ** -0.5)
        v = v_ref[:, vg]
        scores = lax.dot_general(q.astype(BF16), k.astype(BF16), contract_last,
                                 preferred_element_type=F32) * dec_ref[hh]
        state = state_ref[hh]
        out = (jnp.dot(scores.astype(BF16), v, preferred_element_type=F32)
               + jnp.dot((q * qd_ref[hh]).astype(BF16), state.astype(BF16), preferred_element_type=F32))
        state_ref[hh] = state * cd_ref[hh] + lax.dot_general(
            (k * kd_ref[hh]).astype(BF16), v, contract_rows, preferred_element_type=F32)
        on = out * lax.rsqrt(jnp.mean(out * out, axis=-1, keepdims=True) + NORM_EPS)
        o_ref[:, vg] = (_silu(g_ref[:, vg].astype(F32)) * on).astype(o_ref.dtype)


def _retention_tables(seq):
    half = RET_QK_DIM // 2
    inv_freq = ROPE_BASE ** (-np.arange(half, dtype=np.float64) / half)
    ang = np.arange(seq, dtype=np.float64)[:, None] * inv_freq[None, :]
    log_gamma = np.log(1.0 - 2.0 ** (-5.0 - np.arange(RET_HEADS, dtype=np.float64)))[:, None, None]
    pos = np.arange(RET_BLOCK, dtype=np.float64)
    n, m = pos[:, None], pos[None, :]
    cn, cm = np.floor(n / CHUNK), np.floor(m / CHUNK)
    dist = np.where(cn == cm, np.abs(n - m), n - m)
    dec = np.where((cm <= cn)[None], np.exp(log_gamma * dist[None]), 0.0)
    qd = np.broadcast_to(np.exp(log_gamma * (pos[None, :, None] + 1.0)), (RET_HEADS, RET_BLOCK, RET_QK_DIM))
    kd = np.broadcast_to(np.exp(log_gamma * (RET_BLOCK - 1.0 - pos[None, :, None])),
                         (RET_HEADS, RET_BLOCK, RET_QK_DIM))
    cd = np.broadcast_to(np.exp(log_gamma * RET_BLOCK), (RET_HEADS, 1, RET_V_DIM))
    return tuple(jnp.asarray(np.ascontiguousarray(v), F32) for v in (np.cos(ang), np.sin(ang), dec, qd, kd, cd))


def _retention(proj, batch, seq, col0):
    t = proj.shape[0]
    r = RET_BLOCK
    nsb = seq // r
    cos, sin, dec, qd, kd, cd = _retention_tables(seq)
    hp = RET_HEADS_PER_STEP
    groups = RET_HEADS // hp
    qk_w, v_w = hp * RET_QK_DIM, hp * RET_V_DIM
    assert col0 % qk_w == 0 and (col0 + 2 * RET_QK_W) % v_w == 0
    qk_blk0 = col0 // qk_w
    v_blk0 = (col0 + 2 * RET_QK_W) // v_w
    row = lambda b, h, s: b * nsb + s
    head_tab = lambda shape: pl.BlockSpec((hp,) + shape, lambda b, h, s: (h, 0, 0))
    half = RET_QK_DIM // 2
    return pl.pallas_call(
        _retention_kernel,
        out_shape=jax.ShapeDtypeStruct((t, RET_V_W), BF16),
        grid=(batch, groups, nsb),
        in_specs=[
            pl.BlockSpec((r, qk_w), lambda b, h, s: (row(b, h, s), qk_blk0 + h)),
            pl.BlockSpec((r, qk_w), lambda b, h, s: (row(b, h, s), qk_blk0 + groups + h)),
            pl.BlockSpec((r, v_w), lambda b, h, s: (row(b, h, s), v_blk0 + h)),
            pl.BlockSpec((r, v_w), lambda b, h, s: (row(b, h, s), v_blk0 + groups + h)),
            pl.BlockSpec((r, half), lambda b, h, s: (s, 0)),
            pl.BlockSpec((r, half), lambda b, h, s: (s, 0)),
            head_tab((r, r)),
            head_tab((r, RET_QK_DIM)),
            head_tab((r, RET_QK_DIM)),
            head_tab((1, RET_V_DIM)),
        ],
        out_specs=pl.BlockSpec((r, v_w), lambda b, h, s: (row(b, h, s), h)),
        scratch_shapes=[pltpu.VMEM((hp, RET_QK_DIM, RET_V_DIM), F32)],
        compiler_params=_params(("arbitrary", "arbitrary", "arbitrary")),
        name="retention",
    )(proj, proj, proj, proj, cos, sin, dec, qd, kd, cd)


ATT_KEY_BLOCKS = PREV_CHUNKS * CHUNK // ATT_BLOCK + 1
NEG = float(np.finfo(np.float32).min)


def _attention_kernel(q_ref, k0_ref, k1_ref, k2_ref, v0_ref, v1_ref, v2_ref, tab_ref, gq_ref, gk_ref, o_ref):
    i = pl.program_id(1)

    def rms(t, g):
        tf = t.astype(F32)
        return (tf * lax.rsqrt(jnp.mean(tf * tf, axis=-1, keepdims=True) + NORM_EPS)) * g

    dh = ATT_HEAD_DIM
    k_all = jnp.concatenate([k0_ref[...], k1_ref[...], k2_ref[...]], axis=0)
    v_all = jnp.concatenate([v0_ref[...], v1_ref[...], v2_ref[...]], axis=0)
    outs = []
    for hh in range(ATT_HEADS_PER_STEP):
        lanes = slice(hh * dh, (hh + 1) * dh)
        q = rms(q_ref[:, lanes], gq_ref[...]).astype(BF16)
        k = rms(k_all[:, lanes], gk_ref[...]).astype(BF16)
        sc = lax.dot_general(q, k, (((1,), (1,)), ((), ())), preferred_element_type=F32)
        sc = sc * (ATT_HEAD_DIM ** -0.5) + tab_ref[hh]
        col = lax.broadcasted_iota(jnp.int32, sc.shape, 1)
        sc = jnp.where(col >= (ATT_KEY_BLOCKS - 1 - i) * ATT_BLOCK, sc, NEG)
        p = jnp.exp(sc - jnp.max(sc, axis=-1, keepdims=True))
        denom = jnp.sum(p, axis=-1, keepdims=True)
        outs.append(jnp.dot(p.astype(BF16), v_all[:, lanes], preferred_element_type=F32) / denom)
    o_ref[...] = jnp.concatenate(outs, axis=1).astype(o_ref.dtype)


def _attention_table(rel_bias):
    a = ATT_BLOCK
    w = ATT_KEY_BLOCKS * a
    pad = PREV_CHUNKS * CHUNK
    heads = rel_bias.shape[0]
    period = 1024
    assert w + a - 1 <= period
    i = np.arange(period)
    rel_of_i = np.where(i < w, pad - i, pad + period - i)
    idx = np.clip(np.minimum(rel_of_i, MAX_REL) + (CHUNK - 1), 0, rel_bias.shape[-1] - 1)
    u = rel_bias[:, idx].astype(F32).reshape(heads, 1, period)
    return pl.pallas_call(
        _attention_table_kernel,
        out_shape=jax.ShapeDtypeStruct((heads, a, w), F32),
        grid=(heads,),
        in_specs=[pl.BlockSpec((None, 1, period), lambda h: (h, 0, 0))],
        out_specs=pl.BlockSpec((None, a, w), lambda h: (h, 0, 0)),
        compiler_params=_params(("arbitrary",)),
        name="attention_table",
    )(u)


def _attention_table_kernel(u_ref, o_ref):
    a, w = o_ref.shape
    rows = jnp.broadcast_to(u_ref[...], (a, u_ref.shape[1]))
    skew = pltpu.roll(rows, 0, 1, stride=1, stride_axis=0)
    chunk_bits = CHUNK.bit_length() - 1
    assert CHUNK == 1 << chunk_bits
    qc = lax.broadcasted_iota(jnp.int32, (a, w), 0) >> chunk_bits
    kc = (lax.broadcasted_iota(jnp.int32, (a, w), 1) >> chunk_bits) - PREV_CHUNKS
    valid = (kc <= qc) & (kc >= qc - PREV_CHUNKS)
    o_ref[...] = jnp.where(valid, skew[:, :w], NEG)


def _chunk_attention(proj, rel_bias, q_norm_g, k_norm_g, batch, seq, col0):
    assert ATT_KEY_BLOCKS == 3
    t = proj.shape[0]
    a = ATT_BLOCK
    nqb = seq // a
    dh = ATT_HEAD_DIM
    hw = ATT_HEADS_PER_STEP * dh
    groups = ATT_HEADS // ATT_HEADS_PER_STEP
    assert col0 % hw == 0 and ATT_W % hw == 0
    blk0 = col0 // hw
    tab = _attention_table(rel_bias)

    def kv_spec(which, jj):
        return pl.BlockSpec(
            (a, hw),
            lambda h, i, b: (b * nqb + jnp.maximum(i - (ATT_KEY_BLOCKS - 1) + jj, 0), blk0 + which * groups + h))

    return pl.pallas_call(
        _attention_kernel,
        out_shape=jax.ShapeDtypeStruct((t, ATT_W), BF16),
        grid=(groups, nqb, batch),
        in_specs=[
            pl.BlockSpec((a, hw), lambda h, i, b: (b * nqb + i, blk0 + h)),
            kv_spec(1, 0), kv_spec(1, 1), kv_spec(1, 2),
            kv_spec(2, 0), kv_spec(2, 1), kv_spec(2, 2),
            pl.BlockSpec((ATT_HEADS_PER_STEP, a, ATT_KEY_BLOCKS * a), lambda h, i, b: (h, 0, 0)),
            pl.BlockSpec((1, dh), lambda h, i, b: (0, 0)),
            pl.BlockSpec((1, dh), lambda h, i, b: (0, 0)),
        ],
        out_specs=pl.BlockSpec((a, hw), lambda h, i, b: (b * nqb + i, h)),
        compiler_params=_params(("arbitrary", "arbitrary", "arbitrary")),
        name="chunk_attention",
    )(proj, proj, proj, proj, proj, proj, proj, tab, q_norm_g.reshape(1, dh), k_norm_g.reshape(1, dh))


def _merge_kernel(r_ref, a_ref, wr_ref, wa_ref, gr_ref, ga_ref, o_ref):
    yr = jnp.dot(r_ref[...], wr_ref[...].astype(BF16), preferred_element_type=F32)
    ya = jnp.dot(a_ref[...], wa_ref[...].astype(BF16), preferred_element_type=F32)
    merged = jax.nn.sigmoid(gr_ref[...].astype(F32)) * yr + jax.nn.sigmoid(ga_ref[...].astype(F32)) * ya
    o_ref[...] = merged.astype(o_ref.dtype)


def _merge(ret, att, w_ret_out, w_att_out, proj, gate_col0, layer):
    t = ret.shape[0]
    d = w_ret_out.shape[2]
    tm, tn = MERGE_TM, MERGE_TN
    gblk = gate_col0 // tn
    return pl.pallas_call(
        _merge_kernel,
        out_shape=jax.ShapeDtypeStruct((t, d), BF16),
        grid=(d // tn, t // tm),
        in_specs=[
            pl.BlockSpec((tm, ret.shape[1]), lambda j, i: (i, 0)),
            pl.BlockSpec((tm, att.shape[1]), lambda j, i: (i, 0)),
            pl.BlockSpec((None, ret.shape[1], tn), lambda j, i: (layer, 0, j)),
            pl.BlockSpec((None, att.shape[1], tn), lambda j, i: (layer, 0, j)),
            pl.BlockSpec((tm, tn), lambda j, i: (i, gblk + j)),
            pl.BlockSpec((tm, tn), lambda j, i: (i, gblk + d // tn + j)),
        ],
        out_specs=pl.BlockSpec((tm, tn), lambda j, i: (i, j)),
        compiler_params=_params(("arbitrary", "arbitrary")),
        name="merge",
    )(ret, att, w_ret_out, w_att_out, proj, proj)


def _out_proj_kernel(a_ref, w_ref, x_ref, gate_ref, o_ref):
    acc = jnp.dot(a_ref[...], w_ref[...].astype(BF16), preferred_element_type=F32)
    o_ref[...] = x_ref[...] + gate_ref[...] * acc


def _out_proj(a, w, x2, mod3, layer, seq, gate_idx):
    t, k = a.shape
    d = w.shape[2]
    tm, tn = OUT_TM, OUT_TN
    return pl.pallas_call(
        _out_proj_kernel,
        out_shape=jax.ShapeDtypeStruct((t, d), F32),
        grid=(d // tn, t // tm),
        in_specs=[
            pl.BlockSpec((tm, k), lambda j, i: (i, 0)),
            pl.BlockSpec((None, k, tn), lambda j, i: (layer, 0, j)),
            pl.BlockSpec((tm, tn), lambda j, i: (i, j)),
            pl.BlockSpec((None, 1, tn), lambda j, i: ((i * tm // seq) * N_MOD + gate_idx, 0, j)),
        ],
        out_specs=pl.BlockSpec((tm, tn), lambda j, i: (i, j)),
        compiler_params=_params(("arbitrary", "arbitrary")),
        name="out_proj",
    )(a, w, x2, mod3)


def _routing(logits):
    t = logits.shape[0]
    bm = EXPERT_BM
    a = t * TOP_K
    top_val, top_idx = lax.top_k(logits, TOP_K)
    gate = jax.nn.softmax(top_val, axis=-1)
    onehot = top_idx[:, :, None] == jnp.arange(N_EXPERTS, dtype=jnp.int32)[None, None, :]
    picked = jnp.any(onehot, axis=1).astype(jnp.int32)
    inclusive = jnp.cumsum(picked, axis=0)
    sizes = inclusive[-1]
    sub = EXPERT_SUB
    padded = (sizes + sub - 1) // sub * sub
    pad_end = jnp.cumsum(padded)
    pad_start = pad_end - padded
    start = jnp.cumsum(sizes) - sizes
    slot = (pad_start[None, :] + inclusive - picked)[:, None, :]
    dest = jnp.sum(jnp.where(onehot, slot, 0), axis=-1).astype(jnp.int32).reshape(a)
    order = jnp.argsort(top_idx.reshape(a))
    n_blocks = a // bm + N_EXPERTS
    group_blocks = (sizes + bm - 1) // bm
    blocks_end = jnp.cumsum(group_blocks)
    n_valid = blocks_end[-1].astype(jnp.int32)
    blk = jnp.arange(n_blocks, dtype=jnp.int32)
    block_e = jnp.minimum(jnp.searchsorted(blocks_end, blk, side='right'), N_EXPERTS - 1).astype(jnp.int32)
    chunk = blk - (blocks_end - group_blocks)[block_e]
    block_valid = blk < n_valid
    block_rows = jnp.where(block_valid, jnp.clip(sizes[block_e] - chunk * bm, 0, bm), 0)
    block_nsub = ((block_rows + sub - 1) // sub).astype(jnp.int32)
    block_row0 = jnp.where(block_valid, pad_start[block_e] + chunk * bm, 0).astype(jnp.int32)
    padded_len = (a // TOK_ALIGN) * TOK_ALIGN + TOK_WINDOW
    sorted_tok = jnp.zeros((padded_len,), jnp.int32).at[:a].set((order // TOP_K).astype(jnp.int32))
    block_first = jnp.where(block_valid, start[block_e] + chunk * bm, 0).astype(jnp.int32)
    tables = (block_e, block_nsub, block_first, block_row0, jnp.stack([n_valid, pad_end[-1].astype(jnp.int32)]))
    return gate, dest, sorted_tok, tables


def _expert_kernel(be_ref, ns_ref, first_ref, row0_ref, info_ref, tok_hbm, h_hbm, w1_hbm, b1_ref, w2_hbm, b2_ref,
                   perm_ref, ys_hbm, w1buf, w2buf, gbuf, x_scr, act_scr, obuf, zbuf, tokbuf,
                   sem_w1, sem_w2, sem_g, sem_o, sem_t, *, layer, nj1, nj2):
    sub = EXPERT_SUB
    group = GATHER_GROUP
    bm, d = x_scr.shape
    half = d // 2
    tf = act_scr.shape[2]
    tn = obuf.shape[2]
    steps = nj1 + nj2
    nb_max = be_ref.shape[0]
    n_valid = info_ref[0]
    total = n_valid * steps
    per_step = bm // EXPERT_ISSUE_STEPS

    def for_rows(nsub, fn):
        assert bm // sub < 16
        for size in (8, 4, 2, 1):
            before = (nsub // (2 * size)) * (2 * size)

            @pl.when((nsub & size) != 0)
            def _():
                fn(pl.multiple_of(before * sub, size * sub), size * sub)

    def w1_copy(e, j, slot):
        cols = pl.ds(pl.multiple_of(j * 2 * tf, 2 * tf), 2 * tf)
        return pltpu.make_async_copy(w1_hbm.at[layer, e, :, cols], w1buf.at[slot], sem_w1.at[slot])

    def w2_copies(e, n, slot):
        return [pltpu.make_async_copy(w2_hbm.at[layer, e, :, pl.ds(pl.multiple_of(c0, tn), tn)],
                                      w2buf.at[slot, :, pl.ds(k * tn, tn)], sem_w2.at[slot])
                for k, c0 in enumerate((n * tn, half + n * tn))]

    def tok_copy(blk, slot):
        src = tok_hbm.at[pl.ds(pl.multiple_of(first_ref[blk] // TOK_ALIGN * TOK_ALIGN, TOK_ALIGN), TOK_WINDOW)]
        dst = tokbuf.at[pl.ds(pl.multiple_of(slot * TOK_WINDOW, TOK_ALIGN), TOK_WINDOW)]
        return pltpu.make_async_copy(src, dst, sem_t.at[slot])

    def tok_base(blk, slot):
        return slot * TOK_WINDOW + lax.rem(first_ref[blk], TOK_ALIGN)

    def row_copy(tok_at, tile_row, sublane):
        src = h_hbm.at[pl.ds(tokbuf[tok_at], 1)]
        return pltpu.make_async_copy(src, gbuf.at[tile_row, pl.ds(sublane, 1)], sem_g)

    def out_copy(slot, r0, m, row0, n):
        dst = ys_hbm.at[pl.ds(pl.multiple_of(row0 + r0, sub), m), pl.ds(pl.multiple_of(n * tn, tn), tn)]
        return pltpu.make_async_copy(obuf.at[slot, pl.ds(r0, m)], dst, sem_o.at[slot])

    @pl.when(n_valid > 0)
    def _():
        tok_copy(0, 0).start()
        w1_copy(be_ref[0], 0, 0).start()
        tok_copy(0, 0).wait()
        base = tok_base(0, 0)

        def start(q, carry):
            for s in range(SUBLANES):
                row_copy(base + q * SUBLANES + s, q, s).start()
            return carry
        lax.fori_loop(0, ns_ref[0] * (sub // SUBLANES), start, 0)

    def item(it, carry):
        b = it // steps
        j = it - b * steps
        e = be_ref[b]
        nsub = ns_ref[b]
        has_next = b + 1 < n_valid
        nxt = jnp.minimum(b + 1, nb_max - 1)

        nit = it + 1
        nbk = jnp.minimum(nit // steps, nb_max - 1)
        nj = nit - (nit // steps) * steps

        @pl.when((nit < total) & (nj < nj1) & (nj > 0))
        def _():
            w1_copy(be_ref[nbk], nj, lax.rem(nj, 2)).start()

        @pl.when((j == nj1) & has_next)
        def _():
            w1_copy(be_ref[nxt], 0, 0).start()

        @pl.when((nit < total) & (nj >= nj1))
        def _():
            for c in w2_copies(be_ref[nbk], nj - nj1, lax.rem(nj - nj1, 2)):
                c.start()

        @pl.when(j < nj1)
        def _():
            w1_copy(e, j, lax.rem(j, 2)).wait()

        @pl.when(j >= nj1)
        def _():
            for c in w2_copies(e, j - nj1, lax.rem(j - nj1, 2)):
                c.wait()

        @pl.when(j == 0)
        def _():
            def wait_rows(s, carry):
                pltpu.make_async_copy(h_hbm.at[pl.ds(0, sub)], zbuf, sem_g).wait()
                return carry
            lax.fori_loop(0, nsub, wait_rows, 0)

            def unpack_rows(s, carry):
                rows = pl.ds(pl.multiple_of(s * sub, sub), sub)
                tiles = pl.ds(pl.multiple_of(s * (sub // SUBLANES), sub // SUBLANES), sub // SUBLANES)
                xa, xb = _unpack_halves(gbuf[tiles].reshape(sub, half))
                x_scr[rows, pl.ds(0, half)] = xa.astype(BF16)
                x_scr[rows, pl.ds(half, half)] = xb.astype(BF16)
                return carry
            lax.fori_loop(0, nsub, unpack_rows, 0)

            @pl.when(has_next)
            def _():
                tok_copy(nxt, lax.rem(b + 1, 2)).start()

        @pl.when((j == 1) & has_next)
        def _():
            tok_copy(nxt, lax.rem(b + 1, 2)).wait()

        @pl.when((j >= 1) & (j <= EXPERT_ISSUE_STEPS) & has_next)
        def _():
            base = tok_base(nxt, lax.rem(b + 1, 2))
            rows_next = ns_ref[nxt] * sub
            for g in range(per_step // group):
                first_row = (j - 1) * per_step + g * group
                first_tile = (j - 1) * (per_step // SUBLANES) + g * (group // SUBLANES)

                @pl.when(first_row < rows_next)
                def _():
                    for i in range(group):
                        row_copy(base + first_row + i, first_tile + i // SUBLANES, i % SUBLANES).start()

        @pl.when(j < nj1)
        def _():
            wslot = lax.rem(j, 2)
            bias = b1_ref[pl.ds(e * nj1 + j, 1), :]

            def first_matmul(r0, m):
                rows = pl.ds(r0, m)
                hb = jnp.dot(x_scr[rows, :], w1buf[wslot].astype(BF16), preferred_element_type=F32) + bias
                hb = hb.astype(BF16)
                gw = perm_ref.shape[0]
                sel = [jnp.dot(hb[:, c * gw:(c + 1) * gw], perm_ref[...], preferred_element_type=F32)
                       for c in range(2 * tf // gw)]
                x_glu = jnp.minimum(jnp.concatenate([s[:, :gw // 2] for s in sel], axis=1), SWIGLU_LIMIT)
                x_lin = jnp.clip(jnp.concatenate([s[:, gw // 2:] for s in sel], axis=1),
                                 -SWIGLU_LIMIT, SWIGLU_LIMIT)
                act = x_glu * jax.nn.sigmoid(SWIGLU_ALPHA * x_glu) * (x_lin + 1.0)
                act_scr[j, rows, :] = act.astype(BF16)
            for_rows(nsub, first_matmul)

        @pl.when(j >= nj1)
        def _():
            n = j - nj1
            slot = lax.rem(n, 2)
            row0 = row0_ref[b]
            bias = jnp.concatenate([b2_ref[pl.ds(e * 2 * nj2 + n, 1), :],
                                    b2_ref[pl.ds(e * 2 * nj2 + nj2 + n, 1), :]], axis=1)
            sent = jnp.where(n >= 2, nsub, jnp.where(b > 0, ns_ref[jnp.maximum(b - 1, 0)], 0))
            for_rows(sent, lambda r0, m: out_copy(slot, r0, m, 0, 0).wait())

            def second_matmul(r0, m):
                rows = pl.ds(r0, m)
                act = jnp.concatenate([act_scr[t, rows, :] for t in range(nj1)], axis=1)
                y = jnp.dot(act, w2buf[slot].astype(BF16), preferred_element_type=F32) + bias
                obuf[slot, rows, :] = _pack_halves(y)
                out_copy(slot, r0, m, row0, n).start()
            for_rows(nsub, second_matmul)

        return carry

    lax.fori_loop(0, total, item, 0)

    @pl.when(n_valid > 0)
    def _():
        last = ns_ref[jnp.maximum(n_valid - 1, 0)]
        for slot in (0, 1):
            for_rows(last, lambda r0, m: out_copy(slot, r0, m, 0, 0).wait())

    used = info_ref[1]
    n_tail = (ys_hbm.shape[0] - used) // sub
    zbuf[...] = jnp.zeros_like(zbuf)

    def tail_copy(i):
        dst = ys_hbm.at[pl.ds(pl.multiple_of(used + i * sub, sub), sub)]
        return pltpu.make_async_copy(zbuf, dst, sem_g)

    def tail_start(i, carry):
        tail_copy(i).start()
        return carry

    def tail_wait(i, carry):
        tail_copy(i).wait()
        return carry
    lax.fori_loop(0, n_tail, tail_start, 0)
    lax.fori_loop(0, n_tail, tail_wait, 0)


def _deinterleave_matrix(tf):
    p = np.zeros((2 * tf, 2 * tf), np.float32)
    f = np.arange(tf)
    p[2 * f, f] = 1.0
    p[2 * f + 1, tf + f] = 1.0
    return jnp.asarray(p, BF16)


def _experts(h_packed, sorted_tok, tables, w1, b1, w2, b2, layer):
    t, half = h_packed.shape
    d = 2 * half
    n_experts, f = w2.shape[1], w2.shape[2]
    bm, tf, tn = EXPERT_BM, EXPERT_TF, EXPERT_TN
    nj1 = f // tf
    nj2 = half // tn
    assert nj1 % 2 == 0 and nj2 % 2 == 0
    assert 1 + EXPERT_ISSUE_STEPS <= nj1 + nj2 and bm % (EXPERT_ISSUE_STEPS * GATHER_GROUP) == 0
    assert bm % EXPERT_SUB == 0 and t >= EXPERT_SUB and GATHER_GROUP % SUBLANES == 0
    assert bm + TOK_ALIGN - 1 <= TOK_WINDOW
    p_rows = t * TOP_K + n_experts * EXPERT_SUB
    whole = lambda shape: pl.BlockSpec(shape, lambda i, *_: (0,) * len(shape))
    hbm = pl.BlockSpec(memory_space=pl.ANY)
    return pl.pallas_call(
        functools.partial(_expert_kernel, layer=layer, nj1=nj1, nj2=nj2),
        out_shape=jax.ShapeDtypeStruct((p_rows, half), U32),
        grid_spec=pltpu.PrefetchScalarGridSpec(
            num_scalar_prefetch=5,
            grid=(1,),
            in_specs=[hbm, hbm, hbm, whole((n_experts * nj1, 2 * tf)), hbm, whole((n_experts * 2 * nj2, tn)),
                      whole((EXPERT_PERM_W, EXPERT_PERM_W))],
            out_specs=hbm,
            scratch_shapes=[
                pltpu.VMEM((2, d, 2 * tf), F32),
                pltpu.VMEM((2, f, 2 * tn), F32),
                pltpu.VMEM((bm // SUBLANES, SUBLANES, half), U32),
                pltpu.VMEM((bm, d), BF16),
                pltpu.VMEM((nj1, bm, tf), BF16),
                pltpu.VMEM((2, bm, tn), U32),
                pltpu.VMEM((EXPERT_SUB, half), U32),
                pltpu.SMEM((2 * TOK_WINDOW,), jnp.int32),
                pltpu.SemaphoreType.DMA((2,)),
                pltpu.SemaphoreType.DMA((2,)),
                pltpu.SemaphoreType.DMA(()),
                pltpu.SemaphoreType.DMA((2,)),
                pltpu.SemaphoreType.DMA((2,)),
            ],
        ),
        compiler_params=_params(("arbitrary",)),
        name="experts",
    )(*tables, sorted_tok, h_packed, w1, b1[layer].reshape(n_experts * nj1, 2 * tf), w2,
      b2[layer].reshape(n_experts * 2 * nj2, tn), _deinterleave_matrix(EXPERT_PERM_W // 2))


def _combine_kernel(dest_hbm, ys_hbm, x_ref, gate_ref, g_ref, o_ref, buf, idx, sem, sem_idx):
    i = pl.program_id(0)
    tm = x_ref.shape[0]
    half = buf.shape[3]
    sublanes = SUBLANES
    per_step = tm * TOP_K
    slot = lax.rem(i, 2)

    def idx_copy(step, s):
        src = dest_hbm.at[pl.ds(pl.multiple_of(step * per_step, per_step), per_step)]
        return pltpu.make_async_copy(src, idx.at[pl.ds(pl.multiple_of(s * per_step, per_step), per_step)],
                                     sem_idx.at[s])

    @pl.when(i == 0)
    def _():
        idx_copy(0, 0).start()

    idx_copy(i, slot).wait()

    @pl.when(i + 1 < pl.num_programs(0))
    def _():
        idx_copy(i + 1, 1 - slot).start()

    def row_copy(src_row, k, tile, s):
        return pltpu.make_async_copy(ys_hbm.at[pl.ds(src_row, 1)], buf.at[k, tile, pl.ds(s, 1)], sem)

    def start(tile, carry):
        base = slot * per_step + tile * (sublanes * TOP_K)
        for s in range(sublanes):
            for k in range(TOP_K):
                row_copy(idx[base + s * TOP_K + k], k, tile, s).start()
        return carry

    def wait(tile, carry):
        for _ in range(sublanes * TOP_K):
            row_copy(0, 0, 0, 0).wait()
        return carry

    lax.fori_loop(0, tm // sublanes, start, 0)
    lax.fori_loop(0, tm // sublanes, wait, 0)

    def chunk(c, carry):
        rows = pl.ds(pl.multiple_of(c * sublanes, sublanes), sublanes)
        g = g_ref[rows, :]
        ya = jnp.zeros((sublanes, half), F32)
        yb = jnp.zeros((sublanes, half), F32)
        for k in range(TOP_K):
            a, b = _unpack_halves(buf[k, c])
            ya = ya + g[:, k:k + 1] * a
            yb = yb + g[:, k:k + 1] * b
        o_ref[rows, pl.ds(0, half)] = x_ref[rows, pl.ds(0, half)] + gate_ref[:, pl.ds(0, half)] * ya
        o_ref[rows, pl.ds(half, half)] = x_ref[rows, pl.ds(half, half)] + gate_ref[:, pl.ds(half, half)] * yb
        return carry

    lax.fori_loop(0, tm // sublanes, chunk, 0, unroll=4)


def _combine(ys, dest, gates, x2, mod3, seq, gate_idx):
    t, d = x2.shape
    tm = COMBINE_TM
    half = ys.shape[1]
    assert (tm * TOP_K) % TOK_ALIGN == 0
    return pl.pallas_call(
        _combine_kernel,
        out_shape=jax.ShapeDtypeStruct((t, d), F32),
        grid=(t // tm,),
        in_specs=[
            pl.BlockSpec(memory_space=pl.ANY),
            pl.BlockSpec(memory_space=pl.ANY),
            pl.BlockSpec((tm, d), lambda i: (i, 0)),
            pl.BlockSpec((None, 1, d), lambda i: ((i * tm // seq) * N_MOD + gate_idx, 0, 0)),
            pl.BlockSpec((tm, TOP_K), lambda i: (i, 0)),
        ],
        out_specs=pl.BlockSpec((tm, d), lambda i: (i, 0)),
        scratch_shapes=[
            pltpu.VMEM((TOP_K, tm // SUBLANES, SUBLANES, half), U32),
            pltpu.SMEM((2 * tm * TOP_K,), jnp.int32),
            pltpu.SemaphoreType.DMA(()),
            pltpu.SemaphoreType.DMA((2,)),
        ],
        compiler_params=_params(("arbitrary",)),
        name="combine",
    )(dest, ys, x2, mod3, gates)


def kernel(x, c, w_ada, b_ada, norm1_g, w_in, q_norm_g, k_norm_g, rel_bias, w_ret_out, w_att_out, w_out,
           norm2_g, w_router, b_router, w1, b1, w2, b2):
    batch, seq, d = x.shape
    depth = w_ada.shape[0]
    t = batch * seq
    assert batch <= SUBLANES and t % PROJ_TM == 0
    assert seq % max(RET_BLOCK, ATT_BLOCK, NORM_TM, MERGE_TM, OUT_TM, COMBINE_TM) == 0
    x2 = x.reshape(t, d)
    c8 = jnp.zeros((SUBLANES, d), F32).at[:batch].set(c)
    ret_col0 = 0
    att_col0 = 2 * RET_QK_W + 2 * RET_V_W
    gate_col0 = att_col0 + 3 * ATT_W
    for layer in range(depth):
        mod3 = _ada_ln(c8, w_ada, b_ada, layer)[:batch].reshape(batch * N_MOD, 1, d)
        h = _norm_mod(x2, norm1_g.reshape(depth, 1, d), mod3, layer, seq, shift_idx=0, scale_idx=1)
        proj = _project(h, w_in, layer)
        ret = _retention(proj, batch, seq, ret_col0)
        att = _chunk_attention(proj, rel_bias[layer], q_norm_g[layer], k_norm_g[layer], batch, seq, att_col0)
        merged = _merge(ret, att, w_ret_out, w_att_out, proj, gate_col0, layer)
        x2 = _out_proj(merged, w_out, x2, mod3, layer, seq, gate_idx=2)
        h_packed, logits = _norm_router(x2, norm2_g.reshape(depth, 1, d), mod3, w_router, b_router, layer, seq,
                                        shift_idx=3, scale_idx=4)
        gates, dest, sorted_tok, tables = _routing(logits)
        ys = _experts(h_packed, sorted_tok, tables, w1, b1, w2, b2, layer)
        x2 = _combine(ys, dest, gates, x2, mod3, seq, gate_idx=5)
    return x2.reshape(batch, seq, d)
```

```python
import functools

import numpy as np
import jax
import jax.numpy as jnp
from jax import lax
from jax.experimental import pallas as pl
from jax.experimental.pallas import tpu as pltpu

F32 = jnp.float32
BF16 = jnp.bfloat16
U32 = jnp.uint32

CHUNK = 64
NORM_EPS = 1e-6
RET_HEADS = 8
RET_QK_DIM = 256
RET_V_DIM = 512
ROPE_BASE = 10000.0
ATT_HEADS = 16
ATT_HEAD_DIM = 128
PREV_CHUNKS = 8
MAX_REL = 128
N_EXPERTS = 32
TOP_K = 4
SWIGLU_ALPHA = 1.702
SWIGLU_LIMIT = 7.0
N_MOD = 6

RET_QK_W = RET_HEADS * RET_QK_DIM
RET_V_W = RET_HEADS * RET_V_DIM
ATT_W = ATT_HEADS * ATT_HEAD_DIM

V7X_VMEM_LIMIT_BYTES = 58 * 1024 * 1024
HI16 = 0xFFFF0000

ADA_TN = 1024
NORM_TM = 512
PROJ_TM, PROJ_TN = 2048, 512
RET_BLOCK = 256
RET_HEADS_PER_STEP = 8
ATT_BLOCK = 256
ATT_HEADS_PER_STEP = 8
MERGE_TM, MERGE_TN = 1024, 512
OUT_TM, OUT_TN = 1024, 512
EXPERT_BM = 1280
EXPERT_SUB = 128
EXPERT_TF = 256
EXPERT_TN = 256
EXPERT_PERM_W = 256
EXPERT_ISSUE_STEPS = 10
GATHER_GROUP = 32
WEIGHT_DMA_QUEUE = 1
TOK_ALIGN = 1024
TOK_WINDOW = 3 * TOK_ALIGN
SUBLANES = 8
COMBINE_TM = 256


def _params(semantics):
    return pltpu.CompilerParams(dimension_semantics=semantics, vmem_limit_bytes=V7X_VMEM_LIMIT_BYTES)


def _silu(t):
    return t * jax.nn.sigmoid(t)


def _pack_halves(y):
    n = y.shape[1] // 2
    hi = lax.bitcast_convert_type(y[:, :n].astype(BF16).astype(F32), U32)
    lo = lax.bitcast_convert_type(y[:, n:].astype(BF16).astype(F32), U32)
    return hi | (lo >> 16)


def _unpack_halves(p):
    a = lax.bitcast_convert_type(p & jnp.uint32(HI16), F32)
    b = lax.bitcast_convert_type(p << 16, F32)
    return a, b


def _ada_kernel(c_ref, w_ref, b_ref, o_ref):
    a = _silu(c_ref[...]).astype(BF16)
    o_ref[...] = jnp.dot(a, w_ref[...].astype(BF16), preferred_element_type=F32) + b_ref[...]


def _ada_ln(c8, w_ada, b_ada, layer):
    rows, d = c8.shape
    n = w_ada.shape[2]
    return pl.pallas_call(
        _ada_kernel,
        out_shape=jax.ShapeDtypeStruct((rows, n), F32),
        grid=(n // ADA_TN,),
        in_specs=[
            pl.BlockSpec((rows, d), lambda j: (0, 0)),
            pl.BlockSpec((None, d, ADA_TN), lambda j: (layer, 0, j)),
            pl.BlockSpec((None, 1, ADA_TN), lambda j: (layer, 0, j)),
        ],
        out_specs=pl.BlockSpec((rows, ADA_TN), lambda j: (0, j)),
        compiler_params=_params(("arbitrary",)),
        name="ada_ln",
    )(c8, w_ada, b_ada.reshape(b_ada.shape[0], 1, n))


def _modulated_norm(x, g, scale, shift):
    xn = x * lax.rsqrt(jnp.mean(x * x, axis=-1, keepdims=True) + NORM_EPS)
    return (xn * g) * (1.0 + scale) + shift


def _norm_mod_kernel(x_ref, g_ref, sc_ref, sh_ref, o_ref):
    o_ref[...] = _modulated_norm(x_ref[...], g_ref[...], sc_ref[...], sh_ref[...]).astype(o_ref.dtype)


def _norm_mod(x2, g, mod3, layer, seq, shift_idx, scale_idx):
    t, d = x2.shape
    tm = NORM_TM

    def mod_spec(idx):
        return pl.BlockSpec((None, 1, d), lambda i: ((i * tm // seq) * N_MOD + idx, 0, 0))

    return pl.pallas_call(
        _norm_mod_kernel,
        out_shape=jax.ShapeDtypeStruct((t, d), BF16),
        grid=(t // tm,),
        in_specs=[
            pl.BlockSpec((tm, d), lambda i: (i, 0)),
            pl.BlockSpec((None, 1, d), lambda i: (layer, 0, 0)),
            mod_spec(scale_idx),
            mod_spec(shift_idx),
        ],
        out_specs=pl.BlockSpec((tm, d), lambda i: (i, 0)),
        compiler_params=_params(("arbitrary",)),
        name="norm_mod",
    )(x2, g, mod3, mod3)


def _norm_router_kernel(x_ref, g_ref, sc_ref, sh_ref, wr_ref, br_ref, h_ref, lg_ref):
    h = _modulated_norm(x_ref[...], g_ref[...], sc_ref[...], sh_ref[...])
    h_ref[...] = _pack_halves(h)
    h_hi = h.astype(BF16)
    h_lo = (h - h_hi.astype(F32)).astype(BF16)
    w = wr_ref[...]
    w_hi = w.astype(BF16)
    w_lo = (w - w_hi.astype(F32)).astype(BF16)
    e = w.shape[1]
    both = jnp.dot(h_hi, jnp.concatenate([w_hi, w_lo], axis=1), preferred_element_type=F32)
    lg = both[:, :e] + jnp.dot(h_lo, w_hi, preferred_element_type=F32) + both[:, e:]
    lg_ref[...] = lg + br_ref[...]


def _norm_router(x2, g, mod3, w_router, b_router, layer, seq, shift_idx, scale_idx):
    t, d = x2.shape
    e = w_router.shape[2]
    tm = NORM_TM

    def mod_spec(idx):
        return pl.BlockSpec((None, 1, d), lambda i: ((i * tm // seq) * N_MOD + idx, 0, 0))

    return pl.pallas_call(
        _norm_router_kernel,
        out_shape=(jax.ShapeDtypeStruct((t, d // 2), U32), jax.ShapeDtypeStruct((t, e), F32)),
        grid=(t // tm,),
        in_specs=[
            pl.BlockSpec((tm, d), lambda i: (i, 0)),
            pl.BlockSpec((None, 1, d), lambda i: (layer, 0, 0)),
            mod_spec(scale_idx),
            mod_spec(shift_idx),
            pl.BlockSpec((None, d, e), lambda i: (layer, 0, 0)),
            pl.BlockSpec((None, 1, e), lambda i: (layer, 0, 0)),
        ],
        out_specs=(pl.BlockSpec((tm, d // 2), lambda i: (i, 0)), pl.BlockSpec((tm, e), lambda i: (i, 0))),
        compiler_params=_params(("arbitrary",)),
        name="norm_router",
    )(x2, g, mod3, mod3, w_router, b_router.reshape(b_router.shape[0], 1, e))


def _proj_kernel(a_ref, w_ref, o_ref):
    o_ref[...] = jnp.dot(a_ref[...], w_ref[...].astype(BF16), preferred_element_type=F32).astype(o_ref.dtype)


def _project(a, w, layer):
    t, k = a.shape
    n = w.shape[2]
    tm, tn = PROJ_TM, PROJ_TN
    return pl.pallas_call(
        _proj_kernel,
        out_shape=jax.ShapeDtypeStruct((t, n), BF16),
        grid=(n // tn, t // tm),
        in_specs=[
            pl.BlockSpec((tm, k), lambda j, i: (i, 0)),
            pl.BlockSpec((None, k, tn), lambda j, i: (layer, 0, j)),
        ],
        out_specs=pl.BlockSpec((tm, tn), lambda j, i: (i, j)),
        compiler_params=_params(("arbitrary", "arbitrary")),
        name="in_proj",
    )(a, w)


def _retention_kernel(q_ref, k_ref, v_ref, g_ref, cos_ref, sin_ref, dec_ref, qd_ref, kd_ref, cd_ref,
                      o_ref, state_ref):
    @pl.when(pl.program_id(2) == 0)
    def _():
        state_ref[...] = jnp.zeros_like(state_ref)

    cos = cos_ref[...]
    sin = sin_ref[...]
    half = RET_QK_DIM // 2
    contract_last = (((1,), (1,)), ((), ()))
    contract_rows = (((0,), (0,)), ((), ()))

    def rotary(t):
        t1, t2 = t[:, :half], t[:, half:]
        return jnp.concatenate([t1 * cos - t2 * sin, t1 * sin + t2 * cos], axis=1)

    for hh in range(RET_HEADS_PER_STEP):
        qk = slice(hh * RET_QK_DIM, (hh + 1) * RET_QK_DIM)
        vg = slice(hh * RET_V_DIM, (hh + 1) * RET_V_DIM)
        q = rotary(q_ref[:, qk].astype(F32))
        k = rotary(k_ref[:, qk].astype(F32)) * (RET_QK_DIM ** -0.5)
        v = v_ref[:, vg]
        scores = lax.dot_general(q.astype(BF16), k.astype(BF16), contract_last,
                                 preferred_element_type=F32) * dec_ref[hh]
        state = state_ref[hh]
        out = (jnp.dot(scores.astype(BF16), v, preferred_element_type=F32)
               + jnp.dot((q * qd_ref[hh]).astype(BF16), state.astype(BF16), preferred_element_type=F32))
        state_ref[hh] = state * cd_ref[hh] + lax.dot_general(
            (k * kd_ref[hh]).astype(BF16), v, contract_rows, preferred_element_type=F32)
        on = out * lax.rsqrt(jnp.mean(out * out, axis=-1, keepdims=True) + NORM_EPS)
        o_ref[:, vg] = (_silu(g_ref[:, vg].astype(F32)) * on).astype(o_ref.dtype)


def _retention_tables(seq):
    half = RET_QK_DIM // 2
    inv_freq = ROPE_BASE ** (-np.arange(half, dtype=np.float64) / half)
    ang = np.arange(seq, dtype=np.float64)[:, None] * inv_freq[None, :]
    log_gamma = np.log(1.0 - 2.0 ** (-5.0 - np.arange(RET_HEADS, dtype=np.float64)))[:, None, None]
    pos = np.arange(RET_BLOCK, dtype=np.float64)
    n, m = pos[:, None], pos[None, :]
    cn, cm = np.floor(n / CHUNK), np.floor(m / CHUNK)
    dist = np.where(cn == cm, np.abs(n - m), n - m)
    dec = np.where((cm <= cn)[None], np.exp(log_gamma * dist[None]), 0.0)
    qd = np.broadcast_to(np.exp(log_gamma * (pos[None, :, None] + 1.0)), (RET_HEADS, RET_BLOCK, RET_QK_DIM))
    kd = np.broadcast_to(np.exp(log_gamma * (RET_BLOCK - 1.0 - pos[None, :, None])),
                         (RET_HEADS, RET_BLOCK, RET_QK_DIM))
    cd = np.broadcast_to(np.exp(log_gamma * RET_BLOCK), (RET_HEADS, 1, RET_V_DIM))
    return tuple(jnp.asarray(np.ascontiguousarray(v), F32) for v in (np.cos(ang), np.sin(ang), dec, qd, kd, cd))


def _retention(proj, batch, seq, col0):
    t = proj.shape[0]
    r = RET_BLOCK
    nsb = seq // r
    cos, sin, dec, qd, kd, cd = _retention_tables(seq)
    hp = RET_HEADS_PER_STEP
    groups = RET_HEADS // hp
    qk_w, v_w = hp * RET_QK_DIM, hp * RET_V_DIM
    assert col0 % qk_w == 0 and (col0 + 2 * RET_QK_W) % v_w == 0
    qk_blk0 = col0 // qk_w
    v_blk0 = (col0 + 2 * RET_QK_W) // v_w
    row = lambda b, h, s: b * nsb + s
    head_tab = lambda shape: pl.BlockSpec((hp,) + shape, lambda b, h, s: (h, 0, 0))
    half = RET_QK_DIM // 2
    return pl.pallas_call(
        _retention_kernel,
        out_shape=jax.ShapeDtypeStruct((t, RET_V_W), BF16),
        grid=(batch, groups, nsb),
        in_specs=[
            pl.BlockSpec((r, qk_w), lambda b, h, s: (row(b, h, s), qk_blk0 + h)),
            pl.BlockSpec((r, qk_w), lambda b, h, s: (row(b, h, s), qk_blk0 + groups + h)),
            pl.BlockSpec((r, v_w), lambda b, h, s: (row(b, h, s), v_blk0 + h)),
            pl.BlockSpec((r, v_w), lambda b, h, s: (row(b, h, s), v_blk0 + groups + h)),
            pl.BlockSpec((r, half), lambda b, h, s: (s, 0)),
            pl.BlockSpec((r, half), lambda b, h, s: (s, 0)),
            head_tab((r, r)),
            head_tab((r, RET_QK_DIM)),
            head_tab((r, RET_QK_DIM)),
            head_tab((1, RET_V_DIM)),
        ],
        out_specs=pl.BlockSpec((r, v_w), lambda b, h, s: (row(b, h, s), h)),
        scratch_shapes=[pltpu.VMEM((hp, RET_QK_DIM, RET_V_DIM), F32)],
        compiler_params=_params(("arbitrary", "arbitrary", "arbitrary")),
        name="retention",
    )(proj, proj, proj, proj, cos, sin, dec, qd, kd, cd)


ATT_KEY_BLOCKS = PREV_CHUNKS * CHUNK // ATT_BLOCK + 1
NEG = float(np.finfo(np.float32).min)


def _attention_kernel(q_ref, k0_ref, k1_ref, k2_ref, v0_ref, v1_ref, v2_ref, tab_ref, gq_ref, gk_ref, o_ref):
    i = pl.program_id(1)

    def rms(t, g):
        tf = t.astype(F32)
        return (tf * lax.rsqrt(jnp.mean(tf * tf, axis=-1, keepdims=True) + NORM_EPS)) * g

    dh = ATT_HEAD_DIM
    k_all = jnp.concatenate([k0_ref[...], k1_ref[...], k2_ref[...]], axis=0)
    v_all = jnp.concatenate([v0_ref[...], v1_ref[...], v2_ref[...]], axis=0)
    outs = []
    for hh in range(ATT_HEADS_PER_STEP):
        lanes = slice(hh * dh, (hh + 1) * dh)
        q = rms(q_ref[:, lanes], gq_ref[...]).astype(BF16)
        k = rms(k_all[:, lanes], gk_ref[...]).astype(BF16)
        sc = lax.dot_general(q, k, (((1,), (1,)), ((), ())), preferred_element_type=F32)
        sc = sc * (ATT_HEAD_DIM ** -0.5) + tab_ref[hh]
        col = lax.broadcasted_iota(jnp.int32, sc.shape, 1)
        sc = jnp.where(col >= (ATT_KEY_BLOCKS - 1 - i) * ATT_BLOCK, sc, NEG)
        p = jnp.exp(sc - jnp.max(sc, axis=-1, keepdims=True))
        denom = jnp.sum(p, axis=-1, keepdims=True)
        outs.append(jnp.dot(p.astype(BF16), v_all[:, lanes], preferred_element_type=F32) / denom)
    o_ref[...] = jnp.concatenate(outs, axis=1).astype(o_ref.dtype)


def _attention_table(rel_bias):
    a = ATT_BLOCK
    w = ATT_KEY_BLOCKS * a
    pad = PREV_CHUNKS * CHUNK
    heads = rel_bias.shape[0]
    period = 1024
    assert w + a - 1 <= period
    i = np.arange(period)
    rel_of_i = np.where(i < w, pad - i, pad + period - i)
    idx = np.clip(np.minimum(rel_of_i, MAX_REL) + (CHUNK - 1), 0, rel_bias.shape[-1] - 1)
    u = rel_bias[:, idx].astype(F32).reshape(heads, 1, period)
    return pl.pallas_call(
        _attention_table_kernel,
        out_shape=jax.ShapeDtypeStruct((heads, a, w), F32),
        grid=(heads,),
        in_specs=[pl.BlockSpec((None, 1, period), lambda h: (h, 0, 0))],
        out_specs=pl.BlockSpec((None, a, w), lambda h: (h, 0, 0)),
        compiler_params=_params(("arbitrary",)),
        name="attention_table",
    )(u)


def _attention_table_kernel(u_ref, o_ref):
    a, w = o_ref.shape
    rows = jnp.broadcast_to(u_ref[...], (a, u_ref.shape[1]))
    skew = pltpu.roll(rows, 0, 1, stride=1, stride_axis=0)
    chunk_bits = CHUNK.bit_length() - 1
    assert CHUNK == 1 << chunk_bits
    qc = lax.broadcasted_iota(jnp.int32, (a, w), 0) >> chunk_bits
    kc = (lax.broadcasted_iota(jnp.int32, (a, w), 1) >> chunk_bits) - PREV_CHUNKS
    valid = (kc <= qc) & (kc >= qc - PREV_CHUNKS)
    o_ref[...] = jnp.where(valid, skew[:, :w], NEG)


def _chunk_attention(proj, rel_bias, q_norm_g, k_norm_g, batch, seq, col0):
    assert ATT_KEY_BLOCKS == 3
    t = proj.shape[0]
    a = ATT_BLOCK
    nqb = seq // a
    dh = ATT_HEAD_DIM
    hw = ATT_HEADS_PER_STEP * dh
    groups = ATT_HEADS // ATT_HEADS_PER_STEP
    assert col0 % hw == 0 and ATT_W % hw == 0
    blk0 = col0 // hw
    tab = _attention_table(rel_bias)

    def kv_spec(which, jj):
        return pl.BlockSpec(
            (a, hw),
            lambda h, i, b: (b * nqb + jnp.maximum(i - (ATT_KEY_BLOCKS - 1) + jj, 0), blk0 + which * groups + h))

    return pl.pallas_call(
        _attention_kernel,
        out_shape=jax.ShapeDtypeStruct((t, ATT_W), BF16),
        grid=(groups, nqb, batch),
        in_specs=[
            pl.BlockSpec((a, hw), lambda h, i, b: (b * nqb + i, blk0 + h)),
            kv_spec(1, 0), kv_spec(1, 1), kv_spec(1, 2),
            kv_spec(2, 0), kv_spec(2, 1), kv_spec(2, 2),
            pl.BlockSpec((ATT_HEADS_PER_STEP, a, ATT_KEY_BLOCKS * a), lambda h, i, b: (h, 0, 0)),
            pl.BlockSpec((1, dh), lambda h, i, b: (0, 0)),
            pl.BlockSpec((1, dh), lambda h, i, b: (0, 0)),
        ],
        out_specs=pl.BlockSpec((a, hw), lambda h, i, b: (b * nqb + i, h)),
        compiler_params=_params(("arbitrary", "arbitrary", "arbitrary")),
        name="chunk_attention",
    )(proj, proj, proj, proj, proj, proj, proj, tab, q_norm_g.reshape(1, dh), k_norm_g.reshape(1, dh))


def _merge_kernel(r_ref, a_ref, wr_ref, wa_ref, gr_ref, ga_ref, o_ref):
    yr = jnp.dot(r_ref[...], wr_ref[...].astype(BF16), preferred_element_type=F32)
    ya = jnp.dot(a_ref[...], wa_ref[...].astype(BF16), preferred_element_type=F32)
    merged = jax.nn.sigmoid(gr_ref[...].astype(F32)) * yr + jax.nn.sigmoid(ga_ref[...].astype(F32)) * ya
    o_ref[...] = merged.astype(o_ref.dtype)


def _merge(ret, att, w_ret_out, w_att_out, proj, gate_col0, layer):
    t = ret.shape[0]
    d = w_ret_out.shape[2]
    tm, tn = MERGE_TM, MERGE_TN
    gblk = gate_col0 // tn
    return pl.pallas_call(
        _merge_kernel,
        out_shape=jax.ShapeDtypeStruct((t, d), BF16),
        grid=(d // tn, t // tm),
        in_specs=[
            pl.BlockSpec((tm, ret.shape[1]), lambda j, i: (i, 0)),
            pl.BlockSpec((tm, att.shape[1]), lambda j, i: (i, 0)),
            pl.BlockSpec((None, ret.shape[1], tn), lambda j, i: (layer, 0, j)),
            pl.BlockSpec((None, att.shape[1], tn), lambda j, i: (layer, 0, j)),
            pl.BlockSpec((tm, tn), lambda j, i: (i, gblk + j)),
            pl.BlockSpec((tm, tn), lambda j, i: (i, gblk + d // tn + j)),
        ],
        out_specs=pl.BlockSpec((tm, tn), lambda j, i: (i, j)),
        compiler_params=_params(("arbitrary", "arbitrary")),
        name="merge",
    )(ret, att, w_ret_out, w_att_out, proj, proj)


def _out_proj_kernel(a_ref, w_ref, x_ref, gate_ref, o_ref):
    acc = jnp.dot(a_ref[...], w_ref[...].astype(BF16), preferred_element_type=F32)
    o_ref[...] = x_ref[...] + gate_ref[...] * acc


def _out_proj(a, w, x2, mod3, layer, seq, gate_idx):
    t, k = a.shape
    d = w.shape[2]
    tm, tn = OUT_TM, OUT_TN
    return pl.pallas_call(
        _out_proj_kernel,
        out_shape=jax.ShapeDtypeStruct((t, d), F32),
        grid=(d // tn, t // tm),
        in_specs=[
            pl.BlockSpec((tm, k), lambda j, i: (i, 0)),
            pl.BlockSpec((None, k, tn), lambda j, i: (layer, 0, j)),
            pl.BlockSpec((tm, tn), lambda j, i: (i, j)),
            pl.BlockSpec((None, 1, tn), lambda j, i: ((i * tm // seq) * N_MOD + gate_idx, 0, j)),
        ],
        out_specs=pl.BlockSpec((tm, tn), lambda j, i: (i, j)),
        compiler_params=_params(("arbitrary", "arbitrary")),
        name="out_proj",
    )(a, w, x2, mod3)


def _routing(logits):
    t = logits.shape[0]
    bm = EXPERT_BM
    a = t * TOP_K
    top_val, top_idx = lax.top_k(logits, TOP_K)
    gate = jax.nn.softmax(top_val, axis=-1)
    onehot = top_idx[:, :, None] == jnp.arange(N_EXPERTS, dtype=jnp.int32)[None, None, :]
    picked = jnp.any(onehot, axis=1).astype(jnp.int32)
    inclusive = jnp.cumsum(picked, axis=0)
    sizes = inclusive[-1]
    sub = EXPERT_SUB
    padded = (sizes + sub - 1) // sub * sub
    pad_end = jnp.cumsum(padded)
    pad_start = pad_end - padded
    start = jnp.cumsum(sizes) - sizes
    slot = (pad_start[None, :] + inclusive - picked)[:, None, :]
    dest = jnp.sum(jnp.where(onehot, slot, 0), axis=-1).astype(jnp.int32).reshape(a)
    order = jnp.argsort(top_idx.reshape(a))
    n_blocks = a // bm + N_EXPERTS
    group_blocks = (sizes + bm - 1) // bm
    blocks_end = jnp.cumsum(group_blocks)
    n_valid = blocks_end[-1].astype(jnp.int32)
    blk = jnp.arange(n_blocks, dtype=jnp.int32)
    block_e = jnp.minimum(jnp.searchsorted(blocks_end, blk, side='right'), N_EXPERTS - 1).astype(jnp.int32)
    chunk = blk - (blocks_end - group_blocks)[block_e]
    block_valid = blk < n_valid
    block_rows = jnp.where(block_valid, jnp.clip(sizes[block_e] - chunk * bm, 0, bm), 0)
    block_nsub = ((block_rows + sub - 1) // sub).astype(jnp.int32)
    block_row0 = jnp.where(block_valid, pad_start[block_e] + chunk * bm, 0).astype(jnp.int32)
    padded_len = (a // TOK_ALIGN) * TOK_ALIGN + TOK_WINDOW
    sorted_tok = jnp.zeros((padded_len,), jnp.int32).at[:a].set((order // TOP_K).astype(jnp.int32))
    block_first = jnp.where(block_valid, start[block_e] + chunk * bm, 0).astype(jnp.int32)
    tables = (block_e, block_nsub, block_first, block_row0, jnp.stack([n_valid, pad_end[-1].astype(jnp.int32)]))
    return gate, dest, sorted_tok, tables


def _expert_kernel(be_ref, ns_ref, first_ref, row0_ref, info_ref, tok_hbm, h_hbm, w1_hbm, b1_ref, w2_hbm, b2_ref,
                   perm_ref, ys_hbm, w1buf, w2buf, gbuf, x_scr, act_scr, obuf, zbuf, tokbuf,
                   sem_w1, sem_w2, sem_g, sem_o, sem_t, *, layer, nj1, nj2):
    sub = EXPERT_SUB
    group = GATHER_GROUP
    bm, d = x_scr.shape
    half = d // 2
    tf = act_scr.shape[2]
    tn = obuf.shape[2]
    steps = nj1 + nj2
    nb_max = be_ref.shape[0]
    n_valid = info_ref[0]
    total = n_valid * steps
    per_step = bm // EXPERT_ISSUE_STEPS

    def for_rows(nsub, fn):
        assert bm // sub < 16
        for size in (8, 4, 2, 1):
            before = (nsub // (2 * size)) * (2 * size)

            @pl.when((nsub & size) != 0)
            def _():
                fn(pl.multiple_of(before * sub, size * sub), size * sub)

    def w1_copy(e, j, slot):
        cols = pl.ds(pl.multiple_of(j * 2 * tf, 2 * tf), 2 * tf)
        return pltpu.make_async_copy(w1_hbm.at[layer, e, :, cols], w1buf.at[slot], sem_w1.at[slot])

    def w2_copies(e, n, slot):
        return [pltpu.make_async_copy(w2_hbm.at[layer, e, :, pl.ds(pl.multiple_of(c0, tn), tn)],
                                      w2buf.at[slot, :, pl.ds(k * tn, tn)], sem_w2.at[slot])
                for k, c0 in enumerate((n * tn, half + n * tn))]

    def tok_copy(blk, slot):
        src = tok_hbm.at[pl.ds(pl.multiple_of(first_ref[blk] // TOK_ALIGN * TOK_ALIGN, TOK_ALIGN), TOK_WINDOW)]
        dst = tokbuf.at[pl.ds(pl.multiple_of(slot * TOK_WINDOW, TOK_ALIGN), TOK_WINDOW)]
        return pltpu.make_async_copy(src, dst, sem_t.at[slot])

    def tok_base(blk, slot):
        return slot * TOK_WINDOW + lax.rem(first_ref[blk], TOK_ALIGN)

    def row_copy(tok_at, tile_row, sublane):
        src = h_hbm.at[pl.ds(tokbuf[tok_at], 1)]
        return pltpu.make_async_copy(src, gbuf.at[tile_row, pl.ds(sublane, 1)], sem_g)

    def out_copy(slot, r0, m, row0, n):
        dst = ys_hbm.at[pl.ds(pl.multiple_of(row0 + r0, sub), m), pl.ds(pl.multiple_of(n * tn, tn), tn)]
        return pltpu.make_async_copy(obuf.at[slot, pl.ds(r0, m)], dst, sem_o.at[slot])

    @pl.when(n_valid > 0)
    def _():
        tok_copy(0, 0).start()
        w1_copy(be_ref[0], 0, 0).start(priority=WEIGHT_DMA_QUEUE)
        tok_copy(0, 0).wait()
        base = tok_base(0, 0)

        def start(q, carry):
            for s in range(SUBLANES):
                row_copy(base + q * SUBLANES + s, q, s).start()
            return carry
        lax.fori_loop(0, ns_ref[0] * (sub // SUBLANES), start, 0)

    def item(it, carry):
        b = it // steps
        j = it - b * steps
        e = be_ref[b]
        nsub = ns_ref[b]
        has_next = b + 1 < n_valid
        nxt = jnp.minimum(b + 1, nb_max - 1)

        nit = it + 1
        nbk = jnp.minimum(nit // steps, nb_max - 1)
        nj = nit - (nit // steps) * steps

        @pl.when((nit < total) & (nj < nj1) & (nj > 0))
        def _():
            w1_copy(be_ref[nbk], nj, lax.rem(nj, 2)).start(priority=WEIGHT_DMA_QUEUE)

        @pl.when((j == nj1) & has_next)
        def _():
            w1_copy(be_ref[nxt], 0, 0).start(priority=WEIGHT_DMA_QUEUE)

        @pl.when((nit < total) & (nj >= nj1))
        def _():
            for c in w2_copies(be_ref[nbk], nj - nj1, lax.rem(nj - nj1, 2)):
                c.start(priority=WEIGHT_DMA_QUEUE)

        @pl.when(j < nj1)
        def _():
            w1_copy(e, j, lax.rem(j, 2)).wait()

        @pl.when(j >= nj1)
        def _():
            for c in w2_copies(e, j - nj1, lax.rem(j - nj1, 2)):
                c.wait()

        @pl.when(j == 0)
        def _():
            def wait_rows(s, carry):
                pltpu.make_async_copy(h_hbm.at[pl.ds(0, sub)], zbuf, sem_g).wait()
                return carry
            lax.fori_loop(0, nsub, wait_rows, 0)

            def unpack_rows(s, carry):
                rows = pl.ds(pl.multiple_of(s * sub, sub), sub)
                tiles = pl.ds(pl.multiple_of(s * (sub // SUBLANES), sub // SUBLANES), sub // SUBLANES)
                xa, xb = _unpack_halves(gbuf[tiles].reshape(sub, half))
                x_scr[rows, pl.ds(0, half)] = xa.astype(BF16)
                x_scr[rows, pl.ds(half, half)] = xb.astype(BF16)
                return carry
            lax.fori_loop(0, nsub, unpack_rows, 0)

            @pl.when(has_next)
            def _():
                tok_copy(nxt, lax.rem(b + 1, 2)).start()

        @pl.when((j == 1) & has_next)
        def _():
            tok_copy(nxt, lax.rem(b + 1, 2)).wait()

        @pl.when((j >= 1) & (j <= EXPERT_ISSUE_STEPS) & has_next)
        def _():
            base = tok_base(nxt, lax.rem(b + 1, 2))
            rows_next = ns_ref[nxt] * sub
            for g in range(per_step // group):
                first_row = (j - 1) * per_step + g * group
                first_tile = (j - 1) * (per_step // SUBLANES) + g * (group // SUBLANES)

                @pl.when(first_row < rows_next)
                def _():
                    for i in range(group):
                        row_copy(base + first_row + i, first_tile + i // SUBLANES, i % SUBLANES).start()

        @pl.when(j < nj1)
        def _():
            wslot = lax.rem(j, 2)
            bias = b1_ref[pl.ds(e * nj1 + j, 1), :]

            def first_matmul(r0, m):
                rows = pl.ds(r0, m)
                hb = jnp.dot(x_scr[rows, :], w1buf[wslot].astype(BF16), preferred_element_type=F32) + bias
                hb = hb.astype(BF16)
                gw = perm_ref.shape[0]
                sel = [jnp.dot(hb[:, c * gw:(c + 1) * gw], perm_ref[...], preferred_element_type=F32)
                       for c in range(2 * tf // gw)]
                x_glu = jnp.minimum(jnp.concatenate([s[:, :gw // 2] for s in sel], axis=1), SWIGLU_LIMIT)
                x_lin = jnp.clip(jnp.concatenate([s[:, gw // 2:] for s in sel], axis=1),
                                 -SWIGLU_LIMIT, SWIGLU_LIMIT)
                act = x_glu * jax.nn.sigmoid(SWIGLU_ALPHA * x_glu) * (x_lin + 1.0)
                act_scr[j, rows, :] = act.astype(BF16)
            for_rows(nsub, first_matmul)

        @pl.when(j >= nj1)
        def _():
            n = j - nj1
            slot = lax.rem(n, 2)
            row0 = row0_ref[b]
            bias = jnp.concatenate([b2_ref[pl.ds(e * 2 * nj2 + n, 1), :],
                                    b2_ref[pl.ds(e * 2 * nj2 + nj2 + n, 1), :]], axis=1)
            sent = jnp.where(n >= 2, nsub, jnp.where(b > 0, ns_ref[jnp.maximum(b - 1, 0)], 0))
            for_rows(sent, lambda r0, m: out_copy(slot, r0, m, 0, 0).wait())

            def second_matmul(r0, m):
                rows = pl.ds(r0, m)
                act = jnp.concatenate([act_scr[t, rows, :] for t in range(nj1)], axis=1)
                y = jnp.dot(act, w2buf[slot].astype(BF16), preferred_element_type=F32) + bias
                obuf[slot, rows, :] = _pack_halves(y)
                out_copy(slot, r0, m, row0, n).start()
            for_rows(nsub, second_matmul)

        return carry

    lax.fori_loop(0, total, item, 0)

    @pl.when(n_valid > 0)
    def _():
        last = ns_ref[jnp.maximum(n_valid - 1, 0)]
        for slot in (0, 1):
            for_rows(last, lambda r0, m: out_copy(slot, r0, m, 0, 0).wait())

    used = info_ref[1]
    n_tail = (ys_hbm.shape[0] - used) // sub
    zbuf[...] = jnp.zeros_like(zbuf)

    def tail_copy(i):
        dst = ys_hbm.at[pl.ds(pl.multiple_of(used + i * sub, sub), sub)]
        return pltpu.make_async_copy(zbuf, dst, sem_g)

    def tail_start(i, carry):
        tail_copy(i).start()
        return carry

    def tail_wait(i, carry):
        tail_copy(i).wait()
        return carry
    lax.fori_loop(0, n_tail, tail_start, 0)
    lax.fori_loop(0, n_tail, tail_wait, 0)


def _deinterleave_matrix(tf):
    p = np.zeros((2 * tf, 2 * tf), np.float32)
    f = np.arange(tf)
    p[2 * f, f] = 1.0
    p[2 * f + 1, tf + f] = 1.0
    return jnp.asarray(p, BF16)


def _experts(h_packed, sorted_tok, tables, w1, b1, w2, b2, layer):
    t, half = h_packed.shape
    d = 2 * half
    n_experts, f = w2.shape[1], w2.shape[2]
    bm, tf, tn = EXPERT_BM, EXPERT_TF, EXPERT_TN
    nj1 = f // tf
    nj2 = half // tn
    assert nj1 % 2 == 0 and nj2 % 2 == 0
    assert 1 + EXPERT_ISSUE_STEPS <= nj1 + nj2 and bm % (EXPERT_ISSUE_STEPS * GATHER_GROUP) == 0
    assert bm % EXPERT_SUB == 0 and t >= EXPERT_SUB and GATHER_GROUP % SUBLANES == 0
    assert bm + TOK_ALIGN - 1 <= TOK_WINDOW
    p_rows = t * TOP_K + n_experts * EXPERT_SUB
    whole = lambda shape: pl.BlockSpec(shape, lambda i, *_: (0,) * len(shape))
    hbm = pl.BlockSpec(memory_space=pl.ANY)
    return pl.pallas_call(
        functools.partial(_expert_kernel, layer=layer, nj1=nj1, nj2=nj2),
        out_shape=jax.ShapeDtypeStruct((p_rows, half), U32),
        grid_spec=pltpu.PrefetchScalarGridSpec(
            num_scalar_prefetch=5,
            grid=(1,),
            in_specs=[hbm, hbm, hbm, whole((n_experts * nj1, 2 * tf)), hbm, whole((n_experts * 2 * nj2, tn)),
                      whole((EXPERT_PERM_W, EXPERT_PERM_W))],
            out_specs=hbm,
            scratch_shapes=[
                pltpu.VMEM((2, d, 2 * tf), F32),
                pltpu.VMEM((2, f, 2 * tn), F32),
                pltpu.VMEM((bm // SUBLANES, SUBLANES, half), U32),
                pltpu.VMEM((bm, d), BF16),
                pltpu.VMEM((nj1, bm, tf), BF16),
                pltpu.VMEM((2, bm, tn), U32),
                pltpu.VMEM((EXPERT_SUB, half), U32),
                pltpu.SMEM((2 * TOK_WINDOW,), jnp.int32),
                pltpu.SemaphoreType.DMA((2,)),
                pltpu.SemaphoreType.DMA((2,)),
                pltpu.SemaphoreType.DMA(()),
                pltpu.SemaphoreType.DMA((2,)),
                pltpu.SemaphoreType.DMA((2,)),
            ],
        ),
        compiler_params=_params(("arbitrary",)),
        name="experts",
    )(*tables, sorted_tok, h_packed, w1, b1[layer].reshape(n_experts * nj1, 2 * tf), w2,
      b2[layer].reshape(n_experts * 2 * nj2, tn), _deinterleave_matrix(EXPERT_PERM_W // 2))


def _combine_kernel(dest_hbm, ys_hbm, x_ref, gate_ref, g_ref, o_ref, buf, idx, sem, sem_idx):
    i = pl.program_id(0)
    tm = x_ref.shape[0]
    half = buf.shape[3]
    sublanes = SUBLANES
    per_step = tm * TOP_K
    slot = lax.rem(i, 2)

    def idx_copy(step, s):
        src = dest_hbm.at[pl.ds(pl.multiple_of(step * per_step, per_step), per_step)]
        return pltpu.make_async_copy(src, idx.at[pl.ds(pl.multiple_of(s * per_step, per_step), per_step)],
                                     sem_idx.at[s])

    @pl.when(i == 0)
    def _():
        idx_copy(0, 0).start()

    idx_copy(i, slot).wait()

    @pl.when(i + 1 < pl.num_programs(0))
    def _():
        idx_copy(i + 1, 1 - slot).start()

    def row_copy(src_row, k, tile, s):
        return pltpu.make_async_copy(ys_hbm.at[pl.ds(src_row, 1)], buf.at[k, tile, pl.ds(s, 1)], sem)

    def start(tile, carry):
        base = slot * per_step + tile * (sublanes * TOP_K)
        for s in range(sublanes):
            for k in range(TOP_K):
                row_copy(idx[base + s * TOP_K + k], k, tile, s).start()
        return carry

    def wait(tile, carry):
        for _ in range(sublanes * TOP_K):
            row_copy(0, 0, 0, 0).wait()
        return carry

    lax.fori_loop(0, tm // sublanes, start, 0)
    lax.fori_loop(0, tm // sublanes, wait, 0)

    def chunk(c, carry):
        rows = pl.ds(pl.multiple_of(c * sublanes, sublanes), sublanes)
        g = g_ref[rows, :]
        ya = jnp.zeros((sublanes, half), F32)
        yb = jnp.zeros((sublanes, half), F32)
        for k in range(TOP_K):
            a, b = _unpack_halves(buf[k, c])
            ya = ya + g[:, k:k + 1] * a
            yb = yb + g[:, k:k + 1] * b
        o_ref[rows, pl.ds(0, half)] = x_ref[rows, pl.ds(0, half)] + gate_ref[:, pl.ds(0, half)] * ya
        o_ref[rows, pl.ds(half, half)] = x_ref[rows, pl.ds(half, half)] + gate_ref[:, pl.ds(half, half)] * yb
        return carry

    lax.fori_loop(0, tm // sublanes, chunk, 0, unroll=4)


def _combine(ys, dest, gates, x2, mod3, seq, gate_idx):
    t, d = x2.shape
    tm = COMBINE_TM
    half = ys.shape[1]
    assert (tm * TOP_K) % TOK_ALIGN == 0
    return pl.pallas_call(
        _combine_kernel,
        out_shape=jax.ShapeDtypeStruct((t, d), F32),
        grid=(t // tm,),
        in_specs=[
            pl.BlockSpec(memory_space=pl.ANY),
            pl.BlockSpec(memory_space=pl.ANY),
            pl.BlockSpec((tm, d), lambda i: (i, 0)),
            pl.BlockSpec((None, 1, d), lambda i: ((i * tm // seq) * N_MOD + gate_idx, 0, 0)),
            pl.BlockSpec((tm, TOP_K), lambda i: (i, 0)),
        ],
        out_specs=pl.BlockSpec((tm, d), lambda i: (i, 0)),
        scratch_shapes=[
            pltpu.VMEM((TOP_K, tm // SUBLANES, SUBLANES, half), U32),
            pltpu.SMEM((2 * tm * TOP_K,), jnp.int32),
            pltpu.SemaphoreType.DMA(()),
            pltpu.SemaphoreType.DMA((2,)),
        ],
        compiler_params=_params(("arbitrary",)),
        name="combine",
    )(dest, ys, x2, mod3, gates)


def kernel(x, c, w_ada, b_ada, norm1_g, w_in, q_norm_g, k_norm_g, rel_bias, w_ret_out, w_att_out, w_out,
           norm2_g, w_router, b_router, w1, b1, w2, b2):
    batch, seq, d = x.shape
    depth = w_ada.shape[0]
    t = batch * seq
    assert batch <= SUBLANES and t % PROJ_TM == 0
    assert seq % max(RET_BLOCK, ATT_BLOCK, NORM_TM, MERGE_TM, OUT_TM, COMBINE_TM) == 0
    x2 = x.reshape(t, d)
    c8 = jnp.zeros((SUBLANES, d), F32).at[:batch].set(c)
    ret_col0 = 0
    att_col0 = 2 * RET_QK_W + 2 * RET_V_W
    gate_col0 = att_col0 + 3 * ATT_W
    for layer in range(depth):
        mod3 = _ada_ln(c8, w_ada, b_ada, layer)[:batch].reshape(batch * N_MOD, 1, d)
        h = _norm_mod(x2, norm1_g.reshape(depth, 1, d), mod3, layer, seq, shift_idx=0, scale_idx=1)
        proj = _project(h, w_in, layer)
        ret = _retention(proj, batch, seq, ret_col0)
        att = _chunk_attention(proj, rel_bias[layer], q_norm_g[layer], k_norm_g[layer], batch, seq, att_col0)
        merged = _merge(ret, att, w_ret_out, w_att_out, proj, gate_col0, layer)
        x2 = _out_proj(merged, w_out, x2, mod3, layer, seq, gate_idx=2)
        h_packed, logits = _norm_router(x2, norm2_g.reshape(depth, 1, d), mod3, w_router, b_router, layer, seq,
                                        shift_idx=3, scale_idx=4)
        gates, dest, sorted_tok, tables = _routing(logits)
        ys = _experts(h_packed, sorted_tok, tables, w1, b1, w2, b2, layer)
        x2 = _combine(ys, dest, gates, x2, mod3, seq, gate_idx=5)
    return x2.reshape(batch, seq, d)
```
